```python
import math
import jax, jax.numpy as jnp
from jax import lax
import numpy as np

D_MODEL = 1024
BATCH = 16
SEQ = 2048
DEPTH = 4

GRID_W = 64
CTX_LEN = 256
HEAD_DIM = 64
ROPE_THETA = 10000.0
Q_BLOCK = 128
CHUNK = 128
EPS = 1e-6
HALF_W = D_MODEL // 2
GQA_HEADS = HALF_W // HEAD_DIM
GQA_KV_HEADS = GQA_HEADS // 4
DIFF_HEADS = HALF_W // (2 * HEAD_DIM)
RET_HEADS = HALF_W // (2 * HEAD_DIM)
RET_QK_DIM = HEAD_DIM
RET_V_DIM = 2 * HEAD_DIM
SSD_HEAD_DIM = HEAD_DIM
SSD_HEADS = HALF_W // SSD_HEAD_DIM
SSD_GROUPS = 2
SSD_STATE = 128
SSD_CONV = 3
SSD_INNER = SSD_HEADS * SSD_HEAD_DIM
SSD_XBC = SSD_INNER + 2 * SSD_GROUPS * SSD_STATE
N_EXPERTS = 16
EC_CAPACITY = 2
EXPERT_FF = ((8 * D_MODEL // 3 + 255) // 256) * 256
N_EVEN = (DEPTH + 1) // 2
N_ODD = DEPTH // 2

ATT_SPLITS = (GQA_HEADS * HEAD_DIM, GQA_KV_HEADS * HEAD_DIM, GQA_KV_HEADS * HEAD_DIM,
              DIFF_HEADS * 2 * HEAD_DIM, DIFF_HEADS * 2 * HEAD_DIM, DIFF_HEADS * 2 * HEAD_DIM)
ATT_IN = sum(ATT_SPLITS)
REC_SPLITS = (RET_HEADS * RET_QK_DIM, RET_HEADS * RET_QK_DIM, RET_HEADS * RET_V_DIM,
              RET_HEADS * RET_V_DIM, SSD_INNER, SSD_XBC, 2 * SSD_HEADS)
REC_IN = sum(REC_SPLITS)

kernel_name = "hybrid_dit_prefix_gqa_diff_ret_ssd_ecmoe"


def _split(z, sizes):
    idx = np.cumsum(sizes)[:-1].tolist()
    return jnp.split(z, idx, axis=-1)


def _rmsnorm(x, w=None):
    xf = x.astype(jnp.float32)
    y = xf * lax.rsqrt(jnp.mean(xf * xf, axis=-1, keepdims=True) + EPS)
    if w is not None:
        y = y * w
    return y.astype(x.dtype)


def _modulate(x, w, shift, scale):
    return _rmsnorm(x, w) * (1 + scale) + shift


def _axial_rope_tables(n):
    t = jnp.arange(n)
    row = (t // GRID_W).astype(jnp.float32)
    col = (t % GRID_W).astype(jnp.float32)
    n_freq = HEAD_DIM // 4
    inv = ROPE_THETA ** (-jnp.arange(n_freq, dtype=jnp.float32) / n_freq)
    ang = jnp.concatenate([row[:, None] * inv, col[:, None] * inv], axis=-1)
    return jnp.cos(ang), jnp.sin(ang)


def _rope(x, cos, sin):
    shape = (cos.shape[0],) + (1,) * (x.ndim - 3) + (cos.shape[1],)
    c, s = cos.reshape(shape), sin.reshape(shape)
    x1, x2 = jnp.split(x, 2, axis=-1)
    return jnp.concatenate([x1 * c - x2 * s, x2 * c + x1 * s], axis=-1).astype(x.dtype)


def _attend(q, k, v, shared_k):
    scale = q.shape[-1] ** -0.5
    k_sub = 'bkhd' if shared_k else 'bkhmd'

    def one_block(qb):
        s = jnp.einsum(f'bqhmd,{k_sub}->bhmqk', qb, k, preferred_element_type=jnp.float32) * scale
        p = jax.nn.softmax(s, axis=-1).astype(v.dtype)
        return jnp.einsum('bhmqk,bkhv->bqhmv', p, v)

    B, Sq = q.shape[:2]
    nb = Sq // Q_BLOCK
    qb = jnp.moveaxis(q.reshape(B, nb, Q_BLOCK, *q.shape[2:]), 1, 0)
    out = lax.map(one_block, qb)
    return jnp.moveaxis(out, 0, 1).reshape(B, Sq, *out.shape[3:])


def _chunk_scan(q, k, v, log_a, h0):
    B, S = q.shape[:2]
    n = S // CHUNK
    tril = jnp.tril(jnp.ones((CHUNK, CHUNK), dtype=bool))

    def to_chunks(a):
        return jnp.moveaxis(a.reshape(B, n, CHUNK, *a.shape[2:]), 1, 0)

    def step(h, inp):
        qc, kc, vc, lac = inp
        cum = jnp.cumsum(lac.astype(jnp.float32), axis=1)
        cum_t = jnp.moveaxis(cum, 1, -1)
        seg = jnp.exp(jnp.where(tril, cum_t[..., :, None] - cum_t[..., None, :], -jnp.inf))
        qk = jnp.einsum('btgk,bsgk->bgts', qc, kc)
        y = (jnp.einsum('bgts,bgrts,bsgrv->btgrv', qk, seg, vc)
             + jnp.einsum('btgk,bgrkv->btgrv', qc, h) * jnp.exp(cum)[..., None])
        last = cum[:, -1]
        h_new = (h * jnp.exp(last)[..., None, None]
                 + jnp.einsum('bsgk,bsgr,bsgrv->bgrkv', kc, jnp.exp(last[:, None] - cum), vc))
        return h_new, y.astype(v.dtype)

    h, ys = lax.scan(step, h0, tuple(to_chunks(a) for a in (q, k, v, log_a)))
    return jnp.moveaxis(ys, 0, 1).reshape(B, S, *ys.shape[3:]), h


def _prefix_scan(lat, ctx, reverse):
    if reverse:
        lat = tuple(jnp.flip(a, 1) for a in lat)
        ctx = tuple(jnp.flip(a, 1) for a in ctx)
    q, v = ctx[0], ctx[2]
    h0 = jnp.zeros((q.shape[0], q.shape[2], v.shape[3], q.shape[3], v.shape[4]), jnp.float32)
    y_c, h_c = _chunk_scan(*ctx, h0)
    y_l, _ = _chunk_scan(*lat, h_c)
    if reverse:
        y_l, y_c = jnp.flip(y_l, 1), jnp.flip(y_c, 1)
    return y_l, y_c


def _bidir(lat_f, ctx_f, lat_b, ctx_b):
    yl_f, yc_f = _prefix_scan(lat_f, ctx_f, False)
    yl_b, yc_b = _prefix_scan(lat_b, ctx_b, True)
    return yl_f + yl_b, yc_f + yc_b


def _dwconv(x, w, b):
    K = w.shape[0]
    out = lax.conv_general_dilated(x, w[:, None, :].astype(x.dtype), window_strides=(1,),
                                   padding=[(K // 2, K // 2)],
                                   dimension_numbers=('NWC', 'WIO', 'NWC'),
                                   feature_group_count=x.shape[-1])
    return out + b


def _attention_mixer(hl, hc, cos, sin, w_in, w_out, q_norm_w, k_norm_w, lam, diff_norm_w,
                     lambda_init, need_ctx):
    def project(h, use_rope):
        B, n, _ = h.shape
        aq, ak, av, bq, bk, bv = _split(h @ w_in, ATT_SPLITS)
        aq = _rmsnorm(aq.reshape(B, n, GQA_HEADS, HEAD_DIM), q_norm_w)
        ak = _rmsnorm(ak.reshape(B, n, GQA_KV_HEADS, HEAD_DIM), k_norm_w)
        bq = bq.reshape(B, n, DIFF_HEADS, 2, HEAD_DIM)
        bk = bk.reshape(B, n, DIFF_HEADS, 2, HEAD_DIM)
        if use_rope:
            aq, ak, bq, bk = (_rope(t, cos, sin) for t in (aq, ak, bq, bk))
        aq = aq.reshape(B, n, GQA_KV_HEADS, GQA_HEADS // GQA_KV_HEADS, HEAD_DIM)
        av = av.reshape(B, n, GQA_KV_HEADS, HEAD_DIM)
        bv = bv.reshape(B, n, DIFF_HEADS, 2 * HEAD_DIM)
        return (aq, bq), (ak, av, bk, bv)

    lat_q, lat_kv = project(hl, True)
    ctx_q, ctx_kv = project(hc, False)
    lamf = lam.astype(jnp.float32)
    lam_val = jnp.exp(jnp.sum(lamf[0] * lamf[1])) - jnp.exp(jnp.sum(lamf[2] * lamf[3])) + lambda_init

    def mix(q_side, kv):
        aq, bq = q_side
        ak, av, bk, bv = kv
        B, n = aq.shape[:2]
        a = _attend(aq, ak, av, True).reshape(B, n, -1)
        o = _attend(bq, bk, bv, False)
        o = o[..., 0, :] - lam_val.astype(o.dtype) * o[..., 1, :]
        b = (_rmsnorm(o, diff_norm_w) * (1 - lambda_init)).reshape(B, n, -1)
        return jnp.concatenate([a, b], axis=-1) @ w_out

    kv_all = tuple(jnp.concatenate([l, c], axis=1) for l, c in zip(lat_kv, ctx_kv))
    yl = mix(lat_q, kv_all)
    yc = mix(ctx_q, ctx_kv) if need_ctx else None
    return yl, yc


def _recurrent_mixer(hl, hc, cos, sin, w_in, w_out, ret_decay_logit, conv_w, conv_b, dt_bias,
                     a_log, d_skip, ssd_norm_w, need_ctx):
    log_gamma = jax.nn.log_sigmoid(ret_decay_logit.astype(jnp.float32))
    a_neg = -jnp.exp(a_log.astype(jnp.float32))
    r_per_g = SSD_HEADS // SSD_GROUPS

    def project(h, use_rope):
        B, n, _ = h.shape
        rq, rk, rv, rg, z, xbc, dt = _split(h @ w_in, REC_SPLITS)
        rq = rq.reshape(B, n, RET_HEADS, RET_QK_DIM)
        rk = rk.reshape(B, n, RET_HEADS, RET_QK_DIM) * RET_QK_DIM ** -0.5
        if use_rope:
            rq, rk = _rope(rq, cos, sin), _rope(rk, cos, sin)
        rv = rv.reshape(B, n, RET_HEADS, 1, RET_V_DIM)
        ret = [(rq, rk, rv, jnp.broadcast_to(log_gamma[d][:, None], (B, n, RET_HEADS, 1)))
               for d in (0, 1)]
        xbc = jax.nn.silu(_dwconv(xbc, conv_w, conv_b))
        xs, bm, cm = _split(xbc, (SSD_INNER, SSD_GROUPS * SSD_STATE, SSD_GROUPS * SSD_STATE))
        xs = xs.reshape(B, n, SSD_HEADS, SSD_HEAD_DIM)
        bm = bm.reshape(B, n, SSD_GROUPS, SSD_STATE)
        cm = cm.reshape(B, n, SSD_GROUPS, SSD_STATE)
        dt = jax.nn.softplus(dt.reshape(B, n, 2, SSD_HEADS).astype(jnp.float32) + dt_bias)
        ssd = [(cm, bm,
                (xs * dt[:, :, d, :, None].astype(xs.dtype)).reshape(B, n, SSD_GROUPS, r_per_g, SSD_HEAD_DIM),
                (dt[:, :, d] * a_neg[d]).reshape(B, n, SSD_GROUPS, r_per_g))
               for d in (0, 1)]
        return ret, ssd, rg, z, xs

    lat_ret, lat_ssd, lat_rg, lat_z, lat_xs = project(hl, True)
    ctx_ret, ctx_ssd, ctx_rg, ctx_z, ctx_xs = project(hc, False)
    ret_l, ret_c = _bidir(lat_ret[0], ctx_ret[0], lat_ret[1], ctx_ret[1])
    ssd_l, ssd_c = _bidir(lat_ssd[0], ctx_ssd[0], lat_ssd[1], ctx_ssd[1])

    def combine(ret_y, ssd_y, rg, z, xs):
        B, n = rg.shape[:2]
        r = _rmsnorm(ret_y.reshape(B, n, RET_HEADS, RET_V_DIM)).reshape(B, n, -1) * jax.nn.silu(rg)
        s = ssd_y.reshape(B, n, SSD_HEADS, SSD_HEAD_DIM) + d_skip[:, None] * xs
        s = s.reshape(B, n, -1) * jax.nn.silu(z)
        s = _rmsnorm(s.reshape(B, n, SSD_GROUPS, -1)).reshape(B, n, -1) * ssd_norm_w
        return jnp.concatenate([r, s], axis=-1) @ w_out

    yl = combine(ret_l, ssd_l, lat_rg, lat_z, lat_xs)
    yc = combine(ret_c, ssd_c, ctx_rg, ctx_z, ctx_xs) if need_ctx else None
    return yl, yc


def _expert_choice_ffn(h, router_w, w_gate, w_up, w_down):
    B, n, D = h.shape
    cap = EC_CAPACITY * n // N_EXPERTS
    aff = jax.nn.softmax((h @ router_w).astype(jnp.float32), axis=-1)
    g, idx = lax.top_k(jnp.swapaxes(aff, 1, 2), cap)
    xin = jax.vmap(lambda xs, i: xs[i])(h, idx)
    a = jnp.einsum('becd,edf->becf', xin, w_gate)
    u = jnp.einsum('becd,edf->becf', xin, w_up)
    y = jnp.einsum('becf,efd->becd', jax.nn.silu(a) * u, w_down) * g[..., None].astype(h.dtype)
    return jax.vmap(lambda val, i: jax.ops.segment_sum(val.reshape(-1, D), i.reshape(-1),
                                                       num_segments=n))(y, idx)


def setup_inputs(seed: int = 0) -> dict:
    key = jax.random.key(seed)
    keys = jax.random.split(key, 32)
    f32 = jnp.float32
    D = D_MODEL

    def nrm(i, shape, scale):
        return jax.random.normal(keys[i], shape, f32) * scale

    ret_init = jnp.log(2.0 ** (5.0 + jnp.arange(RET_HEADS, dtype=f32)) - 1.0)
    dt0 = jnp.exp(jax.random.uniform(keys[17], (N_ODD, 2, SSD_HEADS), f32,
                                     math.log(1e-3), math.log(1e-1)))
    return {
        "x": nrm(0, (BATCH, SEQ, D), 1.0),
        "c": nrm(1, (BATCH, D), 1.0),
        "ctx": nrm(2, (BATCH, CTX_LEN, D), 1.0),
        "c_ctx": nrm(3, (D,), 1.0),
        "ada_w": nrm(4, (DEPTH, D, 6 * D), 0.5 * D ** -0.5),
        "ada_b": nrm(5, (DEPTH, 6 * D), 0.02),
        "norm1_w": 1.0 + nrm(6, (DEPTH, D), 0.02),
        "norm2_w": 1.0 + nrm(7, (DEPTH, D), 0.02),
        "att_w_in": nrm(8, (N_EVEN, D, ATT_IN), D ** -0.5),
        "att_w_out": nrm(9, (N_EVEN, 2 * HALF_W, D), (2 * HALF_W) ** -0.5),
        "att_q_norm_w": 1.0 + nrm(10, (N_EVEN, HEAD_DIM), 0.02),
        "att_k_norm_w": 1.0 + nrm(11, (N_EVEN, HEAD_DIM), 0.02),
        "diff_lambda": nrm(12, (N_EVEN, 4, HEAD_DIM), 0.1),
        "diff_norm_w": 1.0 + nrm(13, (N_EVEN, 2 * HEAD_DIM), 0.02),
        "rec_w_in": nrm(14, (N_ODD, D, REC_IN), D ** -0.5),
        "rec_w_out": nrm(15, (N_ODD, 2 * HALF_W, D), (2 * HALF_W) ** -0.5),
        "ret_decay_logit": ret_init + nrm(16, (N_ODD, 2, RET_HEADS), 0.05),
        "ssd_conv_w": nrm(18, (N_ODD, SSD_CONV, SSD_XBC), SSD_CONV ** -0.5),
        "ssd_conv_b": nrm(19, (N_ODD, SSD_XBC), 0.02),
        "ssd_dt_bias": dt0 + jnp.log(-jnp.expm1(-dt0)),
        "ssd_a_log": jnp.log(jax.random.uniform(keys[20], (N_ODD, 2, SSD_HEADS), f32, 1.0, 16.0)),
        "ssd_d_skip": 1.0 + nrm(21, (N_ODD, SSD_HEADS), 0.1),
        "ssd_norm_w": 1.0 + nrm(22, (N_ODD, SSD_INNER), 0.02),
        "router_w": nrm(23, (DEPTH, D, N_EXPERTS), D ** -0.5),
        "expert_w_gate": nrm(24, (DEPTH, N_EXPERTS, D, EXPERT_FF), D ** -0.5),
        "expert_w_up": nrm(25, (DEPTH, N_EXPERTS, D, EXPERT_FF), D ** -0.5),
        "expert_w_down": nrm(26, (DEPTH, N_EXPERTS, EXPERT_FF, D), EXPERT_FF ** -0.5),
        "final_norm_w": 1.0 + nrm(27, (D,), 0.02),
    }


def reference(x, c, ctx, c_ctx, ada_w, ada_b, norm1_w, norm2_w, att_w_in, att_w_out,
              att_q_norm_w, att_k_norm_w, diff_lambda, diff_norm_w, rec_w_in, rec_w_out,
              ret_decay_logit, ssd_conv_w, ssd_conv_b, ssd_dt_bias, ssd_a_log, ssd_d_skip,
              ssd_norm_w, router_w, expert_w_gate, expert_w_up, expert_w_down, final_norm_w):
    n_lat = x.shape[1]
    cos, sin = _axial_rope_tables(n_lat)
    cl, cc = jax.nn.silu(c), jax.nn.silu(c_ctx)
    xl, xc = x, ctx
    for layer in range(DEPTH):
        need_ctx = layer < DEPTH - 1
        i = layer // 2
        mod_l = jnp.split((cl @ ada_w[layer] + ada_b[layer])[:, None, :], 6, axis=-1)
        mod_c = jnp.split(cc @ ada_w[layer] + ada_b[layer], 6, axis=-1)
        hl = _modulate(xl, norm1_w[layer], mod_l[0], mod_l[1])
        hc = _modulate(xc, norm1_w[layer], mod_c[0], mod_c[1])
        if layer % 2 == 0:
            lambda_init = 0.8 - 0.6 * math.exp(-0.3 * layer)
            yl, yc = _attention_mixer(hl, hc, cos, sin, att_w_in[i], att_w_out[i], att_q_norm_w[i],
                                      att_k_norm_w[i], diff_lambda[i], diff_norm_w[i],
                                      lambda_init, need_ctx)
        else:
            yl, yc = _recurrent_mixer(hl, hc, cos, sin, rec_w_in[i], rec_w_out[i],
                                      ret_decay_logit[i], ssd_conv_w[i], ssd_conv_b[i],
                                      ssd_dt_bias[i], ssd_a_log[i], ssd_d_skip[i], ssd_norm_w[i],
                                      need_ctx)
        xl = xl + mod_l[2] * yl
        hl = _modulate(xl, norm2_w[layer], mod_l[3], mod_l[4])
        xl = xl + mod_l[5] * _expert_choice_ffn(hl, router_w[layer], expert_w_gate[layer],
                                                expert_w_up[layer], expert_w_down[layer])
        if need_ctx:
            xc = xc + mod_c[2] * yc
            hc = _modulate(xc, norm2_w[layer], mod_c[3], mod_c[4])
            xc = xc + mod_c[5] * _expert_choice_ffn(hc, router_w[layer], expert_w_gate[layer],
                                                    expert_w_up[layer], expert_w_down[layer])
    return _rmsnorm(xl, final_norm_w)
```

```python
import functools
import math

import jax
import jax.numpy as jnp
import numpy as np
from jax import lax
from jax.experimental import pallas as pl
from jax.experimental.pallas import tpu as pltpu

D_MODEL = 1024
DEPTH = 4
GRID_W = 64
HEAD_DIM = 64
ROPE_THETA = 10000.0
Q_BLOCK = 128
CHUNK = 128
EPS = 1e-6
HALF_W = D_MODEL // 2
GQA_HEADS = HALF_W // HEAD_DIM
GQA_KV_HEADS = GQA_HEADS // 4
DIFF_HEADS = HALF_W // (2 * HEAD_DIM)
RET_HEADS = HALF_W // (2 * HEAD_DIM)
RET_QK_DIM = HEAD_DIM
RET_V_DIM = 2 * HEAD_DIM
SSD_HEAD_DIM = HEAD_DIM
SSD_HEADS = HALF_W // SSD_HEAD_DIM
SSD_GROUPS = 2
SSD_STATE = 128
SSD_INNER = SSD_HEADS * SSD_HEAD_DIM
SSD_XBC = SSD_INNER + 2 * SSD_GROUPS * SSD_STATE
N_EXPERTS = 16
EC_CAPACITY = 2
EXPERT_FF = ((8 * D_MODEL // 3 + 255) // 256) * 256

ATT_SPLITS = (GQA_HEADS * HEAD_DIM, GQA_KV_HEADS * HEAD_DIM, GQA_KV_HEADS * HEAD_DIM,
              DIFF_HEADS * 2 * HEAD_DIM, DIFF_HEADS * 2 * HEAD_DIM, DIFF_HEADS * 2 * HEAD_DIM)
REC_SPLITS = (RET_HEADS * RET_QK_DIM, RET_HEADS * RET_QK_DIM, RET_HEADS * RET_V_DIM,
              RET_HEADS * RET_V_DIM, SSD_INNER, SSD_XBC, 2 * SSD_HEADS)

ROW_TILE = 256
FF_TILE = 256
VMEM_LIMIT = 48 * 1024 * 1024
BF16 = jnp.bfloat16
F32 = jnp.float32


def _cparams(sem):
    return pltpu.CompilerParams(dimension_semantics=sem, vmem_limit_bytes=VMEM_LIMIT)


def _ada_kernel(c_ref, w_ref, b_ref, o_ref):
    c = c_ref[...]
    s = c * jax.nn.sigmoid(c)
    o_ref[0] = jnp.dot(s, w_ref[0], precision=lax.Precision.HIGHEST,
                       preferred_element_type=F32) + b_ref[0]


def _ada_modulation(cvec, ada_w, ada_b):
    R, D = cvec.shape
    n_out = ada_w.shape[-1]
    tn = 1536
    return pl.pallas_call(
        _ada_kernel,
        grid=(DEPTH, n_out // tn),
        in_specs=[pl.BlockSpec((R, D), lambda l, j: (0, 0)),
                  pl.BlockSpec((1, D, tn), lambda l, j: (l, 0, j)),
                  pl.BlockSpec((1, 1, tn), lambda l, j: (l, 0, j))],
        out_specs=pl.BlockSpec((1, R, tn), lambda l, j: (l, 0, j)),
        out_shape=jax.ShapeDtypeStruct((DEPTH, R, n_out), F32),
        compiler_params=_cparams(("arbitrary", "arbitrary")),
        name="ada_modulation",
    )(cvec, ada_w, ada_b.reshape(DEPTH, 1, n_out))


def _modulated(x, nw, mod_ref, shift_idx, scale_idx):
    ms = jnp.mean(x * x, axis=-1, keepdims=True)
    y = x * lax.rsqrt(ms + EPS) * nw
    return y * (1.0 + mod_ref[0, 0, scale_idx:scale_idx + 1, :]) + mod_ref[0, 0, shift_idx:shift_idx + 1, :]


def _modproj_kernel(x_ref, nw_ref, mod_ref, w_ref, o_ref, *, shift_idx, scale_idx):
    h = _modulated(x_ref[0], nw_ref[...], mod_ref, shift_idx, scale_idx)
    o_ref[0] = jnp.dot(h.astype(BF16), w_ref[...], preferred_element_type=F32)


def _mod_project(X, nw, modc, w_bf16, shift_idx, scale_idx, n_lat_tiles):
    B, N, D = X.shape
    n_out = w_bf16.shape[1]
    return pl.pallas_call(
        functools.partial(_modproj_kernel, shift_idx=shift_idx, scale_idx=scale_idx),
        grid=(B, N // ROW_TILE),
        in_specs=[pl.BlockSpec((1, ROW_TILE, D), lambda b, i: (b, i, 0)),
                  pl.BlockSpec((1, D), lambda b, i: (0, 0)),
                  pl.BlockSpec((1, 1, 6, D), lambda b, i: (b, i // n_lat_tiles, 0, 0)),
                  pl.BlockSpec((D, n_out), lambda b, i: (0, 0))],
        out_specs=pl.BlockSpec((1, ROW_TILE, n_out), lambda b, i: (b, i, 0)),
        out_shape=jax.ShapeDtypeStruct((B, N, n_out), F32),
        compiler_params=_cparams(("arbitrary", "arbitrary")),
        name="mod_project",
    )(X, nw.reshape(1, D), modc, w_bf16)


def _outproj_kernel(x_ref, y_ref, mod_ref, w_ref, o_ref, *, gate_idx):
    y = jnp.dot(y_ref[0].astype(BF16), w_ref[...], preferred_element_type=F32)
    o_ref[0] = x_ref[0] + mod_ref[0, 0, gate_idx:gate_idx + 1, :] * y


def _out_project(X, Y, modc, w_bf16, gate_idx, n_lat_tiles):
    B, N, D = X.shape
    K = Y.shape[-1]
    return pl.pallas_call(
        functools.partial(_outproj_kernel, gate_idx=gate_idx),
        grid=(B, N // ROW_TILE),
        in_specs=[pl.BlockSpec((1, ROW_TILE, D), lambda b, i: (b, i, 0)),
                  pl.BlockSpec((1, ROW_TILE, K), lambda b, i: (b, i, 0)),
                  pl.BlockSpec((1, 1, 6, D), lambda b, i: (b, i // n_lat_tiles, 0, 0)),
                  pl.BlockSpec((K, D), lambda b, i: (0, 0))],
        out_specs=pl.BlockSpec((1, ROW_TILE, D), lambda b, i: (b, i, 0)),
        out_shape=jax.ShapeDtypeStruct((B, N, D), F32),
        compiler_params=_cparams(("arbitrary", "arbitrary")),
        name="out_project",
    )(X, Y, modc, w_bf16)


def _router_kernel(x_ref, nw_ref, mod_ref, rw_ref, h_ref, aff_ref):
    h = _modulated(x_ref[0], nw_ref[...], mod_ref, 3, 4)
    h_ref[0] = h
    logits = jnp.dot(h, rw_ref[...], precision=lax.Precision.HIGHEST, preferred_element_type=F32)
    m = jnp.max(logits, axis=-1, keepdims=True)
    e = jnp.exp(logits - m)
    aff_ref[0] = e / jnp.sum(e, axis=-1, keepdims=True)


def _mod_router(X, nw, modc, router_w, n_lat_tiles):
    B, N, D = X.shape
    E = router_w.shape[1]
    return pl.pallas_call(
        _router_kernel,
        grid=(B, N // ROW_TILE),
        in_specs=[pl.BlockSpec((1, ROW_TILE, D), lambda b, i: (b, i, 0)),
                  pl.BlockSpec((1, D), lambda b, i: (0, 0)),
                  pl.BlockSpec((1, 1, 6, D), lambda b, i: (b, i // n_lat_tiles, 0, 0)),
                  pl.BlockSpec((D, E), lambda b, i: (0, 0))],
        out_specs=[pl.BlockSpec((1, ROW_TILE, D), lambda b, i: (b, i, 0)),
                   pl.BlockSpec((1, ROW_TILE, E), lambda b, i: (b, i, 0))],
        out_shape=[jax.ShapeDtypeStruct((B, N, D), F32),
                   jax.ShapeDtypeStruct((B, N, E), F32)],
        compiler_params=_cparams(("arbitrary", "arbitrary")),
        name="mod_router",
    )(X, nw.reshape(1, D), modc, router_w)


def _ffn_kernel(x_ref, wg_ref, wu_ref, wd_ref, o_ref, acc_ref, *, row_chunk):
    f = pl.program_id(2)
    n_f = pl.num_programs(2)
    wg = wg_ref[0, 0].astype(BF16)
    wu = wu_ref[0, 0].astype(BF16)
    wd = wd_ref[0, 0].astype(BF16)
    n_rows = x_ref.shape[1]

    def rows_body(r, carry):
        rows = pl.ds(pl.multiple_of(r * row_chunk, row_chunk), row_chunk)
        xr = x_ref[0, rows, :]
        a = jnp.dot(xr, wg, preferred_element_type=F32)
        u = jnp.dot(xr, wu, preferred_element_type=F32)
        hm = (a * jax.nn.sigmoid(a) * u).astype(BF16)
        y = jnp.dot(hm, wd, preferred_element_type=F32)

        @pl.when(f == 0)
        def _():
            acc_ref[rows, :] = y

        @pl.when(f > 0)
        def _():
            acc_ref[rows, :] += y

        return carry

    lax.fori_loop(0, n_rows // row_chunk, rows_body, 0)

    @pl.when(f == n_f - 1)
    def _():
        o_ref[0] = acc_ref[...].astype(o_ref.dtype)


def _expert_ffn(Xe, w_gate, w_up, w_down, layer):
    E, T, D = Xe.shape
    F = w_gate.shape[-1]
    m_tiles = 2
    tm = T // m_tiles
    return pl.pallas_call(
        functools.partial(_ffn_kernel, row_chunk=256),
        grid=(E, m_tiles, F // FF_TILE),
        in_specs=[pl.BlockSpec((1, tm, D), lambda e, m, f: (e, m, 0)),
                  pl.BlockSpec((1, 1, D, FF_TILE), lambda e, m, f: (layer, e, 0, f)),
                  pl.BlockSpec((1, 1, D, FF_TILE), lambda e, m, f: (layer, e, 0, f)),
                  pl.BlockSpec((1, 1, FF_TILE, D), lambda e, m, f: (layer, e, f, 0))],
        out_specs=pl.BlockSpec((1, tm, D), lambda e, m, f: (e, m, 0)),
        out_shape=jax.ShapeDtypeStruct((E, T, D), BF16),
        scratch_shapes=[pltpu.VMEM((tm, D), F32)],
        compiler_params=_cparams(("arbitrary", "arbitrary", "arbitrary")),
        name="expert_ffn",
    )(Xe, w_gate, w_up, w_down)


def _split(z, sizes):
    idx = np.cumsum(sizes)[:-1].tolist()
    return jnp.split(z, idx, axis=-1)


def _rmsnorm(x, w=None):
    y = x * lax.rsqrt(jnp.mean(x * x, axis=-1, keepdims=True) + EPS)
    if w is not None:
        y = y * w
    return y


def _rope_tables(n):
    t = jnp.arange(n)
    row = (t // GRID_W).astype(F32)
    col = (t % GRID_W).astype(F32)
    n_freq = HEAD_DIM // 4
    inv = ROPE_THETA ** (-jnp.arange(n_freq, dtype=F32) / n_freq)
    ang = jnp.concatenate([row[:, None] * inv, col[:, None] * inv], axis=-1)
    return jnp.cos(ang), jnp.sin(ang)


def _rope(x, cos, sin):
    shape = (cos.shape[0],) + (1,) * (x.ndim - 3) + (cos.shape[1],)
    c, s = cos.reshape(shape), sin.reshape(shape)
    x1, x2 = jnp.split(x, 2, axis=-1)
    return jnp.concatenate([x1 * c - x2 * s, x2 * c + x1 * s], axis=-1)


def _attend(q, k, v, shared_k):
    scale = q.shape[-1] ** -0.5
    k_sub = 'bkhd' if shared_k else 'bkhmd'

    def one_block(qb):
        s = jnp.einsum(f'bqhmd,{k_sub}->bhmqk', qb, k, preferred_element_type=F32) * scale
        p = jax.nn.softmax(s, axis=-1)
        return jnp.einsum('bhmqk,bkhv->bqhmv', p, v)

    B, Sq = q.shape[:2]
    nb = Sq // Q_BLOCK
    qb = jnp.moveaxis(q.reshape(B, nb, Q_BLOCK, *q.shape[2:]), 1, 0)
    out = lax.map(one_block, qb)
    return jnp.moveaxis(out, 0, 1).reshape(B, Sq, *out.shape[3:])


def _attention_core(zl, zc, cos, sin, q_norm_w, k_norm_w, lam, diff_norm_w, lambda_init, need_ctx):
    def project(z, use_rope):
        B, n, _ = z.shape
        aq, ak, av, bq, bk, bv = _split(z, ATT_SPLITS)
        aq = _rmsnorm(aq.reshape(B, n, GQA_HEADS, HEAD_DIM), q_norm_w)
        ak = _rmsnorm(ak.reshape(B, n, GQA_KV_HEADS, HEAD_DIM), k_norm_w)
        bq = bq.reshape(B, n, DIFF_HEADS, 2, HEAD_DIM)
        bk = bk.reshape(B, n, DIFF_HEADS, 2, HEAD_DIM)
        if use_rope:
            aq, ak, bq, bk = (_rope(t, cos, sin) for t in (aq, ak, bq, bk))
        aq = aq.reshape(B, n, GQA_KV_HEADS, GQA_HEADS // GQA_KV_HEADS, HEAD_DIM)
        av = av.reshape(B, n, GQA_KV_HEADS, HEAD_DIM)
        bv = bv.reshape(B, n, DIFF_HEADS, 2 * HEAD_DIM)
        return (aq, bq), (ak, av, bk, bv)

    lat_q, lat_kv = project(zl, True)
    ctx_q, ctx_kv = project(zc, False)
    lam_val = jnp.exp(jnp.sum(lam[0] * lam[1])) - jnp.exp(jnp.sum(lam[2] * lam[3])) + lambda_init

    def mix(q_side, kv):
        aq, bq = q_side
        ak, av, bk, bv = kv
        B, n = aq.shape[:2]
        a = _attend(aq, ak, av, True).reshape(B, n, -1)
        o = _attend(bq, bk, bv, False)
        o = o[..., 0, :] - lam_val * o[..., 1, :]
        b = (_rmsnorm(o, diff_norm_w) * (1 - lambda_init)).reshape(B, n, -1)
        return jnp.concatenate([a, b], axis=-1)

    kv_all = tuple(jnp.concatenate([l, c], axis=1) for l, c in zip(lat_kv, ctx_kv))
    yl = mix(lat_q, kv_all)
    yc = mix(ctx_q, ctx_kv) if need_ctx else jnp.zeros((zc.shape[0], zc.shape[1], 2 * HALF_W), F32)
    return yl, yc


def _chunk_scan(q, k, v, log_a, h0):
    B, S = q.shape[:2]
    n = S // CHUNK
    tril = jnp.tril(jnp.ones((CHUNK, CHUNK), dtype=bool))

    def to_chunks(a):
        return jnp.moveaxis(a.reshape(B, n, CHUNK, *a.shape[2:]), 1, 0)

    def step(h, inp):
        qc, kc, vc, lac = inp
        cum = jnp.cumsum(lac.astype(F32), axis=1)
        cum_t = jnp.moveaxis(cum, 1, -1)
        seg = jnp.exp(jnp.where(tril, cum_t[..., :, None] - cum_t[..., None, :], -jnp.inf))
        qk = jnp.einsum('btgk,bsgk->bgts', qc, kc)
        y = (jnp.einsum('bgts,bgrts,bsgrv->btgrv', qk, seg, vc)
             + jnp.einsum('btgk,bgrkv->btgrv', qc, h) * jnp.exp(cum)[..., None])
        last = cum[:, -1]
        h_new = (h * jnp.exp(last)[..., None, None]
                 + jnp.einsum('bsgk,bsgr,bsgrv->bgrkv', kc, jnp.exp(last[:, None] - cum), vc))
        return h_new, y

    h, ys = lax.scan(step, h0, tuple(to_chunks(a) for a in (q, k, v, log_a)))
    return jnp.moveaxis(ys, 0, 1).reshape(B, S, *ys.shape[3:]), h


def _prefix_scan(lat, ctx, reverse):
    if reverse:
        lat = tuple(jnp.flip(a, 1) for a in lat)
        ctx = tuple(jnp.flip(a, 1) for a in ctx)
    q, v = ctx[0], ctx[2]
    h0 = jnp.zeros((q.shape[0], q.shape[2], v.shape[3], q.shape[3], v.shape[4]), F32)
    y_c, h_c = _chunk_scan(*ctx, h0)
    y_l, _ = _chunk_scan(*lat, h_c)
    if reverse:
        y_l, y_c = jnp.flip(y_l, 1), jnp.flip(y_c, 1)
    return y_l, y_c


def _bidir(lat_f, ctx_f, lat_b, ctx_b):
    yl_f, yc_f = _prefix_scan(lat_f, ctx_f, False)
    yl_b, yc_b = _prefix_scan(lat_b, ctx_b, True)
    return yl_f + yl_b, yc_f + yc_b


def _dwconv(x, w, b):
    K = w.shape[0]
    out = lax.conv_general_dilated(x, w[:, None, :], window_strides=(1,),
                                   padding=[(K // 2, K // 2)],
                                   dimension_numbers=('NWC', 'WIO', 'NWC'),
                                   feature_group_count=x.shape[-1])
    return out + b


def _recurrent_core(zl, zc, cos, sin, ret_decay_logit, conv_w, conv_b, dt_bias, a_log, d_skip,
                    ssd_norm_w):
    log_gamma = jax.nn.log_sigmoid(ret_decay_logit)
    a_neg = -jnp.exp(a_log)
    r_per_g = SSD_HEADS // SSD_GROUPS

    def project(z, use_rope):
        B, n, _ = z.shape
        rq, rk, rv, rg, zz, xbc, dt = _split(z, REC_SPLITS)
        rq = rq.reshape(B, n, RET_HEADS, RET_QK_DIM)
        rk = rk.reshape(B, n, RET_HEADS, RET_QK_DIM) * RET_QK_DIM ** -0.5
        if use_rope:
            rq, rk = _rope(rq, cos, sin), _rope(rk, cos, sin)
        rv = rv.reshape(B, n, RET_HEADS, 1, RET_V_DIM)
        ret = [(rq, rk, rv, jnp.broadcast_to(log_gamma[d][:, None], (B, n, RET_HEADS, 1)))
               for d in (0, 1)]
        xbc = jax.nn.silu(_dwconv(xbc, conv_w, conv_b))
        xs, bm, cm = _split(xbc, (SSD_INNER, SSD_GROUPS * SSD_STATE, SSD_GROUPS * SSD_STATE))
        xs = xs.reshape(B, n, SSD_HEADS, SSD_HEAD_DIM)
        bm = bm.reshape(B, n, SSD_GROUPS, SSD_STATE)
        cm = cm.reshape(B, n, SSD_GROUPS, SSD_STATE)
        dt = jax.nn.softplus(dt.reshape(B, n, 2, SSD_HEADS) + dt_bias)
        ssd = [(cm, bm,
                (xs * dt[:, :, d, :, None]).reshape(B, n, SSD_GROUPS, r_per_g, SSD_HEAD_DIM),
                (dt[:, :, d] * a_neg[d]).reshape(B, n, SSD_GROUPS, r_per_g))
               for d in (0, 1)]
        return ret, ssd, rg, zz, xs

    lat_ret, lat_ssd, lat_rg, lat_z, lat_xs = project(zl, True)
    ctx_ret, ctx_ssd, ctx_rg, ctx_z, ctx_xs = project(zc, False)
    ret_l, ret_c = _bidir(lat_ret[0], ctx_ret[0], lat_ret[1], ctx_ret[1])
    ssd_l, ssd_c = _bidir(lat_ssd[0], ctx_ssd[0], lat_ssd[1], ctx_ssd[1])

    def combine(ret_y, ssd_y, rg, zz, xs):
        B, n = rg.shape[:2]
        r = _rmsnorm(ret_y.reshape(B, n, RET_HEADS, RET_V_DIM)).reshape(B, n, -1) * jax.nn.silu(rg)
        s = ssd_y.reshape(B, n, SSD_HEADS, SSD_HEAD_DIM) + d_skip[:, None] * xs
        s = s.reshape(B, n, -1) * jax.nn.silu(zz)
        s = _rmsnorm(s.reshape(B, n, SSD_GROUPS, -1)).reshape(B, n, -1) * ssd_norm_w
        return jnp.concatenate([r, s], axis=-1)

    return combine(ret_l, ssd_l, lat_rg, lat_z, lat_xs), combine(ret_c, ssd_c, ctx_rg, ctx_z, ctx_xs)


def _dispatch_tokens(h, aff):
    B, n, D = h.shape
    cap = EC_CAPACITY * n // N_EXPERTS
    g, idx = lax.top_k(jnp.swapaxes(aff, 1, 2), cap)
    xin = jax.vmap(lambda xs, i: xs[i])(h, idx)
    return xin, g, idx


def _combine_tokens(y, g, idx, n):
    D = y.shape[-1]
    y = y * g[..., None]
    return jax.vmap(lambda val, i: jax.ops.segment_sum(val.reshape(-1, D), i.reshape(-1),
                                                       num_segments=n))(y, idx)


def kernel(x, c, ctx, c_ctx, ada_w, ada_b, norm1_w, norm2_w, att_w_in, att_w_out, att_q_norm_w,
           att_k_norm_w, diff_lambda, diff_norm_w, rec_w_in, rec_w_out, ret_decay_logit, ssd_conv_w,
           ssd_conv_b, ssd_dt_bias, ssd_a_log, ssd_d_skip, ssd_norm_w, router_w, expert_w_gate,
           expert_w_up, expert_w_down, final_norm_w):
    B, n_lat, D = x.shape
    n_ctx = ctx.shape[1]
    n_lat_tiles = n_lat // ROW_TILE
    cos, sin = _rope_tables(n_lat)

    n_cond = 24
    cvec = jnp.concatenate([c, c_ctx[None, :], jnp.zeros((n_cond - B - 1, D), F32)], axis=0)
    mods = _ada_modulation(cvec, ada_w, ada_b)

    X = jnp.concatenate([x, ctx], axis=1)
    for layer in range(DEPTH):
        i = layer // 2
        need_ctx = layer < DEPTH - 1
        mod_lat = mods[layer, :B].reshape(B, 1, 6, D)
        mod_ctx = jnp.broadcast_to(mods[layer, B].reshape(1, 1, 6, D), (B, 1, 6, D))
        modc = jnp.concatenate([mod_lat, mod_ctx], axis=1)

        if layer % 2 == 0:
            lambda_init = 0.8 - 0.6 * math.exp(-0.3 * layer)
            z = _mod_project(X, norm1_w[layer], modc, att_w_in[i].astype(BF16), 0, 1, n_lat_tiles)
            yl, yc = _attention_core(z[:, :n_lat], z[:, n_lat:], cos, sin, att_q_norm_w[i],
                                     att_k_norm_w[i], diff_lambda[i], diff_norm_w[i], lambda_init,
                                     need_ctx)
            w_out = att_w_out[i]
        else:
            z = _mod_project(X, norm1_w[layer], modc, rec_w_in[i].astype(BF16), 0, 1, n_lat_tiles)
            yl, yc = _recurrent_core(z[:, :n_lat], z[:, n_lat:], cos, sin, ret_decay_logit[i],
                                     ssd_conv_w[i], ssd_conv_b[i], ssd_dt_bias[i], ssd_a_log[i],
                                     ssd_d_skip[i], ssd_norm_w[i])
            w_out = rec_w_out[i]
        Y = jnp.concatenate([yl, yc], axis=1)
        X = _out_project(X, Y, modc, w_out.astype(BF16), 2, n_lat_tiles)

        h2, aff = _mod_router(X, norm2_w[layer], modc, router_w[layer], n_lat_tiles)
        xin_l, g_l, idx_l = _dispatch_tokens(h2[:, :n_lat], aff[:, :n_lat])
        xin_c, g_c, idx_c = _dispatch_tokens(h2[:, n_lat:], aff[:, n_lat:])
        cap_l, cap_c = xin_l.shape[2], xin_c.shape[2]
        Xe = jnp.concatenate([jnp.swapaxes(xin_l, 0, 1).reshape(N_EXPERTS, B * cap_l, D),
                              jnp.swapaxes(xin_c, 0, 1).reshape(N_EXPERTS, B * cap_c, D)], axis=1)
        Ye = _expert_ffn(Xe.astype(BF16), expert_w_gate, expert_w_up, expert_w_down, layer).astype(F32)
        y_l = jnp.swapaxes(Ye[:, :B * cap_l].reshape(N_EXPERTS, B, cap_l, D), 0, 1)
        y_c = jnp.swapaxes(Ye[:, B * cap_l:].reshape(N_EXPERTS, B, cap_c, D), 0, 1)
        m_l = _combine_tokens(y_l, g_l, idx_l, n_lat)
        m_c = _combine_tokens(y_c, g_c, idx_c, n_ctx)
        M = jnp.concatenate([m_l, m_c], axis=1)
        gate = jnp.concatenate([jnp.broadcast_to(modc[:, 0:1, 5], (B, n_lat, D)),
                                jnp.broadcast_to(modc[:, 1:2, 5], (B, n_ctx, D))], axis=1)
        X = X + gate * M

    xl = X[:, :n_lat]
    return _rmsnorm(xl, final_norm_w)
```

```python
import functools
import math

import jax
import jax.numpy as jnp
import numpy as np
from jax import lax
from jax.experimental import pallas as pl
from jax.experimental.pallas import tpu as pltpu

D_MODEL = 1024
DEPTH = 4
GRID_W = 64
HEAD_DIM = 64
ROPE_THETA = 10000.0
Q_BLOCK = 128
CHUNK = 128
EPS = 1e-6
HALF_W = D_MODEL // 2
GQA_HEADS = HALF_W // HEAD_DIM
GQA_KV_HEADS = GQA_HEADS // 4
DIFF_HEADS = HALF_W // (2 * HEAD_DIM)
RET_HEADS = HALF_W // (2 * HEAD_DIM)
RET_QK_DIM = HEAD_DIM
RET_V_DIM = 2 * HEAD_DIM
SSD_HEAD_DIM = HEAD_DIM
SSD_HEADS = HALF_W // SSD_HEAD_DIM
SSD_GROUPS = 2
SSD_STATE = 128
SSD_INNER = SSD_HEADS * SSD_HEAD_DIM
SSD_XBC = SSD_INNER + 2 * SSD_GROUPS * SSD_STATE
N_EXPERTS = 16
EC_CAPACITY = 2
EXPERT_FF = ((8 * D_MODEL // 3 + 255) // 256) * 256

ATT_SPLITS = (GQA_HEADS * HEAD_DIM, GQA_KV_HEADS * HEAD_DIM, GQA_KV_HEADS * HEAD_DIM,
              DIFF_HEADS * 2 * HEAD_DIM, DIFF_HEADS * 2 * HEAD_DIM, DIFF_HEADS * 2 * HEAD_DIM)
REC_SPLITS = (RET_HEADS * RET_QK_DIM, RET_HEADS * RET_QK_DIM, RET_HEADS * RET_V_DIM,
              RET_HEADS * RET_V_DIM, SSD_INNER, SSD_XBC, 2 * SSD_HEADS)

ROW_TILE = 256
FF_TILE = 256
VMEM_LIMIT = 56 * 1024 * 1024
BF16 = jnp.bfloat16
F32 = jnp.float32


def _cparams(sem):
    return pltpu.CompilerParams(dimension_semantics=sem, vmem_limit_bytes=VMEM_LIMIT)


def _ada_kernel(c_ref, w_ref, b_ref, o_ref):
    c = c_ref[...]
    s = c * jax.nn.sigmoid(c)
    o_ref[0] = jnp.dot(s, w_ref[0], precision=lax.Precision.HIGHEST,
                       preferred_element_type=F32) + b_ref[0]


def _ada_modulation(cvec, ada_w, ada_b):
    R, D = cvec.shape
    n_out = ada_w.shape[-1]
    tn = 1536
    return pl.pallas_call(
        _ada_kernel,
        grid=(DEPTH, n_out // tn),
        in_specs=[pl.BlockSpec((R, D), lambda l, j: (0, 0)),
                  pl.BlockSpec((1, D, tn), lambda l, j: (l, 0, j)),
                  pl.BlockSpec((1, 1, tn), lambda l, j: (l, 0, j))],
        out_specs=pl.BlockSpec((1, R, tn), lambda l, j: (l, 0, j)),
        out_shape=jax.ShapeDtypeStruct((DEPTH, R, n_out), F32),
        compiler_params=_cparams(("arbitrary", "arbitrary")),
        name="ada_modulation",
    )(cvec, ada_w, ada_b.reshape(DEPTH, 1, n_out))


def _modulated(x, nw, mod_ref, shift_idx, scale_idx):
    ms = jnp.mean(x * x, axis=-1, keepdims=True)
    y = x * lax.rsqrt(ms + EPS) * nw
    return y * (1.0 + mod_ref[0, 0, scale_idx:scale_idx + 1, :]) + mod_ref[0, 0, shift_idx:shift_idx + 1, :]


def _modproj_kernel(x_ref, nw_ref, mod_ref, w_ref, o_ref, *, shift_idx, scale_idx):
    h = _modulated(x_ref[0], nw_ref[...], mod_ref, shift_idx, scale_idx)
    o_ref[0] = jnp.dot(h.astype(BF16), w_ref[...], preferred_element_type=F32)


def _mod_project(X, nw, modc, w_bf16, shift_idx, scale_idx, n_lat_tiles):
    B, N, D = X.shape
    n_out = w_bf16.shape[1]
    return pl.pallas_call(
        functools.partial(_modproj_kernel, shift_idx=shift_idx, scale_idx=scale_idx),
        grid=(B, N // ROW_TILE),
        in_specs=[pl.BlockSpec((1, ROW_TILE, D), lambda b, i: (b, i, 0)),
                  pl.BlockSpec((1, D), lambda b, i: (0, 0)),
                  pl.BlockSpec((1, 1, 6, D), lambda b, i: (b, i // n_lat_tiles, 0, 0)),
                  pl.BlockSpec((D, n_out), lambda b, i: (0, 0))],
        out_specs=pl.BlockSpec((1, ROW_TILE, n_out), lambda b, i: (b, i, 0)),
        out_shape=jax.ShapeDtypeStruct((B, N, n_out), F32),
        compiler_params=_cparams(("arbitrary", "arbitrary")),
        name="mod_project",
    )(X, nw.reshape(1, D), modc, w_bf16)


def _outproj_kernel(x_ref, y_ref, mod_ref, w_ref, o_ref, *, gate_idx):
    y = jnp.dot(y_ref[0].astype(BF16), w_ref[...], preferred_element_type=F32)
    o_ref[0] = x_ref[0] + mod_ref[0, 0, gate_idx:gate_idx + 1, :] * y


def _out_project(X, Y, modc, w_bf16, gate_idx, n_lat_tiles):
    B, N, D = X.shape
    K = Y.shape[-1]
    return pl.pallas_call(
        functools.partial(_outproj_kernel, gate_idx=gate_idx),
        grid=(B, N // ROW_TILE),
        in_specs=[pl.BlockSpec((1, ROW_TILE, D), lambda b, i: (b, i, 0)),
                  pl.BlockSpec((1, ROW_TILE, K), lambda b, i: (b, i, 0)),
                  pl.BlockSpec((1, 1, 6, D), lambda b, i: (b, i // n_lat_tiles, 0, 0)),
                  pl.BlockSpec((K, D), lambda b, i: (0, 0))],
        out_specs=pl.BlockSpec((1, ROW_TILE, D), lambda b, i: (b, i, 0)),
        out_shape=jax.ShapeDtypeStruct((B, N, D), F32),
        compiler_params=_cparams(("arbitrary", "arbitrary")),
        name="out_project",
    )(X, Y, modc, w_bf16)


PAIR = 2 * HEAD_DIM
N_PAIR_GROUPS = 2 * HALF_W // PAIR
ATT_KV_W = GQA_KV_HEADS * PAIR + DIFF_HEADS * PAIR


def _head_sumsq(x, g):
    x2 = x * x
    hi = x2.astype(BF16)
    lo = (x2 - hi.astype(F32)).astype(BF16)
    return jnp.dot(hi, g, preferred_element_type=F32) + jnp.dot(lo, g, preferred_element_type=F32)


def _rope_lanes(x, cos, sin_signed):
    W = x.shape[-1]
    half = HEAD_DIM // 2
    lane = lax.broadcasted_iota(jnp.int32, x.shape, 1) & (HEAD_DIM - 1)
    partner = jnp.where(lane < half, pltpu.roll(x, W - half, 1), pltpu.roll(x, half, 1))
    return x * cos + partner * sin_signed


def _att_project_kernel(x_ref, nw_ref, mod_ref, w_ref, cos_ref, sin_ref, g_ref, qw_ref, kw_ref,
                        q_ref, k_ref, v_ref):
    h = _modulated(x_ref[0], nw_ref[...], mod_ref, 0, 1)
    z = jnp.dot(h.astype(BF16), w_ref[...], preferred_element_type=F32)
    o_aq, o_ak, o_av, o_bq, o_bk, o_bv = np.cumsum((0,) + ATT_SPLITS[:-1]).tolist()
    n_aq, n_ak, n_bq = ATT_SPLITS[0], ATT_SPLITS[1], ATT_SPLITS[3]
    cos, sin = cos_ref[...], sin_ref[...]
    cos4 = jnp.concatenate([cos] * (n_aq // PAIR), axis=1)
    sin4 = jnp.concatenate([sin] * (n_aq // PAIR), axis=1)
    inv_d = 1.0 / HEAD_DIM
    q_scale = HEAD_DIM ** -0.5

    aq = z[:, o_aq:o_aq + n_aq]
    aq = aq * lax.rsqrt(_head_sumsq(aq, g_ref[...]) * inv_d + EPS) * qw_ref[...]
    aq = _rope_lanes(aq, cos4, sin4) * q_scale
    bq = _rope_lanes(z[:, o_bq:o_bq + n_bq], cos4, sin4) * q_scale
    q = jnp.concatenate([aq, bq], axis=1)
    lo = (lax.broadcasted_iota(jnp.int32, q.shape, 1) & (PAIR - 1)) < HEAD_DIM
    q_ref[0, :, 0:q.shape[1]] = jnp.where(lo, q, 0.0).astype(BF16)
    q_ref[0, :, q.shape[1]:2 * q.shape[1]] = jnp.where(lo, 0.0, q).astype(BF16)

    ak = z[:, o_ak:o_ak + n_ak]
    ak = ak * lax.rsqrt(_head_sumsq(ak, g_ref[0:n_ak, 0:n_ak]) * inv_d + EPS) * kw_ref[...]
    ak = _rope_lanes(ak, cos, sin)
    av = z[:, o_av:o_av + n_ak]
    lo_kv = lax.broadcasted_iota(jnp.int32, ak.shape, 1) < HEAD_DIM
    ak_sw, av_sw = pltpu.roll(ak, HEAD_DIM, 1), pltpu.roll(av, HEAD_DIM, 1)
    k_ref[0, :, 0:PAIR] = jnp.where(lo_kv, ak, ak_sw).astype(BF16)
    k_ref[0, :, PAIR:2 * PAIR] = jnp.where(lo_kv, ak_sw, ak).astype(BF16)
    v_ref[0, :, 0:PAIR] = jnp.where(lo_kv, av, av_sw).astype(BF16)
    v_ref[0, :, PAIR:2 * PAIR] = jnp.where(lo_kv, av_sw, av).astype(BF16)
    k_ref[0, :, 2 * PAIR:] = _rope_lanes(z[:, o_bk:o_bk + n_bq], cos4, sin4).astype(BF16)
    v_ref[0, :, 2 * PAIR:] = z[:, o_bv:o_bv + n_bq].astype(BF16)


def _att_project(X, nw, modc, w_bf16, cos_t, sin_t, gmat, qw_t, kw_t, n_lat_tiles):
    B, N, D = X.shape
    n_in = w_bf16.shape[1]
    q_w = 2 * N_PAIR_GROUPS * PAIR
    row = lambda b, i: (b, i, 0)
    const2 = lambda b, i: (0, 0)
    return pl.pallas_call(
        _att_project_kernel,
        grid=(B, N // ROW_TILE),
        in_specs=[pl.BlockSpec((1, ROW_TILE, D), row),
                  pl.BlockSpec((1, D), const2),
                  pl.BlockSpec((1, 1, 6, D), lambda b, i: (b, i // n_lat_tiles, 0, 0)),
                  pl.BlockSpec((D, n_in), const2),
                  pl.BlockSpec((ROW_TILE, PAIR), lambda b, i: (i, 0)),
                  pl.BlockSpec((ROW_TILE, PAIR), lambda b, i: (i, 0)),
                  pl.BlockSpec(gmat.shape, const2),
                  pl.BlockSpec(qw_t.shape, const2),
                  pl.BlockSpec(kw_t.shape, const2)],
        out_specs=[pl.BlockSpec((1, ROW_TILE, q_w), row),
                   pl.BlockSpec((1, ROW_TILE, ATT_KV_W), row),
                   pl.BlockSpec((1, ROW_TILE, ATT_KV_W), row)],
        out_shape=[jax.ShapeDtypeStruct((B, N, q_w), BF16),
                   jax.ShapeDtypeStruct((B, N, ATT_KV_W), BF16),
                   jax.ShapeDtypeStruct((B, N, ATT_KV_W), BF16)],
        compiler_params=_cparams(("arbitrary", "arbitrary")),
        name="att_project",
    )(X, nw.reshape(1, D), modc, w_bf16, cos_t, sin_t, gmat, qw_t, kw_t)


def _attention_kernel(x_ref, q_ref, k_ref, v_ref, mod_ref, lam_ref, dnw_ref, w_ref, o_ref,
                      s_scr, p_scr, y_scr, *, lambda_init, n_lat_tiles):
    n_lat = n_lat_tiles * ROW_TILE
    n_all = k_ref.shape[1]
    q_half = N_PAIR_GROUPS * PAIR
    lam = lam_ref[...]
    lam_val = (jnp.exp(jnp.sum(lam[0:1] * lam[1:2], axis=-1, keepdims=True))
               - jnp.exp(jnp.sum(lam[2:3] * lam[3:4], axis=-1, keepdims=True)) + lambda_init)

    def attend(k0, nk):
        keys = slice(k0, k0 + nk)

        def unit(pg, half):
            qv = q_ref[0, :, half * q_half + pg * PAIR: half * q_half + (pg + 1) * PAIR]
            kd = k_ref[0, keys, _kv_lane(pg): _kv_lane(pg) + PAIR]
            vd = v_ref[0, keys, _kv_lane(pg): _kv_lane(pg) + PAIR]
            s_scr[:, 0:nk] = lax.dot_general(qv, kd, (((1,), (1,)), ((), ())),
                                             preferred_element_type=F32)
            s = s_scr[:, 0:nk]
            p = jnp.exp(s - jnp.max(s, axis=-1, keepdims=True))
            p_scr[:, 0:nk] = p.astype(BF16)
            o = jnp.dot(p_scr[:, 0:nk], vd, preferred_element_type=F32)
            return o * (1.0 / jnp.sum(p, axis=-1, keepdims=True))

        lo = lax.broadcasted_iota(jnp.int32, (ROW_TILE, PAIR), 1) < HEAD_DIM
        for pg in range(2 * GQA_KV_HEADS):
            y_scr[:, pg * PAIR:(pg + 1) * PAIR] = jnp.where(lo, unit(pg, 0), unit(pg, 1)).astype(BF16)
        for pg in range(2 * GQA_KV_HEADS, N_PAIR_GROUPS):
            od = unit(pg, 0) - lam_val * unit(pg, 1)
            od = od * lax.rsqrt(jnp.mean(od * od, axis=-1, keepdims=True) + EPS)
            od = od * dnw_ref[...] * (1.0 - lambda_init)
            y_scr[:, pg * PAIR:(pg + 1) * PAIR] = od.astype(BF16)
        y = jnp.dot(y_scr[...], w_ref[...], preferred_element_type=F32)
        o_ref[0] = x_ref[0] + mod_ref[0, 0, 2:3, :] * y

    i = pl.program_id(1)

    @pl.when(i < n_lat_tiles)
    def _():
        attend(0, n_all)

    @pl.when(i >= n_lat_tiles)
    def _():
        attend(n_lat, n_all - n_lat)


def _kv_lane(pg):
    n_gqa_pairs = 2 * GQA_KV_HEADS
    if pg < n_gqa_pairs:
        return (pg // 2) * PAIR
    return (GQA_KV_HEADS + pg - n_gqa_pairs) * PAIR


def _attention(X, Q, K, V, modc, lam, dnw, w_bf16, lambda_init, n_lat_tiles):
    B, N, D = X.shape
    row = lambda b, i: (b, i, 0)
    whole = lambda b, i: (b, 0, 0)
    const2 = lambda b, i: (0, 0)
    return pl.pallas_call(
        functools.partial(_attention_kernel, lambda_init=lambda_init, n_lat_tiles=n_lat_tiles),
        grid=(B, N // ROW_TILE),
        in_specs=[pl.BlockSpec((1, ROW_TILE, D), row),
                  pl.BlockSpec((1, ROW_TILE, Q.shape[-1]), row),
                  pl.BlockSpec((1, N, K.shape[-1]), whole),
                  pl.BlockSpec((1, N, V.shape[-1]), whole),
                  pl.BlockSpec((1, 1, 6, D), lambda b, i: (b, i // n_lat_tiles, 0, 0)),
                  pl.BlockSpec(lam.shape, const2),
                  pl.BlockSpec((1, PAIR), const2),
                  pl.BlockSpec(w_bf16.shape, const2)],
        out_specs=pl.BlockSpec((1, ROW_TILE, D), row),
        out_shape=jax.ShapeDtypeStruct((B, N, D), F32),
        scratch_shapes=[pltpu.VMEM((ROW_TILE, N), F32), pltpu.VMEM((ROW_TILE, N), BF16),
                        pltpu.VMEM((ROW_TILE, w_bf16.shape[0]), BF16)],
        compiler_params=_cparams(("arbitrary", "arbitrary")),
        name="attention",
    )(X, Q, K, V, modc, lam, dnw.reshape(1, PAIR), w_bf16)


def _router_kernel(x_ref, nw_ref, mod_ref, rw_ref, h_ref, aff_ref):
    h = _modulated(x_ref[0], nw_ref[...], mod_ref, 3, 4)
    h_ref[0] = h
    logits = jnp.dot(h, rw_ref[...], precision=lax.Precision.HIGHEST, preferred_element_type=F32)
    m = jnp.max(logits, axis=-1, keepdims=True)
    e = jnp.exp(logits - m)
    aff_ref[0] = e / jnp.sum(e, axis=-1, keepdims=True)


def _mod_router(X, nw, modc, router_w, n_lat_tiles):
    B, N, D = X.shape
    E = router_w.shape[1]
    return pl.pallas_call(
        _router_kernel,
        grid=(B, N // ROW_TILE),
        in_specs=[pl.BlockSpec((1, ROW_TILE, D), lambda b, i: (b, i, 0)),
                  pl.BlockSpec((1, D), lambda b, i: (0, 0)),
                  pl.BlockSpec((1, 1, 6, D), lambda b, i: (b, i // n_lat_tiles, 0, 0)),
                  pl.BlockSpec((D, E), lambda b, i: (0, 0))],
        out_specs=[pl.BlockSpec((1, ROW_TILE, D), lambda b, i: (b, i, 0)),
                   pl.BlockSpec((1, ROW_TILE, E), lambda b, i: (b, i, 0))],
        out_shape=[jax.ShapeDtypeStruct((B, N, D), F32),
                   jax.ShapeDtypeStruct((B, N, E), F32)],
        compiler_params=_cparams(("arbitrary", "arbitrary")),
        name="mod_router",
    )(X, nw.reshape(1, D), modc, router_w)


def _ffn_kernel(x_ref, wg_ref, wu_ref, wd_ref, o_ref, acc_ref, wg_s, wu_s, wd_s, *, row_chunk, unroll):
    f = pl.program_id(2)
    n_f = pl.num_programs(2)
    wg_s[...] = wg_ref[0, 0].astype(BF16)
    wu_s[...] = wu_ref[0, 0].astype(BF16)
    wd_s[...] = wd_ref[0, 0].astype(BF16)
    n_rows = x_ref.shape[1]

    @pl.when(f == 0)
    def _():
        acc_ref[...] = jnp.zeros_like(acc_ref)

    def rows_body(r, carry):
        for j in range(unroll):
            rows = pl.ds(pl.multiple_of((r * unroll + j) * row_chunk, row_chunk), row_chunk)
            xr = x_ref[0, rows, :]
            a = jnp.dot(xr, wg_s[...], preferred_element_type=F32)
            u = jnp.dot(xr, wu_s[...], preferred_element_type=F32)
            hm = (a * jax.nn.sigmoid(a) * u).astype(BF16)
            acc_ref[rows, :] += jnp.dot(hm, wd_s[...], preferred_element_type=F32)
        return carry

    lax.fori_loop(0, n_rows // (row_chunk * unroll), rows_body, 0)

    @pl.when(f == n_f - 1)
    def _():
        o_ref[0] = acc_ref[...].astype(o_ref.dtype)


def _expert_ffn(Xe, w_gate, w_up, w_down, layer):
    E, T, D = Xe.shape
    F = w_gate.shape[-1]
    m_tiles = 2
    tm = T // m_tiles
    return pl.pallas_call(
        functools.partial(_ffn_kernel, row_chunk=256, unroll=3),
        grid=(E, m_tiles, F // FF_TILE),
        in_specs=[pl.BlockSpec((1, tm, D), lambda e, m, f: (e, m, 0)),
                  pl.BlockSpec((1, 1, D, FF_TILE), lambda e, m, f: (layer, e, 0, f)),
                  pl.BlockSpec((1, 1, D, FF_TILE), lambda e, m, f: (layer, e, 0, f)),
                  pl.BlockSpec((1, 1, FF_TILE, D), lambda e, m, f: (layer, e, f, 0))],
        out_specs=pl.BlockSpec((1, tm, D), lambda e, m, f: (e, m, 0)),
        out_shape=jax.ShapeDtypeStruct((E, T, D), BF16),
        scratch_shapes=[pltpu.VMEM((tm, D), F32), pltpu.VMEM((D, FF_TILE), BF16),
                        pltpu.VMEM((D, FF_TILE), BF16), pltpu.VMEM((FF_TILE, D), BF16)],
        compiler_params=_cparams(("arbitrary", "arbitrary", "arbitrary")),
        name="expert_ffn",
    )(Xe, w_gate, w_up, w_down)


def _split(z, sizes):
    idx = np.cumsum(sizes)[:-1].tolist()
    return jnp.split(z, idx, axis=-1)


def _rmsnorm(x, w=None):
    y = x * lax.rsqrt(jnp.mean(x * x, axis=-1, keepdims=True) + EPS)
    if w is not None:
        y = y * w
    return y


def _rope_tables(n):
    t = jnp.arange(n)
    row = (t // GRID_W).astype(F32)
    col = (t % GRID_W).astype(F32)
    n_freq = HEAD_DIM // 4
    inv = ROPE_THETA ** (-jnp.arange(n_freq, dtype=F32) / n_freq)
    ang = jnp.concatenate([row[:, None] * inv, col[:, None] * inv], axis=-1)
    return jnp.cos(ang), jnp.sin(ang)


def _rope(x, cos, sin):
    shape = (cos.shape[0],) + (1,) * (x.ndim - 3) + (cos.shape[1],)
    c, s = cos.reshape(shape), sin.reshape(shape)
    x1, x2 = jnp.split(x, 2, axis=-1)
    return jnp.concatenate([x1 * c - x2 * s, x2 * c + x1 * s], axis=-1)


def _attend(q, k, v, shared_k):
    scale = q.shape[-1] ** -0.5
    k_sub = 'bkhd' if shared_k else 'bkhmd'

    def one_block(qb):
        s = jnp.einsum(f'bqhmd,{k_sub}->bhmqk', qb, k, preferred_element_type=F32) * scale
        p = jax.nn.softmax(s, axis=-1)
        return jnp.einsum('bhmqk,bkhv->bqhmv', p, v)

    B, Sq = q.shape[:2]
    nb = Sq // Q_BLOCK
    qb = jnp.moveaxis(q.reshape(B, nb, Q_BLOCK, *q.shape[2:]), 1, 0)
    out = lax.map(one_block, qb)
    return jnp.moveaxis(out, 0, 1).reshape(B, Sq, *out.shape[3:])


def _attention_core(zl, zc, cos, sin, q_norm_w, k_norm_w, lam, diff_norm_w, lambda_init, need_ctx):
    def project(z, use_rope):
        B, n, _ = z.shape
        aq, ak, av, bq, bk, bv = _split(z, ATT_SPLITS)
        aq = _rmsnorm(aq.reshape(B, n, GQA_HEADS, HEAD_DIM), q_norm_w)
        ak = _rmsnorm(ak.reshape(B, n, GQA_KV_HEADS, HEAD_DIM), k_norm_w)
        bq = bq.reshape(B, n, DIFF_HEADS, 2, HEAD_DIM)
        bk = bk.reshape(B, n, DIFF_HEADS, 2, HEAD_DIM)
        if use_rope:
            aq, ak, bq, bk = (_rope(t, cos, sin) for t in (aq, ak, bq, bk))
        aq = aq.reshape(B, n, GQA_KV_HEADS, GQA_HEADS // GQA_KV_HEADS, HEAD_DIM)
        av = av.reshape(B, n, GQA_KV_HEADS, HEAD_DIM)
        bv = bv.reshape(B, n, DIFF_HEADS, 2 * HEAD_DIM)
        return (aq, bq), (ak, av, bk, bv)

    lat_q, lat_kv = project(zl, True)
    ctx_q, ctx_kv = project(zc, False)
    lam_val = jnp.exp(jnp.sum(lam[0] * lam[1])) - jnp.exp(jnp.sum(lam[2] * lam[3])) + lambda_init

    def mix(q_side, kv):
        aq, bq = q_side
        ak, av, bk, bv = kv
        B, n = aq.shape[:2]
        a = _attend(aq, ak, av, True).reshape(B, n, -1)
        o = _attend(bq, bk, bv, False)
        o = o[..., 0, :] - lam_val * o[..., 1, :]
        b = (_rmsnorm(o, diff_norm_w) * (1 - lambda_init)).reshape(B, n, -1)
        return jnp.concatenate([a, b], axis=-1)

    kv_all = tuple(jnp.concatenate([l, c], axis=1) for l, c in zip(lat_kv, ctx_kv))
    yl = mix(lat_q, kv_all)
    yc = mix(ctx_q, ctx_kv) if need_ctx else jnp.zeros((zc.shape[0], zc.shape[1], 2 * HALF_W), F32)
    return yl, yc


def _chunk_scan(q, k, v, log_a, h0):
    B, S = q.shape[:2]
    n = S // CHUNK
    tril = jnp.tril(jnp.ones((CHUNK, CHUNK), dtype=bool))

    def to_chunks(a):
        return jnp.moveaxis(a.reshape(B, n, CHUNK, *a.shape[2:]), 1, 0)

    def step(h, inp):
        qc, kc, vc, lac = inp
        cum = jnp.cumsum(lac.astype(F32), axis=1)
        cum_t = jnp.moveaxis(cum, 1, -1)
        seg = jnp.exp(jnp.where(tril, cum_t[..., :, None] - cum_t[..., None, :], -jnp.inf))
        qk = jnp.einsum('btgk,bsgk->bgts', qc, kc)
        y = (jnp.einsum('bgts,bgrts,bsgrv->btgrv', qk, seg, vc)
             + jnp.einsum('btgk,bgrkv->btgrv', qc, h) * jnp.exp(cum)[..., None])
        last = cum[:, -1]
        h_new = (h * jnp.exp(last)[..., None, None]
                 + jnp.einsum('bsgk,bsgr,bsgrv->bgrkv', kc, jnp.exp(last[:, None] - cum), vc))
        return h_new, y

    h, ys = lax.scan(step, h0, tuple(to_chunks(a) for a in (q, k, v, log_a)))
    return jnp.moveaxis(ys, 0, 1).reshape(B, S, *ys.shape[3:]), h


def _prefix_scan(lat, ctx, reverse):
    if reverse:
        lat = tuple(jnp.flip(a, 1) for a in lat)
        ctx = tuple(jnp.flip(a, 1) for a in ctx)
    q, v = ctx[0], ctx[2]
    h0 = jnp.zeros((q.shape[0], q.shape[2], v.shape[3], q.shape[3], v.shape[4]), F32)
    y_c, h_c = _chunk_scan(*ctx, h0)
    y_l, _ = _chunk_scan(*lat, h_c)
    if reverse:
        y_l, y_c = jnp.flip(y_l, 1), jnp.flip(y_c, 1)
    return y_l, y_c


def _bidir(lat_f, ctx_f, lat_b, ctx_b):
    yl_f, yc_f = _prefix_scan(lat_f, ctx_f, False)
    yl_b, yc_b = _prefix_scan(lat_b, ctx_b, True)
    return yl_f + yl_b, yc_f + yc_b


def _dwconv(x, w, b):
    K = w.shape[0]
    out = lax.conv_general_dilated(x, w[:, None, :], window_strides=(1,),
                                   padding=[(K // 2, K // 2)],
                                   dimension_numbers=('NWC', 'WIO', 'NWC'),
                                   feature_group_count=x.shape[-1])
    return out + b


def _recurrent_core(zl, zc, cos, sin, ret_decay_logit, conv_w, conv_b, dt_bias, a_log, d_skip,
                    ssd_norm_w):
    log_gamma = jax.nn.log_sigmoid(ret_decay_logit)
    a_neg = -jnp.exp(a_log)
    r_per_g = SSD_HEADS // SSD_GROUPS

    def project(z, use_rope):
        B, n, _ = z.shape
        rq, rk, rv, rg, zz, xbc, dt = _split(z, REC_SPLITS)
        rq = rq.reshape(B, n, RET_HEADS, RET_QK_DIM)
        rk = rk.reshape(B, n, RET_HEADS, RET_QK_DIM) * RET_QK_DIM ** -0.5
        if use_rope:
            rq, rk = _rope(rq, cos, sin), _rope(rk, cos, sin)
        rv = rv.reshape(B, n, RET_HEADS, 1, RET_V_DIM)
        ret = [(rq, rk, rv, jnp.broadcast_to(log_gamma[d][:, None], (B, n, RET_HEADS, 1)))
               for d in (0, 1)]
        xbc = jax.nn.silu(_dwconv(xbc, conv_w, conv_b))
        xs, bm, cm = _split(xbc, (SSD_INNER, SSD_GROUPS * SSD_STATE, SSD_GROUPS * SSD_STATE))
        xs = xs.reshape(B, n, SSD_HEADS, SSD_HEAD_DIM)
        bm = bm.reshape(B, n, SSD_GROUPS, SSD_STATE)
        cm = cm.reshape(B, n, SSD_GROUPS, SSD_STATE)
        dt = jax.nn.softplus(dt.reshape(B, n, 2, SSD_HEADS) + dt_bias)
        ssd = [(cm, bm,
                (xs * dt[:, :, d, :, None]).reshape(B, n, SSD_GROUPS, r_per_g, SSD_HEAD_DIM),
                (dt[:, :, d] * a_neg[d]).reshape(B, n, SSD_GROUPS, r_per_g))
               for d in (0, 1)]
        return ret, ssd, rg, zz, xs

    lat_ret, lat_ssd, lat_rg, lat_z, lat_xs = project(zl, True)
    ctx_ret, ctx_ssd, ctx_rg, ctx_z, ctx_xs = project(zc, False)
    ret_l, ret_c = _bidir(lat_ret[0], ctx_ret[0], lat_ret[1], ctx_ret[1])
    ssd_l, ssd_c = _bidir(lat_ssd[0], ctx_ssd[0], lat_ssd[1], ctx_ssd[1])

    def combine(ret_y, ssd_y, rg, zz, xs):
        B, n = rg.shape[:2]
        r = _rmsnorm(ret_y.reshape(B, n, RET_HEADS, RET_V_DIM)).reshape(B, n, -1) * jax.nn.silu(rg)
        s = ssd_y.reshape(B, n, SSD_HEADS, SSD_HEAD_DIM) + d_skip[:, None] * xs
        s = s.reshape(B, n, -1) * jax.nn.silu(zz)
        s = _rmsnorm(s.reshape(B, n, SSD_GROUPS, -1)).reshape(B, n, -1) * ssd_norm_w
        return jnp.concatenate([r, s], axis=-1)

    return combine(ret_l, ssd_l, lat_rg, lat_z, lat_xs), combine(ret_c, ssd_c, ctx_rg, ctx_z, ctx_xs)


def _dispatch_tokens(h, aff):
    B, n, D = h.shape
    cap = EC_CAPACITY * n // N_EXPERTS
    g, idx = lax.top_k(jnp.swapaxes(aff, 1, 2), cap)
    xin = jax.vmap(lambda xs, i: xs[i])(h, idx)
    return xin, g, idx


def _combine_tokens(y, g, idx, n):
    D = y.shape[-1]
    y = y * g[..., None]
    return jax.vmap(lambda val, i: jax.ops.segment_sum(val.reshape(-1, D), i.reshape(-1),
                                                       num_segments=n))(y, idx)


def kernel(x, c, ctx, c_ctx, ada_w, ada_b, norm1_w, norm2_w, att_w_in, att_w_out, att_q_norm_w,
           att_k_norm_w, diff_lambda, diff_norm_w, rec_w_in, rec_w_out, ret_decay_logit, ssd_conv_w,
           ssd_conv_b, ssd_dt_bias, ssd_a_log, ssd_d_skip, ssd_norm_w, router_w, expert_w_gate,
           expert_w_up, expert_w_down, final_norm_w):
    B, n_lat, D = x.shape
    n_ctx = ctx.shape[1]
    n_lat_tiles = n_lat // ROW_TILE
    cos, sin = _rope_tables(n_lat)
    reps = PAIR // cos.shape[1]
    cos_t = jnp.concatenate([jnp.tile(cos, (1, reps)), jnp.ones((n_ctx, PAIR), F32)], axis=0)
    sin_t = jnp.concatenate([jnp.tile(jnp.concatenate([-sin, sin], axis=1), (1, reps // 2)),
                             jnp.zeros((n_ctx, PAIR), F32)], axis=0)
    gmat = jnp.kron(jnp.eye(GQA_HEADS, dtype=F32), jnp.ones((HEAD_DIM, HEAD_DIM), F32)).astype(BF16)

    n_cond = 24
    cvec = jnp.concatenate([c, c_ctx[None, :], jnp.zeros((n_cond - B - 1, D), F32)], axis=0)
    mods = _ada_modulation(cvec, ada_w, ada_b)

    X = jnp.concatenate([x, ctx], axis=1)
    for layer in range(DEPTH):
        i = layer // 2
        need_ctx = layer < DEPTH - 1
        mod_lat = mods[layer, :B].reshape(B, 1, 6, D)
        mod_ctx = jnp.broadcast_to(mods[layer, B].reshape(1, 1, 6, D), (B, 1, 6, D))
        modc = jnp.concatenate([mod_lat, mod_ctx], axis=1)

        if layer % 2 == 0:
            lambda_init = 0.8 - 0.6 * math.exp(-0.3 * layer)
            Q, K, V = _att_project(X, norm1_w[layer], modc, att_w_in[i].astype(BF16), cos_t, sin_t, gmat,
                                   jnp.tile(att_q_norm_w[i], GQA_HEADS).reshape(1, -1),
                                   jnp.tile(att_k_norm_w[i], GQA_KV_HEADS).reshape(1, -1), n_lat_tiles)
            X = _attention(X, Q, K, V, modc, diff_lambda[i], diff_norm_w[i], att_w_out[i].astype(BF16),
                           lambda_init, n_lat_tiles)
        else:
            z = _mod_project(X, norm1_w[layer], modc, rec_w_in[i].astype(BF16), 0, 1, n_lat_tiles)
            yl, yc = _recurrent_core(z[:, :n_lat], z[:, n_lat:], cos, sin, ret_decay_logit[i],
                                     ssd_conv_w[i], ssd_conv_b[i], ssd_dt_bias[i], ssd_a_log[i],
                                     ssd_d_skip[i], ssd_norm_w[i])
            Y = jnp.concatenate([yl, yc], axis=1)
            X = _out_project(X, Y, modc, rec_w_out[i].astype(BF16), 2, n_lat_tiles)

        h2, aff = _mod_router(X, norm2_w[layer], modc, router_w[layer], n_lat_tiles)
        xin_l, g_l, idx_l = _dispatch_tokens(h2[:, :n_lat], aff[:, :n_lat])
        xin_c, g_c, idx_c = _dispatch_tokens(h2[:, n_lat:], aff[:, n_lat:])
        cap_l, cap_c = xin_l.shape[2], xin_c.shape[2]
        Xe = jnp.concatenate([jnp.swapaxes(xin_l, 0, 1).reshape(N_EXPERTS, B * cap_l, D),
                              jnp.swapaxes(xin_c, 0, 1).reshape(N_EXPERTS, B * cap_c, D)], axis=1)
        Ye = _expert_ffn(Xe.astype(BF16), expert_w_gate, expert_w_up, expert_w_down, layer).astype(F32)
        y_l = jnp.swapaxes(Ye[:, :B * cap_l].reshape(N_EXPERTS, B, cap_l, D), 0, 1)
        y_c = jnp.swapaxes(Ye[:, B * cap_l:].reshape(N_EXPERTS, B, cap_c, D), 0, 1)
        m_l = _combine_tokens(y_l, g_l, idx_l, n_lat)
        m_c = _combine_tokens(y_c, g_c, idx_c, n_ctx)
        M = jnp.concatenate([m_l, m_c], axis=1)
        gate = jnp.concatenate([jnp.broadcast_to(modc[:, 0:1, 5], (B, n_lat, D)),
                                jnp.broadcast_to(modc[:, 1:2, 5], (B, n_ctx, D))], axis=1)
        X = X + gate * M

    xl = X[:, :n_lat]
    return _rmsnorm(xl, final_norm_w)
```

```python
import functools
import math

import jax
import jax.numpy as jnp
import numpy as np
from jax import lax
from jax.experimental import pallas as pl
from jax.experimental.pallas import tpu as pltpu

D_MODEL = 1024
DEPTH = 4
GRID_W = 64
HEAD_DIM = 64
ROPE_THETA = 10000.0
Q_BLOCK = 128
CHUNK = 128
EPS = 1e-6
HALF_W = D_MODEL // 2
GQA_HEADS = HALF_W // HEAD_DIM
GQA_KV_HEADS = GQA_HEADS // 4
DIFF_HEADS = HALF_W // (2 * HEAD_DIM)
RET_HEADS = HALF_W // (2 * HEAD_DIM)
RET_QK_DIM = HEAD_DIM
RET_V_DIM = 2 * HEAD_DIM
SSD_HEAD_DIM = HEAD_DIM
SSD_HEADS = HALF_W // SSD_HEAD_DIM
SSD_GROUPS = 2
SSD_STATE = 128
SSD_INNER = SSD_HEADS * SSD_HEAD_DIM
SSD_XBC = SSD_INNER + 2 * SSD_GROUPS * SSD_STATE
N_EXPERTS = 16
EC_CAPACITY = 2
EXPERT_FF = ((8 * D_MODEL // 3 + 255) // 256) * 256

ATT_SPLITS = (GQA_HEADS * HEAD_DIM, GQA_KV_HEADS * HEAD_DIM, GQA_KV_HEADS * HEAD_DIM,
              DIFF_HEADS * 2 * HEAD_DIM, DIFF_HEADS * 2 * HEAD_DIM, DIFF_HEADS * 2 * HEAD_DIM)
REC_SPLITS = (RET_HEADS * RET_QK_DIM, RET_HEADS * RET_QK_DIM, RET_HEADS * RET_V_DIM,
              RET_HEADS * RET_V_DIM, SSD_INNER, SSD_XBC, 2 * SSD_HEADS)

ROW_TILE = 256
FF_TILE = 256
VMEM_LIMIT = 56 * 1024 * 1024
BF16 = jnp.bfloat16
F32 = jnp.float32


def _cparams(sem):
    return pltpu.CompilerParams(dimension_semantics=sem, vmem_limit_bytes=VMEM_LIMIT)


def _ada_kernel(c_ref, w_ref, b_ref, o_ref):
    c = c_ref[...]
    s = c * jax.nn.sigmoid(c)
    o_ref[0] = jnp.dot(s, w_ref[0], precision=lax.Precision.HIGHEST,
                       preferred_element_type=F32) + b_ref[0]


def _ada_modulation(cvec, ada_w, ada_b):
    R, D = cvec.shape
    n_out = ada_w.shape[-1]
    tn = 1536
    return pl.pallas_call(
        _ada_kernel,
        grid=(DEPTH, n_out // tn),
        in_specs=[pl.BlockSpec((R, D), lambda l, j: (0, 0)),
                  pl.BlockSpec((1, D, tn), lambda l, j: (l, 0, j)),
                  pl.BlockSpec((1, 1, tn), lambda l, j: (l, 0, j))],
        out_specs=pl.BlockSpec((1, R, tn), lambda l, j: (l, 0, j)),
        out_shape=jax.ShapeDtypeStruct((DEPTH, R, n_out), F32),
        compiler_params=_cparams(("arbitrary", "arbitrary")),
        name="ada_modulation",
    )(cvec, ada_w, ada_b.reshape(DEPTH, 1, n_out))


def _modulated(x, nw, mod_ref, shift_idx, scale_idx):
    ms = jnp.mean(x * x, axis=-1, keepdims=True)
    y = x * lax.rsqrt(ms + EPS) * nw
    return y * (1.0 + mod_ref[0, 0, scale_idx:scale_idx + 1, :]) + mod_ref[0, 0, shift_idx:shift_idx + 1, :]


def _modproj_kernel(x_ref, nw_ref, mod_ref, w_ref, o_ref, *, shift_idx, scale_idx):
    h = _modulated(x_ref[0], nw_ref[...], mod_ref, shift_idx, scale_idx)
    o_ref[0] = jnp.dot(h.astype(BF16), w_ref[...], preferred_element_type=F32)


def _mod_project(X, nw, modc, w_bf16, shift_idx, scale_idx, n_lat_tiles):
    B, N, D = X.shape
    n_out = w_bf16.shape[1]
    return pl.pallas_call(
        functools.partial(_modproj_kernel, shift_idx=shift_idx, scale_idx=scale_idx),
        grid=(B, N // ROW_TILE),
        in_specs=[pl.BlockSpec((1, ROW_TILE, D), lambda b, i: (b, i, 0)),
                  pl.BlockSpec((1, D), lambda b, i: (0, 0)),
                  pl.BlockSpec((1, 1, 6, D), lambda b, i: (b, i // n_lat_tiles, 0, 0)),
                  pl.BlockSpec((D, n_out), lambda b, i: (0, 0))],
        out_specs=pl.BlockSpec((1, ROW_TILE, n_out), lambda b, i: (b, i, 0)),
        out_shape=jax.ShapeDtypeStruct((B, N, n_out), F32),
        compiler_params=_cparams(("arbitrary", "arbitrary")),
        name="mod_project",
    )(X, nw.reshape(1, D), modc, w_bf16)


def _outproj_kernel(x_ref, y_ref, mod_ref, w_ref, o_ref, *, gate_idx):
    y = jnp.dot(y_ref[0].astype(BF16), w_ref[...], preferred_element_type=F32)
    o_ref[0] = x_ref[0] + mod_ref[0, 0, gate_idx:gate_idx + 1, :] * y


def _out_project(X, Y, modc, w_bf16, gate_idx, n_lat_tiles):
    B, N, D = X.shape
    K = Y.shape[-1]
    return pl.pallas_call(
        functools.partial(_outproj_kernel, gate_idx=gate_idx),
        grid=(B, N // ROW_TILE),
        in_specs=[pl.BlockSpec((1, ROW_TILE, D), lambda b, i: (b, i, 0)),
                  pl.BlockSpec((1, ROW_TILE, K), lambda b, i: (b, i, 0)),
                  pl.BlockSpec((1, 1, 6, D), lambda b, i: (b, i // n_lat_tiles, 0, 0)),
                  pl.BlockSpec((K, D), lambda b, i: (0, 0))],
        out_specs=pl.BlockSpec((1, ROW_TILE, D), lambda b, i: (b, i, 0)),
        out_shape=jax.ShapeDtypeStruct((B, N, D), F32),
        compiler_params=_cparams(("arbitrary", "arbitrary")),
        name="out_project",
    )(X, Y, modc, w_bf16)


PAIR = 2 * HEAD_DIM
N_PAIR_GROUPS = 2 * HALF_W // PAIR
ATT_KV_W = GQA_KV_HEADS * PAIR + DIFF_HEADS * PAIR


def _head_sumsq(x, g):
    x2 = x * x
    hi = x2.astype(BF16)
    lo = (x2 - hi.astype(F32)).astype(BF16)
    return jnp.dot(hi, g, preferred_element_type=F32) + jnp.dot(lo, g, preferred_element_type=F32)


def _rope_lanes(x, cos, sin_signed):
    W = x.shape[-1]
    half = HEAD_DIM // 2
    lane = lax.broadcasted_iota(jnp.int32, x.shape, 1) & (HEAD_DIM - 1)
    partner = jnp.where(lane < half, pltpu.roll(x, W - half, 1), pltpu.roll(x, half, 1))
    return x * cos + partner * sin_signed


def _att_project_kernel(x_ref, nw_ref, mod_ref, w_ref, cos_ref, sin_ref, g_ref, qw_ref, kw_ref,
                        q_ref, k_ref, v_ref):
    h = _modulated(x_ref[0], nw_ref[...], mod_ref, 0, 1)
    z = jnp.dot(h.astype(BF16), w_ref[...], preferred_element_type=F32)
    o_aq, o_ak, o_av, o_bq, o_bk, o_bv = np.cumsum((0,) + ATT_SPLITS[:-1]).tolist()
    n_aq, n_ak, n_bq = ATT_SPLITS[0], ATT_SPLITS[1], ATT_SPLITS[3]
    cos, sin = cos_ref[...], sin_ref[...]
    cos4 = jnp.concatenate([cos] * (n_aq // PAIR), axis=1)
    sin4 = jnp.concatenate([sin] * (n_aq // PAIR), axis=1)
    inv_d = 1.0 / HEAD_DIM
    q_scale = HEAD_DIM ** -0.5

    aq = z[:, o_aq:o_aq + n_aq]
    aq = aq * lax.rsqrt(_head_sumsq(aq, g_ref[...]) * inv_d + EPS) * qw_ref[...]
    aq = _rope_lanes(aq, cos4, sin4) * q_scale
    bq = _rope_lanes(z[:, o_bq:o_bq + n_bq], cos4, sin4) * q_scale
    q = jnp.concatenate([aq, bq], axis=1)
    lo = (lax.broadcasted_iota(jnp.int32, q.shape, 1) & (PAIR - 1)) < HEAD_DIM
    q_ref[0, :, 0:q.shape[1]] = jnp.where(lo, q, 0.0).astype(BF16)
    q_ref[0, :, q.shape[1]:2 * q.shape[1]] = jnp.where(lo, 0.0, q).astype(BF16)

    ak = z[:, o_ak:o_ak + n_ak]
    ak = ak * lax.rsqrt(_head_sumsq(ak, g_ref[0:n_ak, 0:n_ak]) * inv_d + EPS) * kw_ref[...]
    ak = _rope_lanes(ak, cos, sin)
    av = z[:, o_av:o_av + n_ak]
    lo_kv = lax.broadcasted_iota(jnp.int32, ak.shape, 1) < HEAD_DIM
    ak_sw, av_sw = pltpu.roll(ak, HEAD_DIM, 1), pltpu.roll(av, HEAD_DIM, 1)
    k_ref[0, :, 0:PAIR] = jnp.where(lo_kv, ak, ak_sw).astype(BF16)
    k_ref[0, :, PAIR:2 * PAIR] = jnp.where(lo_kv, ak_sw, ak).astype(BF16)
    v_ref[0, :, 0:PAIR] = jnp.where(lo_kv, av, av_sw).astype(BF16)
    v_ref[0, :, PAIR:2 * PAIR] = jnp.where(lo_kv, av_sw, av).astype(BF16)
    k_ref[0, :, 2 * PAIR:] = _rope_lanes(z[:, o_bk:o_bk + n_bq], cos4, sin4).astype(BF16)
    v_ref[0, :, 2 * PAIR:] = z[:, o_bv:o_bv + n_bq].astype(BF16)


def _att_project(X, nw, modc, w_bf16, cos_t, sin_t, gmat, qw_t, kw_t, n_lat_tiles):
    B, N, D = X.shape
    n_in = w_bf16.shape[1]
    q_w = 2 * N_PAIR_GROUPS * PAIR
    row = lambda b, i: (b, i, 0)
    const2 = lambda b, i: (0, 0)
    return pl.pallas_call(
        _att_project_kernel,
        grid=(B, N // ROW_TILE),
        in_specs=[pl.BlockSpec((1, ROW_TILE, D), row),
                  pl.BlockSpec((1, D), const2),
                  pl.BlockSpec((1, 1, 6, D), lambda b, i: (b, i // n_lat_tiles, 0, 0)),
                  pl.BlockSpec((D, n_in), const2),
                  pl.BlockSpec((ROW_TILE, PAIR), lambda b, i: (i, 0)),
                  pl.BlockSpec((ROW_TILE, PAIR), lambda b, i: (i, 0)),
                  pl.BlockSpec(gmat.shape, const2),
                  pl.BlockSpec(qw_t.shape, const2),
                  pl.BlockSpec(kw_t.shape, const2)],
        out_specs=[pl.BlockSpec((1, ROW_TILE, q_w), row),
                   pl.BlockSpec((1, ROW_TILE, ATT_KV_W), row),
                   pl.BlockSpec((1, ROW_TILE, ATT_KV_W), row)],
        out_shape=[jax.ShapeDtypeStruct((B, N, q_w), BF16),
                   jax.ShapeDtypeStruct((B, N, ATT_KV_W), BF16),
                   jax.ShapeDtypeStruct((B, N, ATT_KV_W), BF16)],
        compiler_params=_cparams(("arbitrary", "arbitrary")),
        name="att_project",
    )(X, nw.reshape(1, D), modc, w_bf16, cos_t, sin_t, gmat, qw_t, kw_t)


def _attention_kernel(x_ref, q_ref, k_ref, v_ref, mod_ref, lam_ref, dnw_ref, w_ref, o_ref,
                      s_scr, p_scr, y_scr, *, lambda_init, n_lat_tiles):
    n_lat = n_lat_tiles * ROW_TILE
    n_all = k_ref.shape[1]
    q_half = N_PAIR_GROUPS * PAIR
    lam = lam_ref[...]
    lam_val = (jnp.exp(jnp.sum(lam[0:1] * lam[1:2], axis=-1, keepdims=True))
               - jnp.exp(jnp.sum(lam[2:3] * lam[3:4], axis=-1, keepdims=True)) + lambda_init)

    def attend(k0, nk):
        keys = slice(k0, k0 + nk)

        def unit(pg, half):
            qv = q_ref[0, :, half * q_half + pg * PAIR: half * q_half + (pg + 1) * PAIR]
            kd = k_ref[0, keys, _kv_lane(pg): _kv_lane(pg) + PAIR]
            vd = v_ref[0, keys, _kv_lane(pg): _kv_lane(pg) + PAIR]
            s_scr[:, 0:nk] = lax.dot_general(qv, kd, (((1,), (1,)), ((), ())),
                                             preferred_element_type=F32)
            s = s_scr[:, 0:nk]
            p = jnp.exp(s - jnp.max(s, axis=-1, keepdims=True))
            p_scr[:, 0:nk] = p.astype(BF16)
            o = jnp.dot(p_scr[:, 0:nk], vd, preferred_element_type=F32)
            return o * (1.0 / jnp.sum(p, axis=-1, keepdims=True))

        lo = lax.broadcasted_iota(jnp.int32, (ROW_TILE, PAIR), 1) < HEAD_DIM
        for pg in range(2 * GQA_KV_HEADS):
            y_scr[:, pg * PAIR:(pg + 1) * PAIR] = jnp.where(lo, unit(pg, 0), unit(pg, 1)).astype(BF16)
        for pg in range(2 * GQA_KV_HEADS, N_PAIR_GROUPS):
            od = unit(pg, 0) - lam_val * unit(pg, 1)
            od = od * lax.rsqrt(jnp.mean(od * od, axis=-1, keepdims=True) + EPS)
            od = od * dnw_ref[...] * (1.0 - lambda_init)
            y_scr[:, pg * PAIR:(pg + 1) * PAIR] = od.astype(BF16)
        y = jnp.dot(y_scr[...], w_ref[...], preferred_element_type=F32)
        o_ref[0] = x_ref[0] + mod_ref[0, 0, 2:3, :] * y

    i = pl.program_id(1)

    @pl.when(i < n_lat_tiles)
    def _():
        attend(0, n_all)

    @pl.when(i >= n_lat_tiles)
    def _():
        attend(n_lat, n_all - n_lat)


def _kv_lane(pg):
    n_gqa_pairs = 2 * GQA_KV_HEADS
    if pg < n_gqa_pairs:
        return (pg // 2) * PAIR
    return (GQA_KV_HEADS + pg - n_gqa_pairs) * PAIR


def _attention(X, Q, K, V, modc, lam, dnw, w_bf16, lambda_init, n_lat_tiles):
    B, N, D = X.shape
    row = lambda b, i: (b, i, 0)
    whole = lambda b, i: (b, 0, 0)
    const2 = lambda b, i: (0, 0)
    return pl.pallas_call(
        functools.partial(_attention_kernel, lambda_init=lambda_init, n_lat_tiles=n_lat_tiles),
        grid=(B, N // ROW_TILE),
        in_specs=[pl.BlockSpec((1, ROW_TILE, D), row),
                  pl.BlockSpec((1, ROW_TILE, Q.shape[-1]), row),
                  pl.BlockSpec((1, N, K.shape[-1]), whole),
                  pl.BlockSpec((1, N, V.shape[-1]), whole),
                  pl.BlockSpec((1, 1, 6, D), lambda b, i: (b, i // n_lat_tiles, 0, 0)),
                  pl.BlockSpec(lam.shape, const2),
                  pl.BlockSpec((1, PAIR), const2),
                  pl.BlockSpec(w_bf16.shape, const2)],
        out_specs=pl.BlockSpec((1, ROW_TILE, D), row),
        out_shape=jax.ShapeDtypeStruct((B, N, D), F32),
        scratch_shapes=[pltpu.VMEM((ROW_TILE, N), F32), pltpu.VMEM((ROW_TILE, N), BF16),
                        pltpu.VMEM((ROW_TILE, w_bf16.shape[0]), BF16)],
        compiler_params=_cparams(("arbitrary", "arbitrary")),
        name="attention",
    )(X, Q, K, V, modc, lam, dnw.reshape(1, PAIR), w_bf16)


REC_IN_PAD = 3200
N_SSD_DT = 2 * SSD_HEADS
RET_LANE0 = N_SSD_DT


def _softplus(x):
    return jnp.maximum(x, 0.0) + jnp.log(1.0 + jnp.exp(-jnp.abs(x)))


def _silu(x):
    return x * jax.nn.sigmoid(x)


def _rec_project_kernel(x_ref, nw_ref, mod_ref, w_ref, cos_ref, sin_ref, dtb_ref,
                        rq_ref, rk_ref, rv_ref, g_ref, xbc_ref, dt_ref):
    h = _modulated(x_ref[0], nw_ref[...], mod_ref, 0, 1)
    z = jnp.dot(h.astype(BF16), w_ref[...], preferred_element_type=F32)
    o_rq, o_rk, o_rv, o_rg, o_z, o_xbc, o_dt = np.cumsum((0,) + REC_SPLITS[:-1]).tolist()
    n_qk = REC_SPLITS[0]
    cos, sin = cos_ref[...], sin_ref[...]
    cos2 = jnp.concatenate([cos] * (n_qk // PAIR), axis=1)
    sin2 = jnp.concatenate([sin] * (n_qk // PAIR), axis=1)
    rq = _rope_lanes(z[:, o_rq:o_rq + n_qk], cos2, sin2)
    lo = (lax.broadcasted_iota(jnp.int32, rq.shape, 1) & (PAIR - 1)) < HEAD_DIM
    rq_ref[0, :, 0:n_qk] = jnp.where(lo, rq, 0.0).astype(BF16)
    rq_ref[0, :, n_qk:2 * n_qk] = jnp.where(lo, 0.0, rq).astype(BF16)
    rk_ref[0] = _rope_lanes(z[:, o_rk:o_rk + n_qk] * (RET_QK_DIM ** -0.5), cos2, sin2).astype(BF16)
    rv_ref[0] = z[:, o_rv:o_rg].astype(BF16)
    g_ref[0] = z[:, o_rg:o_xbc]
    xbc_ref[0] = z[:, o_xbc:o_dt]
    dt_ref[0] = _softplus(z[:, o_dt:o_dt + PAIR] + dtb_ref[...])


def _rec_project(X, nw, modc, w_bf16, cos_t, sin_t, dtb_row, n_lat_tiles):
    B, N, D = X.shape
    n_in = w_bf16.shape[1]
    row = lambda b, i: (b, i, 0)
    const2 = lambda b, i: (0, 0)
    widths = (2 * REC_SPLITS[0], REC_SPLITS[1], REC_SPLITS[2], REC_SPLITS[3] + REC_SPLITS[4],
              REC_SPLITS[5], PAIR)
    dtypes = (BF16, BF16, BF16, F32, F32, F32)
    return pl.pallas_call(
        _rec_project_kernel,
        grid=(B, N // ROW_TILE),
        in_specs=[pl.BlockSpec((1, ROW_TILE, D), row),
                  pl.BlockSpec((1, D), const2),
                  pl.BlockSpec((1, 1, 6, D), lambda b, i: (b, i // n_lat_tiles, 0, 0)),
                  pl.BlockSpec((D, n_in), const2),
                  pl.BlockSpec((ROW_TILE, PAIR), lambda b, i: (i, 0)),
                  pl.BlockSpec((ROW_TILE, PAIR), lambda b, i: (i, 0)),
                  pl.BlockSpec((1, PAIR), const2)],
        out_specs=[pl.BlockSpec((1, ROW_TILE, w), row) for w in widths],
        out_shape=[jax.ShapeDtypeStruct((B, N, w), dt) for w, dt in zip(widths, dtypes)],
        compiler_params=_cparams(("arbitrary", "arbitrary")),
        name="rec_project",
    )(X, nw.reshape(1, D), modc, w_bf16, cos_t, sin_t, dtb_row)


def _conv_kernel(x_ref, w_ref, b_ref, o_ref, *, n_lat):
    x = x_ref[0]
    n = x.shape[0]
    t = lax.broadcasted_iota(jnp.int32, x.shape, 0)
    first = (t == 0) | (t == n_lat)
    last = (t == n_lat - 1) | (t == n - 1)
    prev = jnp.where(first, 0.0, pltpu.roll(x, 1, 0))
    nxt = jnp.where(last, 0.0, pltpu.roll(x, n - 1, 0))
    y = prev * w_ref[0:1, :] + x * w_ref[1:2, :] + nxt * w_ref[2:3, :] + b_ref[...]
    o_ref[0] = _silu(y)


def _ssd_conv(XBC, conv_w, conv_b, n_lat):
    B, N, C = XBC.shape
    tc = 256
    return pl.pallas_call(
        functools.partial(_conv_kernel, n_lat=n_lat),
        grid=(B, C // tc),
        in_specs=[pl.BlockSpec((1, N, tc), lambda b, j: (b, 0, j)),
                  pl.BlockSpec((conv_w.shape[0], tc), lambda b, j: (0, j)),
                  pl.BlockSpec((1, tc), lambda b, j: (0, j))],
        out_specs=pl.BlockSpec((1, N, tc), lambda b, j: (b, 0, j)),
        out_shape=jax.ShapeDtypeStruct((B, N, C), F32),
        compiler_params=_cparams(("arbitrary", "arbitrary")),
        name="ssd_conv",
    )(XBC, conv_w, conv_b.reshape(1, C))


def _split3(x):
    hi = x.astype(BF16)
    r1 = x - hi.astype(F32)
    mid = r1.astype(BF16)
    lo = (r1 - mid.astype(F32)).astype(BF16)
    return hi, mid, lo


def _scan_kernel(rq_ref, rk_ref, rv_ref, a_ref, dt_ref, alog_ref, logit_ref, y_ref, hret, hssd, *, reverse):
    T = rq_ref.shape[1]
    d = 1 if reverse else 0
    step = pl.program_id(1)

    @pl.when(step == 0)
    def _():
        hret[...] = jnp.zeros_like(hret)
        hssd[...] = jnp.zeros_like(hssd)

    lane = lax.broadcasted_iota(jnp.int32, (1, PAIR), 1)
    a_neg = jnp.where(lane < N_SSD_DT, -jnp.exp(alog_ref[...]), 0.0)
    logit = logit_ref[...]
    log_gamma = jnp.where((lane >= RET_LANE0) & (lane < RET_LANE0 + 2 * RET_HEADS),
                          jnp.minimum(logit, 0.0) - jnp.log(1.0 + jnp.exp(-jnp.abs(logit))), 0.0)
    dt = dt_ref[0]
    la = dt * a_neg + log_gamma

    ti = lax.broadcasted_iota(jnp.int32, (T, T), 0)
    tj = lax.broadcasted_iota(jnp.int32, (T, T), 1)
    causal = (tj >= ti) if reverse else (tj <= ti)
    tri = jnp.where(causal, 1.0, 0.0).astype(BF16)
    hi, mid, lo3 = _split3(la)
    P = (jnp.dot(tri, hi, preferred_element_type=F32) + jnp.dot(tri, mid, preferred_element_type=F32)
         + jnp.dot(tri, lo3, preferred_element_type=F32))
    PT = P.T
    tot = P[0:1, :] if reverse else P[T - 1:T, :]
    E = jnp.exp(P)
    KD = jnp.exp(tot - P)
    ET = jnp.exp(tot)
    lo = lax.broadcasted_iota(jnp.int32, (T, PAIR), 1) < HEAD_DIM
    lo_row = lane < HEAD_DIM
    nt = (((1,), (1,)), ((), ()))
    tn = (((0,), (0,)), ((), ()))

    def decay_matrix(c):
        diff = jnp.minimum(P[:, c:c + 1] - PT[c:c + 1, :], 0.0)
        return jnp.where(causal, jnp.exp(diff), 0.0)

    n_qk = RET_HEADS * RET_QK_DIM
    for h in range(RET_HEADS):
        c = RET_LANE0 + d * RET_HEADS + h
        p, half = h // 2, h % 2
        qv = rq_ref[0, :, half * n_qk + p * PAIR: half * n_qk + (p + 1) * PAIR]
        kp = rk_ref[0, :, p * PAIR:(p + 1) * PAIR]
        v = rv_ref[0, :, h * RET_V_DIM:(h + 1) * RET_V_DIM]
        s = lax.dot_general(qv, kp, nt, preferred_element_type=F32)
        aw = (s * decay_matrix(c)).astype(BF16)
        y = jnp.dot(aw, v, preferred_element_type=F32)
        y = y + E[:, c:c + 1] * jnp.dot(qv, hret[h].astype(BF16), preferred_element_type=F32)
        y_ref[0, :, h * RET_V_DIM:(h + 1) * RET_V_DIM] = y
        kdec = (kp.astype(F32) * KD[:, c:c + 1]).astype(BF16)
        hret[h] = hret[h] * ET[:, c:c + 1] + lax.dot_general(kdec, v, tn, preferred_element_type=F32)

    y0 = RET_HEADS * RET_V_DIM
    r_per_g = SSD_HEADS // SSD_GROUPS
    o_b = SSD_INNER
    o_c = SSD_INNER + SSD_GROUPS * SSD_STATE
    for g in range(SSD_GROUPS):
        cg = a_ref[0, :, o_c + g * SSD_STATE: o_c + (g + 1) * SSD_STATE].astype(BF16)
        bg = a_ref[0, :, o_b + g * SSD_STATE: o_b + (g + 1) * SSD_STATE].astype(BF16)
        s = lax.dot_general(cg, bg, nt, preferred_element_type=F32)
        ch = jnp.dot(cg, hssd[g].astype(BF16), preferred_element_type=F32)
        vdec, dec_rows = [], []
        for p in range(r_per_g // 2):
            pg = g * (r_per_g // 2) + p
            xs = a_ref[0, :, pg * PAIR:(pg + 1) * PAIR]
            ys, vds, cols = [], [], []
            for half in range(2):
                c = d * SSD_HEADS + 2 * pg + half
                aw = (s * decay_matrix(c)).astype(BF16)
                vh = xs * dt[:, c:c + 1]
                ys.append(jnp.dot(aw, vh.astype(BF16), preferred_element_type=F32))
                vds.append(vh * KD[:, c:c + 1])
                cols.append(c)
            e_pair = jnp.where(lo, E[:, cols[0]:cols[0] + 1], E[:, cols[1]:cols[1] + 1])
            y_ref[0, :, y0 + pg * PAIR: y0 + (pg + 1) * PAIR] = (
                jnp.where(lo, ys[0], ys[1]) + e_pair * ch[:, p * PAIR:(p + 1) * PAIR])
            vdec.append(jnp.where(lo, vds[0], vds[1]).astype(BF16))
            dec_rows.append(jnp.where(lo_row, ET[:, cols[0]:cols[0] + 1], ET[:, cols[1]:cols[1] + 1]))
        hssd[g] = (hssd[g] * jnp.concatenate(dec_rows, axis=1)
                   + lax.dot_general(bg, jnp.concatenate(vdec, axis=1), tn, preferred_element_type=F32))


def _bidir_scan(RQ, RK, RV, A, DT, alog_row, logit_row, n_lat_tiles, reverse):
    B, N, _ = RQ.shape
    n_tiles = N // ROW_TILE

    def tile(i):
        lat = (n_lat_tiles - i) if reverse else (i - 1)
        return jnp.where(i == 0, n_tiles - 1, lat)

    row = lambda b, i: (b, tile(i), 0)
    const2 = lambda b, i: (0, 0)
    y_w = RET_HEADS * RET_V_DIM + SSD_INNER
    return pl.pallas_call(
        functools.partial(_scan_kernel, reverse=reverse),
        grid=(B, n_tiles),
        in_specs=[pl.BlockSpec((1, ROW_TILE, RQ.shape[-1]), row),
                  pl.BlockSpec((1, ROW_TILE, RK.shape[-1]), row),
                  pl.BlockSpec((1, ROW_TILE, RV.shape[-1]), row),
                  pl.BlockSpec((1, ROW_TILE, A.shape[-1]), row),
                  pl.BlockSpec((1, ROW_TILE, PAIR), row),
                  pl.BlockSpec((1, PAIR), const2),
                  pl.BlockSpec((1, PAIR), const2)],
        out_specs=pl.BlockSpec((1, ROW_TILE, y_w), row),
        out_shape=jax.ShapeDtypeStruct((B, N, y_w), F32),
        scratch_shapes=[pltpu.VMEM((RET_HEADS, PAIR, RET_V_DIM), F32),
                        pltpu.VMEM((SSD_GROUPS, SSD_STATE, SSD_INNER // SSD_GROUPS), F32)],
        compiler_params=_cparams(("arbitrary", "arbitrary")),
        name="scan_bwd" if reverse else "scan_fwd",
    )(RQ, RK, RV, A, DT, alog_row, logit_row)


def _rec_combine_kernel(x_ref, yf_ref, yb_ref, g_ref, a_ref, mod_ref, dskip_ref, snw_ref, w_ref, o_ref, y_scr):
    n_ret = RET_HEADS * RET_V_DIM
    y = yf_ref[0] + yb_ref[0]
    gates = g_ref[0]
    for h in range(RET_HEADS):
        sl = slice(h * RET_V_DIM, (h + 1) * RET_V_DIM)
        r = y[:, sl]
        r = r * lax.rsqrt(jnp.mean(r * r, axis=-1, keepdims=True) + EPS)
        y_scr[:, sl] = (r * _silu(gates[:, sl])).astype(BF16)
    gw = SSD_INNER // SSD_GROUPS
    for g in range(SSD_GROUPS):
        sl = slice(g * gw, (g + 1) * gw)
        s = y[:, n_ret + g * gw: n_ret + (g + 1) * gw] + dskip_ref[:, sl] * a_ref[0, :, sl]
        s = s * _silu(gates[:, n_ret + g * gw: n_ret + (g + 1) * gw])
        s = s * lax.rsqrt(jnp.mean(s * s, axis=-1, keepdims=True) + EPS) * snw_ref[:, sl]
        y_scr[:, n_ret + g * gw: n_ret + (g + 1) * gw] = s.astype(BF16)
    o_ref[0] = x_ref[0] + mod_ref[0, 0, 2:3, :] * jnp.dot(y_scr[...], w_ref[...], preferred_element_type=F32)


def _rec_combine(X, Yf, Yb, G, A, modc, dskip_row, snw_row, w_bf16, n_lat_tiles):
    B, N, D = X.shape
    row = lambda b, i: (b, i, 0)
    const2 = lambda b, i: (0, 0)
    return pl.pallas_call(
        _rec_combine_kernel,
        grid=(B, N // ROW_TILE),
        in_specs=[pl.BlockSpec((1, ROW_TILE, D), row),
                  pl.BlockSpec((1, ROW_TILE, Yf.shape[-1]), row),
                  pl.BlockSpec((1, ROW_TILE, Yb.shape[-1]), row),
                  pl.BlockSpec((1, ROW_TILE, G.shape[-1]), row),
                  pl.BlockSpec((1, ROW_TILE, SSD_INNER), row),
                  pl.BlockSpec((1, 1, 6, D), lambda b, i: (b, i // n_lat_tiles, 0, 0)),
                  pl.BlockSpec((1, SSD_INNER), const2),
                  pl.BlockSpec((1, SSD_INNER), const2),
                  pl.BlockSpec(w_bf16.shape, const2)],
        out_specs=pl.BlockSpec((1, ROW_TILE, D), row),
        out_shape=jax.ShapeDtypeStruct((B, N, D), F32),
        scratch_shapes=[pltpu.VMEM((ROW_TILE, w_bf16.shape[0]), BF16)],
        compiler_params=_cparams(("arbitrary", "arbitrary")),
        name="rec_combine",
    )(X, Yf, Yb, G, A, modc, dskip_row, snw_row, w_bf16)


def _recurrent_layer(X, nw, modc, w_in, w_out, cos_t, sin_t, ret_decay_logit, conv_w, conv_b, dt_bias,
                     a_log, d_skip, ssd_norm_w, n_lat_tiles):
    D = X.shape[-1]
    w_pad = jnp.pad(w_in, ((0, 0), (0, REC_IN_PAD - w_in.shape[1]))).astype(BF16)
    pad_row = lambda v, at: jnp.pad(v.reshape(1, -1), ((0, 0), (at, PAIR - at - v.size)))
    RQ, RK, RV, G, XBC, DT = _rec_project(X, nw, modc, w_pad, cos_t, sin_t, pad_row(dt_bias, 0), n_lat_tiles)
    A = _ssd_conv(XBC, conv_w, conv_b, n_lat_tiles * ROW_TILE)
    alog_row, logit_row = pad_row(a_log, 0), pad_row(ret_decay_logit, RET_LANE0)
    Yf = _bidir_scan(RQ, RK, RV, A, DT, alog_row, logit_row, n_lat_tiles, False)
    Yb = _bidir_scan(RQ, RK, RV, A, DT, alog_row, logit_row, n_lat_tiles, True)
    return _rec_combine(X, Yf, Yb, G, A, modc, jnp.repeat(d_skip, SSD_HEAD_DIM).reshape(1, -1),
                        ssd_norm_w.reshape(1, -1), w_out.astype(BF16), n_lat_tiles)


def _router_kernel(x_ref, nw_ref, mod_ref, rw_ref, h_ref, aff_ref):
    h = _modulated(x_ref[0], nw_ref[...], mod_ref, 3, 4)
    h_ref[0] = h
    logits = jnp.dot(h, rw_ref[...], precision=lax.Precision.HIGHEST, preferred_element_type=F32)
    m = jnp.max(logits, axis=-1, keepdims=True)
    e = jnp.exp(logits - m)
    aff_ref[0] = e / jnp.sum(e, axis=-1, keepdims=True)


def _mod_router(X, nw, modc, router_w, n_lat_tiles):
    B, N, D = X.shape
    E = router_w.shape[1]
    return pl.pallas_call(
        _router_kernel,
        grid=(B, N // ROW_TILE),
        in_specs=[pl.BlockSpec((1, ROW_TILE, D), lambda b, i: (b, i, 0)),
                  pl.BlockSpec((1, D), lambda b, i: (0, 0)),
                  pl.BlockSpec((1, 1, 6, D), lambda b, i: (b, i // n_lat_tiles, 0, 0)),
                  pl.BlockSpec((D, E), lambda b, i: (0, 0))],
        out_specs=[pl.BlockSpec((1, ROW_TILE, D), lambda b, i: (b, i, 0)),
                   pl.BlockSpec((1, ROW_TILE, E), lambda b, i: (b, i, 0))],
        out_shape=[jax.ShapeDtypeStruct((B, N, D), F32),
                   jax.ShapeDtypeStruct((B, N, E), F32)],
        compiler_params=_cparams(("arbitrary", "arbitrary")),
        name="mod_router",
    )(X, nw.reshape(1, D), modc, router_w)


def _ffn_kernel(x_ref, wg_ref, wu_ref, wd_ref, o_ref, acc_ref, wg_s, wu_s, wd_s, *, row_chunk, unroll):
    f = pl.program_id(2)
    n_f = pl.num_programs(2)
    wg_s[...] = wg_ref[0, 0].astype(BF16)
    wu_s[...] = wu_ref[0, 0].astype(BF16)
    wd_s[...] = wd_ref[0, 0].astype(BF16)
    n_rows = x_ref.shape[1]

    @pl.when(f == 0)
    def _():
        acc_ref[...] = jnp.zeros_like(acc_ref)

    def rows_body(r, carry):
        for j in range(unroll):
            rows = pl.ds(pl.multiple_of((r * unroll + j) * row_chunk, row_chunk), row_chunk)
            xr = x_ref[0, rows, :]
            a = jnp.dot(xr, wg_s[...], preferred_element_type=F32)
            u = jnp.dot(xr, wu_s[...], preferred_element_type=F32)
            hm = (a * jax.nn.sigmoid(a) * u).astype(BF16)
            acc_ref[rows, :] += jnp.dot(hm, wd_s[...], preferred_element_type=F32)
        return carry

    lax.fori_loop(0, n_rows // (row_chunk * unroll), rows_body, 0)

    @pl.when(f == n_f - 1)
    def _():
        o_ref[0] = acc_ref[...].astype(o_ref.dtype)


def _expert_ffn(Xe, w_gate, w_up, w_down, layer):
    E, T, D = Xe.shape
    F = w_gate.shape[-1]
    m_tiles = 2
    tm = T // m_tiles
    return pl.pallas_call(
        functools.partial(_ffn_kernel, row_chunk=256, unroll=3),
        grid=(E, m_tiles, F // FF_TILE),
        in_specs=[pl.BlockSpec((1, tm, D), lambda e, m, f: (e, m, 0)),
                  pl.BlockSpec((1, 1, D, FF_TILE), lambda e, m, f: (layer, e, 0, f)),
                  pl.BlockSpec((1, 1, D, FF_TILE), lambda e, m, f: (layer, e, 0, f)),
                  pl.BlockSpec((1, 1, FF_TILE, D), lambda e, m, f: (layer, e, f, 0))],
        out_specs=pl.BlockSpec((1, tm, D), lambda e, m, f: (e, m, 0)),
        out_shape=jax.ShapeDtypeStruct((E, T, D), BF16),
        scratch_shapes=[pltpu.VMEM((tm, D), F32), pltpu.VMEM((D, FF_TILE), BF16),
                        pltpu.VMEM((D, FF_TILE), BF16), pltpu.VMEM((FF_TILE, D), BF16)],
        compiler_params=_cparams(("arbitrary", "arbitrary", "arbitrary")),
        name="expert_ffn",
    )(Xe, w_gate, w_up, w_down)


def _split(z, sizes):
    idx = np.cumsum(sizes)[:-1].tolist()
    return jnp.split(z, idx, axis=-1)


def _rmsnorm(x, w=None):
    y = x * lax.rsqrt(jnp.mean(x * x, axis=-1, keepdims=True) + EPS)
    if w is not None:
        y = y * w
    return y


def _rope_tables(n):
    t = jnp.arange(n)
    row = (t // GRID_W).astype(F32)
    col = (t % GRID_W).astype(F32)
    n_freq = HEAD_DIM // 4
    inv = ROPE_THETA ** (-jnp.arange(n_freq, dtype=F32) / n_freq)
    ang = jnp.concatenate([row[:, None] * inv, col[:, None] * inv], axis=-1)
    return jnp.cos(ang), jnp.sin(ang)


def _rope(x, cos, sin):
    shape = (cos.shape[0],) + (1,) * (x.ndim - 3) + (cos.shape[1],)
    c, s = cos.reshape(shape), sin.reshape(shape)
    x1, x2 = jnp.split(x, 2, axis=-1)
    return jnp.concatenate([x1 * c - x2 * s, x2 * c + x1 * s], axis=-1)


def _attend(q, k, v, shared_k):
    scale = q.shape[-1] ** -0.5
    k_sub = 'bkhd' if shared_k else 'bkhmd'

    def one_block(qb):
        s = jnp.einsum(f'bqhmd,{k_sub}->bhmqk', qb, k, preferred_element_type=F32) * scale
        p = jax.nn.softmax(s, axis=-1)
        return jnp.einsum('bhmqk,bkhv->bqhmv', p, v)

    B, Sq = q.shape[:2]
    nb = Sq // Q_BLOCK
    qb = jnp.moveaxis(q.reshape(B, nb, Q_BLOCK, *q.shape[2:]), 1, 0)
    out = lax.map(one_block, qb)
    return jnp.moveaxis(out, 0, 1).reshape(B, Sq, *out.shape[3:])


def _attention_core(zl, zc, cos, sin, q_norm_w, k_norm_w, lam, diff_norm_w, lambda_init, need_ctx):
    def project(z, use_rope):
        B, n, _ = z.shape
        aq, ak, av, bq, bk, bv = _split(z, ATT_SPLITS)
        aq = _rmsnorm(aq.reshape(B, n, GQA_HEADS, HEAD_DIM), q_norm_w)
        ak = _rmsnorm(ak.reshape(B, n, GQA_KV_HEADS, HEAD_DIM), k_norm_w)
        bq = bq.reshape(B, n, DIFF_HEADS, 2, HEAD_DIM)
        bk = bk.reshape(B, n, DIFF_HEADS, 2, HEAD_DIM)
        if use_rope:
            aq, ak, bq, bk = (_rope(t, cos, sin) for t in (aq, ak, bq, bk))
        aq = aq.reshape(B, n, GQA_KV_HEADS, GQA_HEADS // GQA_KV_HEADS, HEAD_DIM)
        av = av.reshape(B, n, GQA_KV_HEADS, HEAD_DIM)
        bv = bv.reshape(B, n, DIFF_HEADS, 2 * HEAD_DIM)
        return (aq, bq), (ak, av, bk, bv)

    lat_q, lat_kv = project(zl, True)
    ctx_q, ctx_kv = project(zc, False)
    lam_val = jnp.exp(jnp.sum(lam[0] * lam[1])) - jnp.exp(jnp.sum(lam[2] * lam[3])) + lambda_init

    def mix(q_side, kv):
        aq, bq = q_side
        ak, av, bk, bv = kv
        B, n = aq.shape[:2]
        a = _attend(aq, ak, av, True).reshape(B, n, -1)
        o = _attend(bq, bk, bv, False)
        o = o[..., 0, :] - lam_val * o[..., 1, :]
        b = (_rmsnorm(o, diff_norm_w) * (1 - lambda_init)).reshape(B, n, -1)
        return jnp.concatenate([a, b], axis=-1)

    kv_all = tuple(jnp.concatenate([l, c], axis=1) for l, c in zip(lat_kv, ctx_kv))
    yl = mix(lat_q, kv_all)
    yc = mix(ctx_q, ctx_kv) if need_ctx else jnp.zeros((zc.shape[0], zc.shape[1], 2 * HALF_W), F32)
    return yl, yc


def _chunk_scan(q, k, v, log_a, h0):
    B, S = q.shape[:2]
    n = S // CHUNK
    tril = jnp.tril(jnp.ones((CHUNK, CHUNK), dtype=bool))

    def to_chunks(a):
        return jnp.moveaxis(a.reshape(B, n, CHUNK, *a.shape[2:]), 1, 0)

    def step(h, inp):
        qc, kc, vc, lac = inp
        cum = jnp.cumsum(lac.astype(F32), axis=1)
        cum_t = jnp.moveaxis(cum, 1, -1)
        seg = jnp.exp(jnp.where(tril, cum_t[..., :, None] - cum_t[..., None, :], -jnp.inf))
        qk = jnp.einsum('btgk,bsgk->bgts', qc, kc)
        y = (jnp.einsum('bgts,bgrts,bsgrv->btgrv', qk, seg, vc)
             + jnp.einsum('btgk,bgrkv->btgrv', qc, h) * jnp.exp(cum)[..., None])
        last = cum[:, -1]
        h_new = (h * jnp.exp(last)[..., None, None]
                 + jnp.einsum('bsgk,bsgr,bsgrv->bgrkv', kc, jnp.exp(last[:, None] - cum), vc))
        return h_new, y

    h, ys = lax.scan(step, h0, tuple(to_chunks(a) for a in (q, k, v, log_a)))
    return jnp.moveaxis(ys, 0, 1).reshape(B, S, *ys.shape[3:]), h


def _prefix_scan(lat, ctx, reverse):
    if reverse:
        lat = tuple(jnp.flip(a, 1) for a in lat)
        ctx = tuple(jnp.flip(a, 1) for a in ctx)
    q, v = ctx[0], ctx[2]
    h0 = jnp.zeros((q.shape[0], q.shape[2], v.shape[3], q.shape[3], v.shape[4]), F32)
    y_c, h_c = _chunk_scan(*ctx, h0)
    y_l, _ = _chunk_scan(*lat, h_c)
    if reverse:
        y_l, y_c = jnp.flip(y_l, 1), jnp.flip(y_c, 1)
    return y_l, y_c


def _bidir(lat_f, ctx_f, lat_b, ctx_b):
    yl_f, yc_f = _prefix_scan(lat_f, ctx_f, False)
    yl_b, yc_b = _prefix_scan(lat_b, ctx_b, True)
    return yl_f + yl_b, yc_f + yc_b


def _dwconv(x, w, b):
    K = w.shape[0]
    out = lax.conv_general_dilated(x, w[:, None, :], window_strides=(1,),
                                   padding=[(K // 2, K // 2)],
                                   dimension_numbers=('NWC', 'WIO', 'NWC'),
                                   feature_group_count=x.shape[-1])
    return out + b


def _recurrent_core(zl, zc, cos, sin, ret_decay_logit, conv_w, conv_b, dt_bias, a_log, d_skip,
                    ssd_norm_w):
    log_gamma = jax.nn.log_sigmoid(ret_decay_logit)
    a_neg = -jnp.exp(a_log)
    r_per_g = SSD_HEADS // SSD_GROUPS

    def project(z, use_rope):
        B, n, _ = z.shape
        rq, rk, rv, rg, zz, xbc, dt = _split(z, REC_SPLITS)
        rq = rq.reshape(B, n, RET_HEADS, RET_QK_DIM)
        rk = rk.reshape(B, n, RET_HEADS, RET_QK_DIM) * RET_QK_DIM ** -0.5
        if use_rope:
            rq, rk = _rope(rq, cos, sin), _rope(rk, cos, sin)
        rv = rv.reshape(B, n, RET_HEADS, 1, RET_V_DIM)
        ret = [(rq, rk, rv, jnp.broadcast_to(log_gamma[d][:, None], (B, n, RET_HEADS, 1)))
               for d in (0, 1)]
        xbc = jax.nn.silu(_dwconv(xbc, conv_w, conv_b))
        xs, bm, cm = _split(xbc, (SSD_INNER, SSD_GROUPS * SSD_STATE, SSD_GROUPS * SSD_STATE))
        xs = xs.reshape(B, n, SSD_HEADS, SSD_HEAD_DIM)
        bm = bm.reshape(B, n, SSD_GROUPS, SSD_STATE)
        cm = cm.reshape(B, n, SSD_GROUPS, SSD_STATE)
        dt = jax.nn.softplus(dt.reshape(B, n, 2, SSD_HEADS) + dt_bias)
        ssd = [(cm, bm,
                (xs * dt[:, :, d, :, None]).reshape(B, n, SSD_GROUPS, r_per_g, SSD_HEAD_DIM),
                (dt[:, :, d] * a_neg[d]).reshape(B, n, SSD_GROUPS, r_per_g))
               for d in (0, 1)]
        return ret, ssd, rg, zz, xs

    lat_ret, lat_ssd, lat_rg, lat_z, lat_xs = project(zl, True)
    ctx_ret, ctx_ssd, ctx_rg, ctx_z, ctx_xs = project(zc, False)
    ret_l, ret_c = _bidir(lat_ret[0], ctx_ret[0], lat_ret[1], ctx_ret[1])
    ssd_l, ssd_c = _bidir(lat_ssd[0], ctx_ssd[0], lat_ssd[1], ctx_ssd[1])

    def combine(ret_y, ssd_y, rg, zz, xs):
        B, n = rg.shape[:2]
        r = _rmsnorm(ret_y.reshape(B, n, RET_HEADS, RET_V_DIM)).reshape(B, n, -1) * jax.nn.silu(rg)
        s = ssd_y.reshape(B, n, SSD_HEADS, SSD_HEAD_DIM) + d_skip[:, None] * xs
        s = s.reshape(B, n, -1) * jax.nn.silu(zz)
        s = _rmsnorm(s.reshape(B, n, SSD_GROUPS, -1)).reshape(B, n, -1) * ssd_norm_w
        return jnp.concatenate([r, s], axis=-1)

    return combine(ret_l, ssd_l, lat_rg, lat_z, lat_xs), combine(ret_c, ssd_c, ctx_rg, ctx_z, ctx_xs)


def _dispatch_tokens(h, aff):
    B, n, D = h.shape
    cap = EC_CAPACITY * n // N_EXPERTS
    g, idx = lax.top_k(jnp.swapaxes(aff, 1, 2), cap)
    xin = jax.vmap(lambda xs, i: xs[i])(h, idx)
    return xin, g, idx


def _combine_tokens(y, g, idx, n):
    D = y.shape[-1]
    y = y * g[..., None]
    return jax.vmap(lambda val, i: jax.ops.segment_sum(val.reshape(-1, D), i.reshape(-1),
                                                       num_segments=n))(y, idx)


def kernel(x, c, ctx, c_ctx, ada_w, ada_b, norm1_w, norm2_w, att_w_in, att_w_out, att_q_norm_w,
           att_k_norm_w, diff_lambda, diff_norm_w, rec_w_in, rec_w_out, ret_decay_logit, ssd_conv_w,
           ssd_conv_b, ssd_dt_bias, ssd_a_log, ssd_d_skip, ssd_norm_w, router_w, expert_w_gate,
           expert_w_up, expert_w_down, final_norm_w):
    B, n_lat, D = x.shape
    n_ctx = ctx.shape[1]
    n_lat_tiles = n_lat // ROW_TILE
    cos, sin = _rope_tables(n_lat)
    reps = PAIR // cos.shape[1]
    cos_t = jnp.concatenate([jnp.tile(cos, (1, reps)), jnp.ones((n_ctx, PAIR), F32)], axis=0)
    sin_t = jnp.concatenate([jnp.tile(jnp.concatenate([-sin, sin], axis=1), (1, reps // 2)),
                             jnp.zeros((n_ctx, PAIR), F32)], axis=0)
    gmat = jnp.kron(jnp.eye(GQA_HEADS, dtype=F32), jnp.ones((HEAD_DIM, HEAD_DIM), F32)).astype(BF16)

    n_cond = 24
    cvec = jnp.concatenate([c, c_ctx[None, :], jnp.zeros((n_cond - B - 1, D), F32)], axis=0)
    mods = _ada_modulation(cvec, ada_w, ada_b)

    X = jnp.concatenate([x, ctx], axis=1)
    for layer in range(DEPTH):
        i = layer // 2
        need_ctx = layer < DEPTH - 1
        mod_lat = mods[layer, :B].reshape(B, 1, 6, D)
        mod_ctx = jnp.broadcast_to(mods[layer, B].reshape(1, 1, 6, D), (B, 1, 6, D))
        modc = jnp.concatenate([mod_lat, mod_ctx], axis=1)

        if layer % 2 == 0:
            lambda_init = 0.8 - 0.6 * math.exp(-0.3 * layer)
            Q, K, V = _att_project(X, norm1_w[layer], modc, att_w_in[i].astype(BF16), cos_t, sin_t, gmat,
                                   jnp.tile(att_q_norm_w[i], GQA_HEADS).reshape(1, -1),
                                   jnp.tile(att_k_norm_w[i], GQA_KV_HEADS).reshape(1, -1), n_lat_tiles)
            X = _attention(X, Q, K, V, modc, diff_lambda[i], diff_norm_w[i], att_w_out[i].astype(BF16),
                           lambda_init, n_lat_tiles)
        else:
            X = _recurrent_layer(X, norm1_w[layer], modc, rec_w_in[i], rec_w_out[i], cos_t, sin_t,
                                 ret_decay_logit[i], ssd_conv_w[i], ssd_conv_b[i], ssd_dt_bias[i],
                                 ssd_a_log[i], ssd_d_skip[i], ssd_norm_w[i], n_lat_tiles)

        h2, aff = _mod_router(X, norm2_w[layer], modc, router_w[layer], n_lat_tiles)
        xin_l, g_l, idx_l = _dispatch_tokens(h2[:, :n_lat], aff[:, :n_lat])
        xin_c, g_c, idx_c = _dispatch_tokens(h2[:, n_lat:], aff[:, n_lat:])
        cap_l, cap_c = xin_l.shape[2], xin_c.shape[2]
        Xe = jnp.concatenate([jnp.swapaxes(xin_l, 0, 1).reshape(N_EXPERTS, B * cap_l, D),
                              jnp.swapaxes(xin_c, 0, 1).reshape(N_EXPERTS, B * cap_c, D)], axis=1)
        Ye = _expert_ffn(Xe.astype(BF16), expert_w_gate, expert_w_up, expert_w_down, layer).astype(F32)
        y_l = jnp.swapaxes(Ye[:, :B * cap_l].reshape(N_EXPERTS, B, cap_l, D), 0, 1)
        y_c = jnp.swapaxes(Ye[:, B * cap_l:].reshape(N_EXPERTS, B, cap_c, D), 0, 1)
        m_l = _combine_tokens(y_l, g_l, idx_l, n_lat)
        m_c = _combine_tokens(y_c, g_c, idx_c, n_ctx)
        M = jnp.concatenate([m_l, m_c], axis=1)
        gate = jnp.concatenate([jnp.broadcast_to(modc[:, 0:1, 5], (B, n_lat, D)),
                                jnp.broadcast_to(modc[:, 1:2, 5], (B, n_ctx, D))], axis=1)
        X = X + gate * M

    xl = X[:, :n_lat]
    return _rmsnorm(xl, final_norm_w)
```

```python
import functools
import math

import jax
import jax.numpy as jnp
import numpy as np
from jax import lax
from jax.experimental import pallas as pl
from jax.experimental.pallas import tpu as pltpu

D_MODEL = 1024
DEPTH = 4
GRID_W = 64
HEAD_DIM = 64
ROPE_THETA = 10000.0
Q_BLOCK = 128
CHUNK = 128
EPS = 1e-6
HALF_W = D_MODEL // 2
GQA_HEADS = HALF_W // HEAD_DIM
GQA_KV_HEADS = GQA_HEADS // 4
DIFF_HEADS = HALF_W // (2 * HEAD_DIM)
RET_HEADS = HALF_W // (2 * HEAD_DIM)
RET_QK_DIM = HEAD_DIM
RET_V_DIM = 2 * HEAD_DIM
SSD_HEAD_DIM = HEAD_DIM
SSD_HEADS = HALF_W // SSD_HEAD_DIM
SSD_GROUPS = 2
SSD_STATE = 128
SSD_INNER = SSD_HEADS * SSD_HEAD_DIM
SSD_XBC = SSD_INNER + 2 * SSD_GROUPS * SSD_STATE
N_EXPERTS = 16
EC_CAPACITY = 2
EXPERT_FF = ((8 * D_MODEL // 3 + 255) // 256) * 256

ATT_SPLITS = (GQA_HEADS * HEAD_DIM, GQA_KV_HEADS * HEAD_DIM, GQA_KV_HEADS * HEAD_DIM,
              DIFF_HEADS * 2 * HEAD_DIM, DIFF_HEADS * 2 * HEAD_DIM, DIFF_HEADS * 2 * HEAD_DIM)
REC_SPLITS = (RET_HEADS * RET_QK_DIM, RET_HEADS * RET_QK_DIM, RET_HEADS * RET_V_DIM,
              RET_HEADS * RET_V_DIM, SSD_INNER, SSD_XBC, 2 * SSD_HEADS)

ROW_TILE = 256
FF_TILE = 256
VMEM_LIMIT = 56 * 1024 * 1024
BF16 = jnp.bfloat16
F32 = jnp.float32


def _cparams(sem):
    return pltpu.CompilerParams(dimension_semantics=sem, vmem_limit_bytes=VMEM_LIMIT)


def _ada_kernel(c_ref, w_ref, b_ref, o_ref):
    c = c_ref[...]
    s = c * jax.nn.sigmoid(c)
    o_ref[0] = jnp.dot(s, w_ref[0], precision=lax.Precision.HIGHEST,
                       preferred_element_type=F32) + b_ref[0]


def _ada_modulation(cvec, ada_w, ada_b):
    R, D = cvec.shape
    n_out = ada_w.shape[-1]
    tn = 1536
    return pl.pallas_call(
        _ada_kernel,
        grid=(DEPTH, n_out // tn),
        in_specs=[pl.BlockSpec((R, D), lambda l, j: (0, 0)),
                  pl.BlockSpec((1, D, tn), lambda l, j: (l, 0, j)),
                  pl.BlockSpec((1, 1, tn), lambda l, j: (l, 0, j))],
        out_specs=pl.BlockSpec((1, R, tn), lambda l, j: (l, 0, j)),
        out_shape=jax.ShapeDtypeStruct((DEPTH, R, n_out), F32),
        compiler_params=_cparams(("arbitrary", "arbitrary")),
        name="ada_modulation",
    )(cvec, ada_w, ada_b.reshape(DEPTH, 1, n_out))


def _modulated(x, nw, mod_ref, shift_idx, scale_idx):
    ms = jnp.mean(x * x, axis=-1, keepdims=True)
    y = x * lax.rsqrt(ms + EPS) * nw
    return y * (1.0 + mod_ref[0, 0, scale_idx:scale_idx + 1, :]) + mod_ref[0, 0, shift_idx:shift_idx + 1, :]


def _modproj_kernel(x_ref, nw_ref, mod_ref, w_ref, o_ref, *, shift_idx, scale_idx):
    h = _modulated(x_ref[0], nw_ref[...], mod_ref, shift_idx, scale_idx)
    o_ref[0] = jnp.dot(h.astype(BF16), w_ref[...], preferred_element_type=F32)


def _mod_project(X, nw, modc, w_bf16, shift_idx, scale_idx, n_lat_tiles):
    B, N, D = X.shape
    n_out = w_bf16.shape[1]
    return pl.pallas_call(
        functools.partial(_modproj_kernel, shift_idx=shift_idx, scale_idx=scale_idx),
        grid=(B, N // ROW_TILE),
        in_specs=[pl.BlockSpec((1, ROW_TILE, D), lambda b, i: (b, i, 0)),
                  pl.BlockSpec((1, D), lambda b, i: (0, 0)),
                  pl.BlockSpec((1, 1, 6, D), lambda b, i: (b, i // n_lat_tiles, 0, 0)),
                  pl.BlockSpec((D, n_out), lambda b, i: (0, 0))],
        out_specs=pl.BlockSpec((1, ROW_TILE, n_out), lambda b, i: (b, i, 0)),
        out_shape=jax.ShapeDtypeStruct((B, N, n_out), F32),
        compiler_params=_cparams(("arbitrary", "arbitrary")),
        name="mod_project",
    )(X, nw.reshape(1, D), modc, w_bf16)


def _outproj_kernel(x_ref, y_ref, mod_ref, w_ref, o_ref, *, gate_idx):
    y = jnp.dot(y_ref[0].astype(BF16), w_ref[...], preferred_element_type=F32)
    o_ref[0] = x_ref[0] + mod_ref[0, 0, gate_idx:gate_idx + 1, :] * y


def _out_project(X, Y, modc, w_bf16, gate_idx, n_lat_tiles):
    B, N, D = X.shape
    K = Y.shape[-1]
    return pl.pallas_call(
        functools.partial(_outproj_kernel, gate_idx=gate_idx),
        grid=(B, N // ROW_TILE),
        in_specs=[pl.BlockSpec((1, ROW_TILE, D), lambda b, i: (b, i, 0)),
                  pl.BlockSpec((1, ROW_TILE, K), lambda b, i: (b, i, 0)),
                  pl.BlockSpec((1, 1, 6, D), lambda b, i: (b, i // n_lat_tiles, 0, 0)),
                  pl.BlockSpec((K, D), lambda b, i: (0, 0))],
        out_specs=pl.BlockSpec((1, ROW_TILE, D), lambda b, i: (b, i, 0)),
        out_shape=jax.ShapeDtypeStruct((B, N, D), F32),
        compiler_params=_cparams(("arbitrary", "arbitrary")),
        name="out_project",
    )(X, Y, modc, w_bf16)


PAIR = 2 * HEAD_DIM
N_PAIR_GROUPS = 2 * HALF_W // PAIR
ATT_KV_W = GQA_KV_HEADS * PAIR + DIFF_HEADS * PAIR


def _head_sumsq(x, g):
    x2 = x * x
    hi = x2.astype(BF16)
    lo = (x2 - hi.astype(F32)).astype(BF16)
    return jnp.dot(hi, g, preferred_element_type=F32) + jnp.dot(lo, g, preferred_element_type=F32)


def _rope_lanes(x, cos, sin_signed):
    W = x.shape[-1]
    half = HEAD_DIM // 2
    lane = lax.broadcasted_iota(jnp.int32, x.shape, 1) & (HEAD_DIM - 1)
    partner = jnp.where(lane < half, pltpu.roll(x, W - half, 1), pltpu.roll(x, half, 1))
    return x * cos + partner * sin_signed


def _att_project_kernel(x_ref, nw_ref, mod_ref, w_ref, cos_ref, sin_ref, g_ref, qw_ref, kw_ref,
                        q_ref, k_ref, v_ref):
    h = _modulated(x_ref[0], nw_ref[...], mod_ref, 0, 1)
    z = jnp.dot(h.astype(BF16), w_ref[...], preferred_element_type=F32)
    o_aq, o_ak, o_av, o_bq, o_bk, o_bv = np.cumsum((0,) + ATT_SPLITS[:-1]).tolist()
    n_aq, n_ak, n_bq = ATT_SPLITS[0], ATT_SPLITS[1], ATT_SPLITS[3]
    cos, sin = cos_ref[...], sin_ref[...]
    cos4 = jnp.concatenate([cos] * (n_aq // PAIR), axis=1)
    sin4 = jnp.concatenate([sin] * (n_aq // PAIR), axis=1)
    inv_d = 1.0 / HEAD_DIM
    q_scale = HEAD_DIM ** -0.5

    aq = z[:, o_aq:o_aq + n_aq]
    aq = aq * lax.rsqrt(_head_sumsq(aq, g_ref[...]) * inv_d + EPS) * qw_ref[...]
    aq = _rope_lanes(aq, cos4, sin4) * q_scale
    bq = _rope_lanes(z[:, o_bq:o_bq + n_bq], cos4, sin4) * q_scale
    q = jnp.concatenate([aq, bq], axis=1)
    lo = (lax.broadcasted_iota(jnp.int32, q.shape, 1) & (PAIR - 1)) < HEAD_DIM
    q_ref[0, :, 0:q.shape[1]] = jnp.where(lo, q, 0.0).astype(BF16)
    q_ref[0, :, q.shape[1]:2 * q.shape[1]] = jnp.where(lo, 0.0, q).astype(BF16)

    ak = z[:, o_ak:o_ak + n_ak]
    ak = ak * lax.rsqrt(_head_sumsq(ak, g_ref[0:n_ak, 0:n_ak]) * inv_d + EPS) * kw_ref[...]
    ak = _rope_lanes(ak, cos, sin)
    av = z[:, o_av:o_av + n_ak]
    lo_kv = lax.broadcasted_iota(jnp.int32, ak.shape, 1) < HEAD_DIM
    ak_sw, av_sw = pltpu.roll(ak, HEAD_DIM, 1), pltpu.roll(av, HEAD_DIM, 1)
    k_ref[0, :, 0:PAIR] = jnp.where(lo_kv, ak, ak_sw).astype(BF16)
    k_ref[0, :, PAIR:2 * PAIR] = jnp.where(lo_kv, ak_sw, ak).astype(BF16)
    v_ref[0, :, 0:PAIR] = jnp.where(lo_kv, av, av_sw).astype(BF16)
    v_ref[0, :, PAIR:2 * PAIR] = jnp.where(lo_kv, av_sw, av).astype(BF16)
    k_ref[0, :, 2 * PAIR:] = _rope_lanes(z[:, o_bk:o_bk + n_bq], cos4, sin4).astype(BF16)
    v_ref[0, :, 2 * PAIR:] = z[:, o_bv:o_bv + n_bq].astype(BF16)


def _att_project(X, nw, modc, w_bf16, cos_t, sin_t, gmat, qw_t, kw_t, n_lat_tiles):
    B, N, D = X.shape
    n_in = w_bf16.shape[1]
    q_w = 2 * N_PAIR_GROUPS * PAIR
    row = lambda b, i: (b, i, 0)
    const2 = lambda b, i: (0, 0)
    return pl.pallas_call(
        _att_project_kernel,
        grid=(B, N // ROW_TILE),
        in_specs=[pl.BlockSpec((1, ROW_TILE, D), row),
                  pl.BlockSpec((1, D), const2),
                  pl.BlockSpec((1, 1, 6, D), lambda b, i: (b, i // n_lat_tiles, 0, 0)),
                  pl.BlockSpec((D, n_in), const2),
                  pl.BlockSpec((ROW_TILE, PAIR), lambda b, i: (i, 0)),
                  pl.BlockSpec((ROW_TILE, PAIR), lambda b, i: (i, 0)),
                  pl.BlockSpec(gmat.shape, const2),
                  pl.BlockSpec(qw_t.shape, const2),
                  pl.BlockSpec(kw_t.shape, const2)],
        out_specs=[pl.BlockSpec((1, ROW_TILE, q_w), row),
                   pl.BlockSpec((1, ROW_TILE, ATT_KV_W), row),
                   pl.BlockSpec((1, ROW_TILE, ATT_KV_W), row)],
        out_shape=[jax.ShapeDtypeStruct((B, N, q_w), BF16),
                   jax.ShapeDtypeStruct((B, N, ATT_KV_W), BF16),
                   jax.ShapeDtypeStruct((B, N, ATT_KV_W), BF16)],
        compiler_params=_cparams(("arbitrary", "arbitrary")),
        name="att_project",
    )(X, nw.reshape(1, D), modc, w_bf16, cos_t, sin_t, gmat, qw_t, kw_t)


def _attention_kernel(x_ref, q_ref, k_ref, v_ref, mod_ref, lam_ref, dnw_ref, w_ref, o_ref,
                      s_scr, p_scr, y_scr, *, lambda_init, n_lat_tiles):
    n_lat = n_lat_tiles * ROW_TILE
    n_all = k_ref.shape[1]
    q_half = N_PAIR_GROUPS * PAIR
    lam = lam_ref[...]
    lam_val = (jnp.exp(jnp.sum(lam[0:1] * lam[1:2], axis=-1, keepdims=True))
               - jnp.exp(jnp.sum(lam[2:3] * lam[3:4], axis=-1, keepdims=True)) + lambda_init)

    def attend(k0, nk):
        keys = slice(k0, k0 + nk)

        def unit(pg, half):
            qv = q_ref[0, :, half * q_half + pg * PAIR: half * q_half + (pg + 1) * PAIR]
            kd = k_ref[0, keys, _kv_lane(pg): _kv_lane(pg) + PAIR]
            vd = v_ref[0, keys, _kv_lane(pg): _kv_lane(pg) + PAIR]
            s_scr[:, 0:nk] = lax.dot_general(qv, kd, (((1,), (1,)), ((), ())),
                                             preferred_element_type=F32)
            s = s_scr[:, 0:nk]
            p = jnp.exp(s - jnp.max(s, axis=-1, keepdims=True))
            p_scr[:, 0:nk] = p.astype(BF16)
            o = jnp.dot(p_scr[:, 0:nk], vd, preferred_element_type=F32)
            return o * (1.0 / jnp.sum(p, axis=-1, keepdims=True))

        lo = lax.broadcasted_iota(jnp.int32, (ROW_TILE, PAIR), 1) < HEAD_DIM
        for pg in range(2 * GQA_KV_HEADS):
            y_scr[:, pg * PAIR:(pg + 1) * PAIR] = jnp.where(lo, unit(pg, 0), unit(pg, 1)).astype(BF16)
        for pg in range(2 * GQA_KV_HEADS, N_PAIR_GROUPS):
            od = unit(pg, 0) - lam_val * unit(pg, 1)
            od = od * lax.rsqrt(jnp.mean(od * od, axis=-1, keepdims=True) + EPS)
            od = od * dnw_ref[...] * (1.0 - lambda_init)
            y_scr[:, pg * PAIR:(pg + 1) * PAIR] = od.astype(BF16)
        y = jnp.dot(y_scr[...], w_ref[...], preferred_element_type=F32)
        o_ref[0] = x_ref[0] + mod_ref[0, 0, 2:3, :] * y

    i = pl.program_id(1)

    @pl.when(i < n_lat_tiles)
    def _():
        attend(0, n_all)

    @pl.when(i >= n_lat_tiles)
    def _():
        attend(n_lat, n_all - n_lat)


def _kv_lane(pg):
    n_gqa_pairs = 2 * GQA_KV_HEADS
    if pg < n_gqa_pairs:
        return (pg // 2) * PAIR
    return (GQA_KV_HEADS + pg - n_gqa_pairs) * PAIR


def _attention(X, Q, K, V, modc, lam, dnw, w_bf16, lambda_init, n_lat_tiles):
    B, N, D = X.shape
    row = lambda b, i: (b, i, 0)
    whole = lambda b, i: (b, 0, 0)
    const2 = lambda b, i: (0, 0)
    return pl.pallas_call(
        functools.partial(_attention_kernel, lambda_init=lambda_init, n_lat_tiles=n_lat_tiles),
        grid=(B, N // ROW_TILE),
        in_specs=[pl.BlockSpec((1, ROW_TILE, D), row),
                  pl.BlockSpec((1, ROW_TILE, Q.shape[-1]), row),
                  pl.BlockSpec((1, N, K.shape[-1]), whole),
                  pl.BlockSpec((1, N, V.shape[-1]), whole),
                  pl.BlockSpec((1, 1, 6, D), lambda b, i: (b, i // n_lat_tiles, 0, 0)),
                  pl.BlockSpec(lam.shape, const2),
                  pl.BlockSpec((1, PAIR), const2),
                  pl.BlockSpec(w_bf16.shape, const2)],
        out_specs=pl.BlockSpec((1, ROW_TILE, D), row),
        out_shape=jax.ShapeDtypeStruct((B, N, D), F32),
        scratch_shapes=[pltpu.VMEM((ROW_TILE, N), F32), pltpu.VMEM((ROW_TILE, N), BF16),
                        pltpu.VMEM((ROW_TILE, w_bf16.shape[0]), BF16)],
        compiler_params=_cparams(("arbitrary", "arbitrary")),
        name="attention",
    )(X, Q, K, V, modc, lam, dnw.reshape(1, PAIR), w_bf16)


REC_IN_PAD = 3200
N_SSD_DT = 2 * SSD_HEADS
RET_LANE0 = N_SSD_DT


def _softplus(x):
    return jnp.maximum(x, 0.0) + jnp.log(1.0 + jnp.exp(-jnp.abs(x)))


def _silu(x):
    return x * jax.nn.sigmoid(x)


def _rec_project_kernel(x_ref, nw_ref, mod_ref, w_ref, cos_ref, sin_ref, dtb_ref,
                        rq_ref, rk_ref, rv_ref, g_ref, xbc_ref, dt_ref):
    h = _modulated(x_ref[0], nw_ref[...], mod_ref, 0, 1)
    z = jnp.dot(h.astype(BF16), w_ref[...], preferred_element_type=F32)
    o_rq, o_rk, o_rv, o_rg, o_z, o_xbc, o_dt = np.cumsum((0,) + REC_SPLITS[:-1]).tolist()
    n_qk = REC_SPLITS[0]
    cos, sin = cos_ref[...], sin_ref[...]
    cos2 = jnp.concatenate([cos] * (n_qk // PAIR), axis=1)
    sin2 = jnp.concatenate([sin] * (n_qk // PAIR), axis=1)
    rq = _rope_lanes(z[:, o_rq:o_rq + n_qk], cos2, sin2)
    lo = (lax.broadcasted_iota(jnp.int32, rq.shape, 1) & (PAIR - 1)) < HEAD_DIM
    rq_ref[0, :, 0:n_qk] = jnp.where(lo, rq, 0.0).astype(BF16)
    rq_ref[0, :, n_qk:2 * n_qk] = jnp.where(lo, 0.0, rq).astype(BF16)
    rk_ref[0] = _rope_lanes(z[:, o_rk:o_rk + n_qk] * (RET_QK_DIM ** -0.5), cos2, sin2).astype(BF16)
    rv_ref[0] = z[:, o_rv:o_rg].astype(BF16)
    g_ref[0] = z[:, o_rg:o_xbc]
    xbc_ref[0] = z[:, o_xbc:o_dt]
    dt_ref[0] = _softplus(z[:, o_dt:o_dt + PAIR] + dtb_ref[...])


def _rec_project(X, nw, modc, w_bf16, cos_t, sin_t, dtb_row, n_lat_tiles):
    B, N, D = X.shape
    n_in = w_bf16.shape[1]
    row = lambda b, i: (b, i, 0)
    const2 = lambda b, i: (0, 0)
    widths = (2 * REC_SPLITS[0], REC_SPLITS[1], REC_SPLITS[2], REC_SPLITS[3] + REC_SPLITS[4],
              REC_SPLITS[5], PAIR)
    dtypes = (BF16, BF16, BF16, F32, F32, F32)
    return pl.pallas_call(
        _rec_project_kernel,
        grid=(B, N // ROW_TILE),
        in_specs=[pl.BlockSpec((1, ROW_TILE, D), row),
                  pl.BlockSpec((1, D), const2),
                  pl.BlockSpec((1, 1, 6, D), lambda b, i: (b, i // n_lat_tiles, 0, 0)),
                  pl.BlockSpec((D, n_in), const2),
                  pl.BlockSpec((ROW_TILE, PAIR), lambda b, i: (i, 0)),
                  pl.BlockSpec((ROW_TILE, PAIR), lambda b, i: (i, 0)),
                  pl.BlockSpec((1, PAIR), const2)],
        out_specs=[pl.BlockSpec((1, ROW_TILE, w), row) for w in widths],
        out_shape=[jax.ShapeDtypeStruct((B, N, w), dt) for w, dt in zip(widths, dtypes)],
        compiler_params=_cparams(("arbitrary", "arbitrary")),
        name="rec_project",
    )(X, nw.reshape(1, D), modc, w_bf16, cos_t, sin_t, dtb_row)


def _conv_kernel(x_ref, w_ref, b_ref, o_ref, *, n_lat):
    x = x_ref[0]
    n = x.shape[0]
    t = lax.broadcasted_iota(jnp.int32, x.shape, 0)
    first = (t == 0) | (t == n_lat)
    last = (t == n_lat - 1) | (t == n - 1)
    prev = jnp.where(first, 0.0, pltpu.roll(x, 1, 0))
    nxt = jnp.where(last, 0.0, pltpu.roll(x, n - 1, 0))
    y = prev * w_ref[0:1, :] + x * w_ref[1:2, :] + nxt * w_ref[2:3, :] + b_ref[...]
    o_ref[0] = _silu(y)


def _ssd_conv(XBC, conv_w, conv_b, n_lat):
    B, N, C = XBC.shape
    tc = 256
    return pl.pallas_call(
        functools.partial(_conv_kernel, n_lat=n_lat),
        grid=(B, C // tc),
        in_specs=[pl.BlockSpec((1, N, tc), lambda b, j: (b, 0, j)),
                  pl.BlockSpec((conv_w.shape[0], tc), lambda b, j: (0, j)),
                  pl.BlockSpec((1, tc), lambda b, j: (0, j))],
        out_specs=pl.BlockSpec((1, N, tc), lambda b, j: (b, 0, j)),
        out_shape=jax.ShapeDtypeStruct((B, N, C), F32),
        compiler_params=_cparams(("arbitrary", "arbitrary")),
        name="ssd_conv",
    )(XBC, conv_w, conv_b.reshape(1, C))


def _split3(x):
    hi = x.astype(BF16)
    r1 = x - hi.astype(F32)
    mid = r1.astype(BF16)
    lo = (r1 - mid.astype(F32)).astype(BF16)
    return hi, mid, lo


def _scan_kernel(rq_ref, rk_ref, rv_ref, a_ref, dt_ref, alog_ref, logit_ref, y_ref, hret, hssd, *, reverse):
    T = rq_ref.shape[1]
    d = 1 if reverse else 0
    step = pl.program_id(1)

    @pl.when(step == 0)
    def _():
        hret[...] = jnp.zeros_like(hret)
        hssd[...] = jnp.zeros_like(hssd)

    lane = lax.broadcasted_iota(jnp.int32, (1, PAIR), 1)
    a_neg = jnp.where(lane < N_SSD_DT, -jnp.exp(alog_ref[...]), 0.0)
    logit = logit_ref[...]
    log_gamma = jnp.where((lane >= RET_LANE0) & (lane < RET_LANE0 + 2 * RET_HEADS),
                          jnp.minimum(logit, 0.0) - jnp.log(1.0 + jnp.exp(-jnp.abs(logit))), 0.0)
    dt = dt_ref[0]
    la = dt * a_neg + log_gamma

    ti = lax.broadcasted_iota(jnp.int32, (T, T), 0)
    tj = lax.broadcasted_iota(jnp.int32, (T, T), 1)
    causal = (tj >= ti) if reverse else (tj <= ti)
    tri = jnp.where(causal, 1.0, 0.0).astype(BF16)
    hi, mid, lo3 = _split3(la)
    P = (jnp.dot(tri, hi, preferred_element_type=F32) + jnp.dot(tri, mid, preferred_element_type=F32)
         + jnp.dot(tri, lo3, preferred_element_type=F32))
    PT = P.T
    tot = P[0:1, :] if reverse else P[T - 1:T, :]
    E = jnp.exp(P)
    KD = jnp.exp(tot - P)
    ET = jnp.exp(tot)
    lo = lax.broadcasted_iota(jnp.int32, (T, PAIR), 1) < HEAD_DIM
    lo_row = lane < HEAD_DIM
    nt = (((1,), (1,)), ((), ()))
    tn = (((0,), (0,)), ((), ()))

    def decay_matrix(c):
        diff = jnp.minimum(P[:, c:c + 1] - PT[c:c + 1, :], 0.0)
        return jnp.where(causal, jnp.exp(diff), 0.0)

    n_qk = RET_HEADS * RET_QK_DIM
    for h in range(RET_HEADS):
        c = RET_LANE0 + d * RET_HEADS + h
        p, half = h // 2, h % 2
        qv = rq_ref[0, :, half * n_qk + p * PAIR: half * n_qk + (p + 1) * PAIR]
        kp = rk_ref[0, :, p * PAIR:(p + 1) * PAIR]
        v = rv_ref[0, :, h * RET_V_DIM:(h + 1) * RET_V_DIM]
        s = lax.dot_general(qv, kp, nt, preferred_element_type=F32)
        aw = (s * decay_matrix(c)).astype(BF16)
        y = jnp.dot(aw, v, preferred_element_type=F32)
        y = y + E[:, c:c + 1] * jnp.dot(qv, hret[h].astype(BF16), preferred_element_type=F32)
        y_ref[0, :, h * RET_V_DIM:(h + 1) * RET_V_DIM] = y
        kdec = (kp.astype(F32) * KD[:, c:c + 1]).astype(BF16)
        hret[h] = hret[h] * ET[:, c:c + 1] + lax.dot_general(kdec, v, tn, preferred_element_type=F32)

    y0 = RET_HEADS * RET_V_DIM
    r_per_g = SSD_HEADS // SSD_GROUPS
    o_b = SSD_INNER
    o_c = SSD_INNER + SSD_GROUPS * SSD_STATE
    for g in range(SSD_GROUPS):
        cg = a_ref[0, :, o_c + g * SSD_STATE: o_c + (g + 1) * SSD_STATE].astype(BF16)
        bg = a_ref[0, :, o_b + g * SSD_STATE: o_b + (g + 1) * SSD_STATE].astype(BF16)
        s = lax.dot_general(cg, bg, nt, preferred_element_type=F32)
        ch = jnp.dot(cg, hssd[g].astype(BF16), preferred_element_type=F32)
        vdec, dec_rows = [], []
        for p in range(r_per_g // 2):
            pg = g * (r_per_g // 2) + p
            xs = a_ref[0, :, pg * PAIR:(pg + 1) * PAIR]
            ys, vds, cols = [], [], []
            for half in range(2):
                c = d * SSD_HEADS + 2 * pg + half
                aw = (s * decay_matrix(c)).astype(BF16)
                vh = xs * dt[:, c:c + 1]
                ys.append(jnp.dot(aw, vh.astype(BF16), preferred_element_type=F32))
                vds.append(vh * KD[:, c:c + 1])
                cols.append(c)
            e_pair = jnp.where(lo, E[:, cols[0]:cols[0] + 1], E[:, cols[1]:cols[1] + 1])
            y_ref[0, :, y0 + pg * PAIR: y0 + (pg + 1) * PAIR] = (
                jnp.where(lo, ys[0], ys[1]) + e_pair * ch[:, p * PAIR:(p + 1) * PAIR])
            vdec.append(jnp.where(lo, vds[0], vds[1]).astype(BF16))
            dec_rows.append(jnp.where(lo_row, ET[:, cols[0]:cols[0] + 1], ET[:, cols[1]:cols[1] + 1]))
        hssd[g] = (hssd[g] * jnp.concatenate(dec_rows, axis=1)
                   + lax.dot_general(bg, jnp.concatenate(vdec, axis=1), tn, preferred_element_type=F32))


def _bidir_scan(RQ, RK, RV, A, DT, alog_row, logit_row, n_lat_tiles, reverse):
    B, N, _ = RQ.shape
    n_tiles = N // ROW_TILE

    def tile(i):
        lat = (n_lat_tiles - i) if reverse else (i - 1)
        return jnp.where(i == 0, n_tiles - 1, lat)

    row = lambda b, i: (b, tile(i), 0)
    const2 = lambda b, i: (0, 0)
    y_w = RET_HEADS * RET_V_DIM + SSD_INNER
    return pl.pallas_call(
        functools.partial(_scan_kernel, reverse=reverse),
        grid=(B, n_tiles),
        in_specs=[pl.BlockSpec((1, ROW_TILE, RQ.shape[-1]), row),
                  pl.BlockSpec((1, ROW_TILE, RK.shape[-1]), row),
                  pl.BlockSpec((1, ROW_TILE, RV.shape[-1]), row),
                  pl.BlockSpec((1, ROW_TILE, A.shape[-1]), row),
                  pl.BlockSpec((1, ROW_TILE, PAIR), row),
                  pl.BlockSpec((1, PAIR), const2),
                  pl.BlockSpec((1, PAIR), const2)],
        out_specs=pl.BlockSpec((1, ROW_TILE, y_w), row),
        out_shape=jax.ShapeDtypeStruct((B, N, y_w), F32),
        scratch_shapes=[pltpu.VMEM((RET_HEADS, PAIR, RET_V_DIM), F32),
                        pltpu.VMEM((SSD_GROUPS, SSD_STATE, SSD_INNER // SSD_GROUPS), F32)],
        compiler_params=_cparams(("arbitrary", "arbitrary")),
        name="scan_bwd" if reverse else "scan_fwd",
    )(RQ, RK, RV, A, DT, alog_row, logit_row)


def _rec_combine_kernel(x_ref, yf_ref, yb_ref, g_ref, a_ref, mod_ref, dskip_ref, snw_ref, w_ref, o_ref, y_scr):
    n_ret = RET_HEADS * RET_V_DIM
    y = yf_ref[0] + yb_ref[0]
    gates = g_ref[0]
    for h in range(RET_HEADS):
        sl = slice(h * RET_V_DIM, (h + 1) * RET_V_DIM)
        r = y[:, sl]
        r = r * lax.rsqrt(jnp.mean(r * r, axis=-1, keepdims=True) + EPS)
        y_scr[:, sl] = (r * _silu(gates[:, sl])).astype(BF16)
    gw = SSD_INNER // SSD_GROUPS
    for g in range(SSD_GROUPS):
        sl = slice(g * gw, (g + 1) * gw)
        s = y[:, n_ret + g * gw: n_ret + (g + 1) * gw] + dskip_ref[:, sl] * a_ref[0, :, sl]
        s = s * _silu(gates[:, n_ret + g * gw: n_ret + (g + 1) * gw])
        s = s * lax.rsqrt(jnp.mean(s * s, axis=-1, keepdims=True) + EPS) * snw_ref[:, sl]
        y_scr[:, n_ret + g * gw: n_ret + (g + 1) * gw] = s.astype(BF16)
    o_ref[0] = x_ref[0] + mod_ref[0, 0, 2:3, :] * jnp.dot(y_scr[...], w_ref[...], preferred_element_type=F32)


def _rec_combine(X, Yf, Yb, G, A, modc, dskip_row, snw_row, w_bf16, n_lat_tiles):
    B, N, D = X.shape
    row = lambda b, i: (b, i, 0)
    const2 = lambda b, i: (0, 0)
    return pl.pallas_call(
        _rec_combine_kernel,
        grid=(B, N // ROW_TILE),
        in_specs=[pl.BlockSpec((1, ROW_TILE, D), row),
                  pl.BlockSpec((1, ROW_TILE, Yf.shape[-1]), row),
                  pl.BlockSpec((1, ROW_TILE, Yb.shape[-1]), row),
                  pl.BlockSpec((1, ROW_TILE, G.shape[-1]), row),
                  pl.BlockSpec((1, ROW_TILE, SSD_INNER), row),
                  pl.BlockSpec((1, 1, 6, D), lambda b, i: (b, i // n_lat_tiles, 0, 0)),
                  pl.BlockSpec((1, SSD_INNER), const2),
                  pl.BlockSpec((1, SSD_INNER), const2),
                  pl.BlockSpec(w_bf16.shape, const2)],
        out_specs=pl.BlockSpec((1, ROW_TILE, D), row),
        out_shape=jax.ShapeDtypeStruct((B, N, D), F32),
        scratch_shapes=[pltpu.VMEM((ROW_TILE, w_bf16.shape[0]), BF16)],
        compiler_params=_cparams(("arbitrary", "arbitrary")),
        name="rec_combine",
    )(X, Yf, Yb, G, A, modc, dskip_row, snw_row, w_bf16)


def _recurrent_layer(X, nw, modc, w_in, w_out, cos_t, sin_t, ret_decay_logit, conv_w, conv_b, dt_bias,
                     a_log, d_skip, ssd_norm_w, n_lat_tiles):
    D = X.shape[-1]
    w_pad = jnp.pad(w_in, ((0, 0), (0, REC_IN_PAD - w_in.shape[1]))).astype(BF16)
    pad_row = lambda v, at: jnp.pad(v.reshape(1, -1), ((0, 0), (at, PAIR - at - v.size)))
    RQ, RK, RV, G, XBC, DT = _rec_project(X, nw, modc, w_pad, cos_t, sin_t, pad_row(dt_bias, 0), n_lat_tiles)
    A = _ssd_conv(XBC, conv_w, conv_b, n_lat_tiles * ROW_TILE)
    alog_row, logit_row = pad_row(a_log, 0), pad_row(ret_decay_logit, RET_LANE0)
    Yf = _bidir_scan(RQ, RK, RV, A, DT, alog_row, logit_row, n_lat_tiles, False)
    Yb = _bidir_scan(RQ, RK, RV, A, DT, alog_row, logit_row, n_lat_tiles, True)
    return _rec_combine(X, Yf, Yb, G, A, modc, jnp.repeat(d_skip, SSD_HEAD_DIM).reshape(1, -1),
                        ssd_norm_w.reshape(1, -1), w_out.astype(BF16), n_lat_tiles)


SEL_CHUNK = 256
COMBINE_TILE = 256


def _router_kernel(x_ref, nw_ref, mod_ref, rwt_ref, h_ref, aff_ref):
    h = _modulated(x_ref[0], nw_ref[...], mod_ref, 3, 4)
    h_ref[0] = h.astype(BF16)
    logits = lax.dot_general(rwt_ref[...], h, (((1,), (1,)), ((), ())),
                             precision=lax.Precision.HIGHEST, preferred_element_type=F32)
    e = jnp.exp(logits - jnp.max(logits, axis=0, keepdims=True))
    aff_ref[0] = e / jnp.sum(e, axis=0, keepdims=True)


def _mod_router(X, nw, modc, router_wt, n_lat_tiles):
    B, N, D = X.shape
    E = router_wt.shape[0]
    return pl.pallas_call(
        _router_kernel,
        grid=(B, N // ROW_TILE),
        in_specs=[pl.BlockSpec((1, ROW_TILE, D), lambda b, i: (b, i, 0)),
                  pl.BlockSpec((1, D), lambda b, i: (0, 0)),
                  pl.BlockSpec((1, 1, 6, D), lambda b, i: (b, i // n_lat_tiles, 0, 0)),
                  pl.BlockSpec((E, D), lambda b, i: (0, 0))],
        out_specs=[pl.BlockSpec((1, ROW_TILE, D), lambda b, i: (b, i, 0)),
                   pl.BlockSpec((1, E, ROW_TILE), lambda b, i: (b, 0, i))],
        out_shape=[jax.ShapeDtypeStruct((B, N, D), BF16),
                   jax.ShapeDtypeStruct((B, E, N), F32)],
        compiler_params=_cparams(("arbitrary", "arbitrary")),
        name="mod_router",
    )(X, nw.reshape(1, D), modc, router_wt)


def _count(mask_f32):
    return jnp.sum(mask_f32, axis=-1, keepdims=True)


def _select_top(a, cap):
    E, n = a.shape
    v = pltpu.bitcast(a, jnp.int32)
    thr = jnp.zeros((E, 1), jnp.int32)
    for bit in range(30, -1, -1):
        cand = thr | (1 << bit)
        thr = jnp.where(_count(jnp.where(v >= cand, 1.0, 0.0)) >= cap, cand, thr)
    gt = v > thr
    eq = v == thr
    need = cap - _count(jnp.where(gt, 1.0, 0.0))
    idx = lax.broadcasted_iota(jnp.int32, (E, n), 1)
    last = jnp.zeros((E, 1), jnp.int32)
    for bit in range(n.bit_length() - 2, -1, -1):
        cand = last | (1 << bit)
        below = _count(jnp.where(eq, jnp.where(idx < cand, 1.0, 0.0), 0.0))
        last = jnp.where(below < need, cand, last)
    sel = jnp.where(gt, 1.0, jnp.where(eq, jnp.where(idx <= last, 1.0, 0.0), 0.0))
    si = lax.broadcasted_iota(jnp.int32, (SEL_CHUNK, SEL_CHUNK), 0)
    sj = lax.broadcasted_iota(jnp.int32, (SEL_CHUNK, SEL_CHUNK), 1)
    before = jnp.where(si < sj, 1.0, 0.0).astype(BF16)
    base = jnp.zeros((E, 1), F32)
    pos = []
    for k in range(n // SEL_CHUNK):
        sk = sel[:, k * SEL_CHUNK:(k + 1) * SEL_CHUNK]
        pos.append(jnp.dot(sk.astype(BF16), before, preferred_element_type=F32) + base)
        base = base + _count(sk)
    pos = jnp.concatenate(pos, axis=1) if len(pos) > 1 else pos[0]
    return jnp.where(sel > 0.0, pos, -1.0), jnp.where(sel > 0.0, a, 0.0)


def _select_kernel(aff_ref, slot_ref, slot_t_ref, gate_t_ref, *, n_lat, cap_lat, cap_ctx):
    n_all = aff_ref.shape[2]
    E = aff_ref.shape[1]
    pad_rows = PAIR - E
    for lo, hi, cap in ((0, n_lat, cap_lat), (n_lat, n_all, cap_ctx)):
        slot, gate = _select_top(aff_ref[0, :, lo:hi], cap)
        slot_ref[0, :, lo:hi] = slot
        slot_t_ref[0, lo:hi, :] = jnp.concatenate([slot, jnp.full((pad_rows, hi - lo), -1.0, F32)], axis=0).T
        gate_t_ref[0, lo:hi, :] = jnp.concatenate([gate, jnp.zeros((pad_rows, hi - lo), F32)], axis=0).T


def _ec_select(aff, n_lat, cap_lat, cap_ctx):
    B, E, N = aff.shape
    return pl.pallas_call(
        functools.partial(_select_kernel, n_lat=n_lat, cap_lat=cap_lat, cap_ctx=cap_ctx),
        grid=(B,),
        in_specs=[pl.BlockSpec((1, E, N), lambda b: (b, 0, 0))],
        out_specs=[pl.BlockSpec((1, E, N), lambda b: (b, 0, 0)),
                   pl.BlockSpec((1, N, PAIR), lambda b: (b, 0, 0)),
                   pl.BlockSpec((1, N, PAIR), lambda b: (b, 0, 0))],
        out_shape=[jax.ShapeDtypeStruct((B, E, N), F32),
                   jax.ShapeDtypeStruct((B, N, PAIR), F32),
                   jax.ShapeDtypeStruct((B, N, PAIR), F32)],
        compiler_params=_cparams(("arbitrary",)),
        name="ec_select",
    )(aff)


def _gather_kernel(slot_ref, h_ref, xl_ref, xc_ref, *, n_lat):
    e = pl.program_id(1)
    n_all = h_ref.shape[1]
    for lo, hi, x_ref in ((0, n_lat, xl_ref), (n_lat, n_all, xc_ref)):
        cap = x_ref.shape[1]
        slot = slot_ref[0, pl.ds(e, 1), lo:hi]
        c = lax.broadcasted_iota(jnp.int32, (cap, hi - lo), 0).astype(F32)
        onehot = jnp.where(slot == c, 1.0, 0.0).astype(BF16)
        x_ref[0] = jnp.dot(onehot, h_ref[0, lo:hi, :], preferred_element_type=F32).astype(BF16)


def _ec_gather(slot, H, n_lat, cap_lat, cap_ctx):
    B, E, N = slot.shape
    D = H.shape[-1]
    return pl.pallas_call(
        functools.partial(_gather_kernel, n_lat=n_lat),
        grid=(B, E),
        in_specs=[pl.BlockSpec((1, E, N), lambda b, e: (b, 0, 0)),
                  pl.BlockSpec((1, N, D), lambda b, e: (b, 0, 0))],
        out_specs=[pl.BlockSpec((1, cap_lat, D), lambda b, e: (e, b, 0)),
                   pl.BlockSpec((1, cap_ctx, D), lambda b, e: (e, b, 0))],
        out_shape=[jax.ShapeDtypeStruct((E, B * cap_lat, D), BF16),
                   jax.ShapeDtypeStruct((E, B * cap_ctx, D), BF16)],
        compiler_params=_cparams(("arbitrary", "arbitrary")),
        name="ec_gather",
    )(slot, H)


def _ffn_kernel(*refs, n_groups, row_chunk, unroll):
    x_refs = refs[:n_groups]
    wg_ref, wu_ref, wd_ref = refs[n_groups:n_groups + 3]
    o_refs = refs[n_groups + 3:2 * n_groups + 3]
    acc_refs = refs[2 * n_groups + 3:3 * n_groups + 3]
    wg_s, wu_s, wd_s = refs[3 * n_groups + 3:]
    f = pl.program_id(2)
    n_f = pl.num_programs(2)
    wg_s[...] = wg_ref[0, 0].astype(BF16)
    wu_s[...] = wu_ref[0, 0].astype(BF16)
    wd_s[...] = wd_ref[0, 0].astype(BF16)

    @pl.when(f == 0)
    def _():
        for acc_ref in acc_refs:
            acc_ref[...] = jnp.zeros_like(acc_ref)

    def chunk(x_ref, acc_ref, start, size):
        rows = pl.ds(start, size)
        xr = x_ref[0, rows, :]
        a = jnp.dot(xr, wg_s[...], preferred_element_type=F32)
        u = jnp.dot(xr, wu_s[...], preferred_element_type=F32)
        hm = (a * jax.nn.sigmoid(a) * u).astype(BF16)
        acc_ref[rows, :] += jnp.dot(hm, wd_s[...], preferred_element_type=F32)

    for x_ref, acc_ref in zip(x_refs, acc_refs):
        n_rows = x_ref.shape[1]
        if n_rows <= row_chunk * unroll:
            for start in range(0, n_rows, row_chunk):
                chunk(x_ref, acc_ref, start, min(row_chunk, n_rows - start))
        else:
            def rows_body(r, carry, x_ref=x_ref, acc_ref=acc_ref):
                for j in range(unroll):
                    chunk(x_ref, acc_ref, pl.multiple_of((r * unroll + j) * row_chunk, row_chunk), row_chunk)
                return carry
            lax.fori_loop(0, n_rows // (row_chunk * unroll), rows_body, 0)

    @pl.when(f == n_f - 1)
    def _():
        for o_ref, acc_ref in zip(o_refs, acc_refs):
            o_ref[0] = acc_ref[...].astype(o_ref.dtype)


def _expert_ffn(xs, w_gate, w_up, w_down, layer):
    E, _, D = xs[0].shape
    F = w_gate.shape[-1]
    m_tiles = 2
    tms = [x.shape[1] // m_tiles for x in xs]
    n = len(xs)
    return pl.pallas_call(
        functools.partial(_ffn_kernel, n_groups=n, row_chunk=256, unroll=2),
        grid=(E, m_tiles, F // FF_TILE),
        in_specs=[pl.BlockSpec((1, tm, D), lambda e, m, f: (e, m, 0)) for tm in tms]
        + [pl.BlockSpec((1, 1, D, FF_TILE), lambda e, m, f: (layer, e, 0, f)),
           pl.BlockSpec((1, 1, D, FF_TILE), lambda e, m, f: (layer, e, 0, f)),
           pl.BlockSpec((1, 1, FF_TILE, D), lambda e, m, f: (layer, e, f, 0))],
        out_specs=[pl.BlockSpec((1, tm, D), lambda e, m, f: (e, m, 0)) for tm in tms],
        out_shape=[jax.ShapeDtypeStruct(x.shape, BF16) for x in xs],
        scratch_shapes=[pltpu.VMEM((tm, D), F32) for tm in tms]
        + [pltpu.VMEM((D, FF_TILE), BF16), pltpu.VMEM((D, FF_TILE), BF16), pltpu.VMEM((FF_TILE, D), BF16)],
        compiler_params=_cparams(("arbitrary", "arbitrary", "arbitrary")),
        name="expert_ffn",
    )(*xs, w_gate, w_up, w_down)


def _combine_kernel(x_ref, slot_t_ref, gate_t_ref, yl_ref, yc_ref, mod_ref, o_ref, *, n_lat_tiles):
    E = yl_ref.shape[0]

    def scatter(y_ref):
        cap = y_ref.shape[1]
        c = lax.broadcasted_iota(jnp.int32, (COMBINE_TILE, cap), 1).astype(F32)
        acc = jnp.zeros((COMBINE_TILE, x_ref.shape[2]), F32)
        for e in range(E):
            onehot = jnp.where(slot_t_ref[0, :, e:e + 1] == c, 1.0, 0.0).astype(BF16)
            acc = acc + gate_t_ref[0, :, e:e + 1] * jnp.dot(onehot, y_ref[e], preferred_element_type=F32)
        o_ref[0] = x_ref[0] + mod_ref[0, 0, 5:6, :] * acc

    i = pl.program_id(1)

    @pl.when(i < n_lat_tiles)
    def _():
        scatter(yl_ref)

    @pl.when(i >= n_lat_tiles)
    def _():
        scatter(yc_ref)


def _ec_combine(X, slot_t, gate_t, Yl, Yc, modc, n_lat_tiles, cap_lat, cap_ctx):
    B, N, D = X.shape
    E = Yl.shape[0]
    row = lambda b, i: (b, i, 0)
    return pl.pallas_call(
        functools.partial(_combine_kernel, n_lat_tiles=n_lat_tiles),
        grid=(B, N // COMBINE_TILE),
        in_specs=[pl.BlockSpec((1, COMBINE_TILE, D), row),
                  pl.BlockSpec((1, COMBINE_TILE, PAIR), row),
                  pl.BlockSpec((1, COMBINE_TILE, PAIR), row),
                  pl.BlockSpec((E, cap_lat, D), lambda b, i: (0, b, 0)),
                  pl.BlockSpec((E, cap_ctx, D), lambda b, i: (0, b, 0)),
                  pl.BlockSpec((1, 1, 6, D), lambda b, i: (b, i // n_lat_tiles, 0, 0))],
        out_specs=pl.BlockSpec((1, COMBINE_TILE, D), row),
        out_shape=jax.ShapeDtypeStruct((B, N, D), F32),
        compiler_params=_cparams(("arbitrary", "arbitrary")),
        name="ec_combine",
    )(X, slot_t, gate_t, Yl, Yc, modc)


def _moe_layer(X, nw, modc, router_w, w_gate, w_up, w_down, layer, n_lat_tiles):
    B, N, D = X.shape
    n_lat = n_lat_tiles * ROW_TILE
    cap_lat = EC_CAPACITY * n_lat // N_EXPERTS
    cap_ctx = EC_CAPACITY * (N - n_lat) // N_EXPERTS
    H, aff = _mod_router(X, nw, modc, router_w.T, n_lat_tiles)
    slot, slot_t, gate_t = _ec_select(aff, n_lat, cap_lat, cap_ctx)
    Xl, Xc = _ec_gather(slot, H, n_lat, cap_lat, cap_ctx)
    Yl, Yc = _expert_ffn([Xl, Xc], w_gate, w_up, w_down, layer)
    return _ec_combine(X, slot_t, gate_t, Yl, Yc, modc, n_lat_tiles, cap_lat, cap_ctx)


def _final_norm_kernel(x_ref, w_ref, o_ref):
    x = x_ref[0]
    o_ref[0] = x * lax.rsqrt(jnp.mean(x * x, axis=-1, keepdims=True) + EPS) * w_ref[...]


def _final_norm(X, w, n_lat):
    B, _, D = X.shape
    row = lambda b, i: (b, i, 0)
    return pl.pallas_call(
        _final_norm_kernel,
        grid=(B, n_lat // ROW_TILE),
        in_specs=[pl.BlockSpec((1, ROW_TILE, D), row), pl.BlockSpec((1, D), lambda b, i: (0, 0))],
        out_specs=pl.BlockSpec((1, ROW_TILE, D), row),
        out_shape=jax.ShapeDtypeStruct((B, n_lat, D), F32),
        compiler_params=_cparams(("arbitrary", "arbitrary")),
        name="final_norm",
    )(X, w.reshape(1, D))


def _split(z, sizes):
    idx = np.cumsum(sizes)[:-1].tolist()
    return jnp.split(z, idx, axis=-1)


def _rmsnorm(x, w=None):
    y = x * lax.rsqrt(jnp.mean(x * x, axis=-1, keepdims=True) + EPS)
    if w is not None:
        y = y * w
    return y


def _rope_tables(n):
    t = jnp.arange(n)
    row = (t // GRID_W).astype(F32)
    col = (t % GRID_W).astype(F32)
    n_freq = HEAD_DIM // 4
    inv = ROPE_THETA ** (-jnp.arange(n_freq, dtype=F32) / n_freq)
    ang = jnp.concatenate([row[:, None] * inv, col[:, None] * inv], axis=-1)
    return jnp.cos(ang), jnp.sin(ang)


def _rope(x, cos, sin):
    shape = (cos.shape[0],) + (1,) * (x.ndim - 3) + (cos.shape[1],)
    c, s = cos.reshape(shape), sin.reshape(shape)
    x1, x2 = jnp.split(x, 2, axis=-1)
    return jnp.concatenate([x1 * c - x2 * s, x2 * c + x1 * s], axis=-1)


def _attend(q, k, v, shared_k):
    scale = q.shape[-1] ** -0.5
    k_sub = 'bkhd' if shared_k else 'bkhmd'

    def one_block(qb):
        s = jnp.einsum(f'bqhmd,{k_sub}->bhmqk', qb, k, preferred_element_type=F32) * scale
        p = jax.nn.softmax(s, axis=-1)
        return jnp.einsum('bhmqk,bkhv->bqhmv', p, v)

    B, Sq = q.shape[:2]
    nb = Sq // Q_BLOCK
    qb = jnp.moveaxis(q.reshape(B, nb, Q_BLOCK, *q.shape[2:]), 1, 0)
    out = lax.map(one_block, qb)
    return jnp.moveaxis(out, 0, 1).reshape(B, Sq, *out.shape[3:])


def _attention_core(zl, zc, cos, sin, q_norm_w, k_norm_w, lam, diff_norm_w, lambda_init, need_ctx):
    def project(z, use_rope):
        B, n, _ = z.shape
        aq, ak, av, bq, bk, bv = _split(z, ATT_SPLITS)
        aq = _rmsnorm(aq.reshape(B, n, GQA_HEADS, HEAD_DIM), q_norm_w)
        ak = _rmsnorm(ak.reshape(B, n, GQA_KV_HEADS, HEAD_DIM), k_norm_w)
        bq = bq.reshape(B, n, DIFF_HEADS, 2, HEAD_DIM)
        bk = bk.reshape(B, n, DIFF_HEADS, 2, HEAD_DIM)
        if use_rope:
            aq, ak, bq, bk = (_rope(t, cos, sin) for t in (aq, ak, bq, bk))
        aq = aq.reshape(B, n, GQA_KV_HEADS, GQA_HEADS // GQA_KV_HEADS, HEAD_DIM)
        av = av.reshape(B, n, GQA_KV_HEADS, HEAD_DIM)
        bv = bv.reshape(B, n, DIFF_HEADS, 2 * HEAD_DIM)
        return (aq, bq), (ak, av, bk, bv)

    lat_q, lat_kv = project(zl, True)
    ctx_q, ctx_kv = project(zc, False)
    lam_val = jnp.exp(jnp.sum(lam[0] * lam[1])) - jnp.exp(jnp.sum(lam[2] * lam[3])) + lambda_init

    def mix(q_side, kv):
        aq, bq = q_side
        ak, av, bk, bv = kv
        B, n = aq.shape[:2]
        a = _attend(aq, ak, av, True).reshape(B, n, -1)
        o = _attend(bq, bk, bv, False)
        o = o[..., 0, :] - lam_val * o[..., 1, :]
        b = (_rmsnorm(o, diff_norm_w) * (1 - lambda_init)).reshape(B, n, -1)
        return jnp.concatenate([a, b], axis=-1)

    kv_all = tuple(jnp.concatenate([l, c], axis=1) for l, c in zip(lat_kv, ctx_kv))
    yl = mix(lat_q, kv_all)
    yc = mix(ctx_q, ctx_kv) if need_ctx else jnp.zeros((zc.shape[0], zc.shape[1], 2 * HALF_W), F32)
    return yl, yc


def _chunk_scan(q, k, v, log_a, h0):
    B, S = q.shape[:2]
    n = S // CHUNK
    tril = jnp.tril(jnp.ones((CHUNK, CHUNK), dtype=bool))

    def to_chunks(a):
        return jnp.moveaxis(a.reshape(B, n, CHUNK, *a.shape[2:]), 1, 0)

    def step(h, inp):
        qc, kc, vc, lac = inp
        cum = jnp.cumsum(lac.astype(F32), axis=1)
        cum_t = jnp.moveaxis(cum, 1, -1)
        seg = jnp.exp(jnp.where(tril, cum_t[..., :, None] - cum_t[..., None, :], -jnp.inf))
        qk = jnp.einsum('btgk,bsgk->bgts', qc, kc)
        y = (jnp.einsum('bgts,bgrts,bsgrv->btgrv', qk, seg, vc)
             + jnp.einsum('btgk,bgrkv->btgrv', qc, h) * jnp.exp(cum)[..., None])
        last = cum[:, -1]
        h_new = (h * jnp.exp(last)[..., None, None]
                 + jnp.einsum('bsgk,bsgr,bsgrv->bgrkv', kc, jnp.exp(last[:, None] - cum), vc))
        return h_new, y

    h, ys = lax.scan(step, h0, tuple(to_chunks(a) for a in (q, k, v, log_a)))
    return jnp.moveaxis(ys, 0, 1).reshape(B, S, *ys.shape[3:]), h


def _prefix_scan(lat, ctx, reverse):
    if reverse:
        lat = tuple(jnp.flip(a, 1) for a in lat)
        ctx = tuple(jnp.flip(a, 1) for a in ctx)
    q, v = ctx[0], ctx[2]
    h0 = jnp.zeros((q.shape[0], q.shape[2], v.shape[3], q.shape[3], v.shape[4]), F32)
    y_c, h_c = _chunk_scan(*ctx, h0)
    y_l, _ = _chunk_scan(*lat, h_c)
    if reverse:
        y_l, y_c = jnp.flip(y_l, 1), jnp.flip(y_c, 1)
    return y_l, y_c


def _bidir(lat_f, ctx_f, lat_b, ctx_b):
    yl_f, yc_f = _prefix_scan(lat_f, ctx_f, False)
    yl_b, yc_b = _prefix_scan(lat_b, ctx_b, True)
    return yl_f + yl_b, yc_f + yc_b


def _dwconv(x, w, b):
    K = w.shape[0]
    out = lax.conv_general_dilated(x, w[:, None, :], window_strides=(1,),
                                   padding=[(K // 2, K // 2)],
                                   dimension_numbers=('NWC', 'WIO', 'NWC'),
                                   feature_group_count=x.shape[-1])
    return out + b


def _recurrent_core(zl, zc, cos, sin, ret_decay_logit, conv_w, conv_b, dt_bias, a_log, d_skip,
                    ssd_norm_w):
    log_gamma = jax.nn.log_sigmoid(ret_decay_logit)
    a_neg = -jnp.exp(a_log)
    r_per_g = SSD_HEADS // SSD_GROUPS

    def project(z, use_rope):
        B, n, _ = z.shape
        rq, rk, rv, rg, zz, xbc, dt = _split(z, REC_SPLITS)
        rq = rq.reshape(B, n, RET_HEADS, RET_QK_DIM)
        rk = rk.reshape(B, n, RET_HEADS, RET_QK_DIM) * RET_QK_DIM ** -0.5
        if use_rope:
            rq, rk = _rope(rq, cos, sin), _rope(rk, cos, sin)
        rv = rv.reshape(B, n, RET_HEADS, 1, RET_V_DIM)
        ret = [(rq, rk, rv, jnp.broadcast_to(log_gamma[d][:, None], (B, n, RET_HEADS, 1)))
               for d in (0, 1)]
        xbc = jax.nn.silu(_dwconv(xbc, conv_w, conv_b))
        xs, bm, cm = _split(xbc, (SSD_INNER, SSD_GROUPS * SSD_STATE, SSD_GROUPS * SSD_STATE))
        xs = xs.reshape(B, n, SSD_HEADS, SSD_HEAD_DIM)
        bm = bm.reshape(B, n, SSD_GROUPS, SSD_STATE)
        cm = cm.reshape(B, n, SSD_GROUPS, SSD_STATE)
        dt = jax.nn.softplus(dt.reshape(B, n, 2, SSD_HEADS) + dt_bias)
        ssd = [(cm, bm,
                (xs * dt[:, :, d, :, None]).reshape(B, n, SSD_GROUPS, r_per_g, SSD_HEAD_DIM),
                (dt[:, :, d] * a_neg[d]).reshape(B, n, SSD_GROUPS, r_per_g))
               for d in (0, 1)]
        return ret, ssd, rg, zz, xs

    lat_ret, lat_ssd, lat_rg, lat_z, lat_xs = project(zl, True)
    ctx_ret, ctx_ssd, ctx_rg, ctx_z, ctx_xs = project(zc, False)
    ret_l, ret_c = _bidir(lat_ret[0], ctx_ret[0], lat_ret[1], ctx_ret[1])
    ssd_l, ssd_c = _bidir(lat_ssd[0], ctx_ssd[0], lat_ssd[1], ctx_ssd[1])

    def combine(ret_y, ssd_y, rg, zz, xs):
        B, n = rg.shape[:2]
        r = _rmsnorm(ret_y.reshape(B, n, RET_HEADS, RET_V_DIM)).reshape(B, n, -1) * jax.nn.silu(rg)
        s = ssd_y.reshape(B, n, SSD_HEADS, SSD_HEAD_DIM) + d_skip[:, None] * xs
        s = s.reshape(B, n, -1) * jax.nn.silu(zz)
        s = _rmsnorm(s.reshape(B, n, SSD_GROUPS, -1)).reshape(B, n, -1) * ssd_norm_w
        return jnp.concatenate([r, s], axis=-1)

    return combine(ret_l, ssd_l, lat_rg, lat_z, lat_xs), combine(ret_c, ssd_c, ctx_rg, ctx_z, ctx_xs)


def _dispatch_tokens(h, aff):
    B, n, D = h.shape
    cap = EC_CAPACITY * n // N_EXPERTS
    g, idx = lax.top_k(jnp.swapaxes(aff, 1, 2), cap)
    xin = jax.vmap(lambda xs, i: xs[i])(h, idx)
    return xin, g, idx


def _combine_tokens(y, g, idx, n):
    D = y.shape[-1]
    y = y * g[..., None]
    return jax.vmap(lambda val, i: jax.ops.segment_sum(val.reshape(-1, D), i.reshape(-1),
                                                       num_segments=n))(y, idx)


def kernel(x, c, ctx, c_ctx, ada_w, ada_b, norm1_w, norm2_w, att_w_in, att_w_out, att_q_norm_w,
           att_k_norm_w, diff_lambda, diff_norm_w, rec_w_in, rec_w_out, ret_decay_logit, ssd_conv_w,
           ssd_conv_b, ssd_dt_bias, ssd_a_log, ssd_d_skip, ssd_norm_w, router_w, expert_w_gate,
           expert_w_up, expert_w_down, final_norm_w):
    B, n_lat, D = x.shape
    n_ctx = ctx.shape[1]
    n_lat_tiles = n_lat // ROW_TILE
    cos, sin = _rope_tables(n_lat)
    reps = PAIR // cos.shape[1]
    cos_t = jnp.concatenate([jnp.tile(cos, (1, reps)), jnp.ones((n_ctx, PAIR), F32)], axis=0)
    sin_t = jnp.concatenate([jnp.tile(jnp.concatenate([-sin, sin], axis=1), (1, reps // 2)),
                             jnp.zeros((n_ctx, PAIR), F32)], axis=0)
    gmat = jnp.kron(jnp.eye(GQA_HEADS, dtype=F32), jnp.ones((HEAD_DIM, HEAD_DIM), F32)).astype(BF16)

    n_cond = 24
    cvec = jnp.concatenate([c, c_ctx[None, :], jnp.zeros((n_cond - B - 1, D), F32)], axis=0)
    mods = _ada_modulation(cvec, ada_w, ada_b)

    X = jnp.concatenate([x, ctx], axis=1)
    for layer in range(DEPTH):
        i = layer // 2
        need_ctx = layer < DEPTH - 1
        mod_lat = mods[layer, :B].reshape(B, 1, 6, D)
        mod_ctx = jnp.broadcast_to(mods[layer, B].reshape(1, 1, 6, D), (B, 1, 6, D))
        modc = jnp.concatenate([mod_lat, mod_ctx], axis=1)

        if layer % 2 == 0:
            lambda_init = 0.8 - 0.6 * math.exp(-0.3 * layer)
            Q, K, V = _att_project(X, norm1_w[layer], modc, att_w_in[i].astype(BF16), cos_t, sin_t, gmat,
                                   jnp.tile(att_q_norm_w[i], GQA_HEADS).reshape(1, -1),
                                   jnp.tile(att_k_norm_w[i], GQA_KV_HEADS).reshape(1, -1), n_lat_tiles)
            X = _attention(X, Q, K, V, modc, diff_lambda[i], diff_norm_w[i], att_w_out[i].astype(BF16),
                           lambda_init, n_lat_tiles)
        else:
            X = _recurrent_layer(X, norm1_w[layer], modc, rec_w_in[i], rec_w_out[i], cos_t, sin_t,
                                 ret_decay_logit[i], ssd_conv_w[i], ssd_conv_b[i], ssd_dt_bias[i],
                                 ssd_a_log[i], ssd_d_skip[i], ssd_norm_w[i], n_lat_tiles)

        X = _moe_layer(X, norm2_w[layer], modc, router_w[layer], expert_w_gate, expert_w_up, expert_w_down,
                       layer, n_lat_tiles)

    return _final_norm(X, final_norm_w, n_lat)
```

```python
import functools
import math

import jax
import jax.numpy as jnp
import numpy as np
from jax import lax
from jax.experimental import pallas as pl
from jax.experimental.pallas import tpu as pltpu

D_MODEL = 1024
DEPTH = 4
GRID_W = 64
HEAD_DIM = 64
ROPE_THETA = 10000.0
Q_BLOCK = 128
CHUNK = 128
EPS = 1e-6
HALF_W = D_MODEL // 2
GQA_HEADS = HALF_W // HEAD_DIM
GQA_KV_HEADS = GQA_HEADS // 4
DIFF_HEADS = HALF_W // (2 * HEAD_DIM)
RET_HEADS = HALF_W // (2 * HEAD_DIM)
RET_QK_DIM = HEAD_DIM
RET_V_DIM = 2 * HEAD_DIM
SSD_HEAD_DIM = HEAD_DIM
SSD_HEADS = HALF_W // SSD_HEAD_DIM
SSD_GROUPS = 2
SSD_STATE = 128
SSD_INNER = SSD_HEADS * SSD_HEAD_DIM
SSD_XBC = SSD_INNER + 2 * SSD_GROUPS * SSD_STATE
N_EXPERTS = 16
EC_CAPACITY = 2
EXPERT_FF = ((8 * D_MODEL // 3 + 255) // 256) * 256

ATT_SPLITS = (GQA_HEADS * HEAD_DIM, GQA_KV_HEADS * HEAD_DIM, GQA_KV_HEADS * HEAD_DIM,
              DIFF_HEADS * 2 * HEAD_DIM, DIFF_HEADS * 2 * HEAD_DIM, DIFF_HEADS * 2 * HEAD_DIM)
REC_SPLITS = (RET_HEADS * RET_QK_DIM, RET_HEADS * RET_QK_DIM, RET_HEADS * RET_V_DIM,
              RET_HEADS * RET_V_DIM, SSD_INNER, SSD_XBC, 2 * SSD_HEADS)

ROW_TILE = 256
PROJ_TILE = 768
FF_TILE = 256
VMEM_LIMIT = 56 * 1024 * 1024
BF16 = jnp.bfloat16
F32 = jnp.float32


def _cparams(sem):
    return pltpu.CompilerParams(dimension_semantics=sem, vmem_limit_bytes=VMEM_LIMIT)


def _ada_kernel(c_ref, w_ref, b_ref, o_ref):
    c = c_ref[...]
    s = c * jax.nn.sigmoid(c)
    o_ref[0] = jnp.dot(s, w_ref[0], precision=lax.Precision.HIGHEST,
                       preferred_element_type=F32) + b_ref[0]


def _ada_modulation(cvec, ada_w, ada_b):
    R, D = cvec.shape
    n_out = ada_w.shape[-1]
    tn = 1536
    return pl.pallas_call(
        _ada_kernel,
        grid=(DEPTH, n_out // tn),
        in_specs=[pl.BlockSpec((R, D), lambda l, j: (0, 0)),
                  pl.BlockSpec((1, D, tn), lambda l, j: (l, 0, j)),
                  pl.BlockSpec((1, 1, tn), lambda l, j: (l, 0, j))],
        out_specs=pl.BlockSpec((1, R, tn), lambda l, j: (l, 0, j)),
        out_shape=jax.ShapeDtypeStruct((DEPTH, R, n_out), F32),
        compiler_params=_cparams(("arbitrary", "arbitrary")),
        name="ada_modulation",
    )(cvec, ada_w, ada_b.reshape(DEPTH, 1, n_out))


def _row_mod(mod_ref, idx, n_rows, n_lat):
    row = pl.program_id(1) * n_rows + lax.broadcasted_iota(jnp.int32, (n_rows, 1), 0)
    return jnp.where(row < n_lat, mod_ref[0, 0, idx:idx + 1, :], mod_ref[0, 1, idx:idx + 1, :])


def _modulated(x, nw, mod_ref, shift_idx, scale_idx, n_lat):
    n_rows = x.shape[0]
    ms = jnp.mean(x * x, axis=-1, keepdims=True)
    y = x * lax.rsqrt(ms + EPS) * nw
    return (y * (1.0 + _row_mod(mod_ref, scale_idx, n_rows, n_lat))
            + _row_mod(mod_ref, shift_idx, n_rows, n_lat))


_MOD_BOTH = lambda b, i: (b, 0, 0, 0)
_ONCE = dict(pipeline_mode=pl.Buffered(1))


PAIR = 2 * HEAD_DIM
N_PAIR_GROUPS = 2 * HALF_W // PAIR
ATT_KV_W = GQA_KV_HEADS * PAIR + DIFF_HEADS * PAIR


def _head_sumsq(x, g):
    x2 = x * x
    hi = x2.astype(BF16)
    lo = (x2 - hi.astype(F32)).astype(BF16)
    return jnp.dot(hi, g, preferred_element_type=F32) + jnp.dot(lo, g, preferred_element_type=F32)


def _rope_lanes(x, cos, sin_signed):
    W = x.shape[-1]
    half = HEAD_DIM // 2
    lane = lax.broadcasted_iota(jnp.int32, x.shape, 1) & (HEAD_DIM - 1)
    partner = jnp.where(lane < half, pltpu.roll(x, W - half, 1), pltpu.roll(x, half, 1))
    return x * cos + partner * sin_signed


def _att_project_kernel(x_ref, nw_ref, mod_ref, w_ref, cos_ref, sin_ref, g_ref, qw_ref, kw_ref,
                        q_ref, k_ref, v_ref, *, n_lat):
    h = _modulated(x_ref[0], nw_ref[...], mod_ref, 0, 1, n_lat)
    z = jnp.dot(h.astype(BF16), w_ref[...], preferred_element_type=F32)
    o_aq, o_ak, o_av, o_bq, o_bk, o_bv = np.cumsum((0,) + ATT_SPLITS[:-1]).tolist()
    n_aq, n_ak, n_bq = ATT_SPLITS[0], ATT_SPLITS[1], ATT_SPLITS[3]
    cos, sin = cos_ref[...], sin_ref[...]
    cos4 = jnp.concatenate([cos] * (n_aq // PAIR), axis=1)
    sin4 = jnp.concatenate([sin] * (n_aq // PAIR), axis=1)
    inv_d = 1.0 / HEAD_DIM
    q_scale = HEAD_DIM ** -0.5

    aq = z[:, o_aq:o_aq + n_aq]
    aq = aq * lax.rsqrt(_head_sumsq(aq, g_ref[...]) * inv_d + EPS) * qw_ref[...]
    aq = _rope_lanes(aq, cos4, sin4) * q_scale
    bq = _rope_lanes(z[:, o_bq:o_bq + n_bq], cos4, sin4) * q_scale
    q = jnp.concatenate([aq, bq], axis=1)
    lo = (lax.broadcasted_iota(jnp.int32, q.shape, 1) & (PAIR - 1)) < HEAD_DIM
    q_ref[0, :, 0:q.shape[1]] = jnp.where(lo, q, 0.0).astype(BF16)
    q_ref[0, :, q.shape[1]:2 * q.shape[1]] = jnp.where(lo, 0.0, q).astype(BF16)

    ak = z[:, o_ak:o_ak + n_ak]
    ak = ak * lax.rsqrt(_head_sumsq(ak, g_ref[0:n_ak, 0:n_ak]) * inv_d + EPS) * kw_ref[...]
    ak = _rope_lanes(ak, cos, sin)
    av = z[:, o_av:o_av + n_ak]
    lo_kv = lax.broadcasted_iota(jnp.int32, ak.shape, 1) < HEAD_DIM
    ak_sw, av_sw = pltpu.roll(ak, HEAD_DIM, 1), pltpu.roll(av, HEAD_DIM, 1)
    k_ref[0, :, 0:PAIR] = jnp.where(lo_kv, ak, ak_sw).astype(BF16)
    k_ref[0, :, PAIR:2 * PAIR] = jnp.where(lo_kv, ak_sw, ak).astype(BF16)
    v_ref[0, :, 0:PAIR] = jnp.where(lo_kv, av, av_sw).astype(BF16)
    v_ref[0, :, PAIR:2 * PAIR] = jnp.where(lo_kv, av_sw, av).astype(BF16)
    k_ref[0, :, 2 * PAIR:] = _rope_lanes(z[:, o_bk:o_bk + n_bq], cos4, sin4).astype(BF16)
    v_ref[0, :, 2 * PAIR:] = z[:, o_bv:o_bv + n_bq].astype(BF16)


def _att_project(X, nw, modc, w_bf16, cos_t, sin_t, gmat, qw_t, kw_t, n_lat_tiles):
    B, N, D = X.shape
    n_in = w_bf16.shape[1]
    q_w = 2 * N_PAIR_GROUPS * PAIR
    row = lambda b, i: (b, i, 0)
    const2 = lambda b, i: (0, 0)
    return pl.pallas_call(
        functools.partial(_att_project_kernel, n_lat=n_lat_tiles * ROW_TILE),
        grid=(B, N // PROJ_TILE),
        in_specs=[pl.BlockSpec((1, PROJ_TILE, D), row),
                  pl.BlockSpec((1, D), const2),
                  pl.BlockSpec((1, 2, 6, D), _MOD_BOTH),
                  pl.BlockSpec((D, n_in), const2, **_ONCE),
                  pl.BlockSpec((PROJ_TILE, PAIR), lambda b, i: (i, 0)),
                  pl.BlockSpec((PROJ_TILE, PAIR), lambda b, i: (i, 0)),
                  pl.BlockSpec(gmat.shape, const2),
                  pl.BlockSpec(qw_t.shape, const2),
                  pl.BlockSpec(kw_t.shape, const2)],
        out_specs=[pl.BlockSpec((1, PROJ_TILE, q_w), row),
                   pl.BlockSpec((1, PROJ_TILE, ATT_KV_W), row),
                   pl.BlockSpec((1, PROJ_TILE, ATT_KV_W), row)],
        out_shape=[jax.ShapeDtypeStruct((B, N, q_w), BF16),
                   jax.ShapeDtypeStruct((B, N, ATT_KV_W), BF16),
                   jax.ShapeDtypeStruct((B, N, ATT_KV_W), BF16)],
        compiler_params=_cparams(("arbitrary", "arbitrary")),
        name="att_project",
    )(X, nw.reshape(1, D), modc, w_bf16, cos_t, sin_t, gmat, qw_t, kw_t)


def _attention_kernel(x_ref, q_ref, k_ref, v_ref, mod_ref, lam_ref, dnw_ref, w_ref, o_ref,
                      s_scr, p_scr, y_scr, *, lambda_init, n_lat_tiles):
    n_lat = n_lat_tiles * ROW_TILE
    n_all = k_ref.shape[1]
    q_half = N_PAIR_GROUPS * PAIR
    lam = lam_ref[...]
    lam_val = (jnp.exp(jnp.sum(lam[0:1] * lam[1:2], axis=-1, keepdims=True))
               - jnp.exp(jnp.sum(lam[2:3] * lam[3:4], axis=-1, keepdims=True)) + lambda_init)

    def attend(k0, nk):
        keys = slice(k0, k0 + nk)

        def pair(pg):
            qv = jnp.concatenate([q_ref[0, :, h * q_half + pg * PAIR: h * q_half + (pg + 1) * PAIR]
                                  for h in range(2)], axis=0)
            kd = k_ref[0, keys, _kv_lane(pg): _kv_lane(pg) + PAIR]
            vd = v_ref[0, keys, _kv_lane(pg): _kv_lane(pg) + PAIR]
            s_scr[:, 0:nk] = lax.dot_general(qv, kd, (((1,), (1,)), ((), ())),
                                             preferred_element_type=F32)
            s = s_scr[:, 0:nk]
            p = jnp.exp(s - jnp.max(s, axis=-1, keepdims=True))
            p_scr[:, 0:nk] = p.astype(BF16)
            o = jnp.dot(p_scr[:, 0:nk], vd, preferred_element_type=F32)
            o = o * (1.0 / jnp.sum(p, axis=-1, keepdims=True))
            return o[0:ROW_TILE], o[ROW_TILE:2 * ROW_TILE]

        lo = lax.broadcasted_iota(jnp.int32, (ROW_TILE, PAIR), 1) < HEAD_DIM
        for pg in range(2 * GQA_KV_HEADS):
            o_even, o_odd = pair(pg)
            y_scr[:, pg * PAIR:(pg + 1) * PAIR] = jnp.where(lo, o_even, o_odd).astype(BF16)
        for pg in range(2 * GQA_KV_HEADS, N_PAIR_GROUPS):
            o_0, o_1 = pair(pg)
            od = o_0 - lam_val * o_1
            od = od * lax.rsqrt(jnp.mean(od * od, axis=-1, keepdims=True) + EPS)
            od = od * dnw_ref[...] * (1.0 - lambda_init)
            y_scr[:, pg * PAIR:(pg + 1) * PAIR] = od.astype(BF16)
        y = jnp.dot(y_scr[...], w_ref[...], preferred_element_type=F32)
        o_ref[0] = x_ref[0] + mod_ref[0, 0, 2:3, :] * y

    i = pl.program_id(1)

    @pl.when(i < n_lat_tiles)
    def _():
        attend(0, n_all)

    @pl.when(i >= n_lat_tiles)
    def _():
        attend(n_lat, n_all - n_lat)


def _kv_lane(pg):
    n_gqa_pairs = 2 * GQA_KV_HEADS
    if pg < n_gqa_pairs:
        return (pg // 2) * PAIR
    return (GQA_KV_HEADS + pg - n_gqa_pairs) * PAIR


def _attention(X, Q, K, V, modc, lam, dnw, w_bf16, lambda_init, n_lat_tiles):
    B, N, D = X.shape
    row = lambda b, i: (b, i, 0)
    whole = lambda b, i: (b, 0, 0)
    const2 = lambda b, i: (0, 0)
    return pl.pallas_call(
        functools.partial(_attention_kernel, lambda_init=lambda_init, n_lat_tiles=n_lat_tiles),
        grid=(B, N // ROW_TILE),
        in_specs=[pl.BlockSpec((1, ROW_TILE, D), row),
                  pl.BlockSpec((1, ROW_TILE, Q.shape[-1]), row),
                  pl.BlockSpec((1, N, K.shape[-1]), whole, pipeline_mode=pl.Buffered(1)),
                  pl.BlockSpec((1, N, V.shape[-1]), whole, pipeline_mode=pl.Buffered(1)),
                  pl.BlockSpec((1, 1, 6, D), lambda b, i: (b, i // n_lat_tiles, 0, 0)),
                  pl.BlockSpec(lam.shape, const2),
                  pl.BlockSpec((1, PAIR), const2),
                  pl.BlockSpec(w_bf16.shape, const2, pipeline_mode=pl.Buffered(1))],
        out_specs=pl.BlockSpec((1, ROW_TILE, D), row),
        out_shape=jax.ShapeDtypeStruct((B, N, D), F32),
        scratch_shapes=[pltpu.VMEM((2 * ROW_TILE, N), F32), pltpu.VMEM((2 * ROW_TILE, N), BF16),
                        pltpu.VMEM((ROW_TILE, w_bf16.shape[0]), BF16)],
        compiler_params=_cparams(("arbitrary", "arbitrary")),
        name="attention",
    )(X, Q, K, V, modc, lam, dnw.reshape(1, PAIR), w_bf16)


REC_IN_PAD = 3200
N_SSD_DT = 2 * SSD_HEADS
RET_LANE0 = N_SSD_DT


def _softplus(x):
    return jnp.maximum(x, 0.0) + jnp.log(1.0 + jnp.exp(-jnp.abs(x)))


def _silu(x):
    return x * jax.nn.sigmoid(x)


def _rec_project_kernel(x_ref, nw_ref, mod_ref, w_ref, cos_ref, sin_ref, dtb_ref,
                        rq_ref, rk_ref, rv_ref, g_ref, xbc_ref, dt_ref, *, n_lat):
    h = _modulated(x_ref[0], nw_ref[...], mod_ref, 0, 1, n_lat)
    z = jnp.dot(h.astype(BF16), w_ref[...], preferred_element_type=F32)
    o_rq, o_rk, o_rv, o_rg, o_z, o_xbc, o_dt = np.cumsum((0,) + REC_SPLITS[:-1]).tolist()
    n_qk = REC_SPLITS[0]
    cos, sin = cos_ref[...], sin_ref[...]
    cos2 = jnp.concatenate([cos] * (n_qk // PAIR), axis=1)
    sin2 = jnp.concatenate([sin] * (n_qk // PAIR), axis=1)
    rq = _rope_lanes(z[:, o_rq:o_rq + n_qk], cos2, sin2)
    lo = (lax.broadcasted_iota(jnp.int32, rq.shape, 1) & (PAIR - 1)) < HEAD_DIM
    rq_ref[0, :, 0:n_qk] = jnp.where(lo, rq, 0.0).astype(BF16)
    rq_ref[0, :, n_qk:2 * n_qk] = jnp.where(lo, 0.0, rq).astype(BF16)
    rk_ref[0] = _rope_lanes(z[:, o_rk:o_rk + n_qk] * (RET_QK_DIM ** -0.5), cos2, sin2).astype(BF16)
    rv_ref[0] = z[:, o_rv:o_rg].astype(BF16)
    g_ref[0] = z[:, o_rg:o_xbc]
    xbc_ref[0] = z[:, o_xbc:o_dt]
    dt_ref[0] = _softplus(z[:, o_dt:o_dt + PAIR] + dtb_ref[...])


def _rec_project(X, nw, modc, w_bf16, cos_t, sin_t, dtb_row, n_lat_tiles):
    B, N, D = X.shape
    n_in = w_bf16.shape[1]
    row = lambda b, i: (b, i, 0)
    const2 = lambda b, i: (0, 0)
    widths = (2 * REC_SPLITS[0], REC_SPLITS[1], REC_SPLITS[2], REC_SPLITS[3] + REC_SPLITS[4],
              REC_SPLITS[5], PAIR)
    dtypes = (BF16, BF16, BF16, F32, F32, F32)
    return pl.pallas_call(
        functools.partial(_rec_project_kernel, n_lat=n_lat_tiles * ROW_TILE),
        grid=(B, N // PROJ_TILE),
        in_specs=[pl.BlockSpec((1, PROJ_TILE, D), row),
                  pl.BlockSpec((1, D), const2),
                  pl.BlockSpec((1, 2, 6, D), _MOD_BOTH),
                  pl.BlockSpec((D, n_in), const2, **_ONCE),
                  pl.BlockSpec((PROJ_TILE, PAIR), lambda b, i: (i, 0)),
                  pl.BlockSpec((PROJ_TILE, PAIR), lambda b, i: (i, 0)),
                  pl.BlockSpec((1, PAIR), const2)],
        out_specs=[pl.BlockSpec((1, PROJ_TILE, w), row) for w in widths],
        out_shape=[jax.ShapeDtypeStruct((B, N, w), dt) for w, dt in zip(widths, dtypes)],
        compiler_params=_cparams(("arbitrary", "arbitrary")),
        name="rec_project",
    )(X, nw.reshape(1, D), modc, w_bf16, cos_t, sin_t, dtb_row)


def _conv_kernel(x_ref, w_ref, b_ref, o_ref, *, n_lat):
    x = x_ref[0]
    n = x.shape[0]
    t = lax.broadcasted_iota(jnp.int32, x.shape, 0)
    first = (t == 0) | (t == n_lat)
    last = (t == n_lat - 1) | (t == n - 1)
    prev = jnp.where(first, 0.0, pltpu.roll(x, 1, 0))
    nxt = jnp.where(last, 0.0, pltpu.roll(x, n - 1, 0))
    y = prev * w_ref[0:1, :] + x * w_ref[1:2, :] + nxt * w_ref[2:3, :] + b_ref[...]
    o_ref[0] = _silu(y)


def _ssd_conv(XBC, conv_w, conv_b, n_lat):
    B, N, C = XBC.shape
    tc = 256
    return pl.pallas_call(
        functools.partial(_conv_kernel, n_lat=n_lat),
        grid=(B, C // tc),
        in_specs=[pl.BlockSpec((1, N, tc), lambda b, j: (b, 0, j)),
                  pl.BlockSpec((conv_w.shape[0], tc), lambda b, j: (0, j)),
                  pl.BlockSpec((1, tc), lambda b, j: (0, j))],
        out_specs=pl.BlockSpec((1, N, tc), lambda b, j: (b, 0, j)),
        out_shape=jax.ShapeDtypeStruct((B, N, C), F32),
        compiler_params=_cparams(("arbitrary", "arbitrary")),
        name="ssd_conv",
    )(XBC, conv_w, conv_b.reshape(1, C))


def _split3(x):
    hi = x.astype(BF16)
    r1 = x - hi.astype(F32)
    mid = r1.astype(BF16)
    lo = (r1 - mid.astype(F32)).astype(BF16)
    return hi, mid, lo


def _scan_kernel(rq_ref, rk_ref, rv_ref, a_ref, dt_ref, alog_ref, logit_ref, y_ref, hret, hssd, *, reverse):
    T = rq_ref.shape[1]
    d = 1 if reverse else 0
    step = pl.program_id(1)

    @pl.when(step == 0)
    def _():
        hret[...] = jnp.zeros_like(hret)
        hssd[...] = jnp.zeros_like(hssd)

    lane = lax.broadcasted_iota(jnp.int32, (1, PAIR), 1)
    a_neg = jnp.where(lane < N_SSD_DT, -jnp.exp(alog_ref[...]), 0.0)
    logit = logit_ref[...]
    log_gamma = jnp.where((lane >= RET_LANE0) & (lane < RET_LANE0 + 2 * RET_HEADS),
                          jnp.minimum(logit, 0.0) - jnp.log(1.0 + jnp.exp(-jnp.abs(logit))), 0.0)
    dt = dt_ref[0]
    la = dt * a_neg + log_gamma

    ti = lax.broadcasted_iota(jnp.int32, (T, T), 0)
    tj = lax.broadcasted_iota(jnp.int32, (T, T), 1)
    causal = (tj >= ti) if reverse else (tj <= ti)
    tri = jnp.where(causal, 1.0, 0.0).astype(BF16)
    hi, mid, lo3 = _split3(la)
    P = (jnp.dot(tri, hi, preferred_element_type=F32) + jnp.dot(tri, mid, preferred_element_type=F32)
         + jnp.dot(tri, lo3, preferred_element_type=F32))
    PT = P.T
    tot = P[0:1, :] if reverse else P[T - 1:T, :]
    E = jnp.exp(P)
    KD = jnp.exp(tot - P)
    ET = jnp.exp(tot)
    lo = lax.broadcasted_iota(jnp.int32, (T, PAIR), 1) < HEAD_DIM
    lo_row = lane < HEAD_DIM
    nt = (((1,), (1,)), ((), ()))
    tn = (((0,), (0,)), ((), ()))

    def decay_matrix(c):
        diff = jnp.minimum(P[:, c:c + 1] - PT[c:c + 1, :], 0.0)
        return jnp.where(causal, jnp.exp(diff), 0.0)

    n_qk = RET_HEADS * RET_QK_DIM
    for h in range(RET_HEADS):
        c = RET_LANE0 + d * RET_HEADS + h
        p, half = h // 2, h % 2
        qv = rq_ref[0, :, half * n_qk + p * PAIR: half * n_qk + (p + 1) * PAIR]
        kp = rk_ref[0, :, p * PAIR:(p + 1) * PAIR]
        v = rv_ref[0, :, h * RET_V_DIM:(h + 1) * RET_V_DIM]
        s = lax.dot_general(qv, kp, nt, preferred_element_type=F32)
        aw = (s * decay_matrix(c)).astype(BF16)
        y = jnp.dot(aw, v, preferred_element_type=F32)
        y = y + E[:, c:c + 1] * jnp.dot(qv, hret[h].astype(BF16), preferred_element_type=F32)
        y_ref[0, :, h * RET_V_DIM:(h + 1) * RET_V_DIM] = y
        kdec = (kp.astype(F32) * KD[:, c:c + 1]).astype(BF16)
        hret[h] = hret[h] * ET[:, c:c + 1] + lax.dot_general(kdec, v, tn, preferred_element_type=F32)

    y0 = RET_HEADS * RET_V_DIM
    r_per_g = SSD_HEADS // SSD_GROUPS
    o_b = SSD_INNER
    o_c = SSD_INNER + SSD_GROUPS * SSD_STATE
    for g in range(SSD_GROUPS):
        cg = a_ref[0, :, o_c + g * SSD_STATE: o_c + (g + 1) * SSD_STATE].astype(BF16)
        bg = a_ref[0, :, o_b + g * SSD_STATE: o_b + (g + 1) * SSD_STATE].astype(BF16)
        s = lax.dot_general(cg, bg, nt, preferred_element_type=F32)
        ch = jnp.dot(cg, hssd[g].astype(BF16), preferred_element_type=F32)
        vdec, dec_rows = [], []
        for p in range(r_per_g // 2):
            pg = g * (r_per_g // 2) + p
            xs = a_ref[0, :, pg * PAIR:(pg + 1) * PAIR]
            ys, vds, cols = [], [], []
            for half in range(2):
                c = d * SSD_HEADS + 2 * pg + half
                aw = (s * decay_matrix(c)).astype(BF16)
                vh = xs * dt[:, c:c + 1]
                ys.append(jnp.dot(aw, vh.astype(BF16), preferred_element_type=F32))
                vds.append(vh * KD[:, c:c + 1])
                cols.append(c)
            e_pair = jnp.where(lo, E[:, cols[0]:cols[0] + 1], E[:, cols[1]:cols[1] + 1])
            y_ref[0, :, y0 + pg * PAIR: y0 + (pg + 1) * PAIR] = (
                jnp.where(lo, ys[0], ys[1]) + e_pair * ch[:, p * PAIR:(p + 1) * PAIR])
            vdec.append(jnp.where(lo, vds[0], vds[1]).astype(BF16))
            dec_rows.append(jnp.where(lo_row, ET[:, cols[0]:cols[0] + 1], ET[:, cols[1]:cols[1] + 1]))
        hssd[g] = (hssd[g] * jnp.concatenate(dec_rows, axis=1)
                   + lax.dot_general(bg, jnp.concatenate(vdec, axis=1), tn, preferred_element_type=F32))


def _bidir_scan(RQ, RK, RV, A, DT, alog_row, logit_row, n_lat_tiles, reverse):
    B, N, _ = RQ.shape
    n_tiles = N // ROW_TILE

    def tile(i):
        lat = (n_lat_tiles - i) if reverse else (i - 1)
        return jnp.where(i == 0, n_tiles - 1, lat)

    row = lambda b, i: (b, tile(i), 0)
    const2 = lambda b, i: (0, 0)
    y_w = RET_HEADS * RET_V_DIM + SSD_INNER
    return pl.pallas_call(
        functools.partial(_scan_kernel, reverse=reverse),
        grid=(B, n_tiles),
        in_specs=[pl.BlockSpec((1, ROW_TILE, RQ.shape[-1]), row),
                  pl.BlockSpec((1, ROW_TILE, RK.shape[-1]), row),
                  pl.BlockSpec((1, ROW_TILE, RV.shape[-1]), row),
                  pl.BlockSpec((1, ROW_TILE, A.shape[-1]), row),
                  pl.BlockSpec((1, ROW_TILE, PAIR), row),
                  pl.BlockSpec((1, PAIR), const2),
                  pl.BlockSpec((1, PAIR), const2)],
        out_specs=pl.BlockSpec((1, ROW_TILE, y_w), row),
        out_shape=jax.ShapeDtypeStruct((B, N, y_w), F32),
        scratch_shapes=[pltpu.VMEM((RET_HEADS, PAIR, RET_V_DIM), F32),
                        pltpu.VMEM((SSD_GROUPS, SSD_STATE, SSD_INNER // SSD_GROUPS), F32)],
        compiler_params=_cparams(("arbitrary", "arbitrary")),
        name="scan_bwd" if reverse else "scan_fwd",
    )(RQ, RK, RV, A, DT, alog_row, logit_row)


def _rec_combine_kernel(x_ref, yf_ref, yb_ref, g_ref, a_ref, mod_ref, dskip_ref, snw_ref, w_ref, o_ref, y_scr,
                        *, n_lat):
    n_ret = RET_HEADS * RET_V_DIM
    y = yf_ref[0] + yb_ref[0]
    gates = g_ref[0]
    for h in range(RET_HEADS):
        sl = slice(h * RET_V_DIM, (h + 1) * RET_V_DIM)
        r = y[:, sl]
        r = r * lax.rsqrt(jnp.mean(r * r, axis=-1, keepdims=True) + EPS)
        y_scr[:, sl] = (r * _silu(gates[:, sl])).astype(BF16)
    gw = SSD_INNER // SSD_GROUPS
    for g in range(SSD_GROUPS):
        sl = slice(g * gw, (g + 1) * gw)
        s = y[:, n_ret + g * gw: n_ret + (g + 1) * gw] + dskip_ref[:, sl] * a_ref[0, :, sl]
        s = s * _silu(gates[:, n_ret + g * gw: n_ret + (g + 1) * gw])
        s = s * lax.rsqrt(jnp.mean(s * s, axis=-1, keepdims=True) + EPS) * snw_ref[:, sl]
        y_scr[:, n_ret + g * gw: n_ret + (g + 1) * gw] = s.astype(BF16)
    gate = _row_mod(mod_ref, 2, x_ref.shape[1], n_lat)
    o_ref[0] = x_ref[0] + gate * jnp.dot(y_scr[...], w_ref[...], preferred_element_type=F32)


def _rec_combine(X, Yf, Yb, G, A, modc, dskip_row, snw_row, w_bf16, n_lat_tiles):
    B, N, D = X.shape
    row = lambda b, i: (b, i, 0)
    const2 = lambda b, i: (0, 0)
    return pl.pallas_call(
        functools.partial(_rec_combine_kernel, n_lat=n_lat_tiles * ROW_TILE),
        grid=(B, N // PROJ_TILE),
        in_specs=[pl.BlockSpec((1, PROJ_TILE, D), row),
                  pl.BlockSpec((1, PROJ_TILE, Yf.shape[-1]), row),
                  pl.BlockSpec((1, PROJ_TILE, Yb.shape[-1]), row),
                  pl.BlockSpec((1, PROJ_TILE, G.shape[-1]), row),
                  pl.BlockSpec((1, PROJ_TILE, SSD_INNER), row),
                  pl.BlockSpec((1, 2, 6, D), _MOD_BOTH),
                  pl.BlockSpec((1, SSD_INNER), const2),
                  pl.BlockSpec((1, SSD_INNER), const2),
                  pl.BlockSpec(w_bf16.shape, const2, **_ONCE)],
        out_specs=pl.BlockSpec((1, PROJ_TILE, D), row),
        out_shape=jax.ShapeDtypeStruct((B, N, D), F32),
        scratch_shapes=[pltpu.VMEM((PROJ_TILE, w_bf16.shape[0]), BF16)],
        compiler_params=_cparams(("arbitrary", "arbitrary")),
        name="rec_combine",
    )(X, Yf, Yb, G, A, modc, dskip_row, snw_row, w_bf16)


def _recurrent_layer(X, nw, modc, w_in, w_out, cos_t, sin_t, ret_decay_logit, conv_w, conv_b, dt_bias,
                     a_log, d_skip, ssd_norm_w, n_lat_tiles):
    D = X.shape[-1]
    w_pad = jnp.pad(w_in, ((0, 0), (0, REC_IN_PAD - w_in.shape[1]))).astype(BF16)
    pad_row = lambda v, at: jnp.pad(v.reshape(1, -1), ((0, 0), (at, PAIR - at - v.size)))
    RQ, RK, RV, G, XBC, DT = _rec_project(X, nw, modc, w_pad, cos_t, sin_t, pad_row(dt_bias, 0), n_lat_tiles)
    A = _ssd_conv(XBC, conv_w, conv_b, n_lat_tiles * ROW_TILE)
    alog_row, logit_row = pad_row(a_log, 0), pad_row(ret_decay_logit, RET_LANE0)
    Yf = _bidir_scan(RQ, RK, RV, A, DT, alog_row, logit_row, n_lat_tiles, False)
    Yb = _bidir_scan(RQ, RK, RV, A, DT, alog_row, logit_row, n_lat_tiles, True)
    return _rec_combine(X, Yf, Yb, G, A, modc, jnp.repeat(d_skip, SSD_HEAD_DIM).reshape(1, -1),
                        ssd_norm_w.reshape(1, -1), w_out.astype(BF16), n_lat_tiles)


SEL_CHUNK = 256
COMBINE_TILE = 256
FFN_ROW_CHUNK = 1024
GATHER_EXPERTS = 4


def _router_kernel(x_ref, nw_ref, mod_ref, rwt_ref, h_ref, aff_ref, *, n_lat):
    h = _modulated(x_ref[0], nw_ref[...], mod_ref, 3, 4, n_lat)
    h_ref[0] = h.astype(BF16)
    logits = lax.dot_general(rwt_ref[...], h, (((1,), (1,)), ((), ())),
                             precision=lax.Precision.HIGHEST, preferred_element_type=F32)
    e = jnp.exp(logits - jnp.max(logits, axis=0, keepdims=True))
    aff_ref[0] = e / jnp.sum(e, axis=0, keepdims=True)


def _mod_router(X, nw, modc, router_wt, n_lat_tiles):
    B, N, D = X.shape
    E = router_wt.shape[0]
    return pl.pallas_call(
        functools.partial(_router_kernel, n_lat=n_lat_tiles * ROW_TILE),
        grid=(B, N // PROJ_TILE),
        in_specs=[pl.BlockSpec((1, PROJ_TILE, D), lambda b, i: (b, i, 0)),
                  pl.BlockSpec((1, D), lambda b, i: (0, 0)),
                  pl.BlockSpec((1, 2, 6, D), _MOD_BOTH),
                  pl.BlockSpec((E, D), lambda b, i: (0, 0))],
        out_specs=[pl.BlockSpec((1, PROJ_TILE, D), lambda b, i: (b, i, 0)),
                   pl.BlockSpec((1, E, PROJ_TILE), lambda b, i: (b, 0, i))],
        out_shape=[jax.ShapeDtypeStruct((B, N, D), BF16),
                   jax.ShapeDtypeStruct((B, E, N), F32)],
        compiler_params=_cparams(("arbitrary", "arbitrary")),
        name="mod_router",
    )(X, nw.reshape(1, D), modc, router_wt)


def _count(mask_f32):
    return jnp.sum(mask_f32, axis=-1, keepdims=True)


def _select_top(a, cap):
    E, n = a.shape
    v = pltpu.bitcast(a, jnp.int32)
    thr = jnp.zeros((E, 1), jnp.int32)
    for bit in range(30, -1, -1):
        cand = thr | (1 << bit)
        thr = jnp.where(_count(jnp.where(v >= cand, 1.0, 0.0)) >= cap, cand, thr)
    gt = v > thr
    eq = v == thr
    need = cap - _count(jnp.where(gt, 1.0, 0.0))
    idx = lax.broadcasted_iota(jnp.int32, (E, n), 1)
    last = jnp.zeros((E, 1), jnp.int32)
    for bit in range(n.bit_length() - 2, -1, -1):
        cand = last | (1 << bit)
        below = _count(jnp.where(eq, jnp.where(idx < cand, 1.0, 0.0), 0.0))
        last = jnp.where(below < need, cand, last)
    sel = jnp.where(gt, 1.0, jnp.where(eq, jnp.where(idx <= last, 1.0, 0.0), 0.0))
    si = lax.broadcasted_iota(jnp.int32, (SEL_CHUNK, SEL_CHUNK), 0)
    sj = lax.broadcasted_iota(jnp.int32, (SEL_CHUNK, SEL_CHUNK), 1)
    before = jnp.where(si < sj, 1.0, 0.0).astype(BF16)
    base = jnp.zeros((E, 1), F32)
    pos = []
    for k in range(n // SEL_CHUNK):
        sk = sel[:, k * SEL_CHUNK:(k + 1) * SEL_CHUNK]
        pos.append(jnp.dot(sk.astype(BF16), before, preferred_element_type=F32) + base)
        base = base + _count(sk)
    pos = jnp.concatenate(pos, axis=1) if len(pos) > 1 else pos[0]
    return jnp.where(sel > 0.0, pos, -1.0), jnp.where(sel > 0.0, a, 0.0)


def _select_kernel(aff_ref, slot_ref, slot_t_ref, gate_t_ref, *, n_lat, cap_lat, cap_ctx):
    n_all = aff_ref.shape[2]
    E = aff_ref.shape[1]
    pad_rows = PAIR - E
    for lo, hi, cap in ((0, n_lat, cap_lat), (n_lat, n_all, cap_ctx)):
        slot, gate = _select_top(aff_ref[0, :, lo:hi], cap)
        slot_ref[0, :, lo:hi] = slot
        slot_t_ref[0, lo:hi, :] = jnp.concatenate([slot, jnp.full((pad_rows, hi - lo), -1.0, F32)], axis=0).T
        gate_t_ref[0, lo:hi, :] = jnp.concatenate([gate, jnp.zeros((pad_rows, hi - lo), F32)], axis=0).T


def _ec_select(aff, n_lat, cap_lat, cap_ctx):
    B, E, N = aff.shape
    return pl.pallas_call(
        functools.partial(_select_kernel, n_lat=n_lat, cap_lat=cap_lat, cap_ctx=cap_ctx),
        grid=(B,),
        in_specs=[pl.BlockSpec((1, E, N), lambda b: (b, 0, 0))],
        out_specs=[pl.BlockSpec((1, E, N), lambda b: (b, 0, 0)),
                   pl.BlockSpec((1, N, PAIR), lambda b: (b, 0, 0)),
                   pl.BlockSpec((1, N, PAIR), lambda b: (b, 0, 0))],
        out_shape=[jax.ShapeDtypeStruct((B, E, N), F32),
                   jax.ShapeDtypeStruct((B, N, PAIR), F32),
                   jax.ShapeDtypeStruct((B, N, PAIR), F32)],
        compiler_params=_cparams(("arbitrary",)),
        name="ec_select",
    )(aff)


def _gather_kernel(slot_ref, h_ref, xl_ref, xc_ref, *, n_lat):
    n_e = xl_ref.shape[0]
    e0 = pl.multiple_of(pl.program_id(1) * n_e, n_e)
    n_all = h_ref.shape[1]
    for lo, hi, x_ref in ((0, n_lat, xl_ref), (n_lat, n_all, xc_ref)):
        cap = x_ref.shape[1]
        c = lax.broadcasted_iota(jnp.int32, (cap, hi - lo), 0).astype(F32)
        onehot = jnp.concatenate(
            [jnp.where(slot_ref[0, pl.ds(e0 + j, 1), lo:hi] == c, 1.0, 0.0).astype(BF16)
             for j in range(n_e)], axis=0)
        x = jnp.dot(onehot, h_ref[0, lo:hi, :], preferred_element_type=F32)
        for j in range(n_e):
            x_ref[j] = x[j * cap:(j + 1) * cap, :].astype(BF16)


def _ec_gather(slot, H, n_lat, cap_lat, cap_ctx):
    B, E, N = slot.shape
    D = H.shape[-1]
    return pl.pallas_call(
        functools.partial(_gather_kernel, n_lat=n_lat),
        grid=(B, E // GATHER_EXPERTS),
        in_specs=[pl.BlockSpec((1, E, N), lambda b, e: (b, 0, 0)),
                  pl.BlockSpec((1, N, D), lambda b, e: (b, 0, 0))],
        out_specs=[pl.BlockSpec((GATHER_EXPERTS, cap_lat, D), lambda b, e: (e, b, 0)),
                   pl.BlockSpec((GATHER_EXPERTS, cap_ctx, D), lambda b, e: (e, b, 0))],
        out_shape=[jax.ShapeDtypeStruct((E, B * cap_lat, D), BF16),
                   jax.ShapeDtypeStruct((E, B * cap_ctx, D), BF16)],
        compiler_params=_cparams(("arbitrary", "arbitrary")),
        name="ec_gather",
    )(slot, H)


def _ffn_kernel(*refs, n_groups, row_chunk):
    x_refs = refs[:n_groups]
    wg_ref, wu_ref, wd_ref = refs[n_groups:n_groups + 3]
    o_refs = refs[n_groups + 3:2 * n_groups + 3]
    acc_refs = refs[2 * n_groups + 3:3 * n_groups + 3]
    wg_s, wu_s, wd_s = refs[3 * n_groups + 3:]
    f = pl.program_id(2)
    n_f = pl.num_programs(2)
    wg_s[...] = wg_ref[0, 0].astype(BF16)
    wu_s[...] = wu_ref[0, 0].astype(BF16)
    wd_s[...] = wd_ref[0, 0].astype(BF16)

    @pl.when(f == 0)
    def _():
        for acc_ref in acc_refs:
            acc_ref[...] = jnp.zeros_like(acc_ref)

    def chunk(x_ref, acc_ref, start, size):
        rows = pl.ds(start, size)
        xr = x_ref[0, rows, :]
        a = jnp.dot(xr, wg_s[...], preferred_element_type=F32)
        u = jnp.dot(xr, wu_s[...], preferred_element_type=F32)
        hm = (a * jax.nn.sigmoid(a) * u).astype(BF16)
        acc_ref[rows, :] += jnp.dot(hm, wd_s[...], preferred_element_type=F32)

    for x_ref, acc_ref in zip(x_refs, acc_refs):
        n_rows = x_ref.shape[1]
        for start in range(0, n_rows, row_chunk):
            chunk(x_ref, acc_ref, start, min(row_chunk, n_rows - start))

    @pl.when(f == n_f - 1)
    def _():
        for o_ref, acc_ref in zip(o_refs, acc_refs):
            o_ref[0] = acc_ref[...].astype(o_ref.dtype)


def _expert_ffn(xs, w_gate, w_up, w_down, layer):
    E, _, D = xs[0].shape
    F = w_gate.shape[-1]
    m_tiles = 2
    tms = [x.shape[1] // m_tiles for x in xs]
    n = len(xs)
    return pl.pallas_call(
        functools.partial(_ffn_kernel, n_groups=n, row_chunk=FFN_ROW_CHUNK),
        grid=(E, m_tiles, F // FF_TILE),
        in_specs=[pl.BlockSpec((1, tm, D), lambda e, m, f: (e, m, 0)) for tm in tms]
        + [pl.BlockSpec((1, 1, D, FF_TILE), lambda e, m, f: (layer, e, 0, f)),
           pl.BlockSpec((1, 1, D, FF_TILE), lambda e, m, f: (layer, e, 0, f)),
           pl.BlockSpec((1, 1, FF_TILE, D), lambda e, m, f: (layer, e, f, 0))],
        out_specs=[pl.BlockSpec((1, tm, D), lambda e, m, f: (e, m, 0)) for tm in tms],
        out_shape=[jax.ShapeDtypeStruct(x.shape, BF16) for x in xs],
        scratch_shapes=[pltpu.VMEM((tm, D), F32) for tm in tms]
        + [pltpu.VMEM((D, FF_TILE), BF16), pltpu.VMEM((D, FF_TILE), BF16), pltpu.VMEM((FF_TILE, D), BF16)],
        compiler_params=_cparams(("arbitrary", "arbitrary", "arbitrary")),
        name="expert_ffn",
    )(*xs, w_gate, w_up, w_down)


def _combine_kernel(x_ref, slot_t_ref, gate_t_ref, yl_ref, yc_ref, mod_ref, o_ref, *, n_lat_tiles):
    E = yl_ref.shape[0]

    def scatter(y_ref):
        cap = y_ref.shape[1]
        c = lax.broadcasted_iota(jnp.int32, (COMBINE_TILE, cap), 1).astype(F32)
        acc = jnp.zeros((COMBINE_TILE, x_ref.shape[2]), F32)
        for e in range(E):
            onehot = jnp.where(slot_t_ref[0, :, e:e + 1] == c, 1.0, 0.0).astype(BF16)
            acc = acc + gate_t_ref[0, :, e:e + 1] * jnp.dot(onehot, y_ref[e], preferred_element_type=F32)
        o_ref[0] = x_ref[0] + mod_ref[0, 0, 5:6, :] * acc

    i = pl.program_id(1)

    @pl.when(i < n_lat_tiles)
    def _():
        scatter(yl_ref)

    @pl.when(i >= n_lat_tiles)
    def _():
        scatter(yc_ref)


def _ec_combine(X, slot_t, gate_t, Yl, Yc, modc, n_lat_tiles, cap_lat, cap_ctx):
    B, N, D = X.shape
    E = Yl.shape[0]
    row = lambda b, i: (b, i, 0)
    return pl.pallas_call(
        functools.partial(_combine_kernel, n_lat_tiles=n_lat_tiles),
        grid=(B, N // COMBINE_TILE),
        in_specs=[pl.BlockSpec((1, COMBINE_TILE, D), row),
                  pl.BlockSpec((1, COMBINE_TILE, PAIR), row),
                  pl.BlockSpec((1, COMBINE_TILE, PAIR), row),
                  pl.BlockSpec((E, cap_lat, D), lambda b, i: (0, b, 0)),
                  pl.BlockSpec((E, cap_ctx, D), lambda b, i: (0, b, 0)),
                  pl.BlockSpec((1, 1, 6, D), lambda b, i: (b, i // n_lat_tiles, 0, 0))],
        out_specs=pl.BlockSpec((1, COMBINE_TILE, D), row),
        out_shape=jax.ShapeDtypeStruct((B, N, D), F32),
        compiler_params=_cparams(("arbitrary", "arbitrary")),
        name="ec_combine",
    )(X, slot_t, gate_t, Yl, Yc, modc)


def _moe_layer(X, nw, modc, router_w, w_gate, w_up, w_down, layer, n_lat_tiles):
    B, N, D = X.shape
    n_lat = n_lat_tiles * ROW_TILE
    cap_lat = EC_CAPACITY * n_lat // N_EXPERTS
    cap_ctx = EC_CAPACITY * (N - n_lat) // N_EXPERTS
    H, aff = _mod_router(X, nw, modc, router_w.T, n_lat_tiles)
    slot, slot_t, gate_t = _ec_select(aff, n_lat, cap_lat, cap_ctx)
    Xl, Xc = _ec_gather(slot, H, n_lat, cap_lat, cap_ctx)
    Yl, Yc = _expert_ffn([Xl, Xc], w_gate, w_up, w_down, layer)
    return _ec_combine(X, slot_t, gate_t, Yl, Yc, modc, n_lat_tiles, cap_lat, cap_ctx)


def _final_norm_kernel(x_ref, w_ref, o_ref):
    x = x_ref[0]
    o_ref[0] = x * lax.rsqrt(jnp.mean(x * x, axis=-1, keepdims=True) + EPS) * w_ref[...]


def _final_norm(X, w, n_lat):
    B, _, D = X.shape
    row = lambda b, i: (b, i, 0)
    return pl.pallas_call(
        _final_norm_kernel,
        grid=(B, n_lat // ROW_TILE),
        in_specs=[pl.BlockSpec((1, ROW_TILE, D), row), pl.BlockSpec((1, D), lambda b, i: (0, 0))],
        out_specs=pl.BlockSpec((1, ROW_TILE, D), row),
        out_shape=jax.ShapeDtypeStruct((B, n_lat, D), F32),
        compiler_params=_cparams(("arbitrary", "arbitrary")),
        name="final_norm",
    )(X, w.reshape(1, D))


def _split(z, sizes):
    idx = np.cumsum(sizes)[:-1].tolist()
    return jnp.split(z, idx, axis=-1)


def _rmsnorm(x, w=None):
    y = x * lax.rsqrt(jnp.mean(x * x, axis=-1, keepdims=True) + EPS)
    if w is not None:
        y = y * w
    return y


def _rope_tables(n):
    t = jnp.arange(n)
    row = (t // GRID_W).astype(F32)
    col = (t % GRID_W).astype(F32)
    n_freq = HEAD_DIM // 4
    inv = ROPE_THETA ** (-jnp.arange(n_freq, dtype=F32) / n_freq)
    ang = jnp.concatenate([row[:, None] * inv, col[:, None] * inv], axis=-1)
    return jnp.cos(ang), jnp.sin(ang)


def _rope(x, cos, sin):
    shape = (cos.shape[0],) + (1,) * (x.ndim - 3) + (cos.shape[1],)
    c, s = cos.reshape(shape), sin.reshape(shape)
    x1, x2 = jnp.split(x, 2, axis=-1)
    return jnp.concatenate([x1 * c - x2 * s, x2 * c + x1 * s], axis=-1)


def _attend(q, k, v, shared_k):
    scale = q.shape[-1] ** -0.5
    k_sub = 'bkhd' if shared_k else 'bkhmd'

    def one_block(qb):
        s = jnp.einsum(f'bqhmd,{k_sub}->bhmqk', qb, k, preferred_element_type=F32) * scale
        p = jax.nn.softmax(s, axis=-1)
        return jnp.einsum('bhmqk,bkhv->bqhmv', p, v)

    B, Sq = q.shape[:2]
    nb = Sq // Q_BLOCK
    qb = jnp.moveaxis(q.reshape(B, nb, Q_BLOCK, *q.shape[2:]), 1, 0)
    out = lax.map(one_block, qb)
    return jnp.moveaxis(out, 0, 1).reshape(B, Sq, *out.shape[3:])


def _attention_core(zl, zc, cos, sin, q_norm_w, k_norm_w, lam, diff_norm_w, lambda_init, need_ctx):
    def project(z, use_rope):
        B, n, _ = z.shape
        aq, ak, av, bq, bk, bv = _split(z, ATT_SPLITS)
        aq = _rmsnorm(aq.reshape(B, n, GQA_HEADS, HEAD_DIM), q_norm_w)
        ak = _rmsnorm(ak.reshape(B, n, GQA_KV_HEADS, HEAD_DIM), k_norm_w)
        bq = bq.reshape(B, n, DIFF_HEADS, 2, HEAD_DIM)
        bk = bk.reshape(B, n, DIFF_HEADS, 2, HEAD_DIM)
        if use_rope:
            aq, ak, bq, bk = (_rope(t, cos, sin) for t in (aq, ak, bq, bk))
        aq = aq.reshape(B, n, GQA_KV_HEADS, GQA_HEADS // GQA_KV_HEADS, HEAD_DIM)
        av = av.reshape(B, n, GQA_KV_HEADS, HEAD_DIM)
        bv = bv.reshape(B, n, DIFF_HEADS, 2 * HEAD_DIM)
        return (aq, bq), (ak, av, bk, bv)

    lat_q, lat_kv = project(zl, True)
    ctx_q, ctx_kv = project(zc, False)
    lam_val = jnp.exp(jnp.sum(lam[0] * lam[1])) - jnp.exp(jnp.sum(lam[2] * lam[3])) + lambda_init

    def mix(q_side, kv):
        aq, bq = q_side
        ak, av, bk, bv = kv
        B, n = aq.shape[:2]
        a = _attend(aq, ak, av, True).reshape(B, n, -1)
        o = _attend(bq, bk, bv, False)
        o = o[..., 0, :] - lam_val * o[..., 1, :]
        b = (_rmsnorm(o, diff_norm_w) * (1 - lambda_init)).reshape(B, n, -1)
        return jnp.concatenate([a, b], axis=-1)

    kv_all = tuple(jnp.concatenate([l, c], axis=1) for l, c in zip(lat_kv, ctx_kv))
    yl = mix(lat_q, kv_all)
    yc = mix(ctx_q, ctx_kv) if need_ctx else jnp.zeros((zc.shape[0], zc.shape[1], 2 * HALF_W), F32)
    return yl, yc


def _chunk_scan(q, k, v, log_a, h0):
    B, S = q.shape[:2]
    n = S // CHUNK
    tril = jnp.tril(jnp.ones((CHUNK, CHUNK), dtype=bool))

    def to_chunks(a):
        return jnp.moveaxis(a.reshape(B, n, CHUNK, *a.shape[2:]), 1, 0)

    def step(h, inp):
        qc, kc, vc, lac = inp
        cum = jnp.cumsum(lac.astype(F32), axis=1)
        cum_t = jnp.moveaxis(cum, 1, -1)
        seg = jnp.exp(jnp.where(tril, cum_t[..., :, None] - cum_t[..., None, :], -jnp.inf))
        qk = jnp.einsum('btgk,bsgk->bgts', qc, kc)
        y = (jnp.einsum('bgts,bgrts,bsgrv->btgrv', qk, seg, vc)
             + jnp.einsum('btgk,bgrkv->btgrv', qc, h) * jnp.exp(cum)[..., None])
        last = cum[:, -1]
        h_new = (h * jnp.exp(last)[..., None, None]
                 + jnp.einsum('bsgk,bsgr,bsgrv->bgrkv', kc, jnp.exp(last[:, None] - cum), vc))
        return h_new, y

    h, ys = lax.scan(step, h0, tuple(to_chunks(a) for a in (q, k, v, log_a)))
    return jnp.moveaxis(ys, 0, 1).reshape(B, S, *ys.shape[3:]), h


def _prefix_scan(lat, ctx, reverse):
    if reverse:
        lat = tuple(jnp.flip(a, 1) for a in lat)
        ctx = tuple(jnp.flip(a, 1) for a in ctx)
    q, v = ctx[0], ctx[2]
    h0 = jnp.zeros((q.shape[0], q.shape[2], v.shape[3], q.shape[3], v.shape[4]), F32)
    y_c, h_c = _chunk_scan(*ctx, h0)
    y_l, _ = _chunk_scan(*lat, h_c)
    if reverse:
        y_l, y_c = jnp.flip(y_l, 1), jnp.flip(y_c, 1)
    return y_l, y_c


def _bidir(lat_f, ctx_f, lat_b, ctx_b):
    yl_f, yc_f = _prefix_scan(lat_f, ctx_f, False)
    yl_b, yc_b = _prefix_scan(lat_b, ctx_b, True)
    return yl_f + yl_b, yc_f + yc_b


def _dwconv(x, w, b):
    K = w.shape[0]
    out = lax.conv_general_dilated(x, w[:, None, :], window_strides=(1,),
                                   padding=[(K // 2, K // 2)],
                                   dimension_numbers=('NWC', 'WIO', 'NWC'),
                                   feature_group_count=x.shape[-1])
    return out + b


def _recurrent_core(zl, zc, cos, sin, ret_decay_logit, conv_w, conv_b, dt_bias, a_log, d_skip,
                    ssd_norm_w):
    log_gamma = jax.nn.log_sigmoid(ret_decay_logit)
    a_neg = -jnp.exp(a_log)
    r_per_g = SSD_HEADS // SSD_GROUPS

    def project(z, use_rope):
        B, n, _ = z.shape
        rq, rk, rv, rg, zz, xbc, dt = _split(z, REC_SPLITS)
        rq = rq.reshape(B, n, RET_HEADS, RET_QK_DIM)
        rk = rk.reshape(B, n, RET_HEADS, RET_QK_DIM) * RET_QK_DIM ** -0.5
        if use_rope:
            rq, rk = _rope(rq, cos, sin), _rope(rk, cos, sin)
        rv = rv.reshape(B, n, RET_HEADS, 1, RET_V_DIM)
        ret = [(rq, rk, rv, jnp.broadcast_to(log_gamma[d][:, None], (B, n, RET_HEADS, 1)))
               for d in (0, 1)]
        xbc = jax.nn.silu(_dwconv(xbc, conv_w, conv_b))
        xs, bm, cm = _split(xbc, (SSD_INNER, SSD_GROUPS * SSD_STATE, SSD_GROUPS * SSD_STATE))
        xs = xs.reshape(B, n, SSD_HEADS, SSD_HEAD_DIM)
        bm = bm.reshape(B, n, SSD_GROUPS, SSD_STATE)
        cm = cm.reshape(B, n, SSD_GROUPS, SSD_STATE)
        dt = jax.nn.softplus(dt.reshape(B, n, 2, SSD_HEADS) + dt_bias)
        ssd = [(cm, bm,
                (xs * dt[:, :, d, :, None]).reshape(B, n, SSD_GROUPS, r_per_g, SSD_HEAD_DIM),
                (dt[:, :, d] * a_neg[d]).reshape(B, n, SSD_GROUPS, r_per_g))
               for d in (0, 1)]
        return ret, ssd, rg, zz, xs

    lat_ret, lat_ssd, lat_rg, lat_z, lat_xs = project(zl, True)
    ctx_ret, ctx_ssd, ctx_rg, ctx_z, ctx_xs = project(zc, False)
    ret_l, ret_c = _bidir(lat_ret[0], ctx_ret[0], lat_ret[1], ctx_ret[1])
    ssd_l, ssd_c = _bidir(lat_ssd[0], ctx_ssd[0], lat_ssd[1], ctx_ssd[1])

    def combine(ret_y, ssd_y, rg, zz, xs):
        B, n = rg.shape[:2]
        r = _rmsnorm(ret_y.reshape(B, n, RET_HEADS, RET_V_DIM)).reshape(B, n, -1) * jax.nn.silu(rg)
        s = ssd_y.reshape(B, n, SSD_HEADS, SSD_HEAD_DIM) + d_skip[:, None] * xs
        s = s.reshape(B, n, -1) * jax.nn.silu(zz)
        s = _rmsnorm(s.reshape(B, n, SSD_GROUPS, -1)).reshape(B, n, -1) * ssd_norm_w
        return jnp.concatenate([r, s], axis=-1)

    return combine(ret_l, ssd_l, lat_rg, lat_z, lat_xs), combine(ret_c, ssd_c, ctx_rg, ctx_z, ctx_xs)


def _dispatch_tokens(h, aff):
    B, n, D = h.shape
    cap = EC_CAPACITY * n // N_EXPERTS
    g, idx = lax.top_k(jnp.swapaxes(aff, 1, 2), cap)
    xin = jax.vmap(lambda xs, i: xs[i])(h, idx)
    return xin, g, idx


def _combine_tokens(y, g, idx, n):
    D = y.shape[-1]
    y = y * g[..., None]
    return jax.vmap(lambda val, i: jax.ops.segment_sum(val.reshape(-1, D), i.reshape(-1),
                                                       num_segments=n))(y, idx)


def kernel(x, c, ctx, c_ctx, ada_w, ada_b, norm1_w, norm2_w, att_w_in, att_w_out, att_q_norm_w,
           att_k_norm_w, diff_lambda, diff_norm_w, rec_w_in, rec_w_out, ret_decay_logit, ssd_conv_w,
           ssd_conv_b, ssd_dt_bias, ssd_a_log, ssd_d_skip, ssd_norm_w, router_w, expert_w_gate,
           expert_w_up, expert_w_down, final_norm_w):
    B, n_lat, D = x.shape
    n_ctx = ctx.shape[1]
    n_lat_tiles = n_lat // ROW_TILE
    cos, sin = _rope_tables(n_lat)
    reps = PAIR // cos.shape[1]
    cos_t = jnp.concatenate([jnp.tile(cos, (1, reps)), jnp.ones((n_ctx, PAIR), F32)], axis=0)
    sin_t = jnp.concatenate([jnp.tile(jnp.concatenate([-sin, sin], axis=1), (1, reps // 2)),
                             jnp.zeros((n_ctx, PAIR), F32)], axis=0)
    gmat = jnp.kron(jnp.eye(GQA_HEADS, dtype=F32), jnp.ones((HEAD_DIM, HEAD_DIM), F32)).astype(BF16)

    n_cond = 24
    cvec = jnp.concatenate([c, c_ctx[None, :], jnp.zeros((n_cond - B - 1, D), F32)], axis=0)
    mods = _ada_modulation(cvec, ada_w, ada_b)

    X = jnp.concatenate([x, ctx], axis=1)
    for layer in range(DEPTH):
        i = layer // 2
        need_ctx = layer < DEPTH - 1
        mod_lat = mods[layer, :B].reshape(B, 1, 6, D)
        mod_ctx = jnp.broadcast_to(mods[layer, B].reshape(1, 1, 6, D), (B, 1, 6, D))
        modc = jnp.concatenate([mod_lat, mod_ctx], axis=1)

        if layer % 2 == 0:
            lambda_init = 0.8 - 0.6 * math.exp(-0.3 * layer)
            Q, K, V = _att_project(X, norm1_w[layer], modc, att_w_in[i].astype(BF16), cos_t, sin_t, gmat,
                                   jnp.tile(att_q_norm_w[i], GQA_HEADS).reshape(1, -1),
                                   jnp.tile(att_k_norm_w[i], GQA_KV_HEADS).reshape(1, -1), n_lat_tiles)
            X = _attention(X, Q, K, V, modc, diff_lambda[i], diff_norm_w[i], att_w_out[i].astype(BF16),
                           lambda_init, n_lat_tiles)
        else:
            X = _recurrent_layer(X, norm1_w[layer], modc, rec_w_in[i], rec_w_out[i], cos_t, sin_t,
                                 ret_decay_logit[i], ssd_conv_w[i], ssd_conv_b[i], ssd_dt_bias[i],
                                 ssd_a_log[i], ssd_d_skip[i], ssd_norm_w[i], n_lat_tiles)

        X = _moe_layer(X, norm2_w[layer], modc, router_w[layer], expert_w_gate, expert_w_up, expert_w_down,
                       layer, n_lat_tiles)

    return _final_norm(X, final_norm_w, n_lat)
```

```python
import functools
import math

import jax
import jax.numpy as jnp
import numpy as np
from jax import lax
from jax.experimental import pallas as pl
from jax.experimental.pallas import tpu as pltpu

D_MODEL = 1024
DEPTH = 4
GRID_W = 64
HEAD_DIM = 64
ROPE_THETA = 10000.0
Q_BLOCK = 128
CHUNK = 128
EPS = 1e-6
HALF_W = D_MODEL // 2
GQA_HEADS = HALF_W // HEAD_DIM
GQA_KV_HEADS = GQA_HEADS // 4
DIFF_HEADS = HALF_W // (2 * HEAD_DIM)
RET_HEADS = HALF_W // (2 * HEAD_DIM)
RET_QK_DIM = HEAD_DIM
RET_V_DIM = 2 * HEAD_DIM
SSD_HEAD_DIM = HEAD_DIM
SSD_HEADS = HALF_W // SSD_HEAD_DIM
SSD_GROUPS = 2
SSD_STATE = 128
SSD_INNER = SSD_HEADS * SSD_HEAD_DIM
SSD_XBC = SSD_INNER + 2 * SSD_GROUPS * SSD_STATE
N_EXPERTS = 16
EC_CAPACITY = 2
EXPERT_FF = ((8 * D_MODEL // 3 + 255) // 256) * 256

ATT_SPLITS = (GQA_HEADS * HEAD_DIM, GQA_KV_HEADS * HEAD_DIM, GQA_KV_HEADS * HEAD_DIM,
              DIFF_HEADS * 2 * HEAD_DIM, DIFF_HEADS * 2 * HEAD_DIM, DIFF_HEADS * 2 * HEAD_DIM)
REC_SPLITS = (RET_HEADS * RET_QK_DIM, RET_HEADS * RET_QK_DIM, RET_HEADS * RET_V_DIM,
              RET_HEADS * RET_V_DIM, SSD_INNER, SSD_XBC, 2 * SSD_HEADS)

ROW_TILE = 256
PROJ_TILE = 768
FF_TILE = 256
VMEM_LIMIT = 56 * 1024 * 1024
BF16 = jnp.bfloat16
F32 = jnp.float32


def _cparams(sem):
    return pltpu.CompilerParams(dimension_semantics=sem, vmem_limit_bytes=VMEM_LIMIT)


def _ada_kernel(c_ref, w_ref, b_ref, o_ref):
    c = c_ref[...]
    s = c * jax.nn.sigmoid(c)
    o_ref[0] = jnp.dot(s, w_ref[0], precision=lax.Precision.HIGHEST,
                       preferred_element_type=F32) + b_ref[0]


def _ada_modulation(cvec, ada_w, ada_b):
    R, D = cvec.shape
    n_out = ada_w.shape[-1]
    tn = 1536
    return pl.pallas_call(
        _ada_kernel,
        grid=(DEPTH, n_out // tn),
        in_specs=[pl.BlockSpec((R, D), lambda l, j: (0, 0)),
                  pl.BlockSpec((1, D, tn), lambda l, j: (l, 0, j)),
                  pl.BlockSpec((1, 1, tn), lambda l, j: (l, 0, j))],
        out_specs=pl.BlockSpec((1, R, tn), lambda l, j: (l, 0, j)),
        out_shape=jax.ShapeDtypeStruct((DEPTH, R, n_out), F32),
        compiler_params=_cparams(("arbitrary", "arbitrary")),
        name="ada_modulation",
    )(cvec, ada_w, ada_b.reshape(DEPTH, 1, n_out))


def _row_mod(mod_ref, idx, n_rows, n_lat):
    row = pl.program_id(1) * n_rows + lax.broadcasted_iota(jnp.int32, (n_rows, 1), 0)
    return jnp.where(row < n_lat, mod_ref[0, 0, idx:idx + 1, :], mod_ref[0, 1, idx:idx + 1, :])


def _modulated(x, nw, mod_ref, shift_idx, scale_idx, n_lat):
    n_rows = x.shape[0]
    ms = jnp.mean(x * x, axis=-1, keepdims=True)
    y = x * lax.rsqrt(ms + EPS) * nw
    return (y * (1.0 + _row_mod(mod_ref, scale_idx, n_rows, n_lat))
            + _row_mod(mod_ref, shift_idx, n_rows, n_lat))


_MOD_BOTH = lambda b, i: (b, 0, 0, 0)
_ONCE = dict(pipeline_mode=pl.Buffered(1))


PAIR = 2 * HEAD_DIM
N_PAIR_GROUPS = 2 * HALF_W // PAIR
ATT_KV_W = GQA_KV_HEADS * PAIR + DIFF_HEADS * PAIR


def _head_sumsq(x, g):
    x2 = x * x
    hi = x2.astype(BF16)
    lo = (x2 - hi.astype(F32)).astype(BF16)
    return jnp.dot(hi, g, preferred_element_type=F32) + jnp.dot(lo, g, preferred_element_type=F32)


def _rope_lanes(x, cos, sin_signed):
    W = x.shape[-1]
    half = HEAD_DIM // 2
    lane = lax.broadcasted_iota(jnp.int32, x.shape, 1) & (HEAD_DIM - 1)
    partner = jnp.where(lane < half, pltpu.roll(x, W - half, 1), pltpu.roll(x, half, 1))
    return x * cos + partner * sin_signed


def _att_project_kernel(x_ref, nw_ref, mod_ref, w_ref, cos_ref, sin_ref, g_ref, qw_ref, kw_ref,
                        q_ref, k_ref, v_ref, *, n_lat):
    h = _modulated(x_ref[0], nw_ref[...], mod_ref, 0, 1, n_lat)
    z = jnp.dot(h.astype(BF16), w_ref[...], preferred_element_type=F32)
    o_aq, o_ak, o_av, o_bq, o_bk, o_bv = np.cumsum((0,) + ATT_SPLITS[:-1]).tolist()
    n_aq, n_ak, n_bq = ATT_SPLITS[0], ATT_SPLITS[1], ATT_SPLITS[3]
    cos, sin = cos_ref[...], sin_ref[...]
    cos4 = jnp.concatenate([cos] * (n_aq // PAIR), axis=1)
    sin4 = jnp.concatenate([sin] * (n_aq // PAIR), axis=1)
    inv_d = 1.0 / HEAD_DIM
    q_scale = HEAD_DIM ** -0.5 * math.log2(math.e)

    aq = z[:, o_aq:o_aq + n_aq]
    aq = aq * lax.rsqrt(_head_sumsq(aq, g_ref[...]) * inv_d + EPS) * qw_ref[...]
    aq = _rope_lanes(aq, cos4, sin4) * q_scale
    bq = _rope_lanes(z[:, o_bq:o_bq + n_bq], cos4, sin4) * q_scale
    q = jnp.concatenate([aq, bq], axis=1)
    lo = (lax.broadcasted_iota(jnp.int32, q.shape, 1) & (PAIR - 1)) < HEAD_DIM
    q_ref[0, :, 0:q.shape[1]] = jnp.where(lo, q, 0.0).astype(BF16)
    q_ref[0, :, q.shape[1]:2 * q.shape[1]] = jnp.where(lo, 0.0, q).astype(BF16)

    ak = z[:, o_ak:o_ak + n_ak]
    ak = ak * lax.rsqrt(_head_sumsq(ak, g_ref[0:n_ak, 0:n_ak]) * inv_d + EPS) * kw_ref[...]
    ak = _rope_lanes(ak, cos, sin)
    av = z[:, o_av:o_av + n_ak]
    lo_kv = lax.broadcasted_iota(jnp.int32, ak.shape, 1) < HEAD_DIM
    ak_sw, av_sw = pltpu.roll(ak, HEAD_DIM, 1), pltpu.roll(av, HEAD_DIM, 1)
    k_ref[0, :, 0:PAIR] = jnp.where(lo_kv, ak, ak_sw).astype(BF16)
    k_ref[0, :, PAIR:2 * PAIR] = jnp.where(lo_kv, ak_sw, ak).astype(BF16)
    v_ref[0, :, 0:PAIR] = jnp.where(lo_kv, av, av_sw).astype(BF16)
    v_ref[0, :, PAIR:2 * PAIR] = jnp.where(lo_kv, av_sw, av).astype(BF16)
    k_ref[0, :, 2 * PAIR:] = _rope_lanes(z[:, o_bk:o_bk + n_bq], cos4, sin4).astype(BF16)
    v_ref[0, :, 2 * PAIR:] = z[:, o_bv:o_bv + n_bq].astype(BF16)


def _att_project(X, nw, modc, w_bf16, cos_t, sin_t, gmat, qw_t, kw_t, n_lat_tiles):
    B, N, D = X.shape
    n_in = w_bf16.shape[1]
    q_w = 2 * N_PAIR_GROUPS * PAIR
    row = lambda b, i: (b, i, 0)
    const2 = lambda b, i: (0, 0)
    return pl.pallas_call(
        functools.partial(_att_project_kernel, n_lat=n_lat_tiles * ROW_TILE),
        grid=(B, N // PROJ_TILE),
        in_specs=[pl.BlockSpec((1, PROJ_TILE, D), row),
                  pl.BlockSpec((1, D), const2),
                  pl.BlockSpec((1, 2, 6, D), _MOD_BOTH),
                  pl.BlockSpec((D, n_in), const2, **_ONCE),
                  pl.BlockSpec((PROJ_TILE, PAIR), lambda b, i: (i, 0)),
                  pl.BlockSpec((PROJ_TILE, PAIR), lambda b, i: (i, 0)),
                  pl.BlockSpec(gmat.shape, const2),
                  pl.BlockSpec(qw_t.shape, const2),
                  pl.BlockSpec(kw_t.shape, const2)],
        out_specs=[pl.BlockSpec((1, PROJ_TILE, q_w), row),
                   pl.BlockSpec((1, PROJ_TILE, ATT_KV_W), row),
                   pl.BlockSpec((1, PROJ_TILE, ATT_KV_W), row)],
        out_shape=[jax.ShapeDtypeStruct((B, N, q_w), BF16),
                   jax.ShapeDtypeStruct((B, N, ATT_KV_W), BF16),
                   jax.ShapeDtypeStruct((B, N, ATT_KV_W), BF16)],
        compiler_params=_cparams(("arbitrary", "arbitrary")),
        name="att_project",
    )(X, nw.reshape(1, D), modc, w_bf16, cos_t, sin_t, gmat, qw_t, kw_t)


def _attention_kernel(x_ref, q_ref, k_ref, v_ref, mod_ref, lam_ref, dnw_ref, w_ref, o_ref,
                      s_scr, p_scr, y_scr, *, lambda_init, n_lat_tiles):
    n_lat = n_lat_tiles * ROW_TILE
    n_all = k_ref.shape[1]
    q_half = N_PAIR_GROUPS * PAIR
    lam = lam_ref[...]
    lam_val = (jnp.exp(jnp.sum(lam[0:1] * lam[1:2], axis=-1, keepdims=True))
               - jnp.exp(jnp.sum(lam[2:3] * lam[3:4], axis=-1, keepdims=True)) + lambda_init)

    def attend(k0, nk):
        keys = slice(k0, k0 + nk)

        def pair(pg):
            qv = jnp.concatenate([q_ref[0, :, h * q_half + pg * PAIR: h * q_half + (pg + 1) * PAIR]
                                  for h in range(2)], axis=0)
            kd = k_ref[0, keys, _kv_lane(pg): _kv_lane(pg) + PAIR]
            vd = v_ref[0, keys, _kv_lane(pg): _kv_lane(pg) + PAIR]
            s_scr[:, 0:nk] = lax.dot_general(qv, kd, (((1,), (1,)), ((), ())),
                                             preferred_element_type=F32)
            s = s_scr[:, 0:nk]
            p = jnp.exp2(s - jnp.max(s, axis=-1, keepdims=True))
            p_scr[:, 0:nk] = p.astype(BF16)
            o = jnp.dot(p_scr[:, 0:nk], vd, preferred_element_type=F32)
            o = o * (1.0 / jnp.sum(p, axis=-1, keepdims=True))
            return o[0:ROW_TILE], o[ROW_TILE:2 * ROW_TILE]

        lo = lax.broadcasted_iota(jnp.int32, (ROW_TILE, PAIR), 1) < HEAD_DIM
        for pg in range(2 * GQA_KV_HEADS):
            o_even, o_odd = pair(pg)
            y_scr[:, pg * PAIR:(pg + 1) * PAIR] = jnp.where(lo, o_even, o_odd).astype(BF16)
        for pg in range(2 * GQA_KV_HEADS, N_PAIR_GROUPS):
            o_0, o_1 = pair(pg)
            od = o_0 - lam_val * o_1
            od = od * lax.rsqrt(jnp.mean(od * od, axis=-1, keepdims=True) + EPS)
            od = od * dnw_ref[...] * (1.0 - lambda_init)
            y_scr[:, pg * PAIR:(pg + 1) * PAIR] = od.astype(BF16)
        y = jnp.dot(y_scr[...], w_ref[...], preferred_element_type=F32)
        o_ref[0] = x_ref[0] + mod_ref[0, 0, 2:3, :] * y

    i = pl.program_id(1)

    @pl.when(i < n_lat_tiles)
    def _():
        attend(0, n_all)

    @pl.when(i >= n_lat_tiles)
    def _():
        attend(n_lat, n_all - n_lat)


def _kv_lane(pg):
    n_gqa_pairs = 2 * GQA_KV_HEADS
    if pg < n_gqa_pairs:
        return (pg // 2) * PAIR
    return (GQA_KV_HEADS + pg - n_gqa_pairs) * PAIR


def _attention(X, Q, K, V, modc, lam, dnw, w_bf16, lambda_init, n_lat_tiles):
    B, N, D = X.shape
    row = lambda b, i: (b, i, 0)
    whole = lambda b, i: (b, 0, 0)
    const2 = lambda b, i: (0, 0)
    return pl.pallas_call(
        functools.partial(_attention_kernel, lambda_init=lambda_init, n_lat_tiles=n_lat_tiles),
        grid=(B, N // ROW_TILE),
        in_specs=[pl.BlockSpec((1, ROW_TILE, D), row),
                  pl.BlockSpec((1, ROW_TILE, Q.shape[-1]), row),
                  pl.BlockSpec((1, N, K.shape[-1]), whole, pipeline_mode=pl.Buffered(1)),
                  pl.BlockSpec((1, N, V.shape[-1]), whole, pipeline_mode=pl.Buffered(1)),
                  pl.BlockSpec((1, 1, 6, D), lambda b, i: (b, i // n_lat_tiles, 0, 0)),
                  pl.BlockSpec(lam.shape, const2),
                  pl.BlockSpec((1, PAIR), const2),
                  pl.BlockSpec(w_bf16.shape, const2, pipeline_mode=pl.Buffered(1))],
        out_specs=pl.BlockSpec((1, ROW_TILE, D), row),
        out_shape=jax.ShapeDtypeStruct((B, N, D), F32),
        scratch_shapes=[pltpu.VMEM((2 * ROW_TILE, N), F32), pltpu.VMEM((2 * ROW_TILE, N), BF16),
                        pltpu.VMEM((ROW_TILE, w_bf16.shape[0]), BF16)],
        compiler_params=_cparams(("arbitrary", "arbitrary")),
        name="attention",
    )(X, Q, K, V, modc, lam, dnw.reshape(1, PAIR), w_bf16)


REC_IN_PAD = 3200
N_SSD_DT = 2 * SSD_HEADS
RET_LANE0 = N_SSD_DT


def _softplus(x):
    return jnp.maximum(x, 0.0) + jnp.log(1.0 + jnp.exp(-jnp.abs(x)))


def _silu(x):
    return x * jax.nn.sigmoid(x)


def _rec_project_kernel(x_ref, nw_ref, mod_ref, w_ref, cos_ref, sin_ref, dtb_ref,
                        rq_ref, rk_ref, rv_ref, g_ref, xbc_ref, dt_ref, *, n_lat):
    h = _modulated(x_ref[0], nw_ref[...], mod_ref, 0, 1, n_lat)
    z = jnp.dot(h.astype(BF16), w_ref[...], preferred_element_type=F32)
    o_rq, o_rk, o_rv, o_rg, o_z, o_xbc, o_dt = np.cumsum((0,) + REC_SPLITS[:-1]).tolist()
    n_qk = REC_SPLITS[0]
    cos, sin = cos_ref[...], sin_ref[...]
    cos2 = jnp.concatenate([cos] * (n_qk // PAIR), axis=1)
    sin2 = jnp.concatenate([sin] * (n_qk // PAIR), axis=1)
    rq = _rope_lanes(z[:, o_rq:o_rq + n_qk], cos2, sin2)
    lo = (lax.broadcasted_iota(jnp.int32, rq.shape, 1) & (PAIR - 1)) < HEAD_DIM
    rq_ref[0, :, 0:n_qk] = jnp.where(lo, rq, 0.0).astype(BF16)
    rq_ref[0, :, n_qk:2 * n_qk] = jnp.where(lo, 0.0, rq).astype(BF16)
    rk_ref[0] = _rope_lanes(z[:, o_rk:o_rk + n_qk] * (RET_QK_DIM ** -0.5), cos2, sin2).astype(BF16)
    rv_ref[0] = z[:, o_rv:o_rg].astype(BF16)
    g_ref[0] = z[:, o_rg:o_xbc].astype(BF16)
    xbc_ref[0] = z[:, o_xbc:o_dt].astype(BF16)
    dt_ref[0] = _softplus(z[:, o_dt:o_dt + PAIR] + dtb_ref[...])


def _rec_project(X, nw, modc, w_bf16, cos_t, sin_t, dtb_row, n_lat_tiles):
    B, N, D = X.shape
    n_in = w_bf16.shape[1]
    row = lambda b, i: (b, i, 0)
    const2 = lambda b, i: (0, 0)
    widths = (2 * REC_SPLITS[0], REC_SPLITS[1], REC_SPLITS[2], REC_SPLITS[3] + REC_SPLITS[4],
              REC_SPLITS[5], PAIR)
    dtypes = (BF16, BF16, BF16, BF16, BF16, F32)
    return pl.pallas_call(
        functools.partial(_rec_project_kernel, n_lat=n_lat_tiles * ROW_TILE),
        grid=(B, N // PROJ_TILE),
        in_specs=[pl.BlockSpec((1, PROJ_TILE, D), row),
                  pl.BlockSpec((1, D), const2),
                  pl.BlockSpec((1, 2, 6, D), _MOD_BOTH),
                  pl.BlockSpec((D, n_in), const2, **_ONCE),
                  pl.BlockSpec((PROJ_TILE, PAIR), lambda b, i: (i, 0)),
                  pl.BlockSpec((PROJ_TILE, PAIR), lambda b, i: (i, 0)),
                  pl.BlockSpec((1, PAIR), const2)],
        out_specs=[pl.BlockSpec((1, PROJ_TILE, w), row) for w in widths],
        out_shape=[jax.ShapeDtypeStruct((B, N, w), dt) for w, dt in zip(widths, dtypes)],
        compiler_params=_cparams(("arbitrary", "arbitrary")),
        name="rec_project",
    )(X, nw.reshape(1, D), modc, w_bf16, cos_t, sin_t, dtb_row)


def _conv_kernel(x_ref, w_ref, b_ref, o_ref, *, n_lat):
    x = x_ref[0].astype(F32)
    n = x.shape[0]
    t = lax.broadcasted_iota(jnp.int32, x.shape, 0)
    first = (t == 0) | (t == n_lat)
    last = (t == n_lat - 1) | (t == n - 1)
    prev = jnp.where(first, 0.0, pltpu.roll(x, 1, 0))
    nxt = jnp.where(last, 0.0, pltpu.roll(x, n - 1, 0))
    y = prev * w_ref[0:1, :] + x * w_ref[1:2, :] + nxt * w_ref[2:3, :] + b_ref[...]
    o_ref[0] = _silu(y).astype(BF16)


def _ssd_conv(XBC, conv_w, conv_b, n_lat):
    B, N, C = XBC.shape
    tc = 256
    return pl.pallas_call(
        functools.partial(_conv_kernel, n_lat=n_lat),
        grid=(B, C // tc),
        in_specs=[pl.BlockSpec((1, N, tc), lambda b, j: (b, 0, j)),
                  pl.BlockSpec((conv_w.shape[0], tc), lambda b, j: (0, j)),
                  pl.BlockSpec((1, tc), lambda b, j: (0, j))],
        out_specs=pl.BlockSpec((1, N, tc), lambda b, j: (b, 0, j)),
        out_shape=jax.ShapeDtypeStruct((B, N, C), BF16),
        compiler_params=_cparams(("arbitrary", "arbitrary")),
        name="ssd_conv",
    )(XBC, conv_w, conv_b.reshape(1, C))


def _split3(x):
    hi = x.astype(BF16)
    r1 = x - hi.astype(F32)
    mid = r1.astype(BF16)
    lo = (r1 - mid.astype(F32)).astype(BF16)
    return hi, mid, lo


def _scan_kernel(rq_ref, rk_ref, rv_ref, a_ref, dt_ref, alog_ref, logit_ref, y_ref, hret, hssd, *, reverse):
    T = rq_ref.shape[1]
    d = 1 if reverse else 0
    step = pl.program_id(1)

    @pl.when(step == 0)
    def _():
        hret[...] = jnp.zeros_like(hret)
        hssd[...] = jnp.zeros_like(hssd)

    lane = lax.broadcasted_iota(jnp.int32, (1, PAIR), 1)
    a_neg = jnp.where(lane < N_SSD_DT, -jnp.exp(alog_ref[...]), 0.0)
    logit = logit_ref[...]
    log_gamma = jnp.where((lane >= RET_LANE0) & (lane < RET_LANE0 + 2 * RET_HEADS),
                          jnp.minimum(logit, 0.0) - jnp.log(1.0 + jnp.exp(-jnp.abs(logit))), 0.0)
    dt = dt_ref[0]
    la = dt * a_neg + log_gamma

    ti = lax.broadcasted_iota(jnp.int32, (T, T), 0)
    tj = lax.broadcasted_iota(jnp.int32, (T, T), 1)
    causal = (tj >= ti) if reverse else (tj <= ti)
    tri = jnp.where(causal, 1.0, 0.0).astype(BF16)
    hi, mid, lo3 = _split3(la)
    P = (jnp.dot(tri, hi, preferred_element_type=F32) + jnp.dot(tri, mid, preferred_element_type=F32)
         + jnp.dot(tri, lo3, preferred_element_type=F32))
    PT = P.T
    tot = P[0:1, :] if reverse else P[T - 1:T, :]
    E = jnp.exp(P)
    KD = jnp.exp(tot - P)
    ET = jnp.exp(tot)
    lo = lax.broadcasted_iota(jnp.int32, (T, PAIR), 1) < HEAD_DIM
    lo_row = lane < HEAD_DIM
    nt = (((1,), (1,)), ((), ()))
    tn = (((0,), (0,)), ((), ()))

    def decay_matrix(c):
        diff = jnp.minimum(P[:, c:c + 1] - PT[c:c + 1, :], 0.0)
        return jnp.where(causal, jnp.exp(diff), 0.0)

    n_qk = RET_HEADS * RET_QK_DIM
    for h in range(RET_HEADS):
        c = RET_LANE0 + d * RET_HEADS + h
        p, half = h // 2, h % 2
        qv = rq_ref[0, :, half * n_qk + p * PAIR: half * n_qk + (p + 1) * PAIR]
        kp = rk_ref[0, :, p * PAIR:(p + 1) * PAIR]
        v = rv_ref[0, :, h * RET_V_DIM:(h + 1) * RET_V_DIM]
        s = lax.dot_general(qv, kp, nt, preferred_element_type=F32)
        aw = (s * decay_matrix(c)).astype(BF16)
        y = jnp.dot(aw, v, preferred_element_type=F32)
        y = y + E[:, c:c + 1] * jnp.dot(qv, hret[h].astype(BF16), preferred_element_type=F32)
        y_ref[0, :, h * RET_V_DIM:(h + 1) * RET_V_DIM] = y.astype(y_ref.dtype)
        kdec = (kp.astype(F32) * KD[:, c:c + 1]).astype(BF16)
        hret[h] = hret[h] * ET[:, c:c + 1] + lax.dot_general(kdec, v, tn, preferred_element_type=F32)

    y0 = RET_HEADS * RET_V_DIM
    r_per_g = SSD_HEADS // SSD_GROUPS
    o_b = SSD_INNER
    o_c = SSD_INNER + SSD_GROUPS * SSD_STATE
    for g in range(SSD_GROUPS):
        cg = a_ref[0, :, o_c + g * SSD_STATE: o_c + (g + 1) * SSD_STATE]
        bg = a_ref[0, :, o_b + g * SSD_STATE: o_b + (g + 1) * SSD_STATE]
        s = lax.dot_general(cg, bg, nt, preferred_element_type=F32)
        ch = jnp.dot(cg, hssd[g].astype(BF16), preferred_element_type=F32)
        vdec, dec_rows = [], []
        for p in range(r_per_g // 2):
            pg = g * (r_per_g // 2) + p
            xs = a_ref[0, :, pg * PAIR:(pg + 1) * PAIR].astype(F32)
            ys, vds, cols = [], [], []
            for half in range(2):
                c = d * SSD_HEADS + 2 * pg + half
                aw = (s * decay_matrix(c)).astype(BF16)
                vh = xs * dt[:, c:c + 1]
                ys.append(jnp.dot(aw, vh.astype(BF16), preferred_element_type=F32))
                vds.append(vh * KD[:, c:c + 1])
                cols.append(c)
            e_pair = jnp.where(lo, E[:, cols[0]:cols[0] + 1], E[:, cols[1]:cols[1] + 1])
            y_ref[0, :, y0 + pg * PAIR: y0 + (pg + 1) * PAIR] = (
                jnp.where(lo, ys[0], ys[1]) + e_pair * ch[:, p * PAIR:(p + 1) * PAIR]).astype(y_ref.dtype)
            vdec.append(jnp.where(lo, vds[0], vds[1]).astype(BF16))
            dec_rows.append(jnp.where(lo_row, ET[:, cols[0]:cols[0] + 1], ET[:, cols[1]:cols[1] + 1]))
        hssd[g] = (hssd[g] * jnp.concatenate(dec_rows, axis=1)
                   + lax.dot_general(bg, jnp.concatenate(vdec, axis=1), tn, preferred_element_type=F32))


def _bidir_scan(RQ, RK, RV, A, DT, alog_row, logit_row, n_lat_tiles, reverse):
    B, N, _ = RQ.shape
    n_tiles = N // ROW_TILE

    def tile(i):
        lat = (n_lat_tiles - i) if reverse else (i - 1)
        return jnp.where(i == 0, n_tiles - 1, lat)

    row = lambda b, i: (b, tile(i), 0)
    const2 = lambda b, i: (0, 0)
    y_w = RET_HEADS * RET_V_DIM + SSD_INNER
    return pl.pallas_call(
        functools.partial(_scan_kernel, reverse=reverse),
        grid=(B, n_tiles),
        in_specs=[pl.BlockSpec((1, ROW_TILE, RQ.shape[-1]), row),
                  pl.BlockSpec((1, ROW_TILE, RK.shape[-1]), row),
                  pl.BlockSpec((1, ROW_TILE, RV.shape[-1]), row),
                  pl.BlockSpec((1, ROW_TILE, A.shape[-1]), row),
                  pl.BlockSpec((1, ROW_TILE, PAIR), row),
                  pl.BlockSpec((1, PAIR), const2),
                  pl.BlockSpec((1, PAIR), const2)],
        out_specs=pl.BlockSpec((1, ROW_TILE, y_w), row),
        out_shape=jax.ShapeDtypeStruct((B, N, y_w), BF16),
        scratch_shapes=[pltpu.VMEM((RET_HEADS, PAIR, RET_V_DIM), F32),
                        pltpu.VMEM((SSD_GROUPS, SSD_STATE, SSD_INNER // SSD_GROUPS), F32)],
        compiler_params=_cparams(("arbitrary", "arbitrary")),
        name="scan_bwd" if reverse else "scan_fwd",
    )(RQ, RK, RV, A, DT, alog_row, logit_row)


def _rec_combine_kernel(x_ref, yf_ref, yb_ref, g_ref, a_ref, mod_ref, dskip_ref, snw_ref, w_ref, o_ref, y_scr,
                        *, n_lat):
    n_ret = RET_HEADS * RET_V_DIM
    y = yf_ref[0].astype(F32) + yb_ref[0].astype(F32)
    gates = g_ref[0].astype(F32)
    for h in range(RET_HEADS):
        sl = slice(h * RET_V_DIM, (h + 1) * RET_V_DIM)
        r = y[:, sl]
        r = r * lax.rsqrt(jnp.mean(r * r, axis=-1, keepdims=True) + EPS)
        y_scr[:, sl] = (r * _silu(gates[:, sl])).astype(BF16)
    gw = SSD_INNER // SSD_GROUPS
    for g in range(SSD_GROUPS):
        sl = slice(g * gw, (g + 1) * gw)
        s = y[:, n_ret + g * gw: n_ret + (g + 1) * gw] + dskip_ref[:, sl] * a_ref[0, :, sl].astype(F32)
        s = s * _silu(gates[:, n_ret + g * gw: n_ret + (g + 1) * gw])
        s = s * lax.rsqrt(jnp.mean(s * s, axis=-1, keepdims=True) + EPS) * snw_ref[:, sl]
        y_scr[:, n_ret + g * gw: n_ret + (g + 1) * gw] = s.astype(BF16)
    gate = _row_mod(mod_ref, 2, x_ref.shape[1], n_lat)
    o_ref[0] = x_ref[0] + gate * jnp.dot(y_scr[...], w_ref[...], preferred_element_type=F32)


def _rec_combine(X, Yf, Yb, G, A, modc, dskip_row, snw_row, w_bf16, n_lat_tiles):
    B, N, D = X.shape
    row = lambda b, i: (b, i, 0)
    const2 = lambda b, i: (0, 0)
    return pl.pallas_call(
        functools.partial(_rec_combine_kernel, n_lat=n_lat_tiles * ROW_TILE),
        grid=(B, N // PROJ_TILE),
        in_specs=[pl.BlockSpec((1, PROJ_TILE, D), row),
                  pl.BlockSpec((1, PROJ_TILE, Yf.shape[-1]), row),
                  pl.BlockSpec((1, PROJ_TILE, Yb.shape[-1]), row),
                  pl.BlockSpec((1, PROJ_TILE, G.shape[-1]), row),
                  pl.BlockSpec((1, PROJ_TILE, SSD_INNER), row),
                  pl.BlockSpec((1, 2, 6, D), _MOD_BOTH),
                  pl.BlockSpec((1, SSD_INNER), const2),
                  pl.BlockSpec((1, SSD_INNER), const2),
                  pl.BlockSpec(w_bf16.shape, const2, **_ONCE)],
        out_specs=pl.BlockSpec((1, PROJ_TILE, D), row),
        out_shape=jax.ShapeDtypeStruct((B, N, D), F32),
        scratch_shapes=[pltpu.VMEM((PROJ_TILE, w_bf16.shape[0]), BF16)],
        compiler_params=_cparams(("arbitrary", "arbitrary")),
        name="rec_combine",
    )(X, Yf, Yb, G, A, modc, dskip_row, snw_row, w_bf16)


def _recurrent_layer(X, nw, modc, w_in, w_out, cos_t, sin_t, ret_decay_logit, conv_w, conv_b, dt_bias,
                     a_log, d_skip, ssd_norm_w, n_lat_tiles):
    D = X.shape[-1]
    w_pad = jnp.pad(w_in, ((0, 0), (0, REC_IN_PAD - w_in.shape[1]))).astype(BF16)
    pad_row = lambda v, at: jnp.pad(v.reshape(1, -1), ((0, 0), (at, PAIR - at - v.size)))
    RQ, RK, RV, G, XBC, DT = _rec_project(X, nw, modc, w_pad, cos_t, sin_t, pad_row(dt_bias, 0), n_lat_tiles)
    A = _ssd_conv(XBC, conv_w, conv_b, n_lat_tiles * ROW_TILE)
    alog_row, logit_row = pad_row(a_log, 0), pad_row(ret_decay_logit, RET_LANE0)
    Yf = _bidir_scan(RQ, RK, RV, A, DT, alog_row, logit_row, n_lat_tiles, False)
    Yb = _bidir_scan(RQ, RK, RV, A, DT, alog_row, logit_row, n_lat_tiles, True)
    return _rec_combine(X, Yf, Yb, G, A, modc, jnp.repeat(d_skip, SSD_HEAD_DIM).reshape(1, -1),
                        ssd_norm_w.reshape(1, -1), w_out.astype(BF16), n_lat_tiles)


SEL_CHUNK = 256
COMBINE_TILE = 768
FFN_ROW_CHUNK = 1024
GATHER_EXPERTS = 8


def _router_kernel(x_ref, nw_ref, mod_ref, rw_ref, h_ref, aff_ref, *, n_lat):
    h = _modulated(x_ref[0], nw_ref[...], mod_ref, 3, 4, n_lat)
    h_hi = h.astype(BF16)
    h_ref[0] = h_hi
    h_lo = (h - h_hi.astype(F32)).astype(BF16)
    w = rw_ref[...]
    w_hi = w.astype(BF16)
    w_lo = (w - w_hi.astype(F32)).astype(BF16)
    logits = (jnp.dot(h_hi, w_hi, preferred_element_type=F32) + jnp.dot(h_lo, w_hi, preferred_element_type=F32)
              + jnp.dot(h_hi, w_lo, preferred_element_type=F32))
    logits = logits.T[0:aff_ref.shape[1], :]
    e = jnp.exp(logits - jnp.max(logits, axis=0, keepdims=True))
    aff_ref[0] = e / jnp.sum(e, axis=0, keepdims=True)


def _mod_router(X, nw, modc, router_w, n_lat_tiles):
    B, N, D = X.shape
    E = router_w.shape[1]
    rw_pad = jnp.pad(router_w, ((0, 0), (0, PAIR - E)))
    return pl.pallas_call(
        functools.partial(_router_kernel, n_lat=n_lat_tiles * ROW_TILE),
        grid=(B, N // PROJ_TILE),
        in_specs=[pl.BlockSpec((1, PROJ_TILE, D), lambda b, i: (b, i, 0)),
                  pl.BlockSpec((1, D), lambda b, i: (0, 0)),
                  pl.BlockSpec((1, 2, 6, D), _MOD_BOTH),
                  pl.BlockSpec((D, PAIR), lambda b, i: (0, 0))],
        out_specs=[pl.BlockSpec((1, PROJ_TILE, D), lambda b, i: (b, i, 0)),
                   pl.BlockSpec((1, E, PROJ_TILE), lambda b, i: (b, 0, i))],
        out_shape=[jax.ShapeDtypeStruct((B, N, D), BF16),
                   jax.ShapeDtypeStruct((B, E, N), F32)],
        compiler_params=_cparams(("arbitrary", "arbitrary")),
        name="mod_router",
    )(X, nw.reshape(1, D), modc, rw_pad)


def _count(mask_f32):
    return jnp.sum(mask_f32, axis=-1, keepdims=True)


def _select_top(a, cap):
    E, n = a.shape
    v = pltpu.bitcast(a, jnp.int32)
    thr = jnp.zeros((E, 1), jnp.int32)
    for bit in range(30, -1, -1):
        cand = thr | (1 << bit)
        thr = jnp.where(_count(jnp.where(v >= cand, 1.0, 0.0)) >= cap, cand, thr)
    gt = v > thr
    eq = v == thr
    need = cap - _count(jnp.where(gt, 1.0, 0.0))
    idx = lax.broadcasted_iota(jnp.int32, (E, n), 1)
    last = jnp.zeros((E, 1), jnp.int32)
    for bit in range(n.bit_length() - 2, -1, -1):
        cand = last | (1 << bit)
        below = _count(jnp.where(eq, jnp.where(idx < cand, 1.0, 0.0), 0.0))
        last = jnp.where(below < need, cand, last)
    sel = jnp.where(gt, 1.0, jnp.where(eq, jnp.where(idx <= last, 1.0, 0.0), 0.0))
    si = lax.broadcasted_iota(jnp.int32, (SEL_CHUNK, SEL_CHUNK), 0)
    sj = lax.broadcasted_iota(jnp.int32, (SEL_CHUNK, SEL_CHUNK), 1)
    before = jnp.where(si < sj, 1.0, 0.0).astype(BF16)
    base = jnp.zeros((E, 1), F32)
    pos = []
    for k in range(n // SEL_CHUNK):
        sk = sel[:, k * SEL_CHUNK:(k + 1) * SEL_CHUNK]
        pos.append(jnp.dot(sk.astype(BF16), before, preferred_element_type=F32) + base)
        base = base + _count(sk)
    pos = jnp.concatenate(pos, axis=1) if len(pos) > 1 else pos[0]
    return jnp.where(sel > 0.0, pos, -1.0), jnp.where(sel > 0.0, a, 0.0)


def _select_kernel(aff_ref, slot_ref, slot_t_ref, gate_t_ref, *, n_lat, cap_lat, cap_ctx):
    n_all = aff_ref.shape[2]
    E = aff_ref.shape[1]
    pad_rows = PAIR - E
    for lo, hi, cap in ((0, n_lat, cap_lat), (n_lat, n_all, cap_ctx)):
        slot, gate = _select_top(aff_ref[0, :, lo:hi], cap)
        slot_ref[0, :, lo:hi] = slot
        slot_t_ref[0, lo:hi, :] = jnp.concatenate([slot, jnp.full((pad_rows, hi - lo), -1.0, F32)], axis=0).T
        gate_t_ref[0, lo:hi, :] = jnp.concatenate([gate, jnp.zeros((pad_rows, hi - lo), F32)], axis=0).T


def _ec_select(aff, n_lat, cap_lat, cap_ctx):
    B, E, N = aff.shape
    return pl.pallas_call(
        functools.partial(_select_kernel, n_lat=n_lat, cap_lat=cap_lat, cap_ctx=cap_ctx),
        grid=(B,),
        in_specs=[pl.BlockSpec((1, E, N), lambda b: (b, 0, 0))],
        out_specs=[pl.BlockSpec((1, E, N), lambda b: (b, 0, 0)),
                   pl.BlockSpec((1, N, PAIR), lambda b: (b, 0, 0)),
                   pl.BlockSpec((1, N, PAIR), lambda b: (b, 0, 0))],
        out_shape=[jax.ShapeDtypeStruct((B, E, N), F32),
                   jax.ShapeDtypeStruct((B, N, PAIR), F32),
                   jax.ShapeDtypeStruct((B, N, PAIR), F32)],
        compiler_params=_cparams(("arbitrary",)),
        name="ec_select",
    )(aff)


def _gather_kernel(slot_ref, h_ref, xl_ref, xc_ref, *, n_lat):
    n_e = xl_ref.shape[0]
    e0 = pl.multiple_of(pl.program_id(1) * n_e, n_e)
    n_all = h_ref.shape[1]
    for lo, hi, x_ref in ((0, n_lat, xl_ref), (n_lat, n_all, xc_ref)):
        cap = x_ref.shape[1]
        c = lax.broadcasted_iota(jnp.int32, (cap, hi - lo), 0).astype(F32)
        onehot = jnp.concatenate(
            [jnp.where(slot_ref[0, pl.ds(e0 + j, 1), lo:hi] == c, 1.0, 0.0).astype(BF16)
             for j in range(n_e)], axis=0)
        x = jnp.dot(onehot, h_ref[0, lo:hi, :], preferred_element_type=F32)
        for j in range(n_e):
            x_ref[j] = x[j * cap:(j + 1) * cap, :].astype(BF16)


def _ec_gather(slot, H, n_lat, cap_lat, cap_ctx):
    B, E, N = slot.shape
    D = H.shape[-1]
    return pl.pallas_call(
        functools.partial(_gather_kernel, n_lat=n_lat),
        grid=(B, E // GATHER_EXPERTS),
        in_specs=[pl.BlockSpec((1, E, N), lambda b, e: (b, 0, 0)),
                  pl.BlockSpec((1, N, D), lambda b, e: (b, 0, 0))],
        out_specs=[pl.BlockSpec((GATHER_EXPERTS, cap_lat, D), lambda b, e: (e, b, 0)),
                   pl.BlockSpec((GATHER_EXPERTS, cap_ctx, D), lambda b, e: (e, b, 0))],
        out_shape=[jax.ShapeDtypeStruct((E, B * cap_lat, D), BF16),
                   jax.ShapeDtypeStruct((E, B * cap_ctx, D), BF16)],
        compiler_params=_cparams(("arbitrary", "arbitrary")),
        name="ec_gather",
    )(slot, H)


def _ffn_kernel(*refs, n_groups, row_chunk):
    x_refs = refs[:n_groups]
    wg_ref, wu_ref, wd_ref = refs[n_groups:n_groups + 3]
    o_refs = refs[n_groups + 3:2 * n_groups + 3]
    acc_refs = refs[2 * n_groups + 3:3 * n_groups + 3]
    wg_s, wu_s, wd_s = refs[3 * n_groups + 3:]
    f = pl.program_id(2)
    n_f = pl.num_programs(2)
    wg_s[...] = wg_ref[0, 0].astype(BF16)
    wu_s[...] = wu_ref[0, 0].astype(BF16)
    wd_s[...] = wd_ref[0, 0].astype(BF16)

    @pl.when(f == 0)
    def _():
        for acc_ref in acc_refs:
            acc_ref[...] = jnp.zeros_like(acc_ref)

    def chunk(x_ref, acc_ref, start, size):
        rows = pl.ds(start, size)
        xr = x_ref[0, rows, :]
        a = jnp.dot(xr, wg_s[...], preferred_element_type=F32)
        u = jnp.dot(xr, wu_s[...], preferred_element_type=F32)
        hm = (a * jax.nn.sigmoid(a) * u).astype(BF16)
        acc_ref[rows, :] += jnp.dot(hm, wd_s[...], preferred_element_type=F32)

    for x_ref, acc_ref in zip(x_refs, acc_refs):
        n_rows = x_ref.shape[1]
        for start in range(0, n_rows, row_chunk):
            chunk(x_ref, acc_ref, start, min(row_chunk, n_rows - start))

    @pl.when(f == n_f - 1)
    def _():
        for o_ref, acc_ref in zip(o_refs, acc_refs):
            o_ref[0] = acc_ref[...].astype(o_ref.dtype)


def _expert_ffn(xs, w_gate, w_up, w_down, layer):
    E, _, D = xs[0].shape
    F = w_gate.shape[-1]
    m_tiles = 2
    tms = [x.shape[1] // m_tiles for x in xs]
    n = len(xs)
    return pl.pallas_call(
        functools.partial(_ffn_kernel, n_groups=n, row_chunk=FFN_ROW_CHUNK),
        grid=(E, m_tiles, F // FF_TILE),
        in_specs=[pl.BlockSpec((1, tm, D), lambda e, m, f: (e, m, 0)) for tm in tms]
        + [pl.BlockSpec((1, 1, D, FF_TILE), lambda e, m, f: (layer, e, 0, f)),
           pl.BlockSpec((1, 1, D, FF_TILE), lambda e, m, f: (layer, e, 0, f)),
           pl.BlockSpec((1, 1, FF_TILE, D), lambda e, m, f: (layer, e, f, 0))],
        out_specs=[pl.BlockSpec((1, tm, D), lambda e, m, f: (e, m, 0)) for tm in tms],
        out_shape=[jax.ShapeDtypeStruct(x.shape, BF16) for x in xs],
        scratch_shapes=[pltpu.VMEM((tm, D), F32) for tm in tms]
        + [pltpu.VMEM((D, FF_TILE), BF16), pltpu.VMEM((D, FF_TILE), BF16), pltpu.VMEM((FF_TILE, D), BF16)],
        compiler_params=_cparams(("arbitrary", "arbitrary", "arbitrary")),
        name="expert_ffn",
    )(*xs, w_gate, w_up, w_down)


def _combine_kernel(x_ref, slot_t_ref, gate_t_ref, yl_ref, yc_ref, mod_ref, o_ref, *, n_lat, n_all):
    E = yl_ref.shape[0]
    tile = x_ref.shape[1]

    def scatter(rows, y_ref, seg):
        n = rows.stop - rows.start
        cap = y_ref.shape[1]
        c = lax.broadcasted_iota(jnp.int32, (n, cap), 1).astype(F32)
        acc = jnp.zeros((n, x_ref.shape[2]), F32)
        for e in range(E):
            onehot = jnp.where(slot_t_ref[0, rows, e:e + 1] == c, 1.0, 0.0).astype(BF16)
            acc = acc + gate_t_ref[0, rows, e:e + 1] * jnp.dot(onehot, y_ref[e], preferred_element_type=F32)
        o_ref[0, rows, :] = x_ref[0, rows, :] + mod_ref[0, seg, 5:6, :] * acc

    splits = {}
    for j in range(n_all // tile):
        splits.setdefault(min(max(n_lat - j * tile, 0), tile), []).append(j)
    i = pl.program_id(1)
    for lat_rows, tiles in splits.items():
        @pl.when(functools.reduce(jnp.logical_or, [i == j for j in tiles]))
        def _(lat_rows=lat_rows):
            if lat_rows > 0:
                scatter(slice(0, lat_rows), yl_ref, 0)
            if lat_rows < tile:
                scatter(slice(lat_rows, tile), yc_ref, 1)


def _ec_combine(X, slot_t, gate_t, Yl, Yc, modc, n_lat_tiles, cap_lat, cap_ctx):
    B, N, D = X.shape
    E = Yl.shape[0]
    row = lambda b, i: (b, i, 0)
    return pl.pallas_call(
        functools.partial(_combine_kernel, n_lat=n_lat_tiles * ROW_TILE, n_all=N),
        grid=(B, N // COMBINE_TILE),
        in_specs=[pl.BlockSpec((1, COMBINE_TILE, D), row),
                  pl.BlockSpec((1, COMBINE_TILE, PAIR), row),
                  pl.BlockSpec((1, COMBINE_TILE, PAIR), row),
                  pl.BlockSpec((E, cap_lat, D), lambda b, i: (0, b, 0)),
                  pl.BlockSpec((E, cap_ctx, D), lambda b, i: (0, b, 0)),
                  pl.BlockSpec((1, 2, 6, D), _MOD_BOTH)],
        out_specs=pl.BlockSpec((1, COMBINE_TILE, D), row),
        out_shape=jax.ShapeDtypeStruct((B, N, D), F32),
        compiler_params=_cparams(("arbitrary", "arbitrary")),
        name="ec_combine",
    )(X, slot_t, gate_t, Yl, Yc, modc)


def _moe_layer(X, nw, modc, router_w, w_gate, w_up, w_down, layer, n_lat_tiles):
    B, N, D = X.shape
    n_lat = n_lat_tiles * ROW_TILE
    cap_lat = EC_CAPACITY * n_lat // N_EXPERTS
    cap_ctx = EC_CAPACITY * (N - n_lat) // N_EXPERTS
    H, aff = _mod_router(X, nw, modc, router_w, n_lat_tiles)
    slot, slot_t, gate_t = _ec_select(aff, n_lat, cap_lat, cap_ctx)
    Xl, Xc = _ec_gather(slot, H, n_lat, cap_lat, cap_ctx)
    Yl, Yc = _expert_ffn([Xl, Xc], w_gate, w_up, w_down, layer)
    return _ec_combine(X, slot_t, gate_t, Yl, Yc, modc, n_lat_tiles, cap_lat, cap_ctx)


def _final_norm_kernel(x_ref, w_ref, o_ref):
    x = x_ref[0]
    o_ref[0] = x * lax.rsqrt(jnp.mean(x * x, axis=-1, keepdims=True) + EPS) * w_ref[...]


def _final_norm(X, w, n_lat):
    B, _, D = X.shape
    row = lambda b, i: (b, i, 0)
    return pl.pallas_call(
        _final_norm_kernel,
        grid=(B, n_lat // ROW_TILE),
        in_specs=[pl.BlockSpec((1, ROW_TILE, D), row), pl.BlockSpec((1, D), lambda b, i: (0, 0))],
        out_specs=pl.BlockSpec((1, ROW_TILE, D), row),
        out_shape=jax.ShapeDtypeStruct((B, n_lat, D), F32),
        compiler_params=_cparams(("arbitrary", "arbitrary")),
        name="final_norm",
    )(X, w.reshape(1, D))


def _split(z, sizes):
    idx = np.cumsum(sizes)[:-1].tolist()
    return jnp.split(z, idx, axis=-1)


def _rmsnorm(x, w=None):
    y = x * lax.rsqrt(jnp.mean(x * x, axis=-1, keepdims=True) + EPS)
    if w is not None:
        y = y * w
    return y


def _rope_tables(n):
    t = jnp.arange(n)
    row = (t // GRID_W).astype(F32)
    col = (t % GRID_W).astype(F32)
    n_freq = HEAD_DIM // 4
    inv = ROPE_THETA ** (-jnp.arange(n_freq, dtype=F32) / n_freq)
    ang = jnp.concatenate([row[:, None] * inv, col[:, None] * inv], axis=-1)
    return jnp.cos(ang), jnp.sin(ang)


def _rope(x, cos, sin):
    shape = (cos.shape[0],) + (1,) * (x.ndim - 3) + (cos.shape[1],)
    c, s = cos.reshape(shape), sin.reshape(shape)
    x1, x2 = jnp.split(x, 2, axis=-1)
    return jnp.concatenate([x1 * c - x2 * s, x2 * c + x1 * s], axis=-1)


def _attend(q, k, v, shared_k):
    scale = q.shape[-1] ** -0.5
    k_sub = 'bkhd' if shared_k else 'bkhmd'

    def one_block(qb):
        s = jnp.einsum(f'bqhmd,{k_sub}->bhmqk', qb, k, preferred_element_type=F32) * scale
        p = jax.nn.softmax(s, axis=-1)
        return jnp.einsum('bhmqk,bkhv->bqhmv', p, v)

    B, Sq = q.shape[:2]
    nb = Sq // Q_BLOCK
    qb = jnp.moveaxis(q.reshape(B, nb, Q_BLOCK, *q.shape[2:]), 1, 0)
    out = lax.map(one_block, qb)
    return jnp.moveaxis(out, 0, 1).reshape(B, Sq, *out.shape[3:])


def _attention_core(zl, zc, cos, sin, q_norm_w, k_norm_w, lam, diff_norm_w, lambda_init, need_ctx):
    def project(z, use_rope):
        B, n, _ = z.shape
        aq, ak, av, bq, bk, bv = _split(z, ATT_SPLITS)
        aq = _rmsnorm(aq.reshape(B, n, GQA_HEADS, HEAD_DIM), q_norm_w)
        ak = _rmsnorm(ak.reshape(B, n, GQA_KV_HEADS, HEAD_DIM), k_norm_w)
        bq = bq.reshape(B, n, DIFF_HEADS, 2, HEAD_DIM)
        bk = bk.reshape(B, n, DIFF_HEADS, 2, HEAD_DIM)
        if use_rope:
            aq, ak, bq, bk = (_rope(t, cos, sin) for t in (aq, ak, bq, bk))
        aq = aq.reshape(B, n, GQA_KV_HEADS, GQA_HEADS // GQA_KV_HEADS, HEAD_DIM)
        av = av.reshape(B, n, GQA_KV_HEADS, HEAD_DIM)
        bv = bv.reshape(B, n, DIFF_HEADS, 2 * HEAD_DIM)
        return (aq, bq), (ak, av, bk, bv)

    lat_q, lat_kv = project(zl, True)
    ctx_q, ctx_kv = project(zc, False)
    lam_val = jnp.exp(jnp.sum(lam[0] * lam[1])) - jnp.exp(jnp.sum(lam[2] * lam[3])) + lambda_init

    def mix(q_side, kv):
        aq, bq = q_side
        ak, av, bk, bv = kv
        B, n = aq.shape[:2]
        a = _attend(aq, ak, av, True).reshape(B, n, -1)
        o = _attend(bq, bk, bv, False)
        o = o[..., 0, :] - lam_val * o[..., 1, :]
        b = (_rmsnorm(o, diff_norm_w) * (1 - lambda_init)).reshape(B, n, -1)
        return jnp.concatenate([a, b], axis=-1)

    kv_all = tuple(jnp.concatenate([l, c], axis=1) for l, c in zip(lat_kv, ctx_kv))
    yl = mix(lat_q, kv_all)
    yc = mix(ctx_q, ctx_kv) if need_ctx else jnp.zeros((zc.shape[0], zc.shape[1], 2 * HALF_W), F32)
    return yl, yc


def _chunk_scan(q, k, v, log_a, h0):
    B, S = q.shape[:2]
    n = S // CHUNK
    tril = jnp.tril(jnp.ones((CHUNK, CHUNK), dtype=bool))

    def to_chunks(a):
        return jnp.moveaxis(a.reshape(B, n, CHUNK, *a.shape[2:]), 1, 0)

    def step(h, inp):
        qc, kc, vc, lac = inp
        cum = jnp.cumsum(lac.astype(F32), axis=1)
        cum_t = jnp.moveaxis(cum, 1, -1)
        seg = jnp.exp(jnp.where(tril, cum_t[..., :, None] - cum_t[..., None, :], -jnp.inf))
        qk = jnp.einsum('btgk,bsgk->bgts', qc, kc)
        y = (jnp.einsum('bgts,bgrts,bsgrv->btgrv', qk, seg, vc)
             + jnp.einsum('btgk,bgrkv->btgrv', qc, h) * jnp.exp(cum)[..., None])
        last = cum[:, -1]
        h_new = (h * jnp.exp(last)[..., None, None]
                 + jnp.einsum('bsgk,bsgr,bsgrv->bgrkv', kc, jnp.exp(last[:, None] - cum), vc))
        return h_new, y

    h, ys = lax.scan(step, h0, tuple(to_chunks(a) for a in (q, k, v, log_a)))
    return jnp.moveaxis(ys, 0, 1).reshape(B, S, *ys.shape[3:]), h


def _prefix_scan(lat, ctx, reverse):
    if reverse:
        lat = tuple(jnp.flip(a, 1) for a in lat)
        ctx = tuple(jnp.flip(a, 1) for a in ctx)
    q, v = ctx[0], ctx[2]
    h0 = jnp.zeros((q.shape[0], q.shape[2], v.shape[3], q.shape[3], v.shape[4]), F32)
    y_c, h_c = _chunk_scan(*ctx, h0)
    y_l, _ = _chunk_scan(*lat, h_c)
    if reverse:
        y_l, y_c = jnp.flip(y_l, 1), jnp.flip(y_c, 1)
    return y_l, y_c


def _bidir(lat_f, ctx_f, lat_b, ctx_b):
    yl_f, yc_f = _prefix_scan(lat_f, ctx_f, False)
    yl_b, yc_b = _prefix_scan(lat_b, ctx_b, True)
    return yl_f + yl_b, yc_f + yc_b


def _dwconv(x, w, b):
    K = w.shape[0]
    out = lax.conv_general_dilated(x, w[:, None, :], window_strides=(1,),
                                   padding=[(K // 2, K // 2)],
                                   dimension_numbers=('NWC', 'WIO', 'NWC'),
                                   feature_group_count=x.shape[-1])
    return out + b


def _recurrent_core(zl, zc, cos, sin, ret_decay_logit, conv_w, conv_b, dt_bias, a_log, d_skip,
                    ssd_norm_w):
    log_gamma = jax.nn.log_sigmoid(ret_decay_logit)
    a_neg = -jnp.exp(a_log)
    r_per_g = SSD_HEADS // SSD_GROUPS

    def project(z, use_rope):
        B, n, _ = z.shape
        rq, rk, rv, rg, zz, xbc, dt = _split(z, REC_SPLITS)
        rq = rq.reshape(B, n, RET_HEADS, RET_QK_DIM)
        rk = rk.reshape(B, n, RET_HEADS, RET_QK_DIM) * RET_QK_DIM ** -0.5
        if use_rope:
            rq, rk = _rope(rq, cos, sin), _rope(rk, cos, sin)
        rv = rv.reshape(B, n, RET_HEADS, 1, RET_V_DIM)
        ret = [(rq, rk, rv, jnp.broadcast_to(log_gamma[d][:, None], (B, n, RET_HEADS, 1)))
               for d in (0, 1)]
        xbc = jax.nn.silu(_dwconv(xbc, conv_w, conv_b))
        xs, bm, cm = _split(xbc, (SSD_INNER, SSD_GROUPS * SSD_STATE, SSD_GROUPS * SSD_STATE))
        xs = xs.reshape(B, n, SSD_HEADS, SSD_HEAD_DIM)
        bm = bm.reshape(B, n, SSD_GROUPS, SSD_STATE)
        cm = cm.reshape(B, n, SSD_GROUPS, SSD_STATE)
        dt = jax.nn.softplus(dt.reshape(B, n, 2, SSD_HEADS) + dt_bias)
        ssd = [(cm, bm,
                (xs * dt[:, :, d, :, None]).reshape(B, n, SSD_GROUPS, r_per_g, SSD_HEAD_DIM),
                (dt[:, :, d] * a_neg[d]).reshape(B, n, SSD_GROUPS, r_per_g))
               for d in (0, 1)]
        return ret, ssd, rg, zz, xs

    lat_ret, lat_ssd, lat_rg, lat_z, lat_xs = project(zl, True)
    ctx_ret, ctx_ssd, ctx_rg, ctx_z, ctx_xs = project(zc, False)
    ret_l, ret_c = _bidir(lat_ret[0], ctx_ret[0], lat_ret[1], ctx_ret[1])
    ssd_l, ssd_c = _bidir(lat_ssd[0], ctx_ssd[0], lat_ssd[1], ctx_ssd[1])

    def combine(ret_y, ssd_y, rg, zz, xs):
        B, n = rg.shape[:2]
        r = _rmsnorm(ret_y.reshape(B, n, RET_HEADS, RET_V_DIM)).reshape(B, n, -1) * jax.nn.silu(rg)
        s = ssd_y.reshape(B, n, SSD_HEADS, SSD_HEAD_DIM) + d_skip[:, None] * xs
        s = s.reshape(B, n, -1) * jax.nn.silu(zz)
        s = _rmsnorm(s.reshape(B, n, SSD_GROUPS, -1)).reshape(B, n, -1) * ssd_norm_w
        return jnp.concatenate([r, s], axis=-1)

    return combine(ret_l, ssd_l, lat_rg, lat_z, lat_xs), combine(ret_c, ssd_c, ctx_rg, ctx_z, ctx_xs)


def _dispatch_tokens(h, aff):
    B, n, D = h.shape
    cap = EC_CAPACITY * n // N_EXPERTS
    g, idx = lax.top_k(jnp.swapaxes(aff, 1, 2), cap)
    xin = jax.vmap(lambda xs, i: xs[i])(h, idx)
    return xin, g, idx


def _combine_tokens(y, g, idx, n):
    D = y.shape[-1]
    y = y * g[..., None]
    return jax.vmap(lambda val, i: jax.ops.segment_sum(val.reshape(-1, D), i.reshape(-1),
                                                       num_segments=n))(y, idx)


def kernel(x, c, ctx, c_ctx, ada_w, ada_b, norm1_w, norm2_w, att_w_in, att_w_out, att_q_norm_w,
           att_k_norm_w, diff_lambda, diff_norm_w, rec_w_in, rec_w_out, ret_decay_logit, ssd_conv_w,
           ssd_conv_b, ssd_dt_bias, ssd_a_log, ssd_d_skip, ssd_norm_w, router_w, expert_w_gate,
           expert_w_up, expert_w_down, final_norm_w):
    B, n_lat, D = x.shape
    n_ctx = ctx.shape[1]
    n_lat_tiles = n_lat // ROW_TILE
    cos, sin = _rope_tables(n_lat)
    reps = PAIR // cos.shape[1]
    cos_t = jnp.concatenate([jnp.tile(cos, (1, reps)), jnp.ones((n_ctx, PAIR), F32)], axis=0)
    sin_t = jnp.concatenate([jnp.tile(jnp.concatenate([-sin, sin], axis=1), (1, reps // 2)),
                             jnp.zeros((n_ctx, PAIR), F32)], axis=0)
    gmat = jnp.kron(jnp.eye(GQA_HEADS, dtype=F32), jnp.ones((HEAD_DIM, HEAD_DIM), F32)).astype(BF16)

    n_cond = 24
    cvec = jnp.concatenate([c, c_ctx[None, :], jnp.zeros((n_cond - B - 1, D), F32)], axis=0)
    mods = _ada_modulation(cvec, ada_w, ada_b)

    X = jnp.concatenate([x, ctx], axis=1)
    for layer in range(DEPTH):
        i = layer // 2
        need_ctx = layer < DEPTH - 1
        mod_lat = mods[layer, :B].reshape(B, 1, 6, D)
        mod_ctx = jnp.broadcast_to(mods[layer, B].reshape(1, 1, 6, D), (B, 1, 6, D))
        modc = jnp.concatenate([mod_lat, mod_ctx], axis=1)

        if layer % 2 == 0:
            lambda_init = 0.8 - 0.6 * math.exp(-0.3 * layer)
            Q, K, V = _att_project(X, norm1_w[layer], modc, att_w_in[i].astype(BF16), cos_t, sin_t, gmat,
                                   jnp.tile(att_q_norm_w[i], GQA_HEADS).reshape(1, -1),
                                   jnp.tile(att_k_norm_w[i], GQA_KV_HEADS).reshape(1, -1), n_lat_tiles)
            X = _attention(X, Q, K, V, modc, diff_lambda[i], diff_norm_w[i], att_w_out[i].astype(BF16),
                           lambda_init, n_lat_tiles)
        else:
            X = _recurrent_layer(X, norm1_w[layer], modc, rec_w_in[i], rec_w_out[i], cos_t, sin_t,
                                 ret_decay_logit[i], ssd_conv_w[i], ssd_conv_b[i], ssd_dt_bias[i],
                                 ssd_a_log[i], ssd_d_skip[i], ssd_norm_w[i], n_lat_tiles)

        X = _moe_layer(X, norm2_w[layer], modc, router_w[layer], expert_w_gate, expert_w_up, expert_w_down,
                       layer, n_lat_tiles)

    return _final_norm(X, final_norm_w, n_lat)
```

```python
import functools
import math

import jax
import jax.numpy as jnp
import numpy as np
from jax import lax
from jax.experimental import pallas as pl
from jax.experimental.pallas import tpu as pltpu

D_MODEL = 1024
DEPTH = 4
GRID_W = 64
HEAD_DIM = 64
ROPE_THETA = 10000.0
Q_BLOCK = 128
CHUNK = 128
EPS = 1e-6
HALF_W = D_MODEL // 2
GQA_HEADS = HALF_W // HEAD_DIM
GQA_KV_HEADS = GQA_HEADS // 4
DIFF_HEADS = HALF_W // (2 * HEAD_DIM)
RET_HEADS = HALF_W // (2 * HEAD_DIM)
RET_QK_DIM = HEAD_DIM
RET_V_DIM = 2 * HEAD_DIM
SSD_HEAD_DIM = HEAD_DIM
SSD_HEADS = HALF_W // SSD_HEAD_DIM
SSD_GROUPS = 2
SSD_STATE = 128
SSD_INNER = SSD_HEADS * SSD_HEAD_DIM
SSD_XBC = SSD_INNER + 2 * SSD_GROUPS * SSD_STATE
N_EXPERTS = 16
EC_CAPACITY = 2
EXPERT_FF = ((8 * D_MODEL // 3 + 255) // 256) * 256

ATT_SPLITS = (GQA_HEADS * HEAD_DIM, GQA_KV_HEADS * HEAD_DIM, GQA_KV_HEADS * HEAD_DIM,
              DIFF_HEADS * 2 * HEAD_DIM, DIFF_HEADS * 2 * HEAD_DIM, DIFF_HEADS * 2 * HEAD_DIM)
REC_SPLITS = (RET_HEADS * RET_QK_DIM, RET_HEADS * RET_QK_DIM, RET_HEADS * RET_V_DIM,
              RET_HEADS * RET_V_DIM, SSD_INNER, SSD_XBC, 2 * SSD_HEADS)

ROW_TILE = 256
PROJ_TILE = 768
FF_TILE = 256
VMEM_LIMIT = 56 * 1024 * 1024
BF16 = jnp.bfloat16
F32 = jnp.float32


def _cparams(sem):
    return pltpu.CompilerParams(dimension_semantics=sem, vmem_limit_bytes=VMEM_LIMIT)


def _ada_kernel(c_ref, w_ref, b_ref, o_ref):
    c = c_ref[...]
    s = c * jax.nn.sigmoid(c)
    o_ref[0] = jnp.dot(s, w_ref[0], precision=lax.Precision.HIGHEST,
                       preferred_element_type=F32) + b_ref[0]


def _ada_modulation(cvec, ada_w, ada_b):
    R, D = cvec.shape
    n_out = ada_w.shape[-1]
    tn = 1536
    return pl.pallas_call(
        _ada_kernel,
        grid=(DEPTH, n_out // tn),
        in_specs=[pl.BlockSpec((R, D), lambda l, j: (0, 0)),
                  pl.BlockSpec((1, D, tn), lambda l, j: (l, 0, j)),
                  pl.BlockSpec((1, 1, tn), lambda l, j: (l, 0, j))],
        out_specs=pl.BlockSpec((1, R, tn), lambda l, j: (l, 0, j)),
        out_shape=jax.ShapeDtypeStruct((DEPTH, R, n_out), F32),
        compiler_params=_cparams(("arbitrary", "arbitrary")),
        name="ada_modulation",
    )(cvec, ada_w, ada_b.reshape(DEPTH, 1, n_out))


def _row_mod(mod_ref, idx, n_rows, n_lat):
    row = pl.program_id(1) * n_rows + lax.broadcasted_iota(jnp.int32, (n_rows, 1), 0)
    return jnp.where(row < n_lat, mod_ref[0, 0, idx:idx + 1, :], mod_ref[0, 1, idx:idx + 1, :])


def _modulated(x, nw, mod_ref, shift_idx, scale_idx, n_lat):
    n_rows = x.shape[0]
    ms = jnp.mean(x * x, axis=-1, keepdims=True)
    y = x * lax.rsqrt(ms + EPS) * nw
    return (y * (1.0 + _row_mod(mod_ref, scale_idx, n_rows, n_lat))
            + _row_mod(mod_ref, shift_idx, n_rows, n_lat))


_MOD_BOTH = lambda b, i: (b, 0, 0, 0)
_ONCE = dict(pipeline_mode=pl.Buffered(1))


PAIR = 2 * HEAD_DIM
N_PAIR_GROUPS = 2 * HALF_W // PAIR
ATT_KV_W = GQA_KV_HEADS * PAIR + DIFF_HEADS * PAIR


def _head_sumsq(x, g):
    x2 = x * x
    hi = x2.astype(BF16)
    lo = (x2 - hi.astype(F32)).astype(BF16)
    return jnp.dot(hi, g, preferred_element_type=F32) + jnp.dot(lo, g, preferred_element_type=F32)


def _rope_lanes(x, cos, sin_signed):
    W = x.shape[-1]
    half = HEAD_DIM // 2
    lane = lax.broadcasted_iota(jnp.int32, x.shape, 1) & (HEAD_DIM - 1)
    partner = jnp.where(lane < half, pltpu.roll(x, W - half, 1), pltpu.roll(x, half, 1))
    return x * cos + partner * sin_signed


def _att_project_kernel(x_ref, nw_ref, mod_ref, w_ref, cos_ref, sin_ref, g_ref, qw_ref, kw_ref,
                        q_ref, k_ref, v_ref, *, n_lat):
    h = _modulated(x_ref[0], nw_ref[...], mod_ref, 0, 1, n_lat)
    z = jnp.dot(h.astype(BF16), w_ref[...], preferred_element_type=F32)
    o_aq, o_ak, o_av, o_bq, o_bk, o_bv = np.cumsum((0,) + ATT_SPLITS[:-1]).tolist()
    n_aq, n_ak, n_bq = ATT_SPLITS[0], ATT_SPLITS[1], ATT_SPLITS[3]
    cos, sin = cos_ref[...], sin_ref[...]
    cos4 = jnp.concatenate([cos] * (n_aq // PAIR), axis=1)
    sin4 = jnp.concatenate([sin] * (n_aq // PAIR), axis=1)
    inv_d = 1.0 / HEAD_DIM
    q_scale = HEAD_DIM ** -0.5 * math.log2(math.e)

    aq = z[:, o_aq:o_aq + n_aq]
    aq = aq * lax.rsqrt(_head_sumsq(aq, g_ref[...]) * inv_d + EPS) * qw_ref[...]
    aq = _rope_lanes(aq, cos4, sin4) * q_scale
    bq = _rope_lanes(z[:, o_bq:o_bq + n_bq], cos4, sin4) * q_scale
    q = jnp.concatenate([aq, bq], axis=1)
    lo = (lax.broadcasted_iota(jnp.int32, q.shape, 1) & (PAIR - 1)) < HEAD_DIM
    q_ref[0, :, 0:q.shape[1]] = jnp.where(lo, q, 0.0).astype(BF16)
    q_ref[0, :, q.shape[1]:2 * q.shape[1]] = jnp.where(lo, 0.0, q).astype(BF16)

    ak = z[:, o_ak:o_ak + n_ak]
    ak = ak * lax.rsqrt(_head_sumsq(ak, g_ref[0:n_ak, 0:n_ak]) * inv_d + EPS) * kw_ref[...]
    ak = _rope_lanes(ak, cos, sin)
    av = z[:, o_av:o_av + n_ak]
    lo_kv = lax.broadcasted_iota(jnp.int32, ak.shape, 1) < HEAD_DIM
    ak_sw, av_sw = pltpu.roll(ak, HEAD_DIM, 1), pltpu.roll(av, HEAD_DIM, 1)
    k_ref[0, :, 0:PAIR] = jnp.where(lo_kv, ak, ak_sw).astype(BF16)
    k_ref[0, :, PAIR:2 * PAIR] = jnp.where(lo_kv, ak_sw, ak).astype(BF16)
    v_ref[0, :, 0:PAIR] = jnp.where(lo_kv, av, av_sw).astype(BF16)
    v_ref[0, :, PAIR:2 * PAIR] = jnp.where(lo_kv, av_sw, av).astype(BF16)
    k_ref[0, :, 2 * PAIR:] = _rope_lanes(z[:, o_bk:o_bk + n_bq], cos4, sin4).astype(BF16)
    v_ref[0, :, 2 * PAIR:] = z[:, o_bv:o_bv + n_bq].astype(BF16)


def _att_project(X, nw, modc, w_bf16, cos_t, sin_t, gmat, qw_t, kw_t, n_lat_tiles):
    B, N, D = X.shape
    n_in = w_bf16.shape[1]
    q_w = 2 * N_PAIR_GROUPS * PAIR
    row = lambda b, i: (b, i, 0)
    const2 = lambda b, i: (0, 0)
    return pl.pallas_call(
        functools.partial(_att_project_kernel, n_lat=n_lat_tiles * ROW_TILE),
        grid=(B, N // PROJ_TILE),
        in_specs=[pl.BlockSpec((1, PROJ_TILE, D), row),
                  pl.BlockSpec((1, D), const2),
                  pl.BlockSpec((1, 2, 6, D), _MOD_BOTH),
                  pl.BlockSpec((D, n_in), const2, **_ONCE),
                  pl.BlockSpec((PROJ_TILE, PAIR), lambda b, i: (i, 0)),
                  pl.BlockSpec((PROJ_TILE, PAIR), lambda b, i: (i, 0)),
                  pl.BlockSpec(gmat.shape, const2),
                  pl.BlockSpec(qw_t.shape, const2),
                  pl.BlockSpec(kw_t.shape, const2)],
        out_specs=[pl.BlockSpec((1, PROJ_TILE, q_w), row),
                   pl.BlockSpec((1, PROJ_TILE, ATT_KV_W), row),
                   pl.BlockSpec((1, PROJ_TILE, ATT_KV_W), row)],
        out_shape=[jax.ShapeDtypeStruct((B, N, q_w), BF16),
                   jax.ShapeDtypeStruct((B, N, ATT_KV_W), BF16),
                   jax.ShapeDtypeStruct((B, N, ATT_KV_W), BF16)],
        compiler_params=_cparams(("arbitrary", "arbitrary")),
        name="att_project",
    )(X, nw.reshape(1, D), modc, w_bf16, cos_t, sin_t, gmat, qw_t, kw_t)


def _attention_kernel(x_ref, q_ref, k_ref, v_ref, mod_ref, lam_ref, dnw_ref, w_ref, o_ref,
                      s_scr, p_scr, y_scr, *, lambda_init, n_lat_tiles):
    n_lat = n_lat_tiles * ROW_TILE
    n_all = k_ref.shape[1]
    q_half = N_PAIR_GROUPS * PAIR
    lam = lam_ref[...]
    lam_val = (jnp.exp(jnp.sum(lam[0:1] * lam[1:2], axis=-1, keepdims=True))
               - jnp.exp(jnp.sum(lam[2:3] * lam[3:4], axis=-1, keepdims=True)) + lambda_init)

    def attend(k0, nk):
        keys = slice(k0, k0 + nk)

        def pair(pg):
            qv = jnp.concatenate([q_ref[0, :, h * q_half + pg * PAIR: h * q_half + (pg + 1) * PAIR]
                                  for h in range(2)], axis=0)
            kd = k_ref[0, keys, _kv_lane(pg): _kv_lane(pg) + PAIR]
            vd = v_ref[0, keys, _kv_lane(pg): _kv_lane(pg) + PAIR]
            s_scr[:, 0:nk] = lax.dot_general(qv, kd, (((1,), (1,)), ((), ())),
                                             preferred_element_type=F32)
            s = s_scr[:, 0:nk]
            p = jnp.exp2(s - jnp.max(s, axis=-1, keepdims=True))
            p_scr[:, 0:nk] = p.astype(BF16)
            o = jnp.dot(p_scr[:, 0:nk], vd, preferred_element_type=F32)
            o = o * (1.0 / jnp.sum(p, axis=-1, keepdims=True))
            return o[0:ROW_TILE], o[ROW_TILE:2 * ROW_TILE]

        lo = lax.broadcasted_iota(jnp.int32, (ROW_TILE, PAIR), 1) < HEAD_DIM
        for pg in range(2 * GQA_KV_HEADS):
            o_even, o_odd = pair(pg)
            y_scr[:, pg * PAIR:(pg + 1) * PAIR] = jnp.where(lo, o_even, o_odd).astype(BF16)
        for pg in range(2 * GQA_KV_HEADS, N_PAIR_GROUPS):
            o_0, o_1 = pair(pg)
            od = o_0 - lam_val * o_1
            od = od * lax.rsqrt(jnp.mean(od * od, axis=-1, keepdims=True) + EPS)
            od = od * dnw_ref[...] * (1.0 - lambda_init)
            y_scr[:, pg * PAIR:(pg + 1) * PAIR] = od.astype(BF16)
        y = jnp.dot(y_scr[...], w_ref[...], preferred_element_type=F32)
        o_ref[0] = x_ref[0] + mod_ref[0, 0, 2:3, :] * y

    i = pl.program_id(1)

    @pl.when(i < n_lat_tiles)
    def _():
        attend(0, n_all)

    @pl.when(i >= n_lat_tiles)
    def _():
        attend(n_lat, n_all - n_lat)


def _kv_lane(pg):
    n_gqa_pairs = 2 * GQA_KV_HEADS
    if pg < n_gqa_pairs:
        return (pg // 2) * PAIR
    return (GQA_KV_HEADS + pg - n_gqa_pairs) * PAIR


def _attention(X, Q, K, V, modc, lam, dnw, w_bf16, lambda_init, n_lat_tiles):
    B, N, D = X.shape
    row = lambda b, i: (b, i, 0)
    whole = lambda b, i: (b, 0, 0)
    const2 = lambda b, i: (0, 0)
    return pl.pallas_call(
        functools.partial(_attention_kernel, lambda_init=lambda_init, n_lat_tiles=n_lat_tiles),
        grid=(B, N // ROW_TILE),
        in_specs=[pl.BlockSpec((1, ROW_TILE, D), row),
                  pl.BlockSpec((1, ROW_TILE, Q.shape[-1]), row),
                  pl.BlockSpec((1, N, K.shape[-1]), whole, pipeline_mode=pl.Buffered(1)),
                  pl.BlockSpec((1, N, V.shape[-1]), whole, pipeline_mode=pl.Buffered(1)),
                  pl.BlockSpec((1, 1, 6, D), lambda b, i: (b, i // n_lat_tiles, 0, 0)),
                  pl.BlockSpec(lam.shape, const2),
                  pl.BlockSpec((1, PAIR), const2),
                  pl.BlockSpec(w_bf16.shape, const2, pipeline_mode=pl.Buffered(1))],
        out_specs=pl.BlockSpec((1, ROW_TILE, D), row),
        out_shape=jax.ShapeDtypeStruct((B, N, D), F32),
        scratch_shapes=[pltpu.VMEM((2 * ROW_TILE, N), F32), pltpu.VMEM((2 * ROW_TILE, N), BF16),
                        pltpu.VMEM((ROW_TILE, w_bf16.shape[0]), BF16)],
        compiler_params=_cparams(("arbitrary", "arbitrary")),
        name="attention",
    )(X, Q, K, V, modc, lam, dnw.reshape(1, PAIR), w_bf16)


REC_IN_PAD = 3200
N_SSD_DT = 2 * SSD_HEADS
RET_LANE0 = N_SSD_DT


def _softplus(x):
    return jnp.maximum(x, 0.0) + jnp.log(1.0 + jnp.exp(-jnp.abs(x)))


def _silu(x):
    return x * jax.nn.sigmoid(x)


def _rec_project_kernel(x_ref, nw_ref, mod_ref, w_ref, cos_ref, sin_ref, dtb_ref,
                        rq_ref, rk_ref, rv_ref, g_ref, xbc_ref, dt_ref, *, n_lat):
    h = _modulated(x_ref[0], nw_ref[...], mod_ref, 0, 1, n_lat)
    z = jnp.dot(h.astype(BF16), w_ref[...], preferred_element_type=F32)
    o_rq, o_rk, o_rv, o_rg, o_z, o_xbc, o_dt = np.cumsum((0,) + REC_SPLITS[:-1]).tolist()
    n_qk = REC_SPLITS[0]
    cos, sin = cos_ref[...], sin_ref[...]
    cos2 = jnp.concatenate([cos] * (n_qk // PAIR), axis=1)
    sin2 = jnp.concatenate([sin] * (n_qk // PAIR), axis=1)
    rq = _rope_lanes(z[:, o_rq:o_rq + n_qk], cos2, sin2)
    lo = (lax.broadcasted_iota(jnp.int32, rq.shape, 1) & (PAIR - 1)) < HEAD_DIM
    rq_ref[0, :, 0:n_qk] = jnp.where(lo, rq, 0.0).astype(BF16)
    rq_ref[0, :, n_qk:2 * n_qk] = jnp.where(lo, 0.0, rq).astype(BF16)
    rk_ref[0] = _rope_lanes(z[:, o_rk:o_rk + n_qk] * (RET_QK_DIM ** -0.5), cos2, sin2).astype(BF16)
    rv_ref[0] = z[:, o_rv:o_rg].astype(BF16)
    g_ref[0] = z[:, o_rg:o_xbc].astype(BF16)
    xbc_ref[0] = z[:, o_xbc:o_dt].astype(BF16)
    dt_ref[0] = _softplus(z[:, o_dt:o_dt + PAIR] + dtb_ref[...])


def _rec_project(X, nw, modc, w_bf16, cos_t, sin_t, dtb_row, n_lat_tiles):
    B, N, D = X.shape
    n_in = w_bf16.shape[1]
    row = lambda b, i: (b, i, 0)
    const2 = lambda b, i: (0, 0)
    widths = (2 * REC_SPLITS[0], REC_SPLITS[1], REC_SPLITS[2], REC_SPLITS[3] + REC_SPLITS[4],
              REC_SPLITS[5], PAIR)
    dtypes = (BF16, BF16, BF16, BF16, BF16, F32)
    return pl.pallas_call(
        functools.partial(_rec_project_kernel, n_lat=n_lat_tiles * ROW_TILE),
        grid=(B, N // PROJ_TILE),
        in_specs=[pl.BlockSpec((1, PROJ_TILE, D), row),
                  pl.BlockSpec((1, D), const2),
                  pl.BlockSpec((1, 2, 6, D), _MOD_BOTH),
                  pl.BlockSpec((D, n_in), const2, **_ONCE),
                  pl.BlockSpec((PROJ_TILE, PAIR), lambda b, i: (i, 0)),
                  pl.BlockSpec((PROJ_TILE, PAIR), lambda b, i: (i, 0)),
                  pl.BlockSpec((1, PAIR), const2)],
        out_specs=[pl.BlockSpec((1, PROJ_TILE, w), row) for w in widths],
        out_shape=[jax.ShapeDtypeStruct((B, N, w), dt) for w, dt in zip(widths, dtypes)],
        compiler_params=_cparams(("arbitrary", "arbitrary")),
        name="rec_project",
    )(X, nw.reshape(1, D), modc, w_bf16, cos_t, sin_t, dtb_row)


def _conv_kernel(x_ref, w_ref, b_ref, o_ref, *, n_lat):
    x = x_ref[0].astype(F32)
    n = x.shape[0]
    t = lax.broadcasted_iota(jnp.int32, x.shape, 0)
    first = (t == 0) | (t == n_lat)
    last = (t == n_lat - 1) | (t == n - 1)
    prev = jnp.where(first, 0.0, pltpu.roll(x, 1, 0))
    nxt = jnp.where(last, 0.0, pltpu.roll(x, n - 1, 0))
    y = prev * w_ref[0:1, :] + x * w_ref[1:2, :] + nxt * w_ref[2:3, :] + b_ref[...]
    o_ref[0] = _silu(y).astype(BF16)


def _ssd_conv(XBC, conv_w, conv_b, n_lat):
    B, N, C = XBC.shape
    tc = 256
    return pl.pallas_call(
        functools.partial(_conv_kernel, n_lat=n_lat),
        grid=(B, C // tc),
        in_specs=[pl.BlockSpec((1, N, tc), lambda b, j: (b, 0, j)),
                  pl.BlockSpec((conv_w.shape[0], tc), lambda b, j: (0, j)),
                  pl.BlockSpec((1, tc), lambda b, j: (0, j))],
        out_specs=pl.BlockSpec((1, N, tc), lambda b, j: (b, 0, j)),
        out_shape=jax.ShapeDtypeStruct((B, N, C), BF16),
        compiler_params=_cparams(("arbitrary", "arbitrary")),
        name="ssd_conv",
    )(XBC, conv_w, conv_b.reshape(1, C))


def _split3(x):
    hi = x.astype(BF16)
    r1 = x - hi.astype(F32)
    mid = r1.astype(BF16)
    lo = (r1 - mid.astype(F32)).astype(BF16)
    return hi, mid, lo


def _scan_kernel(rq_ref, rk_ref, rv_ref, a_ref, dt_ref, alog_ref, logit_ref, y_ref, hret, hssd, *, reverse):
    T = rq_ref.shape[1]
    d = 1 if reverse else 0
    step = pl.program_id(1)

    @pl.when(step == 0)
    def _():
        hret[...] = jnp.zeros_like(hret)
        hssd[...] = jnp.zeros_like(hssd)

    lane = lax.broadcasted_iota(jnp.int32, (1, PAIR), 1)
    a_neg = jnp.where(lane < N_SSD_DT, -jnp.exp(alog_ref[...]), 0.0)
    logit = logit_ref[...]
    log_gamma = jnp.where((lane >= RET_LANE0) & (lane < RET_LANE0 + 2 * RET_HEADS),
                          jnp.minimum(logit, 0.0) - jnp.log(1.0 + jnp.exp(-jnp.abs(logit))), 0.0)
    dt = dt_ref[0]
    la = dt * a_neg + log_gamma

    ti = lax.broadcasted_iota(jnp.int32, (T, T), 0)
    tj = lax.broadcasted_iota(jnp.int32, (T, T), 1)
    causal = (tj >= ti) if reverse else (tj <= ti)
    tri = jnp.where(causal, 1.0, 0.0).astype(BF16)
    hi, mid, lo3 = _split3(la)
    P = (jnp.dot(tri, hi, preferred_element_type=F32) + jnp.dot(tri, mid, preferred_element_type=F32)
         + jnp.dot(tri, lo3, preferred_element_type=F32))
    PT = P.T
    tot = P[0:1, :] if reverse else P[T - 1:T, :]
    E = jnp.exp(P)
    KD = jnp.exp(tot - P)
    ET = jnp.exp(tot)
    lo = lax.broadcasted_iota(jnp.int32, (T, PAIR), 1) < HEAD_DIM
    lo_row = lane < HEAD_DIM
    nt = (((1,), (1,)), ((), ()))
    tn = (((0,), (0,)), ((), ()))

    def decay_matrix(c):
        diff = jnp.minimum(P[:, c:c + 1] - PT[c:c + 1, :], 0.0)
        return jnp.where(causal, jnp.exp(diff), 0.0)

    n_qk = RET_HEADS * RET_QK_DIM
    for h in range(RET_HEADS):
        c = RET_LANE0 + d * RET_HEADS + h
        p, half = h // 2, h % 2
        qv = rq_ref[0, :, half * n_qk + p * PAIR: half * n_qk + (p + 1) * PAIR]
        kp = rk_ref[0, :, p * PAIR:(p + 1) * PAIR]
        v = rv_ref[0, :, h * RET_V_DIM:(h + 1) * RET_V_DIM]
        s = lax.dot_general(qv, kp, nt, preferred_element_type=F32)
        aw = (s * decay_matrix(c)).astype(BF16)
        y = jnp.dot(aw, v, preferred_element_type=F32)
        y = y + E[:, c:c + 1] * jnp.dot(qv, hret[h].astype(BF16), preferred_element_type=F32)
        y_ref[0, :, h * RET_V_DIM:(h + 1) * RET_V_DIM] = y.astype(y_ref.dtype)
        kdec = (kp.astype(F32) * KD[:, c:c + 1]).astype(BF16)
        hret[h] = hret[h] * ET[:, c:c + 1] + lax.dot_general(kdec, v, tn, preferred_element_type=F32)

    y0 = RET_HEADS * RET_V_DIM
    r_per_g = SSD_HEADS // SSD_GROUPS
    o_b = SSD_INNER
    o_c = SSD_INNER + SSD_GROUPS * SSD_STATE
    for g in range(SSD_GROUPS):
        cg = a_ref[0, :, o_c + g * SSD_STATE: o_c + (g + 1) * SSD_STATE]
        bg = a_ref[0, :, o_b + g * SSD_STATE: o_b + (g + 1) * SSD_STATE]
        s = lax.dot_general(cg, bg, nt, preferred_element_type=F32)
        ch = jnp.dot(cg, hssd[g].astype(BF16), preferred_element_type=F32)
        vdec, dec_rows = [], []
        for p in range(r_per_g // 2):
            pg = g * (r_per_g // 2) + p
            xs = a_ref[0, :, pg * PAIR:(pg + 1) * PAIR].astype(F32)
            ys, vds, cols = [], [], []
            for half in range(2):
                c = d * SSD_HEADS + 2 * pg + half
                aw = (s * decay_matrix(c)).astype(BF16)
                vh = xs * dt[:, c:c + 1]
                ys.append(jnp.dot(aw, vh.astype(BF16), preferred_element_type=F32))
                vds.append(vh * KD[:, c:c + 1])
                cols.append(c)
            e_pair = jnp.where(lo, E[:, cols[0]:cols[0] + 1], E[:, cols[1]:cols[1] + 1])
            y_ref[0, :, y0 + pg * PAIR: y0 + (pg + 1) * PAIR] = (
                jnp.where(lo, ys[0], ys[1]) + e_pair * ch[:, p * PAIR:(p + 1) * PAIR]).astype(y_ref.dtype)
            vdec.append(jnp.where(lo, vds[0], vds[1]).astype(BF16))
            dec_rows.append(jnp.where(lo_row, ET[:, cols[0]:cols[0] + 1], ET[:, cols[1]:cols[1] + 1]))
        hssd[g] = (hssd[g] * jnp.concatenate(dec_rows, axis=1)
                   + lax.dot_general(bg, jnp.concatenate(vdec, axis=1), tn, preferred_element_type=F32))


def _bidir_scan(RQ, RK, RV, A, DT, alog_row, logit_row, n_lat_tiles, reverse):
    B, N, _ = RQ.shape
    n_tiles = N // ROW_TILE

    def tile(i):
        lat = (n_lat_tiles - i) if reverse else (i - 1)
        return jnp.where(i == 0, n_tiles - 1, lat)

    row = lambda b, i: (b, tile(i), 0)
    const2 = lambda b, i: (0, 0)
    y_w = RET_HEADS * RET_V_DIM + SSD_INNER
    return pl.pallas_call(
        functools.partial(_scan_kernel, reverse=reverse),
        grid=(B, n_tiles),
        in_specs=[pl.BlockSpec((1, ROW_TILE, RQ.shape[-1]), row),
                  pl.BlockSpec((1, ROW_TILE, RK.shape[-1]), row),
                  pl.BlockSpec((1, ROW_TILE, RV.shape[-1]), row),
                  pl.BlockSpec((1, ROW_TILE, A.shape[-1]), row),
                  pl.BlockSpec((1, ROW_TILE, PAIR), row),
                  pl.BlockSpec((1, PAIR), const2),
                  pl.BlockSpec((1, PAIR), const2)],
        out_specs=pl.BlockSpec((1, ROW_TILE, y_w), row),
        out_shape=jax.ShapeDtypeStruct((B, N, y_w), BF16),
        scratch_shapes=[pltpu.VMEM((RET_HEADS, PAIR, RET_V_DIM), F32),
                        pltpu.VMEM((SSD_GROUPS, SSD_STATE, SSD_INNER // SSD_GROUPS), F32)],
        compiler_params=_cparams(("arbitrary", "arbitrary")),
        name="scan_bwd" if reverse else "scan_fwd",
    )(RQ, RK, RV, A, DT, alog_row, logit_row)


def _rec_combine_kernel(x_ref, yf_ref, yb_ref, g_ref, a_ref, mod_ref, dskip_ref, snw_ref, w_ref, o_ref, y_scr,
                        *, n_lat):
    n_ret = RET_HEADS * RET_V_DIM
    y = yf_ref[0].astype(F32) + yb_ref[0].astype(F32)
    gates = g_ref[0].astype(F32)
    for h in range(RET_HEADS):
        sl = slice(h * RET_V_DIM, (h + 1) * RET_V_DIM)
        r = y[:, sl]
        r = r * lax.rsqrt(jnp.mean(r * r, axis=-1, keepdims=True) + EPS)
        y_scr[:, sl] = (r * _silu(gates[:, sl])).astype(BF16)
    gw = SSD_INNER // SSD_GROUPS
    for g in range(SSD_GROUPS):
        sl = slice(g * gw, (g + 1) * gw)
        s = y[:, n_ret + g * gw: n_ret + (g + 1) * gw] + dskip_ref[:, sl] * a_ref[0, :, sl].astype(F32)
        s = s * _silu(gates[:, n_ret + g * gw: n_ret + (g + 1) * gw])
        s = s * lax.rsqrt(jnp.mean(s * s, axis=-1, keepdims=True) + EPS) * snw_ref[:, sl]
        y_scr[:, n_ret + g * gw: n_ret + (g + 1) * gw] = s.astype(BF16)
    gate = _row_mod(mod_ref, 2, x_ref.shape[1], n_lat)
    o_ref[0] = x_ref[0] + gate * jnp.dot(y_scr[...], w_ref[...], preferred_element_type=F32)


def _rec_combine(X, Yf, Yb, G, A, modc, dskip_row, snw_row, w_bf16, n_lat_tiles):
    B, N, D = X.shape
    row = lambda b, i: (b, i, 0)
    const2 = lambda b, i: (0, 0)
    return pl.pallas_call(
        functools.partial(_rec_combine_kernel, n_lat=n_lat_tiles * ROW_TILE),
        grid=(B, N // PROJ_TILE),
        in_specs=[pl.BlockSpec((1, PROJ_TILE, D), row),
                  pl.BlockSpec((1, PROJ_TILE, Yf.shape[-1]), row),
                  pl.BlockSpec((1, PROJ_TILE, Yb.shape[-1]), row),
                  pl.BlockSpec((1, PROJ_TILE, G.shape[-1]), row),
                  pl.BlockSpec((1, PROJ_TILE, SSD_INNER), row),
                  pl.BlockSpec((1, 2, 6, D), _MOD_BOTH),
                  pl.BlockSpec((1, SSD_INNER), const2),
                  pl.BlockSpec((1, SSD_INNER), const2),
                  pl.BlockSpec(w_bf16.shape, const2, **_ONCE)],
        out_specs=pl.BlockSpec((1, PROJ_TILE, D), row),
        out_shape=jax.ShapeDtypeStruct((B, N, D), F32),
        scratch_shapes=[pltpu.VMEM((PROJ_TILE, w_bf16.shape[0]), BF16)],
        compiler_params=_cparams(("arbitrary", "arbitrary")),
        name="rec_combine",
    )(X, Yf, Yb, G, A, modc, dskip_row, snw_row, w_bf16)


def _recurrent_layer(X, nw, modc, w_in, w_out, cos_t, sin_t, ret_decay_logit, conv_w, conv_b, dt_bias,
                     a_log, d_skip, ssd_norm_w, n_lat_tiles):
    D = X.shape[-1]
    w_pad = jnp.pad(w_in, ((0, 0), (0, REC_IN_PAD - w_in.shape[1]))).astype(BF16)
    pad_row = lambda v, at: jnp.pad(v.reshape(1, -1), ((0, 0), (at, PAIR - at - v.size)))
    RQ, RK, RV, G, XBC, DT = _rec_project(X, nw, modc, w_pad, cos_t, sin_t, pad_row(dt_bias, 0), n_lat_tiles)
    A = _ssd_conv(XBC, conv_w, conv_b, n_lat_tiles * ROW_TILE)
    alog_row, logit_row = pad_row(a_log, 0), pad_row(ret_decay_logit, RET_LANE0)
    Yf = _bidir_scan(RQ, RK, RV, A, DT, alog_row, logit_row, n_lat_tiles, False)
    Yb = _bidir_scan(RQ, RK, RV, A, DT, alog_row, logit_row, n_lat_tiles, True)
    return _rec_combine(X, Yf, Yb, G, A, modc, jnp.repeat(d_skip, SSD_HEAD_DIM).reshape(1, -1),
                        ssd_norm_w.reshape(1, -1), w_out.astype(BF16), n_lat_tiles)


SEL_CHUNK = 256
COMBINE_TILE = 768
FFN_ROW_CHUNK = 1024
GATHER_EXPERTS = 8


def _router_kernel(x_ref, nw_ref, mod_ref, rw_ref, h_ref, aff_ref, *, n_lat):
    h = _modulated(x_ref[0], nw_ref[...], mod_ref, 3, 4, n_lat)
    h_hi = h.astype(BF16)
    h_ref[0] = h_hi
    h_lo = (h - h_hi.astype(F32)).astype(BF16)
    w = rw_ref[...]
    w_hi = w.astype(BF16)
    w_lo = (w - w_hi.astype(F32)).astype(BF16)
    logits = (jnp.dot(h_hi, w_hi, preferred_element_type=F32) + jnp.dot(h_lo, w_hi, preferred_element_type=F32)
              + jnp.dot(h_hi, w_lo, preferred_element_type=F32))
    logits = logits.T[0:aff_ref.shape[1], :]
    e = jnp.exp(logits - jnp.max(logits, axis=0, keepdims=True))
    aff_ref[0] = e / jnp.sum(e, axis=0, keepdims=True)


def _mod_router(X, nw, modc, router_w, n_lat, n_rows):
    B, _, D = X.shape
    E = router_w.shape[1]
    rw_pad = jnp.pad(router_w, ((0, 0), (0, PAIR - E)))
    tile = _row_tile(n_rows, PROJ_TILE)
    return pl.pallas_call(
        functools.partial(_router_kernel, n_lat=n_lat),
        grid=(B, n_rows // tile),
        in_specs=[pl.BlockSpec((1, tile, D), lambda b, i: (b, i, 0)),
                  pl.BlockSpec((1, D), lambda b, i: (0, 0)),
                  pl.BlockSpec((1, 2, 6, D), _MOD_BOTH),
                  pl.BlockSpec((D, PAIR), lambda b, i: (0, 0))],
        out_specs=[pl.BlockSpec((1, tile, D), lambda b, i: (b, i, 0)),
                   pl.BlockSpec((1, E, tile), lambda b, i: (b, 0, i))],
        out_shape=[jax.ShapeDtypeStruct((B, n_rows, D), BF16),
                   jax.ShapeDtypeStruct((B, E, n_rows), F32)],
        compiler_params=_cparams(("arbitrary", "arbitrary")),
        name="mod_router",
    )(X, nw.reshape(1, D), modc, rw_pad)


def _count(mask_f32):
    return jnp.sum(mask_f32, axis=-1, keepdims=True)


def _select_top(a, cap):
    E, n = a.shape
    v = pltpu.bitcast(a, jnp.int32)
    thr = jnp.zeros((E, 1), jnp.int32)
    for bit in range(30, -1, -1):
        cand = thr | (1 << bit)
        thr = jnp.where(_count(jnp.where(v >= cand, 1.0, 0.0)) >= cap, cand, thr)
    gt = v > thr
    eq = v == thr
    need = cap - _count(jnp.where(gt, 1.0, 0.0))
    idx = lax.broadcasted_iota(jnp.int32, (E, n), 1)
    last = jnp.zeros((E, 1), jnp.int32)
    for bit in range(n.bit_length() - 2, -1, -1):
        cand = last | (1 << bit)
        below = _count(jnp.where(eq, jnp.where(idx < cand, 1.0, 0.0), 0.0))
        last = jnp.where(below < need, cand, last)
    sel = jnp.where(gt, 1.0, jnp.where(eq, jnp.where(idx <= last, 1.0, 0.0), 0.0))
    si = lax.broadcasted_iota(jnp.int32, (SEL_CHUNK, SEL_CHUNK), 0)
    sj = lax.broadcasted_iota(jnp.int32, (SEL_CHUNK, SEL_CHUNK), 1)
    before = jnp.where(si < sj, 1.0, 0.0).astype(BF16)
    base = jnp.zeros((E, 1), F32)
    pos = []
    for k in range(n // SEL_CHUNK):
        sk = sel[:, k * SEL_CHUNK:(k + 1) * SEL_CHUNK]
        pos.append(jnp.dot(sk.astype(BF16), before, preferred_element_type=F32) + base)
        base = base + _count(sk)
    pos = jnp.concatenate(pos, axis=1) if len(pos) > 1 else pos[0]
    return jnp.where(sel > 0.0, pos, -1.0), jnp.where(sel > 0.0, a, 0.0)


def _token_sets(n_lat, n_rows):
    sets = [(0, n_lat, EC_CAPACITY * n_lat // N_EXPERTS)]
    if n_rows > n_lat:
        sets.append((n_lat, n_rows, EC_CAPACITY * (n_rows - n_lat) // N_EXPERTS))
    return sets


def _select_kernel(aff_ref, slot_ref, slot_t_ref, gate_t_ref, *, sets):
    E = aff_ref.shape[1]
    pad_rows = PAIR - E
    for lo, hi, cap in sets:
        slot, gate = _select_top(aff_ref[0, :, lo:hi], cap)
        slot_ref[0, :, lo:hi] = slot
        slot_t_ref[0, lo:hi, :] = jnp.concatenate([slot, jnp.full((pad_rows, hi - lo), -1.0, F32)], axis=0).T
        gate_t_ref[0, lo:hi, :] = jnp.concatenate([gate, jnp.zeros((pad_rows, hi - lo), F32)], axis=0).T


def _ec_select(aff, sets):
    B, E, N = aff.shape
    return pl.pallas_call(
        functools.partial(_select_kernel, sets=sets),
        grid=(B,),
        in_specs=[pl.BlockSpec((1, E, N), lambda b: (b, 0, 0))],
        out_specs=[pl.BlockSpec((1, E, N), lambda b: (b, 0, 0)),
                   pl.BlockSpec((1, N, PAIR), lambda b: (b, 0, 0)),
                   pl.BlockSpec((1, N, PAIR), lambda b: (b, 0, 0))],
        out_shape=[jax.ShapeDtypeStruct((B, E, N), F32),
                   jax.ShapeDtypeStruct((B, N, PAIR), F32),
                   jax.ShapeDtypeStruct((B, N, PAIR), F32)],
        compiler_params=_cparams(("arbitrary",)),
        name="ec_select",
    )(aff)


def _gather_kernel(slot_ref, h_ref, *x_refs, sets):
    n_e = x_refs[0].shape[0]
    e0 = pl.multiple_of(pl.program_id(1) * n_e, n_e)
    for (lo, hi, cap), x_ref in zip(sets, x_refs):
        c = lax.broadcasted_iota(jnp.int32, (cap, hi - lo), 0).astype(F32)
        onehot = jnp.concatenate(
            [jnp.where(slot_ref[0, pl.ds(e0 + j, 1), lo:hi] == c, 1.0, 0.0).astype(BF16)
             for j in range(n_e)], axis=0)
        x = jnp.dot(onehot, h_ref[0, lo:hi, :], preferred_element_type=F32)
        for j in range(n_e):
            x_ref[j] = x[j * cap:(j + 1) * cap, :].astype(BF16)


def _ec_gather(slot, H, sets):
    B, E, N = slot.shape
    D = H.shape[-1]
    return pl.pallas_call(
        functools.partial(_gather_kernel, sets=sets),
        grid=(B, E // GATHER_EXPERTS),
        in_specs=[pl.BlockSpec((1, E, N), lambda b, e: (b, 0, 0)),
                  pl.BlockSpec((1, N, D), lambda b, e: (b, 0, 0))],
        out_specs=[pl.BlockSpec((GATHER_EXPERTS, cap, D), lambda b, e: (e, b, 0)) for _, _, cap in sets],
        out_shape=[jax.ShapeDtypeStruct((E, B * cap, D), BF16) for _, _, cap in sets],
        compiler_params=_cparams(("arbitrary", "arbitrary")),
        name="ec_gather",
    )(slot, H)


def _ffn_kernel(*refs, n_groups, row_chunk):
    x_refs = refs[:n_groups]
    wg_ref, wu_ref, wd_ref = refs[n_groups:n_groups + 3]
    o_refs = refs[n_groups + 3:2 * n_groups + 3]
    acc_refs = refs[2 * n_groups + 3:3 * n_groups + 3]
    wg_s, wu_s, wd_s = refs[3 * n_groups + 3:]
    f = pl.program_id(2)
    n_f = pl.num_programs(2)
    wg_s[...] = wg_ref[0, 0].astype(BF16)
    wu_s[...] = wu_ref[0, 0].astype(BF16)
    wd_s[...] = wd_ref[0, 0].astype(BF16)

    @pl.when(f == 0)
    def _():
        for acc_ref in acc_refs:
            acc_ref[...] = jnp.zeros_like(acc_ref)

    def chunk(x_ref, acc_ref, start, size):
        rows = pl.ds(start, size)
        xr = x_ref[0, rows, :]
        a = jnp.dot(xr, wg_s[...], preferred_element_type=F32)
        u = jnp.dot(xr, wu_s[...], preferred_element_type=F32)
        hm = (a * jax.nn.sigmoid(a) * u).astype(BF16)
        acc_ref[rows, :] += jnp.dot(hm, wd_s[...], preferred_element_type=F32)

    for x_ref, acc_ref in zip(x_refs, acc_refs):
        n_rows = x_ref.shape[1]
        for start in range(0, n_rows, row_chunk):
            chunk(x_ref, acc_ref, start, min(row_chunk, n_rows - start))

    @pl.when(f == n_f - 1)
    def _():
        for o_ref, acc_ref in zip(o_refs, acc_refs):
            o_ref[0] = acc_ref[...].astype(o_ref.dtype)


def _expert_ffn(xs, w_gate, w_up, w_down, layer):
    E, _, D = xs[0].shape
    F = w_gate.shape[-1]
    m_tiles = 2
    tms = [x.shape[1] // m_tiles for x in xs]
    n = len(xs)
    return pl.pallas_call(
        functools.partial(_ffn_kernel, n_groups=n, row_chunk=FFN_ROW_CHUNK),
        grid=(E, m_tiles, F // FF_TILE),
        in_specs=[pl.BlockSpec((1, tm, D), lambda e, m, f: (e, m, 0)) for tm in tms]
        + [pl.BlockSpec((1, 1, D, FF_TILE), lambda e, m, f: (layer, e, 0, f)),
           pl.BlockSpec((1, 1, D, FF_TILE), lambda e, m, f: (layer, e, 0, f)),
           pl.BlockSpec((1, 1, FF_TILE, D), lambda e, m, f: (layer, e, f, 0))],
        out_specs=[pl.BlockSpec((1, tm, D), lambda e, m, f: (e, m, 0)) for tm in tms],
        out_shape=[jax.ShapeDtypeStruct(x.shape, BF16) for x in xs],
        scratch_shapes=[pltpu.VMEM((tm, D), F32) for tm in tms]
        + [pltpu.VMEM((D, FF_TILE), BF16), pltpu.VMEM((D, FF_TILE), BF16), pltpu.VMEM((FF_TILE, D), BF16)],
        compiler_params=_cparams(("arbitrary", "arbitrary", "arbitrary")),
        name="expert_ffn",
    )(*xs, w_gate, w_up, w_down)


def _combine_kernel(x_ref, slot_t_ref, gate_t_ref, mod_ref, fnw_ref, *refs, n_lat, n_rows, final_norm):
    y_refs, o_ref = refs[:-1], refs[-1]
    E = y_refs[0].shape[0]
    tile = x_ref.shape[1]

    def scatter(rows, seg):
        y_ref = y_refs[seg]
        n = rows.stop - rows.start
        cap = y_ref.shape[1]
        c = lax.broadcasted_iota(jnp.int32, (n, cap), 1).astype(F32)
        acc = jnp.zeros((n, x_ref.shape[2]), F32)
        for e in range(E):
            onehot = jnp.where(slot_t_ref[0, rows, e:e + 1] == c, 1.0, 0.0).astype(BF16)
            acc = acc + gate_t_ref[0, rows, e:e + 1] * jnp.dot(onehot, y_ref[e], preferred_element_type=F32)
        out = x_ref[0, rows, :] + mod_ref[0, seg, 5:6, :] * acc
        if final_norm:
            out = out * lax.rsqrt(jnp.mean(out * out, axis=-1, keepdims=True) + EPS) * fnw_ref[...]
        o_ref[0, rows, :] = out

    splits = {}
    for j in range(n_rows // tile):
        splits.setdefault(min(max(n_lat - j * tile, 0), tile), []).append(j)
    i = pl.program_id(1)
    for lat_rows, tiles in splits.items():
        @pl.when(functools.reduce(jnp.logical_or, [i == j for j in tiles]))
        def _(lat_rows=lat_rows):
            if lat_rows > 0:
                scatter(slice(0, lat_rows), 0)
            if lat_rows < tile:
                scatter(slice(lat_rows, tile), 1)


def _row_tile(n_rows, preferred):
    return preferred if n_rows % preferred == 0 else 2 * ROW_TILE


def _ec_combine(X, slot_t, gate_t, Ys, modc, sets, n_lat, final_norm_w):
    B, _, D = X.shape
    E = Ys[0].shape[0]
    n_rows = sets[-1][1]
    tile = _row_tile(n_rows, COMBINE_TILE)
    row = lambda b, i: (b, i, 0)
    final_norm = final_norm_w is not None
    fnw = final_norm_w.reshape(1, D) if final_norm else jnp.ones((1, D), F32)
    return pl.pallas_call(
        functools.partial(_combine_kernel, n_lat=n_lat, n_rows=n_rows, final_norm=final_norm),
        grid=(B, n_rows // tile),
        in_specs=[pl.BlockSpec((1, tile, D), row),
                  pl.BlockSpec((1, tile, PAIR), row),
                  pl.BlockSpec((1, tile, PAIR), row),
                  pl.BlockSpec((1, 2, 6, D), _MOD_BOTH),
                  pl.BlockSpec((1, D), lambda b, i: (0, 0))]
        + [pl.BlockSpec((E, cap, D), lambda b, i: (0, b, 0)) for _, _, cap in sets],
        out_specs=pl.BlockSpec((1, tile, D), row),
        out_shape=jax.ShapeDtypeStruct((B, n_rows, D), F32),
        compiler_params=_cparams(("arbitrary", "arbitrary")),
        name="ec_combine",
    )(X, slot_t, gate_t, modc, fnw, *Ys)


def _moe_layer(X, nw, modc, router_w, w_gate, w_up, w_down, layer, n_lat, n_rows, final_norm_w=None):
    sets = _token_sets(n_lat, n_rows)
    H, aff = _mod_router(X, nw, modc, router_w, n_lat, n_rows)
    slot, slot_t, gate_t = _ec_select(aff, sets)
    Xs = _ec_gather(slot, H, sets)
    Ys = _expert_ffn(Xs, w_gate, w_up, w_down, layer)
    return _ec_combine(X, slot_t, gate_t, Ys, modc, sets, n_lat, final_norm_w)


def _split(z, sizes):
    idx = np.cumsum(sizes)[:-1].tolist()
    return jnp.split(z, idx, axis=-1)


def _rmsnorm(x, w=None):
    y = x * lax.rsqrt(jnp.mean(x * x, axis=-1, keepdims=True) + EPS)
    if w is not None:
        y = y * w
    return y


def _rope_tables(n):
    t = jnp.arange(n)
    row = (t // GRID_W).astype(F32)
    col = (t % GRID_W).astype(F32)
    n_freq = HEAD_DIM // 4
    inv = ROPE_THETA ** (-jnp.arange(n_freq, dtype=F32) / n_freq)
    ang = jnp.concatenate([row[:, None] * inv, col[:, None] * inv], axis=-1)
    return jnp.cos(ang), jnp.sin(ang)


def _rope(x, cos, sin):
    shape = (cos.shape[0],) + (1,) * (x.ndim - 3) + (cos.shape[1],)
    c, s = cos.reshape(shape), sin.reshape(shape)
    x1, x2 = jnp.split(x, 2, axis=-1)
    return jnp.concatenate([x1 * c - x2 * s, x2 * c + x1 * s], axis=-1)


def _attend(q, k, v, shared_k):
    scale = q.shape[-1] ** -0.5
    k_sub = 'bkhd' if shared_k else 'bkhmd'

    def one_block(qb):
        s = jnp.einsum(f'bqhmd,{k_sub}->bhmqk', qb, k, preferred_element_type=F32) * scale
        p = jax.nn.softmax(s, axis=-1)
        return jnp.einsum('bhmqk,bkhv->bqhmv', p, v)

    B, Sq = q.shape[:2]
    nb = Sq // Q_BLOCK
    qb = jnp.moveaxis(q.reshape(B, nb, Q_BLOCK, *q.shape[2:]), 1, 0)
    out = lax.map(one_block, qb)
    return jnp.moveaxis(out, 0, 1).reshape(B, Sq, *out.shape[3:])


def _attention_core(zl, zc, cos, sin, q_norm_w, k_norm_w, lam, diff_norm_w, lambda_init, need_ctx):
    def project(z, use_rope):
        B, n, _ = z.shape
        aq, ak, av, bq, bk, bv = _split(z, ATT_SPLITS)
        aq = _rmsnorm(aq.reshape(B, n, GQA_HEADS, HEAD_DIM), q_norm_w)
        ak = _rmsnorm(ak.reshape(B, n, GQA_KV_HEADS, HEAD_DIM), k_norm_w)
        bq = bq.reshape(B, n, DIFF_HEADS, 2, HEAD_DIM)
        bk = bk.reshape(B, n, DIFF_HEADS, 2, HEAD_DIM)
        if use_rope:
            aq, ak, bq, bk = (_rope(t, cos, sin) for t in (aq, ak, bq, bk))
        aq = aq.reshape(B, n, GQA_KV_HEADS, GQA_HEADS // GQA_KV_HEADS, HEAD_DIM)
        av = av.reshape(B, n, GQA_KV_HEADS, HEAD_DIM)
        bv = bv.reshape(B, n, DIFF_HEADS, 2 * HEAD_DIM)
        return (aq, bq), (ak, av, bk, bv)

    lat_q, lat_kv = project(zl, True)
    ctx_q, ctx_kv = project(zc, False)
    lam_val = jnp.exp(jnp.sum(lam[0] * lam[1])) - jnp.exp(jnp.sum(lam[2] * lam[3])) + lambda_init

    def mix(q_side, kv):
        aq, bq = q_side
        ak, av, bk, bv = kv
        B, n = aq.shape[:2]
        a = _attend(aq, ak, av, True).reshape(B, n, -1)
        o = _attend(bq, bk, bv, False)
        o = o[..., 0, :] - lam_val * o[..., 1, :]
        b = (_rmsnorm(o, diff_norm_w) * (1 - lambda_init)).reshape(B, n, -1)
        return jnp.concatenate([a, b], axis=-1)

    kv_all = tuple(jnp.concatenate([l, c], axis=1) for l, c in zip(lat_kv, ctx_kv))
    yl = mix(lat_q, kv_all)
    yc = mix(ctx_q, ctx_kv) if need_ctx else jnp.zeros((zc.shape[0], zc.shape[1], 2 * HALF_W), F32)
    return yl, yc


def _chunk_scan(q, k, v, log_a, h0):
    B, S = q.shape[:2]
    n = S // CHUNK
    tril = jnp.tril(jnp.ones((CHUNK, CHUNK), dtype=bool))

    def to_chunks(a):
        return jnp.moveaxis(a.reshape(B, n, CHUNK, *a.shape[2:]), 1, 0)

    def step(h, inp):
        qc, kc, vc, lac = inp
        cum = jnp.cumsum(lac.astype(F32), axis=1)
        cum_t = jnp.moveaxis(cum, 1, -1)
        seg = jnp.exp(jnp.where(tril, cum_t[..., :, None] - cum_t[..., None, :], -jnp.inf))
        qk = jnp.einsum('btgk,bsgk->bgts', qc, kc)
        y = (jnp.einsum('bgts,bgrts,bsgrv->btgrv', qk, seg, vc)
             + jnp.einsum('btgk,bgrkv->btgrv', qc, h) * jnp.exp(cum)[..., None])
        last = cum[:, -1]
        h_new = (h * jnp.exp(last)[..., None, None]
                 + jnp.einsum('bsgk,bsgr,bsgrv->bgrkv', kc, jnp.exp(last[:, None] - cum), vc))
        return h_new, y

    h, ys = lax.scan(step, h0, tuple(to_chunks(a) for a in (q, k, v, log_a)))
    return jnp.moveaxis(ys, 0, 1).reshape(B, S, *ys.shape[3:]), h


def _prefix_scan(lat, ctx, reverse):
    if reverse:
        lat = tuple(jnp.flip(a, 1) for a in lat)
        ctx = tuple(jnp.flip(a, 1) for a in ctx)
    q, v = ctx[0], ctx[2]
    h0 = jnp.zeros((q.shape[0], q.shape[2], v.shape[3], q.shape[3], v.shape[4]), F32)
    y_c, h_c = _chunk_scan(*ctx, h0)
    y_l, _ = _chunk_scan(*lat, h_c)
    if reverse:
        y_l, y_c = jnp.flip(y_l, 1), jnp.flip(y_c, 1)
    return y_l, y_c


def _bidir(lat_f, ctx_f, lat_b, ctx_b):
    yl_f, yc_f = _prefix_scan(lat_f, ctx_f, False)
    yl_b, yc_b = _prefix_scan(lat_b, ctx_b, True)
    return yl_f + yl_b, yc_f + yc_b


def _dwconv(x, w, b):
    K = w.shape[0]
    out = lax.conv_general_dilated(x, w[:, None, :], window_strides=(1,),
                                   padding=[(K // 2, K // 2)],
                                   dimension_numbers=('NWC', 'WIO', 'NWC'),
                                   feature_group_count=x.shape[-1])
    return out + b


def _recurrent_core(zl, zc, cos, sin, ret_decay_logit, conv_w, conv_b, dt_bias, a_log, d_skip,
                    ssd_norm_w):
    log_gamma = jax.nn.log_sigmoid(ret_decay_logit)
    a_neg = -jnp.exp(a_log)
    r_per_g = SSD_HEADS // SSD_GROUPS

    def project(z, use_rope):
        B, n, _ = z.shape
        rq, rk, rv, rg, zz, xbc, dt = _split(z, REC_SPLITS)
        rq = rq.reshape(B, n, RET_HEADS, RET_QK_DIM)
        rk = rk.reshape(B, n, RET_HEADS, RET_QK_DIM) * RET_QK_DIM ** -0.5
        if use_rope:
            rq, rk = _rope(rq, cos, sin), _rope(rk, cos, sin)
        rv = rv.reshape(B, n, RET_HEADS, 1, RET_V_DIM)
        ret = [(rq, rk, rv, jnp.broadcast_to(log_gamma[d][:, None], (B, n, RET_HEADS, 1)))
               for d in (0, 1)]
        xbc = jax.nn.silu(_dwconv(xbc, conv_w, conv_b))
        xs, bm, cm = _split(xbc, (SSD_INNER, SSD_GROUPS * SSD_STATE, SSD_GROUPS * SSD_STATE))
        xs = xs.reshape(B, n, SSD_HEADS, SSD_HEAD_DIM)
        bm = bm.reshape(B, n, SSD_GROUPS, SSD_STATE)
        cm = cm.reshape(B, n, SSD_GROUPS, SSD_STATE)
        dt = jax.nn.softplus(dt.reshape(B, n, 2, SSD_HEADS) + dt_bias)
        ssd = [(cm, bm,
                (xs * dt[:, :, d, :, None]).reshape(B, n, SSD_GROUPS, r_per_g, SSD_HEAD_DIM),
                (dt[:, :, d] * a_neg[d]).reshape(B, n, SSD_GROUPS, r_per_g))
               for d in (0, 1)]
        return ret, ssd, rg, zz, xs

    lat_ret, lat_ssd, lat_rg, lat_z, lat_xs = project(zl, True)
    ctx_ret, ctx_ssd, ctx_rg, ctx_z, ctx_xs = project(zc, False)
    ret_l, ret_c = _bidir(lat_ret[0], ctx_ret[0], lat_ret[1], ctx_ret[1])
    ssd_l, ssd_c = _bidir(lat_ssd[0], ctx_ssd[0], lat_ssd[1], ctx_ssd[1])

    def combine(ret_y, ssd_y, rg, zz, xs):
        B, n = rg.shape[:2]
        r = _rmsnorm(ret_y.reshape(B, n, RET_HEADS, RET_V_DIM)).reshape(B, n, -1) * jax.nn.silu(rg)
        s = ssd_y.reshape(B, n, SSD_HEADS, SSD_HEAD_DIM) + d_skip[:, None] * xs
        s = s.reshape(B, n, -1) * jax.nn.silu(zz)
        s = _rmsnorm(s.reshape(B, n, SSD_GROUPS, -1)).reshape(B, n, -1) * ssd_norm_w
        return jnp.concatenate([r, s], axis=-1)

    return combine(ret_l, ssd_l, lat_rg, lat_z, lat_xs), combine(ret_c, ssd_c, ctx_rg, ctx_z, ctx_xs)


def _dispatch_tokens(h, aff):
    B, n, D = h.shape
    cap = EC_CAPACITY * n // N_EXPERTS
    g, idx = lax.top_k(jnp.swapaxes(aff, 1, 2), cap)
    xin = jax.vmap(lambda xs, i: xs[i])(h, idx)
    return xin, g, idx


def _combine_tokens(y, g, idx, n):
    D = y.shape[-1]
    y = y * g[..., None]
    return jax.vmap(lambda val, i: jax.ops.segment_sum(val.reshape(-1, D), i.reshape(-1),
                                                       num_segments=n))(y, idx)


def kernel(x, c, ctx, c_ctx, ada_w, ada_b, norm1_w, norm2_w, att_w_in, att_w_out, att_q_norm_w,
           att_k_norm_w, diff_lambda, diff_norm_w, rec_w_in, rec_w_out, ret_decay_logit, ssd_conv_w,
           ssd_conv_b, ssd_dt_bias, ssd_a_log, ssd_d_skip, ssd_norm_w, router_w, expert_w_gate,
           expert_w_up, expert_w_down, final_norm_w):
    B, n_lat, D = x.shape
    n_ctx = ctx.shape[1]
    n_lat_tiles = n_lat // ROW_TILE
    cos, sin = _rope_tables(n_lat)
    reps = PAIR // cos.shape[1]
    cos_t = jnp.concatenate([jnp.tile(cos, (1, reps)), jnp.ones((n_ctx, PAIR), F32)], axis=0)
    sin_t = jnp.concatenate([jnp.tile(jnp.concatenate([-sin, sin], axis=1), (1, reps // 2)),
                             jnp.zeros((n_ctx, PAIR), F32)], axis=0)
    gmat = jnp.kron(jnp.eye(GQA_HEADS, dtype=F32), jnp.ones((HEAD_DIM, HEAD_DIM), F32)).astype(BF16)

    n_cond = 24
    cvec = jnp.concatenate([c, c_ctx[None, :], jnp.zeros((n_cond - B - 1, D), F32)], axis=0)
    mods = _ada_modulation(cvec, ada_w, ada_b)

    X = jnp.concatenate([x, ctx], axis=1)
    for layer in range(DEPTH):
        i = layer // 2
        need_ctx = layer < DEPTH - 1
        mod_lat = mods[layer, :B].reshape(B, 1, 6, D)
        mod_ctx = jnp.broadcast_to(mods[layer, B].reshape(1, 1, 6, D), (B, 1, 6, D))
        modc = jnp.concatenate([mod_lat, mod_ctx], axis=1)

        if layer % 2 == 0:
            lambda_init = 0.8 - 0.6 * math.exp(-0.3 * layer)
            Q, K, V = _att_project(X, norm1_w[layer], modc, att_w_in[i].astype(BF16), cos_t, sin_t, gmat,
                                   jnp.tile(att_q_norm_w[i], GQA_HEADS).reshape(1, -1),
                                   jnp.tile(att_k_norm_w[i], GQA_KV_HEADS).reshape(1, -1), n_lat_tiles)
            X = _attention(X, Q, K, V, modc, diff_lambda[i], diff_norm_w[i], att_w_out[i].astype(BF16),
                           lambda_init, n_lat_tiles)
        else:
            X = _recurrent_layer(X, norm1_w[layer], modc, rec_w_in[i], rec_w_out[i], cos_t, sin_t,
                                 ret_decay_logit[i], ssd_conv_w[i], ssd_conv_b[i], ssd_dt_bias[i],
                                 ssd_a_log[i], ssd_d_skip[i], ssd_norm_w[i], n_lat_tiles)

        X = _moe_layer(X, norm2_w[layer], modc, router_w[layer], expert_w_gate, expert_w_up, expert_w_down,
                       layer, n_lat, n_lat + n_ctx if need_ctx else n_lat,
                       None if need_ctx else final_norm_w)

    return X
```

```python
import functools
import math

import jax
import jax.numpy as jnp
import numpy as np
from jax import lax
from jax.experimental import pallas as pl
from jax.experimental.pallas import tpu as pltpu

D_MODEL = 1024
DEPTH = 4
GRID_W = 64
HEAD_DIM = 64
ROPE_THETA = 10000.0
EPS = 1e-6
HALF_W = D_MODEL // 2
GQA_HEADS = HALF_W // HEAD_DIM
GQA_KV_HEADS = GQA_HEADS // 4
DIFF_HEADS = HALF_W // (2 * HEAD_DIM)
RET_HEADS = HALF_W // (2 * HEAD_DIM)
RET_QK_DIM = HEAD_DIM
RET_V_DIM = 2 * HEAD_DIM
SSD_HEAD_DIM = HEAD_DIM
SSD_HEADS = HALF_W // SSD_HEAD_DIM
SSD_GROUPS = 2
SSD_STATE = 128
SSD_INNER = SSD_HEADS * SSD_HEAD_DIM
SSD_XBC = SSD_INNER + 2 * SSD_GROUPS * SSD_STATE
N_EXPERTS = 16
EC_CAPACITY = 2
EXPERT_FF = ((8 * D_MODEL // 3 + 255) // 256) * 256

ATT_SPLITS = (GQA_HEADS * HEAD_DIM, GQA_KV_HEADS * HEAD_DIM, GQA_KV_HEADS * HEAD_DIM,
              DIFF_HEADS * 2 * HEAD_DIM, DIFF_HEADS * 2 * HEAD_DIM, DIFF_HEADS * 2 * HEAD_DIM)
REC_SPLITS = (RET_HEADS * RET_QK_DIM, RET_HEADS * RET_QK_DIM, RET_HEADS * RET_V_DIM,
              RET_HEADS * RET_V_DIM, SSD_INNER, SSD_XBC, 2 * SSD_HEADS)

ROW_TILE = 256
PROJ_TILE = 768
FF_TILE = 256
VMEM_LIMIT = 56 * 1024 * 1024
BF16 = jnp.bfloat16
F32 = jnp.float32


def _cparams(sem):
    return pltpu.CompilerParams(dimension_semantics=sem, vmem_limit_bytes=VMEM_LIMIT)


def _ada_kernel(c_ref, w_ref, b_ref, o_ref):
    c = c_ref[...]
    s = c * jax.nn.sigmoid(c)
    o_ref[0] = jnp.dot(s, w_ref[0], precision=lax.Precision.HIGHEST,
                       preferred_element_type=F32) + b_ref[0]


def _ada_modulation(cvec, ada_w, ada_b):
    R, D = cvec.shape
    n_out = ada_w.shape[-1]
    tn = 1536
    return pl.pallas_call(
        _ada_kernel,
        grid=(DEPTH, n_out // tn),
        in_specs=[pl.BlockSpec((R, D), lambda l, j: (0, 0)),
                  pl.BlockSpec((1, D, tn), lambda l, j: (l, 0, j)),
                  pl.BlockSpec((1, 1, tn), lambda l, j: (l, 0, j))],
        out_specs=pl.BlockSpec((1, R, tn), lambda l, j: (l, 0, j)),
        out_shape=jax.ShapeDtypeStruct((DEPTH, R, n_out), F32),
        compiler_params=_cparams(("arbitrary", "arbitrary")),
        name="ada_modulation",
    )(cvec, ada_w, ada_b.reshape(DEPTH, 1, n_out))


def _row_mod(mod_ref, idx, n_rows, n_lat):
    row = pl.program_id(1) * n_rows + lax.broadcasted_iota(jnp.int32, (n_rows, 1), 0)
    return jnp.where(row < n_lat, mod_ref[0, 0, idx:idx + 1, :], mod_ref[0, 1, idx:idx + 1, :])


def _modulated(x, nw, mod_ref, shift_idx, scale_idx, n_lat):
    n_rows = x.shape[0]
    ms = jnp.mean(x * x, axis=-1, keepdims=True)
    y = x * lax.rsqrt(ms + EPS) * nw
    return (y * (1.0 + _row_mod(mod_ref, scale_idx, n_rows, n_lat))
            + _row_mod(mod_ref, shift_idx, n_rows, n_lat))


_MOD_BOTH = lambda b, i: (b, 0, 0, 0)
_ONCE = dict(pipeline_mode=pl.Buffered(1))


PAIR = 2 * HEAD_DIM
N_PAIR_GROUPS = 2 * HALF_W // PAIR
ATT_KV_W = GQA_KV_HEADS * PAIR + DIFF_HEADS * PAIR


def _head_sumsq(x, g):
    x2 = x * x
    hi = x2.astype(BF16)
    lo = (x2 - hi.astype(F32)).astype(BF16)
    return jnp.dot(hi, g, preferred_element_type=F32) + jnp.dot(lo, g, preferred_element_type=F32)


def _rope_lanes(x, cos, sin_signed):
    W = x.shape[-1]
    half = HEAD_DIM // 2
    lane = lax.broadcasted_iota(jnp.int32, x.shape, 1) & (HEAD_DIM - 1)
    partner = jnp.where(lane < half, pltpu.roll(x, W - half, 1), pltpu.roll(x, half, 1))
    return x * cos + partner * sin_signed


def _att_project_kernel(x_ref, nw_ref, mod_ref, w_ref, cos_ref, sin_ref, g_ref, qw_ref, kw_ref,
                        q_ref, k_ref, v_ref, *, n_lat):
    h = _modulated(x_ref[0], nw_ref[...], mod_ref, 0, 1, n_lat)
    z = jnp.dot(h.astype(BF16), w_ref[...], preferred_element_type=F32)
    o_aq, o_ak, o_av, o_bq, o_bk, o_bv = np.cumsum((0,) + ATT_SPLITS[:-1]).tolist()
    n_aq, n_ak, n_bq = ATT_SPLITS[0], ATT_SPLITS[1], ATT_SPLITS[3]
    cos, sin = cos_ref[...], sin_ref[...]
    cos4 = jnp.concatenate([cos] * (n_aq // PAIR), axis=1)
    sin4 = jnp.concatenate([sin] * (n_aq // PAIR), axis=1)
    inv_d = 1.0 / HEAD_DIM
    q_scale = HEAD_DIM ** -0.5 * math.log2(math.e)

    aq = z[:, o_aq:o_aq + n_aq]
    aq = aq * lax.rsqrt(_head_sumsq(aq, g_ref[...]) * inv_d + EPS) * qw_ref[...]
    aq = _rope_lanes(aq, cos4, sin4) * q_scale
    bq = _rope_lanes(z[:, o_bq:o_bq + n_bq], cos4, sin4) * q_scale
    q = jnp.concatenate([aq, bq], axis=1)
    lo = (lax.broadcasted_iota(jnp.int32, q.shape, 1) & (PAIR - 1)) < HEAD_DIM
    q_ref[0, :, 0:q.shape[1]] = jnp.where(lo, q, 0.0).astype(BF16)
    q_ref[0, :, q.shape[1]:2 * q.shape[1]] = jnp.where(lo, 0.0, q).astype(BF16)

    ak = z[:, o_ak:o_ak + n_ak]
    ak = ak * lax.rsqrt(_head_sumsq(ak, g_ref[0:n_ak, 0:n_ak]) * inv_d + EPS) * kw_ref[...]
    ak = _rope_lanes(ak, cos, sin)
    av = z[:, o_av:o_av + n_ak]
    lo_kv = lax.broadcasted_iota(jnp.int32, ak.shape, 1) < HEAD_DIM
    ak_sw, av_sw = pltpu.roll(ak, HEAD_DIM, 1), pltpu.roll(av, HEAD_DIM, 1)
    k_ref[0, :, 0:PAIR] = jnp.where(lo_kv, ak, ak_sw).astype(BF16)
    k_ref[0, :, PAIR:2 * PAIR] = jnp.where(lo_kv, ak_sw, ak).astype(BF16)
    v_ref[0, :, 0:PAIR] = jnp.where(lo_kv, av, av_sw).astype(BF16)
    v_ref[0, :, PAIR:2 * PAIR] = jnp.where(lo_kv, av_sw, av).astype(BF16)
    k_ref[0, :, 2 * PAIR:] = _rope_lanes(z[:, o_bk:o_bk + n_bq], cos4, sin4).astype(BF16)
    v_ref[0, :, 2 * PAIR:] = z[:, o_bv:o_bv + n_bq].astype(BF16)


def _att_project(X, nw, modc, w_bf16, cos_t, sin_t, gmat, qw_t, kw_t, n_lat_tiles):
    B, N, D = X.shape
    n_in = w_bf16.shape[1]
    q_w = 2 * N_PAIR_GROUPS * PAIR
    row = lambda b, i: (b, i, 0)
    const2 = lambda b, i: (0, 0)
    return pl.pallas_call(
        functools.partial(_att_project_kernel, n_lat=n_lat_tiles * ROW_TILE),
        grid=(B, N // PROJ_TILE),
        in_specs=[pl.BlockSpec((1, PROJ_TILE, D), row),
                  pl.BlockSpec((1, D), const2),
                  pl.BlockSpec((1, 2, 6, D), _MOD_BOTH),
                  pl.BlockSpec((D, n_in), const2, **_ONCE),
                  pl.BlockSpec((PROJ_TILE, PAIR), lambda b, i: (i, 0)),
                  pl.BlockSpec((PROJ_TILE, PAIR), lambda b, i: (i, 0)),
                  pl.BlockSpec(gmat.shape, const2),
                  pl.BlockSpec(qw_t.shape, const2),
                  pl.BlockSpec(kw_t.shape, const2)],
        out_specs=[pl.BlockSpec((1, PROJ_TILE, q_w), row),
                   pl.BlockSpec((1, PROJ_TILE, ATT_KV_W), row),
                   pl.BlockSpec((1, PROJ_TILE, ATT_KV_W), row)],
        out_shape=[jax.ShapeDtypeStruct((B, N, q_w), BF16),
                   jax.ShapeDtypeStruct((B, N, ATT_KV_W), BF16),
                   jax.ShapeDtypeStruct((B, N, ATT_KV_W), BF16)],
        compiler_params=_cparams(("arbitrary", "arbitrary")),
        name="att_project",
    )(X, nw.reshape(1, D), modc, w_bf16, cos_t, sin_t, gmat, qw_t, kw_t)


def _attention_kernel(x_ref, q_ref, k_ref, v_ref, mod_ref, lam_ref, dnw_ref, w_ref, o_ref,
                      s_scr, p_scr, y_scr, *, lambda_init, n_lat_tiles):
    n_lat = n_lat_tiles * ROW_TILE
    n_all = k_ref.shape[1]
    q_half = N_PAIR_GROUPS * PAIR
    lam = lam_ref[...]
    lam_val = (jnp.exp(jnp.sum(lam[0:1] * lam[1:2], axis=-1, keepdims=True))
               - jnp.exp(jnp.sum(lam[2:3] * lam[3:4], axis=-1, keepdims=True)) + lambda_init)

    def attend(k0, nk):
        keys = slice(k0, k0 + nk)

        def pair(pg):
            qv = jnp.concatenate([q_ref[0, :, h * q_half + pg * PAIR: h * q_half + (pg + 1) * PAIR]
                                  for h in range(2)], axis=0)
            kd = k_ref[0, keys, _kv_lane(pg): _kv_lane(pg) + PAIR]
            vd = v_ref[0, keys, _kv_lane(pg): _kv_lane(pg) + PAIR]
            s_scr[:, 0:nk] = lax.dot_general(qv, kd, (((1,), (1,)), ((), ())),
                                             preferred_element_type=F32)
            s = s_scr[:, 0:nk]
            p_scr[:, 0:nk] = jnp.exp2(s - jnp.max(s, axis=-1, keepdims=True)).astype(BF16)
            p = p_scr[:, 0:nk]
            o = jnp.dot(p, vd, preferred_element_type=F32)
            o = o * (1.0 / jnp.sum(p.astype(F32), axis=-1, keepdims=True))
            return o[0:ROW_TILE], o[ROW_TILE:2 * ROW_TILE]

        lo = lax.broadcasted_iota(jnp.int32, (ROW_TILE, PAIR), 1) < HEAD_DIM
        for pg in range(2 * GQA_KV_HEADS):
            o_even, o_odd = pair(pg)
            y_scr[:, pg * PAIR:(pg + 1) * PAIR] = jnp.where(lo, o_even, o_odd).astype(BF16)
        for pg in range(2 * GQA_KV_HEADS, N_PAIR_GROUPS):
            o_0, o_1 = pair(pg)
            od = o_0 - lam_val * o_1
            od = od * lax.rsqrt(jnp.mean(od * od, axis=-1, keepdims=True) + EPS)
            od = od * dnw_ref[...] * (1.0 - lambda_init)
            y_scr[:, pg * PAIR:(pg + 1) * PAIR] = od.astype(BF16)
        y = jnp.dot(y_scr[...], w_ref[...], preferred_element_type=F32)
        o_ref[0] = x_ref[0] + mod_ref[0, 0, 2:3, :] * y

    i = pl.program_id(1)

    @pl.when(i < n_lat_tiles)
    def _():
        attend(0, n_all)

    @pl.when(i >= n_lat_tiles)
    def _():
        attend(n_lat, n_all - n_lat)


def _kv_lane(pg):
    n_gqa_pairs = 2 * GQA_KV_HEADS
    if pg < n_gqa_pairs:
        return (pg // 2) * PAIR
    return (GQA_KV_HEADS + pg - n_gqa_pairs) * PAIR


def _attention(X, Q, K, V, modc, lam, dnw, w_bf16, lambda_init, n_lat_tiles):
    B, N, D = X.shape
    row = lambda b, i: (b, i, 0)
    whole = lambda b, i: (b, 0, 0)
    const2 = lambda b, i: (0, 0)
    return pl.pallas_call(
        functools.partial(_attention_kernel, lambda_init=lambda_init, n_lat_tiles=n_lat_tiles),
        grid=(B, N // ROW_TILE),
        in_specs=[pl.BlockSpec((1, ROW_TILE, D), row),
                  pl.BlockSpec((1, ROW_TILE, Q.shape[-1]), row),
                  pl.BlockSpec((1, N, K.shape[-1]), whole, pipeline_mode=pl.Buffered(1)),
                  pl.BlockSpec((1, N, V.shape[-1]), whole, pipeline_mode=pl.Buffered(1)),
                  pl.BlockSpec((1, 1, 6, D), lambda b, i: (b, i // n_lat_tiles, 0, 0)),
                  pl.BlockSpec(lam.shape, const2),
                  pl.BlockSpec((1, PAIR), const2),
                  pl.BlockSpec(w_bf16.shape, const2, pipeline_mode=pl.Buffered(1))],
        out_specs=pl.BlockSpec((1, ROW_TILE, D), row),
        out_shape=jax.ShapeDtypeStruct((B, N, D), F32),
        scratch_shapes=[pltpu.VMEM((2 * ROW_TILE, N), F32), pltpu.VMEM((2 * ROW_TILE, N), BF16),
                        pltpu.VMEM((ROW_TILE, w_bf16.shape[0]), BF16)],
        compiler_params=_cparams(("arbitrary", "arbitrary")),
        name="attention",
    )(X, Q, K, V, modc, lam, dnw.reshape(1, PAIR), w_bf16)


REC_IN_PAD = 3200
N_SSD_DT = 2 * SSD_HEADS
RET_LANE0 = N_SSD_DT


def _softplus(x):
    return jnp.maximum(x, 0.0) + jnp.log(1.0 + jnp.exp(-jnp.abs(x)))


def _silu(x):
    return x * jax.nn.sigmoid(x)


def _rec_project_kernel(x_ref, nw_ref, mod_ref, w_ref, cos_ref, sin_ref, dtb_ref,
                        rq_ref, rk_ref, rv_ref, g_ref, xbc_ref, dt_ref, *, n_lat):
    h = _modulated(x_ref[0], nw_ref[...], mod_ref, 0, 1, n_lat)
    z = jnp.dot(h.astype(BF16), w_ref[...], preferred_element_type=F32)
    o_rq, o_rk, o_rv, o_rg, o_z, o_xbc, o_dt = np.cumsum((0,) + REC_SPLITS[:-1]).tolist()
    n_qk = REC_SPLITS[0]
    cos, sin = cos_ref[...], sin_ref[...]
    cos2 = jnp.concatenate([cos] * (n_qk // PAIR), axis=1)
    sin2 = jnp.concatenate([sin] * (n_qk // PAIR), axis=1)
    rq = _rope_lanes(z[:, o_rq:o_rq + n_qk], cos2, sin2)
    lo = (lax.broadcasted_iota(jnp.int32, rq.shape, 1) & (PAIR - 1)) < HEAD_DIM
    rq_ref[0, :, 0:n_qk] = jnp.where(lo, rq, 0.0).astype(BF16)
    rq_ref[0, :, n_qk:2 * n_qk] = jnp.where(lo, 0.0, rq).astype(BF16)
    rk_ref[0] = _rope_lanes(z[:, o_rk:o_rk + n_qk] * (RET_QK_DIM ** -0.5), cos2, sin2).astype(BF16)
    rv_ref[0] = z[:, o_rv:o_rg].astype(BF16)
    g_ref[0] = z[:, o_rg:o_xbc].astype(BF16)
    xbc_ref[0] = z[:, o_xbc:o_dt].astype(BF16)
    dt_ref[0] = _softplus(z[:, o_dt:o_dt + PAIR] + dtb_ref[...])


def _rec_project(X, nw, modc, w_bf16, cos_t, sin_t, dtb_row, n_lat_tiles):
    B, N, D = X.shape
    n_in = w_bf16.shape[1]
    row = lambda b, i: (b, i, 0)
    const2 = lambda b, i: (0, 0)
    widths = (2 * REC_SPLITS[0], REC_SPLITS[1], REC_SPLITS[2], REC_SPLITS[3] + REC_SPLITS[4],
              REC_SPLITS[5], PAIR)
    dtypes = (BF16, BF16, BF16, BF16, BF16, F32)
    return pl.pallas_call(
        functools.partial(_rec_project_kernel, n_lat=n_lat_tiles * ROW_TILE),
        grid=(B, N // PROJ_TILE),
        in_specs=[pl.BlockSpec((1, PROJ_TILE, D), row),
                  pl.BlockSpec((1, D), const2),
                  pl.BlockSpec((1, 2, 6, D), _MOD_BOTH),
                  pl.BlockSpec((D, n_in), const2, **_ONCE),
                  pl.BlockSpec((PROJ_TILE, PAIR), lambda b, i: (i, 0)),
                  pl.BlockSpec((PROJ_TILE, PAIR), lambda b, i: (i, 0)),
                  pl.BlockSpec((1, PAIR), const2)],
        out_specs=[pl.BlockSpec((1, PROJ_TILE, w), row) for w in widths],
        out_shape=[jax.ShapeDtypeStruct((B, N, w), dt) for w, dt in zip(widths, dtypes)],
        compiler_params=_cparams(("arbitrary", "arbitrary")),
        name="rec_project",
    )(X, nw.reshape(1, D), modc, w_bf16, cos_t, sin_t, dtb_row)


def _conv_kernel(x_ref, w_ref, b_ref, o_ref, *, n_lat):
    x = x_ref[0].astype(F32)
    n = x.shape[0]
    t = lax.broadcasted_iota(jnp.int32, x.shape, 0)
    first = (t == 0) | (t == n_lat)
    last = (t == n_lat - 1) | (t == n - 1)
    prev = jnp.where(first, 0.0, pltpu.roll(x, 1, 0))
    nxt = jnp.where(last, 0.0, pltpu.roll(x, n - 1, 0))
    y = prev * w_ref[0:1, :] + x * w_ref[1:2, :] + nxt * w_ref[2:3, :] + b_ref[...]
    o_ref[0] = _silu(y).astype(BF16)


def _ssd_conv(XBC, conv_w, conv_b, n_lat):
    B, N, C = XBC.shape
    tc = 256
    return pl.pallas_call(
        functools.partial(_conv_kernel, n_lat=n_lat),
        grid=(B, C // tc),
        in_specs=[pl.BlockSpec((1, N, tc), lambda b, j: (b, 0, j)),
                  pl.BlockSpec((conv_w.shape[0], tc), lambda b, j: (0, j)),
                  pl.BlockSpec((1, tc), lambda b, j: (0, j))],
        out_specs=pl.BlockSpec((1, N, tc), lambda b, j: (b, 0, j)),
        out_shape=jax.ShapeDtypeStruct((B, N, C), BF16),
        compiler_params=_cparams(("arbitrary", "arbitrary")),
        name="ssd_conv",
    )(XBC, conv_w, conv_b.reshape(1, C))


def _split3(x):
    hi = x.astype(BF16)
    r1 = x - hi.astype(F32)
    mid = r1.astype(BF16)
    lo = (r1 - mid.astype(F32)).astype(BF16)
    return hi, mid, lo


def _scan_kernel(rq_ref, rk_ref, rv_ref, a_ref, dt_ref, alog_ref, logit_ref, y_ref, hret, hssd, *, reverse):
    T = rq_ref.shape[1]
    d = 1 if reverse else 0
    step = pl.program_id(1)

    @pl.when(step == 0)
    def _():
        hret[...] = jnp.zeros_like(hret)
        hssd[...] = jnp.zeros_like(hssd)

    lane = lax.broadcasted_iota(jnp.int32, (1, PAIR), 1)
    a_neg = jnp.where(lane < N_SSD_DT, -jnp.exp(alog_ref[...]), 0.0)
    logit = logit_ref[...]
    log_gamma = jnp.where((lane >= RET_LANE0) & (lane < RET_LANE0 + 2 * RET_HEADS),
                          jnp.minimum(logit, 0.0) - jnp.log(1.0 + jnp.exp(-jnp.abs(logit))), 0.0)
    dt = dt_ref[0]
    la = dt * a_neg + log_gamma

    ti = lax.broadcasted_iota(jnp.int32, (T, T), 0)
    tj = lax.broadcasted_iota(jnp.int32, (T, T), 1)
    causal = (tj >= ti) if reverse else (tj <= ti)
    tri = jnp.where(causal, 1.0, 0.0).astype(BF16)
    hi, mid, lo3 = _split3(la)
    P = (jnp.dot(tri, hi, preferred_element_type=F32) + jnp.dot(tri, mid, preferred_element_type=F32)
         + jnp.dot(tri, lo3, preferred_element_type=F32))
    PT = P.T
    tot = P[0:1, :] if reverse else P[T - 1:T, :]
    E = jnp.exp(P)
    KD = jnp.exp(tot - P)
    ET = jnp.exp(tot)
    lo = lax.broadcasted_iota(jnp.int32, (T, PAIR), 1) < HEAD_DIM
    lo_row = lane < HEAD_DIM
    nt = (((1,), (1,)), ((), ()))
    tn = (((0,), (0,)), ((), ()))

    def decay_matrix(c):
        diff = jnp.minimum(P[:, c:c + 1] - PT[c:c + 1, :], 0.0)
        return jnp.where(causal, jnp.exp(diff), 0.0)

    n_qk = RET_HEADS * RET_QK_DIM
    for h in range(RET_HEADS):
        c = RET_LANE0 + d * RET_HEADS + h
        p, half = h // 2, h % 2
        qv = rq_ref[0, :, half * n_qk + p * PAIR: half * n_qk + (p + 1) * PAIR]
        kp = rk_ref[0, :, p * PAIR:(p + 1) * PAIR]
        v = rv_ref[0, :, h * RET_V_DIM:(h + 1) * RET_V_DIM]
        s = lax.dot_general(qv, kp, nt, preferred_element_type=F32)
        aw = (s * decay_matrix(c)).astype(BF16)
        y = jnp.dot(aw, v, preferred_element_type=F32)
        y = y + E[:, c:c + 1] * jnp.dot(qv, hret[h].astype(BF16), preferred_element_type=F32)
        y_ref[0, :, h * RET_V_DIM:(h + 1) * RET_V_DIM] = y.astype(y_ref.dtype)
        kdec = (kp.astype(F32) * KD[:, c:c + 1]).astype(BF16)
        hret[h] = hret[h] * ET[:, c:c + 1] + lax.dot_general(kdec, v, tn, preferred_element_type=F32)

    y0 = RET_HEADS * RET_V_DIM
    r_per_g = SSD_HEADS // SSD_GROUPS
    o_b = SSD_INNER
    o_c = SSD_INNER + SSD_GROUPS * SSD_STATE
    for g in range(SSD_GROUPS):
        cg = a_ref[0, :, o_c + g * SSD_STATE: o_c + (g + 1) * SSD_STATE]
        bg = a_ref[0, :, o_b + g * SSD_STATE: o_b + (g + 1) * SSD_STATE]
        s = lax.dot_general(cg, bg, nt, preferred_element_type=F32)
        ch = jnp.dot(cg, hssd[g].astype(BF16), preferred_element_type=F32)
        vdec, dec_rows = [], []
        for p in range(r_per_g // 2):
            pg = g * (r_per_g // 2) + p
            xs = a_ref[0, :, pg * PAIR:(pg + 1) * PAIR].astype(F32)
            ys, vds, cols = [], [], []
            for half in range(2):
                c = d * SSD_HEADS + 2 * pg + half
                aw = (s * decay_matrix(c)).astype(BF16)
                vh = xs * dt[:, c:c + 1]
                ys.append(jnp.dot(aw, vh.astype(BF16), preferred_element_type=F32))
                vds.append(vh * KD[:, c:c + 1])
                cols.append(c)
            e_pair = jnp.where(lo, E[:, cols[0]:cols[0] + 1], E[:, cols[1]:cols[1] + 1])
            y_ref[0, :, y0 + pg * PAIR: y0 + (pg + 1) * PAIR] = (
                jnp.where(lo, ys[0], ys[1]) + e_pair * ch[:, p * PAIR:(p + 1) * PAIR]).astype(y_ref.dtype)
            vdec.append(jnp.where(lo, vds[0], vds[1]).astype(BF16))
            dec_rows.append(jnp.where(lo_row, ET[:, cols[0]:cols[0] + 1], ET[:, cols[1]:cols[1] + 1]))
        hssd[g] = (hssd[g] * jnp.concatenate(dec_rows, axis=1)
                   + lax.dot_general(bg, jnp.concatenate(vdec, axis=1), tn, preferred_element_type=F32))


def _bidir_scan(RQ, RK, RV, A, DT, alog_row, logit_row, n_lat_tiles, reverse):
    B, N, _ = RQ.shape
    n_tiles = N // ROW_TILE

    def tile(i):
        lat = (n_lat_tiles - i) if reverse else (i - 1)
        return jnp.where(i == 0, n_tiles - 1, lat)

    row = lambda b, i: (b, tile(i), 0)
    const2 = lambda b, i: (0, 0)
    y_w = RET_HEADS * RET_V_DIM + SSD_INNER
    return pl.pallas_call(
        functools.partial(_scan_kernel, reverse=reverse),
        grid=(B, n_tiles),
        in_specs=[pl.BlockSpec((1, ROW_TILE, RQ.shape[-1]), row),
                  pl.BlockSpec((1, ROW_TILE, RK.shape[-1]), row),
                  pl.BlockSpec((1, ROW_TILE, RV.shape[-1]), row),
                  pl.BlockSpec((1, ROW_TILE, A.shape[-1]), row),
                  pl.BlockSpec((1, ROW_TILE, PAIR), row),
                  pl.BlockSpec((1, PAIR), const2),
                  pl.BlockSpec((1, PAIR), const2)],
        out_specs=pl.BlockSpec((1, ROW_TILE, y_w), row),
        out_shape=jax.ShapeDtypeStruct((B, N, y_w), BF16),
        scratch_shapes=[pltpu.VMEM((RET_HEADS, PAIR, RET_V_DIM), F32),
                        pltpu.VMEM((SSD_GROUPS, SSD_STATE, SSD_INNER // SSD_GROUPS), F32)],
        compiler_params=_cparams(("arbitrary", "arbitrary")),
        name="scan_bwd" if reverse else "scan_fwd",
    )(RQ, RK, RV, A, DT, alog_row, logit_row)


def _rec_combine_kernel(x_ref, yf_ref, yb_ref, g_ref, a_ref, mod_ref, dskip_ref, snw_ref, w_ref, o_ref, y_scr,
                        *, n_lat):
    n_ret = RET_HEADS * RET_V_DIM
    y = yf_ref[0].astype(F32) + yb_ref[0].astype(F32)
    gates = g_ref[0].astype(F32)
    for h in range(RET_HEADS):
        sl = slice(h * RET_V_DIM, (h + 1) * RET_V_DIM)
        r = y[:, sl]
        r = r * lax.rsqrt(jnp.mean(r * r, axis=-1, keepdims=True) + EPS)
        y_scr[:, sl] = (r * _silu(gates[:, sl])).astype(BF16)
    gw = SSD_INNER // SSD_GROUPS
    for g in range(SSD_GROUPS):
        sl = slice(g * gw, (g + 1) * gw)
        s = y[:, n_ret + g * gw: n_ret + (g + 1) * gw] + dskip_ref[:, sl] * a_ref[0, :, sl].astype(F32)
        s = s * _silu(gates[:, n_ret + g * gw: n_ret + (g + 1) * gw])
        s = s * lax.rsqrt(jnp.mean(s * s, axis=-1, keepdims=True) + EPS) * snw_ref[:, sl]
        y_scr[:, n_ret + g * gw: n_ret + (g + 1) * gw] = s.astype(BF16)
    gate = _row_mod(mod_ref, 2, x_ref.shape[1], n_lat)
    o_ref[0] = x_ref[0] + gate * jnp.dot(y_scr[...], w_ref[...], preferred_element_type=F32)


def _rec_combine(X, Yf, Yb, G, A, modc, dskip_row, snw_row, w_bf16, n_lat_tiles):
    B, N, D = X.shape
    row = lambda b, i: (b, i, 0)
    const2 = lambda b, i: (0, 0)
    return pl.pallas_call(
        functools.partial(_rec_combine_kernel, n_lat=n_lat_tiles * ROW_TILE),
        grid=(B, N // PROJ_TILE),
        in_specs=[pl.BlockSpec((1, PROJ_TILE, D), row),
                  pl.BlockSpec((1, PROJ_TILE, Yf.shape[-1]), row),
                  pl.BlockSpec((1, PROJ_TILE, Yb.shape[-1]), row),
                  pl.BlockSpec((1, PROJ_TILE, G.shape[-1]), row),
                  pl.BlockSpec((1, PROJ_TILE, SSD_INNER), row),
                  pl.BlockSpec((1, 2, 6, D), _MOD_BOTH),
                  pl.BlockSpec((1, SSD_INNER), const2),
                  pl.BlockSpec((1, SSD_INNER), const2),
                  pl.BlockSpec(w_bf16.shape, const2, **_ONCE)],
        out_specs=pl.BlockSpec((1, PROJ_TILE, D), row),
        out_shape=jax.ShapeDtypeStruct((B, N, D), F32),
        scratch_shapes=[pltpu.VMEM((PROJ_TILE, w_bf16.shape[0]), BF16)],
        compiler_params=_cparams(("arbitrary", "arbitrary")),
        name="rec_combine",
    )(X, Yf, Yb, G, A, modc, dskip_row, snw_row, w_bf16)


def _recurrent_layer(X, nw, modc, w_in, w_out, cos_t, sin_t, ret_decay_logit, conv_w, conv_b, dt_bias,
                     a_log, d_skip, ssd_norm_w, n_lat_tiles):
    D = X.shape[-1]
    w_pad = jnp.pad(w_in, ((0, 0), (0, REC_IN_PAD - w_in.shape[1]))).astype(BF16)
    pad_row = lambda v, at: jnp.pad(v.reshape(1, -1), ((0, 0), (at, PAIR - at - v.size)))
    RQ, RK, RV, G, XBC, DT = _rec_project(X, nw, modc, w_pad, cos_t, sin_t, pad_row(dt_bias, 0), n_lat_tiles)
    A = _ssd_conv(XBC, conv_w, conv_b, n_lat_tiles * ROW_TILE)
    alog_row, logit_row = pad_row(a_log, 0), pad_row(ret_decay_logit, RET_LANE0)
    Yf = _bidir_scan(RQ, RK, RV, A, DT, alog_row, logit_row, n_lat_tiles, False)
    Yb = _bidir_scan(RQ, RK, RV, A, DT, alog_row, logit_row, n_lat_tiles, True)
    return _rec_combine(X, Yf, Yb, G, A, modc, jnp.repeat(d_skip, SSD_HEAD_DIM).reshape(1, -1),
                        ssd_norm_w.reshape(1, -1), w_out.astype(BF16), n_lat_tiles)


SEL_CHUNK = 256
COMBINE_TILE = 768
FFN_ROW_CHUNK = 1024
GATHER_EXPERTS = 8


def _router_kernel(x_ref, nw_ref, mod_ref, rw_ref, h_ref, aff_ref, *, n_lat):
    h = _modulated(x_ref[0], nw_ref[...], mod_ref, 3, 4, n_lat)
    h_hi = h.astype(BF16)
    h_ref[0] = h_hi
    h_lo = (h - h_hi.astype(F32)).astype(BF16)
    w = rw_ref[...]
    w_hi = w.astype(BF16)
    w_lo = (w - w_hi.astype(F32)).astype(BF16)
    logits = (jnp.dot(h_hi, w_hi, preferred_element_type=F32) + jnp.dot(h_lo, w_hi, preferred_element_type=F32)
              + jnp.dot(h_hi, w_lo, preferred_element_type=F32))
    logits = logits.T[0:aff_ref.shape[1], :]
    e = jnp.exp(logits - jnp.max(logits, axis=0, keepdims=True))
    aff_ref[0] = e / jnp.sum(e, axis=0, keepdims=True)


def _mod_router(X, nw, modc, router_w, n_lat, n_rows):
    B, _, D = X.shape
    E = router_w.shape[1]
    rw_pad = jnp.pad(router_w, ((0, 0), (0, PAIR - E)))
    tile = _row_tile(n_rows, PROJ_TILE)
    return pl.pallas_call(
        functools.partial(_router_kernel, n_lat=n_lat),
        grid=(B, n_rows // tile),
        in_specs=[pl.BlockSpec((1, tile, D), lambda b, i: (b, i, 0)),
                  pl.BlockSpec((1, D), lambda b, i: (0, 0)),
                  pl.BlockSpec((1, 2, 6, D), _MOD_BOTH),
                  pl.BlockSpec((D, PAIR), lambda b, i: (0, 0))],
        out_specs=[pl.BlockSpec((1, tile, D), lambda b, i: (b, i, 0)),
                   pl.BlockSpec((1, E, tile), lambda b, i: (b, 0, i))],
        out_shape=[jax.ShapeDtypeStruct((B, n_rows, D), BF16),
                   jax.ShapeDtypeStruct((B, E, n_rows), F32)],
        compiler_params=_cparams(("arbitrary", "arbitrary")),
        name="mod_router",
    )(X, nw.reshape(1, D), modc, rw_pad)


def _count(mask_f32):
    return jnp.sum(mask_f32, axis=-1, keepdims=True)


def _select_top(a, cap):
    E, n = a.shape
    v = pltpu.bitcast(a, jnp.int32)
    thr = jnp.zeros((E, 1), jnp.int32)
    for bit in range(30, -1, -1):
        cand = thr | (1 << bit)
        thr = jnp.where(_count(jnp.where(v >= cand, 1.0, 0.0)) >= cap, cand, thr)
    gt = v > thr
    eq = v == thr
    need = cap - _count(jnp.where(gt, 1.0, 0.0))
    idx = lax.broadcasted_iota(jnp.int32, (E, n), 1)
    last = jnp.zeros((E, 1), jnp.int32)
    for bit in range(n.bit_length() - 2, -1, -1):
        cand = last | (1 << bit)
        below = _count(jnp.where(eq, jnp.where(idx < cand, 1.0, 0.0), 0.0))
        last = jnp.where(below < need, cand, last)
    sel = jnp.where(gt, 1.0, jnp.where(eq, jnp.where(idx <= last, 1.0, 0.0), 0.0))
    si = lax.broadcasted_iota(jnp.int32, (SEL_CHUNK, SEL_CHUNK), 0)
    sj = lax.broadcasted_iota(jnp.int32, (SEL_CHUNK, SEL_CHUNK), 1)
    before = jnp.where(si < sj, 1.0, 0.0).astype(BF16)
    base = jnp.zeros((E, 1), F32)
    pos = []
    for k in range(n // SEL_CHUNK):
        sk = sel[:, k * SEL_CHUNK:(k + 1) * SEL_CHUNK]
        pos.append(jnp.dot(sk.astype(BF16), before, preferred_element_type=F32) + base)
        base = base + _count(sk)
    pos = jnp.concatenate(pos, axis=1) if len(pos) > 1 else pos[0]
    return jnp.where(sel > 0.0, pos, -1.0), jnp.where(sel > 0.0, a, 0.0)


def _token_sets(n_lat, n_rows):
    sets = [(0, n_lat, EC_CAPACITY * n_lat // N_EXPERTS)]
    if n_rows > n_lat:
        sets.append((n_lat, n_rows, EC_CAPACITY * (n_rows - n_lat) // N_EXPERTS))
    return sets


def _select_kernel(aff_ref, slot_ref, slot_t_ref, gate_t_ref, *, sets):
    E = aff_ref.shape[1]
    pad_rows = PAIR - E
    for lo, hi, cap in sets:
        slot, gate = _select_top(aff_ref[0, :, lo:hi], cap)
        slot_ref[0, :, lo:hi] = slot
        slot_t_ref[0, lo:hi, :] = jnp.concatenate([slot, jnp.full((pad_rows, hi - lo), -1.0, F32)], axis=0).T
        gate_t_ref[0, lo:hi, :] = jnp.concatenate([gate, jnp.zeros((pad_rows, hi - lo), F32)], axis=0).T


def _ec_select(aff, sets):
    B, E, N = aff.shape
    return pl.pallas_call(
        functools.partial(_select_kernel, sets=sets),
        grid=(B,),
        in_specs=[pl.BlockSpec((1, E, N), lambda b: (b, 0, 0))],
        out_specs=[pl.BlockSpec((1, E, N), lambda b: (b, 0, 0)),
                   pl.BlockSpec((1, N, PAIR), lambda b: (b, 0, 0)),
                   pl.BlockSpec((1, N, PAIR), lambda b: (b, 0, 0))],
        out_shape=[jax.ShapeDtypeStruct((B, E, N), F32),
                   jax.ShapeDtypeStruct((B, N, PAIR), F32),
                   jax.ShapeDtypeStruct((B, N, PAIR), F32)],
        compiler_params=_cparams(("arbitrary",)),
        name="ec_select",
    )(aff)


def _gather_kernel(slot_ref, h_ref, *x_refs, sets):
    n_e = x_refs[0].shape[0]
    e0 = pl.multiple_of(pl.program_id(1) * n_e, n_e)
    for (lo, hi, cap), x_ref in zip(sets, x_refs):
        c = lax.broadcasted_iota(jnp.int32, (cap, hi - lo), 0).astype(F32)
        onehot = jnp.concatenate(
            [jnp.where(slot_ref[0, pl.ds(e0 + j, 1), lo:hi] == c, 1.0, 0.0).astype(BF16)
             for j in range(n_e)], axis=0)
        x = jnp.dot(onehot, h_ref[0, lo:hi, :], preferred_element_type=F32)
        for j in range(n_e):
            x_ref[j] = x[j * cap:(j + 1) * cap, :].astype(BF16)


def _ec_gather(slot, H, sets):
    B, E, N = slot.shape
    D = H.shape[-1]
    return pl.pallas_call(
        functools.partial(_gather_kernel, sets=sets),
        grid=(B, E // GATHER_EXPERTS),
        in_specs=[pl.BlockSpec((1, E, N), lambda b, e: (b, 0, 0)),
                  pl.BlockSpec((1, N, D), lambda b, e: (b, 0, 0))],
        out_specs=[pl.BlockSpec((GATHER_EXPERTS, cap, D), lambda b, e: (e, b, 0)) for _, _, cap in sets],
        out_shape=[jax.ShapeDtypeStruct((E, B * cap, D), BF16) for _, _, cap in sets],
        compiler_params=_cparams(("arbitrary", "arbitrary")),
        name="ec_gather",
    )(slot, H)


def _ffn_kernel(*refs, n_groups, row_chunk):
    x_refs = refs[:n_groups]
    wg_ref, wu_ref, wd_ref = refs[n_groups:n_groups + 3]
    o_refs = refs[n_groups + 3:2 * n_groups + 3]
    acc_refs = refs[2 * n_groups + 3:3 * n_groups + 3]
    wg_s, wu_s, wd_s = refs[3 * n_groups + 3:]
    f = pl.program_id(2)
    n_f = pl.num_programs(2)
    wg_s[...] = wg_ref[0, 0].astype(BF16)
    wu_s[...] = wu_ref[0, 0].astype(BF16)
    wd_s[...] = wd_ref[0, 0].astype(BF16)

    @pl.when(f == 0)
    def _():
        for acc_ref in acc_refs:
            acc_ref[...] = jnp.zeros_like(acc_ref)

    def chunk(x_ref, acc_ref, start, size):
        rows = pl.ds(start, size)
        xr = x_ref[0, rows, :]
        a = jnp.dot(xr, wg_s[...], preferred_element_type=F32)
        u = jnp.dot(xr, wu_s[...], preferred_element_type=F32)
        hm = (a * jax.nn.sigmoid(a) * u).astype(BF16)
        acc_ref[rows, :] += jnp.dot(hm, wd_s[...], preferred_element_type=F32)

    for x_ref, acc_ref in zip(x_refs, acc_refs):
        n_rows = x_ref.shape[1]
        for start in range(0, n_rows, row_chunk):
            chunk(x_ref, acc_ref, start, min(row_chunk, n_rows - start))

    @pl.when(f == n_f - 1)
    def _():
        for o_ref, acc_ref in zip(o_refs, acc_refs):
            o_ref[0] = acc_ref[...].astype(o_ref.dtype)


def _expert_ffn(xs, w_gate, w_up, w_down, layer):
    E, _, D = xs[0].shape
    F = w_gate.shape[-1]
    m_tiles = 2
    tms = [x.shape[1] // m_tiles for x in xs]
    n = len(xs)
    return pl.pallas_call(
        functools.partial(_ffn_kernel, n_groups=n, row_chunk=FFN_ROW_CHUNK),
        grid=(E, m_tiles, F // FF_TILE),
        in_specs=[pl.BlockSpec((1, tm, D), lambda e, m, f: (e, m, 0)) for tm in tms]
        + [pl.BlockSpec((1, 1, D, FF_TILE), lambda e, m, f: (layer, e, 0, f)),
           pl.BlockSpec((1, 1, D, FF_TILE), lambda e, m, f: (layer, e, 0, f)),
           pl.BlockSpec((1, 1, FF_TILE, D), lambda e, m, f: (layer, e, f, 0))],
        out_specs=[pl.BlockSpec((1, tm, D), lambda e, m, f: (e, m, 0)) for tm in tms],
        out_shape=[jax.ShapeDtypeStruct(x.shape, BF16) for x in xs],
        scratch_shapes=[pltpu.VMEM((tm, D), F32) for tm in tms]
        + [pltpu.VMEM((D, FF_TILE), BF16), pltpu.VMEM((D, FF_TILE), BF16), pltpu.VMEM((FF_TILE, D), BF16)],
        compiler_params=_cparams(("arbitrary", "arbitrary", "arbitrary")),
        name="expert_ffn",
    )(*xs, w_gate, w_up, w_down)


def _combine_kernel(x_ref, slot_t_ref, gate_t_ref, mod_ref, fnw_ref, *refs, n_lat, n_rows, final_norm):
    y_refs, o_ref = refs[:-1], refs[-1]
    E = y_refs[0].shape[0]
    tile = x_ref.shape[1]

    def scatter(rows, seg):
        y_ref = y_refs[seg]
        n = rows.stop - rows.start
        cap = y_ref.shape[1]
        c = lax.broadcasted_iota(jnp.int32, (n, cap), 1).astype(F32)
        acc = jnp.zeros((n, x_ref.shape[2]), F32)
        for e in range(E):
            onehot = jnp.where(slot_t_ref[0, rows, e:e + 1] == c, 1.0, 0.0).astype(BF16)
            acc = acc + gate_t_ref[0, rows, e:e + 1] * jnp.dot(onehot, y_ref[e], preferred_element_type=F32)
        out = x_ref[0, rows, :] + mod_ref[0, seg, 5:6, :] * acc
        if final_norm:
            out = out * lax.rsqrt(jnp.mean(out * out, axis=-1, keepdims=True) + EPS) * fnw_ref[...]
        o_ref[0, rows, :] = out

    splits = {}
    for j in range(n_rows // tile):
        splits.setdefault(min(max(n_lat - j * tile, 0), tile), []).append(j)
    i = pl.program_id(1)
    for lat_rows, tiles in splits.items():
        @pl.when(functools.reduce(jnp.logical_or, [i == j for j in tiles]))
        def _(lat_rows=lat_rows):
            if lat_rows > 0:
                scatter(slice(0, lat_rows), 0)
            if lat_rows < tile:
                scatter(slice(lat_rows, tile), 1)


def _row_tile(n_rows, preferred):
    return preferred if n_rows % preferred == 0 else 2 * ROW_TILE


def _ec_combine(X, slot_t, gate_t, Ys, modc, sets, n_lat, final_norm_w):
    B, _, D = X.shape
    E = Ys[0].shape[0]
    n_rows = sets[-1][1]
    tile = _row_tile(n_rows, COMBINE_TILE)
    row = lambda b, i: (b, i, 0)
    final_norm = final_norm_w is not None
    fnw = final_norm_w.reshape(1, D) if final_norm else jnp.ones((1, D), F32)
    return pl.pallas_call(
        functools.partial(_combine_kernel, n_lat=n_lat, n_rows=n_rows, final_norm=final_norm),
        grid=(B, n_rows // tile),
        in_specs=[pl.BlockSpec((1, tile, D), row),
                  pl.BlockSpec((1, tile, PAIR), row),
                  pl.BlockSpec((1, tile, PAIR), row),
                  pl.BlockSpec((1, 2, 6, D), _MOD_BOTH),
                  pl.BlockSpec((1, D), lambda b, i: (0, 0))]
        + [pl.BlockSpec((E, cap, D), lambda b, i: (0, b, 0)) for _, _, cap in sets],
        out_specs=pl.BlockSpec((1, tile, D), row),
        out_shape=jax.ShapeDtypeStruct((B, n_rows, D), F32),
        compiler_params=_cparams(("arbitrary", "arbitrary")),
        name="ec_combine",
    )(X, slot_t, gate_t, modc, fnw, *Ys)


def _moe_layer(X, nw, modc, router_w, w_gate, w_up, w_down, layer, n_lat, n_rows, final_norm_w=None):
    sets = _token_sets(n_lat, n_rows)
    H, aff = _mod_router(X, nw, modc, router_w, n_lat, n_rows)
    slot, slot_t, gate_t = _ec_select(aff, sets)
    Xs = _ec_gather(slot, H, sets)
    Ys = _expert_ffn(Xs, w_gate, w_up, w_down, layer)
    return _ec_combine(X, slot_t, gate_t, Ys, modc, sets, n_lat, final_norm_w)


def _rope_tables(n):
    t = jnp.arange(n)
    row = (t // GRID_W).astype(F32)
    col = (t % GRID_W).astype(F32)
    n_freq = HEAD_DIM // 4
    inv = ROPE_THETA ** (-jnp.arange(n_freq, dtype=F32) / n_freq)
    ang = jnp.concatenate([row[:, None] * inv, col[:, None] * inv], axis=-1)
    return jnp.cos(ang), jnp.sin(ang)


def kernel(x, c, ctx, c_ctx, ada_w, ada_b, norm1_w, norm2_w, att_w_in, att_w_out, att_q_norm_w,
           att_k_norm_w, diff_lambda, diff_norm_w, rec_w_in, rec_w_out, ret_decay_logit, ssd_conv_w,
           ssd_conv_b, ssd_dt_bias, ssd_a_log, ssd_d_skip, ssd_norm_w, router_w, expert_w_gate,
           expert_w_up, expert_w_down, final_norm_w):
    B, n_lat, D = x.shape
    n_ctx = ctx.shape[1]
    n_lat_tiles = n_lat // ROW_TILE
    cos, sin = _rope_tables(n_lat)
    reps = PAIR // cos.shape[1]
    cos_t = jnp.concatenate([jnp.tile(cos, (1, reps)), jnp.ones((n_ctx, PAIR), F32)], axis=0)
    sin_t = jnp.concatenate([jnp.tile(jnp.concatenate([-sin, sin], axis=1), (1, reps // 2)),
                             jnp.zeros((n_ctx, PAIR), F32)], axis=0)
    gmat = jnp.kron(jnp.eye(GQA_HEADS, dtype=F32), jnp.ones((HEAD_DIM, HEAD_DIM), F32)).astype(BF16)

    n_cond = 24
    cvec = jnp.concatenate([c, c_ctx[None, :], jnp.zeros((n_cond - B - 1, D), F32)], axis=0)
    mods = _ada_modulation(cvec, ada_w, ada_b)

    X = jnp.concatenate([x, ctx], axis=1)
    for layer in range(DEPTH):
        i = layer // 2
        need_ctx = layer < DEPTH - 1
        mod_lat = mods[layer, :B].reshape(B, 1, 6, D)
        mod_ctx = jnp.broadcast_to(mods[layer, B].reshape(1, 1, 6, D), (B, 1, 6, D))
        modc = jnp.concatenate([mod_lat, mod_ctx], axis=1)

        if layer % 2 == 0:
            lambda_init = 0.8 - 0.6 * math.exp(-0.3 * layer)
            Q, K, V = _att_project(X, norm1_w[layer], modc, att_w_in[i].astype(BF16), cos_t, sin_t, gmat,
                                   jnp.tile(att_q_norm_w[i], GQA_HEADS).reshape(1, -1),
                                   jnp.tile(att_k_norm_w[i], GQA_KV_HEADS).reshape(1, -1), n_lat_tiles)
            X = _attention(X, Q, K, V, modc, diff_lambda[i], diff_norm_w[i], att_w_out[i].astype(BF16),
                           lambda_init, n_lat_tiles)
        else:
            X = _recurrent_layer(X, norm1_w[layer], modc, rec_w_in[i], rec_w_out[i], cos_t, sin_t,
                                 ret_decay_logit[i], ssd_conv_w[i], ssd_conv_b[i], ssd_dt_bias[i],
                                 ssd_a_log[i], ssd_d_skip[i], ssd_norm_w[i], n_lat_tiles)

        X = _moe_layer(X, norm2_w[layer], modc, router_w[layer], expert_w_gate, expert_w_up, expert_w_down,
                       layer, n_lat, n_lat + n_ctx if need_ctx else n_lat,
                       None if need_ctx else final_norm_w)

    return X
```

```python
import functools
import math

import jax
import jax.numpy as jnp
import numpy as np
from jax import lax
from jax.experimental import pallas as pl
from jax.experimental.pallas import tpu as pltpu

D_MODEL = 1024
DEPTH = 4
GRID_W = 64
HEAD_DIM = 64
ROPE_THETA = 10000.0
EPS = 1e-6
HALF_W = D_MODEL // 2
GQA_HEADS = HALF_W // HEAD_DIM
GQA_KV_HEADS = GQA_HEADS // 4
DIFF_HEADS = HALF_W // (2 * HEAD_DIM)
RET_HEADS = HALF_W // (2 * HEAD_DIM)
RET_QK_DIM = HEAD_DIM
RET_V_DIM = 2 * HEAD_DIM
SSD_HEAD_DIM = HEAD_DIM
SSD_HEADS = HALF_W // SSD_HEAD_DIM
SSD_GROUPS = 2
SSD_STATE = 128
SSD_INNER = SSD_HEADS * SSD_HEAD_DIM
SSD_XBC = SSD_INNER + 2 * SSD_GROUPS * SSD_STATE
N_EXPERTS = 16
EC_CAPACITY = 2
EXPERT_FF = ((8 * D_MODEL // 3 + 255) // 256) * 256

ATT_SPLITS = (GQA_HEADS * HEAD_DIM, GQA_KV_HEADS * HEAD_DIM, GQA_KV_HEADS * HEAD_DIM,
              DIFF_HEADS * 2 * HEAD_DIM, DIFF_HEADS * 2 * HEAD_DIM, DIFF_HEADS * 2 * HEAD_DIM)
REC_SPLITS = (RET_HEADS * RET_QK_DIM, RET_HEADS * RET_QK_DIM, RET_HEADS * RET_V_DIM,
              RET_HEADS * RET_V_DIM, SSD_INNER, SSD_XBC, 2 * SSD_HEADS)

ROW_TILE = 256
PROJ_TILE = 768
FF_TILE = 256
VMEM_LIMIT = 56 * 1024 * 1024
BF16 = jnp.bfloat16
F32 = jnp.float32


def _cparams(sem):
    return pltpu.CompilerParams(dimension_semantics=sem, vmem_limit_bytes=VMEM_LIMIT)


def _ada_kernel(c_ref, w_ref, b_ref, o_ref):
    c = c_ref[...]
    s = c * jax.nn.sigmoid(c)
    o_ref[0] = jnp.dot(s, w_ref[0], precision=lax.Precision.HIGHEST,
                       preferred_element_type=F32) + b_ref[0]


def _ada_modulation(cvec, ada_w, ada_b):
    R, D = cvec.shape
    n_out = ada_w.shape[-1]
    tn = 1536
    return pl.pallas_call(
        _ada_kernel,
        grid=(DEPTH, n_out // tn),
        in_specs=[pl.BlockSpec((R, D), lambda l, j: (0, 0)),
                  pl.BlockSpec((1, D, tn), lambda l, j: (l, 0, j)),
                  pl.BlockSpec((1, 1, tn), lambda l, j: (l, 0, j))],
        out_specs=pl.BlockSpec((1, R, tn), lambda l, j: (l, 0, j)),
        out_shape=jax.ShapeDtypeStruct((DEPTH, R, n_out), F32),
        compiler_params=_cparams(("arbitrary", "arbitrary")),
        name="ada_modulation",
    )(cvec, ada_w, ada_b.reshape(DEPTH, 1, n_out))


def _row_mod(mod_ref, idx, n_rows, n_lat):
    row = pl.program_id(1) * n_rows + lax.broadcasted_iota(jnp.int32, (n_rows, 1), 0)
    return jnp.where(row < n_lat, mod_ref[0, 0, idx:idx + 1, :], mod_ref[0, 1, idx:idx + 1, :])


def _modulated(x, nw, mod_ref, shift_idx, scale_idx, n_lat):
    n_rows = x.shape[0]
    ms = jnp.mean(x * x, axis=-1, keepdims=True)
    y = x * lax.rsqrt(ms + EPS) * nw
    return (y * (1.0 + _row_mod(mod_ref, scale_idx, n_rows, n_lat))
            + _row_mod(mod_ref, shift_idx, n_rows, n_lat))


_MOD_BOTH = lambda b, i: (b, 0, 0, 0)
_ONCE = dict(pipeline_mode=pl.Buffered(1))


PAIR = 2 * HEAD_DIM
N_PAIR_GROUPS = 2 * HALF_W // PAIR
ATT_KV_W = GQA_KV_HEADS * PAIR + DIFF_HEADS * PAIR


def _head_sumsq(x, g):
    x2 = x * x
    hi = x2.astype(BF16)
    lo = (x2 - hi.astype(F32)).astype(BF16)
    return jnp.dot(hi, g, preferred_element_type=F32) + jnp.dot(lo, g, preferred_element_type=F32)


def _rope_lanes(x, cos, sin_signed):
    W = x.shape[-1]
    half = HEAD_DIM // 2
    lane = lax.broadcasted_iota(jnp.int32, x.shape, 1) & (HEAD_DIM - 1)
    partner = jnp.where(lane < half, pltpu.roll(x, W - half, 1), pltpu.roll(x, half, 1))
    return x * cos + partner * sin_signed


def _att_project_kernel(x_ref, nw_ref, mod_ref, w_ref, cos_ref, sin_ref, g_ref, qw_ref, kw_ref,
                        q_ref, k_ref, v_ref, *, n_lat):
    h = _modulated(x_ref[0], nw_ref[...], mod_ref, 0, 1, n_lat)
    z = jnp.dot(h.astype(BF16), w_ref[...], preferred_element_type=F32)
    o_aq, o_ak, o_av, o_bq, o_bk, o_bv = np.cumsum((0,) + ATT_SPLITS[:-1]).tolist()
    n_aq, n_ak, n_bq = ATT_SPLITS[0], ATT_SPLITS[1], ATT_SPLITS[3]
    cos, sin = cos_ref[...], sin_ref[...]
    cos4 = jnp.concatenate([cos] * (n_aq // PAIR), axis=1)
    sin4 = jnp.concatenate([sin] * (n_aq // PAIR), axis=1)
    inv_d = 1.0 / HEAD_DIM
    q_scale = HEAD_DIM ** -0.5 * math.log2(math.e)

    aq = z[:, o_aq:o_aq + n_aq]
    aq = aq * lax.rsqrt(_head_sumsq(aq, g_ref[...]) * inv_d + EPS) * qw_ref[...]
    aq = _rope_lanes(aq, cos4, sin4) * q_scale
    bq = _rope_lanes(z[:, o_bq:o_bq + n_bq], cos4, sin4) * q_scale
    q = jnp.concatenate([aq, bq], axis=1)
    lo = (lax.broadcasted_iota(jnp.int32, q.shape, 1) & (PAIR - 1)) < HEAD_DIM
    q_ref[0, :, 0:q.shape[1]] = jnp.where(lo, q, 0.0).astype(BF16)
    q_ref[0, :, q.shape[1]:2 * q.shape[1]] = jnp.where(lo, 0.0, q).astype(BF16)

    ak = z[:, o_ak:o_ak + n_ak]
    ak = ak * lax.rsqrt(_head_sumsq(ak, g_ref[0:n_ak, 0:n_ak]) * inv_d + EPS) * kw_ref[...]
    ak = _rope_lanes(ak, cos, sin)
    av = z[:, o_av:o_av + n_ak]
    lo_kv = lax.broadcasted_iota(jnp.int32, ak.shape, 1) < HEAD_DIM
    ak_sw, av_sw = pltpu.roll(ak, HEAD_DIM, 1), pltpu.roll(av, HEAD_DIM, 1)
    k_ref[0, :, 0:PAIR] = jnp.where(lo_kv, ak, ak_sw).astype(BF16)
    k_ref[0, :, PAIR:2 * PAIR] = jnp.where(lo_kv, ak_sw, ak).astype(BF16)
    v_ref[0, :, 0:PAIR] = jnp.where(lo_kv, av, av_sw).astype(BF16)
    v_ref[0, :, PAIR:2 * PAIR] = jnp.where(lo_kv, av_sw, av).astype(BF16)
    k_ref[0, :, 2 * PAIR:] = _rope_lanes(z[:, o_bk:o_bk + n_bq], cos4, sin4).astype(BF16)
    v_ref[0, :, 2 * PAIR:] = z[:, o_bv:o_bv + n_bq].astype(BF16)


def _att_project(X, nw, modc, w_bf16, cos_t, sin_t, gmat, qw_t, kw_t, n_lat_tiles):
    B, N, D = X.shape
    n_in = w_bf16.shape[1]
    q_w = 2 * N_PAIR_GROUPS * PAIR
    row = lambda b, i: (b, i, 0)
    const2 = lambda b, i: (0, 0)
    return pl.pallas_call(
        functools.partial(_att_project_kernel, n_lat=n_lat_tiles * ROW_TILE),
        grid=(B, N // PROJ_TILE),
        in_specs=[pl.BlockSpec((1, PROJ_TILE, D), row),
                  pl.BlockSpec((1, D), const2),
                  pl.BlockSpec((1, 2, 6, D), _MOD_BOTH),
                  pl.BlockSpec((D, n_in), const2, **_ONCE),
                  pl.BlockSpec((PROJ_TILE, PAIR), lambda b, i: (i, 0)),
                  pl.BlockSpec((PROJ_TILE, PAIR), lambda b, i: (i, 0)),
                  pl.BlockSpec(gmat.shape, const2),
                  pl.BlockSpec(qw_t.shape, const2),
                  pl.BlockSpec(kw_t.shape, const2)],
        out_specs=[pl.BlockSpec((1, PROJ_TILE, q_w), row),
                   pl.BlockSpec((1, PROJ_TILE, ATT_KV_W), row),
                   pl.BlockSpec((1, PROJ_TILE, ATT_KV_W), row)],
        out_shape=[jax.ShapeDtypeStruct((B, N, q_w), BF16),
                   jax.ShapeDtypeStruct((B, N, ATT_KV_W), BF16),
                   jax.ShapeDtypeStruct((B, N, ATT_KV_W), BF16)],
        compiler_params=_cparams(("arbitrary", "arbitrary")),
        name="att_project",
    )(X, nw.reshape(1, D), modc, w_bf16, cos_t, sin_t, gmat, qw_t, kw_t)


def _attention_kernel(x_ref, q_ref, k_ref, v_ref, mod_ref, lam_ref, dnw_ref, w_ref, o_ref,
                      s_scr, p_scr, y_scr, *, lambda_init, n_lat_tiles):
    n_lat = n_lat_tiles * ROW_TILE
    n_all = k_ref.shape[1]
    q_half = N_PAIR_GROUPS * PAIR
    lam = lam_ref[...]
    lam_val = (jnp.exp(jnp.sum(lam[0:1] * lam[1:2], axis=-1, keepdims=True))
               - jnp.exp(jnp.sum(lam[2:3] * lam[3:4], axis=-1, keepdims=True)) + lambda_init)

    def attend(k0, nk):
        keys = slice(k0, k0 + nk)

        def pair(pg):
            qv = jnp.concatenate([q_ref[0, :, h * q_half + pg * PAIR: h * q_half + (pg + 1) * PAIR]
                                  for h in range(2)], axis=0)
            kd = k_ref[0, keys, _kv_lane(pg): _kv_lane(pg) + PAIR]
            vd = v_ref[0, keys, _kv_lane(pg): _kv_lane(pg) + PAIR]
            s_scr[:, 0:nk] = lax.dot_general(qv, kd, (((1,), (1,)), ((), ())),
                                             preferred_element_type=F32)
            s = s_scr[:, 0:nk]
            p_scr[:, 0:nk] = jnp.exp2(s - jnp.max(s, axis=-1, keepdims=True)).astype(BF16)
            p = p_scr[:, 0:nk]
            o = jnp.dot(p, vd, preferred_element_type=F32)
            o = o * (1.0 / jnp.sum(p.astype(F32), axis=-1, keepdims=True))
            return o[0:ROW_TILE], o[ROW_TILE:2 * ROW_TILE]

        lo = lax.broadcasted_iota(jnp.int32, (ROW_TILE, PAIR), 1) < HEAD_DIM
        for pg in range(2 * GQA_KV_HEADS):
            o_even, o_odd = pair(pg)
            y_scr[:, pg * PAIR:(pg + 1) * PAIR] = jnp.where(lo, o_even, o_odd).astype(BF16)
        for pg in range(2 * GQA_KV_HEADS, N_PAIR_GROUPS):
            o_0, o_1 = pair(pg)
            od = o_0 - lam_val * o_1
            od = od * lax.rsqrt(jnp.mean(od * od, axis=-1, keepdims=True) + EPS)
            od = od * dnw_ref[...] * (1.0 - lambda_init)
            y_scr[:, pg * PAIR:(pg + 1) * PAIR] = od.astype(BF16)
        y = jnp.dot(y_scr[...], w_ref[...], preferred_element_type=F32)
        o_ref[0] = x_ref[0] + mod_ref[0, 0, 2:3, :] * y

    i = pl.program_id(1)

    @pl.when(i < n_lat_tiles)
    def _():
        attend(0, n_all)

    @pl.when(i >= n_lat_tiles)
    def _():
        attend(n_lat, n_all - n_lat)


def _kv_lane(pg):
    n_gqa_pairs = 2 * GQA_KV_HEADS
    if pg < n_gqa_pairs:
        return (pg // 2) * PAIR
    return (GQA_KV_HEADS + pg - n_gqa_pairs) * PAIR


def _attention(X, Q, K, V, modc, lam, dnw, w_bf16, lambda_init, n_lat_tiles):
    B, N, D = X.shape
    row = lambda b, i: (b, i, 0)
    whole = lambda b, i: (b, 0, 0)
    const2 = lambda b, i: (0, 0)
    return pl.pallas_call(
        functools.partial(_attention_kernel, lambda_init=lambda_init, n_lat_tiles=n_lat_tiles),
        grid=(B, N // ROW_TILE),
        in_specs=[pl.BlockSpec((1, ROW_TILE, D), row),
                  pl.BlockSpec((1, ROW_TILE, Q.shape[-1]), row),
                  pl.BlockSpec((1, N, K.shape[-1]), whole, pipeline_mode=pl.Buffered(1)),
                  pl.BlockSpec((1, N, V.shape[-1]), whole, pipeline_mode=pl.Buffered(1)),
                  pl.BlockSpec((1, 1, 6, D), lambda b, i: (b, i // n_lat_tiles, 0, 0)),
                  pl.BlockSpec(lam.shape, const2),
                  pl.BlockSpec((1, PAIR), const2),
                  pl.BlockSpec(w_bf16.shape, const2, pipeline_mode=pl.Buffered(1))],
        out_specs=pl.BlockSpec((1, ROW_TILE, D), row),
        out_shape=jax.ShapeDtypeStruct((B, N, D), F32),
        scratch_shapes=[pltpu.VMEM((2 * ROW_TILE, N), F32), pltpu.VMEM((2 * ROW_TILE, N), BF16),
                        pltpu.VMEM((ROW_TILE, w_bf16.shape[0]), BF16)],
        compiler_params=_cparams(("arbitrary", "arbitrary")),
        name="attention",
    )(X, Q, K, V, modc, lam, dnw.reshape(1, PAIR), w_bf16)


REC_IN_PAD = 3200
N_SSD_DT = 2 * SSD_HEADS
RET_LANE0 = N_SSD_DT


def _softplus(x):
    return jnp.maximum(x, 0.0) + jnp.log(1.0 + jnp.exp(-jnp.abs(x)))


def _silu(x):
    return x * jax.nn.sigmoid(x)


def _rec_project_kernel(x_ref, nw_ref, mod_ref, w_ref, cos_ref, sin_ref, dtb_ref,
                        rq_ref, rk_ref, rv_ref, g_ref, xbc_ref, dt_ref, *, n_lat):
    h = _modulated(x_ref[0], nw_ref[...], mod_ref, 0, 1, n_lat)
    z = jnp.dot(h.astype(BF16), w_ref[...], preferred_element_type=F32)
    o_rq, o_rk, o_rv, o_rg, o_z, o_xbc, o_dt = np.cumsum((0,) + REC_SPLITS[:-1]).tolist()
    n_qk = REC_SPLITS[0]
    cos, sin = cos_ref[...], sin_ref[...]
    cos2 = jnp.concatenate([cos] * (n_qk // PAIR), axis=1)
    sin2 = jnp.concatenate([sin] * (n_qk // PAIR), axis=1)
    rq = _rope_lanes(z[:, o_rq:o_rq + n_qk], cos2, sin2)
    lo = (lax.broadcasted_iota(jnp.int32, rq.shape, 1) & (PAIR - 1)) < HEAD_DIM
    rq_ref[0, :, 0:n_qk] = jnp.where(lo, rq, 0.0).astype(BF16)
    rq_ref[0, :, n_qk:2 * n_qk] = jnp.where(lo, 0.0, rq).astype(BF16)
    rk_ref[0] = _rope_lanes(z[:, o_rk:o_rk + n_qk] * (RET_QK_DIM ** -0.5), cos2, sin2).astype(BF16)
    rv_ref[0] = z[:, o_rv:o_rg].astype(BF16)
    g_ref[0] = z[:, o_rg:o_xbc].astype(BF16)
    xbc_ref[0] = z[:, o_xbc:o_dt].astype(BF16)
    dt_ref[0] = _softplus(z[:, o_dt:o_dt + PAIR] + dtb_ref[...])


def _rec_project(X, nw, modc, w_bf16, cos_t, sin_t, dtb_row, n_lat_tiles):
    B, N, D = X.shape
    n_in = w_bf16.shape[1]
    row = lambda b, i: (b, i, 0)
    const2 = lambda b, i: (0, 0)
    widths = (2 * REC_SPLITS[0], REC_SPLITS[1], REC_SPLITS[2], REC_SPLITS[3] + REC_SPLITS[4],
              REC_SPLITS[5], PAIR)
    dtypes = (BF16, BF16, BF16, BF16, BF16, F32)
    return pl.pallas_call(
        functools.partial(_rec_project_kernel, n_lat=n_lat_tiles * ROW_TILE),
        grid=(B, N // PROJ_TILE),
        in_specs=[pl.BlockSpec((1, PROJ_TILE, D), row),
                  pl.BlockSpec((1, D), const2),
                  pl.BlockSpec((1, 2, 6, D), _MOD_BOTH),
                  pl.BlockSpec((D, n_in), const2, **_ONCE),
                  pl.BlockSpec((PROJ_TILE, PAIR), lambda b, i: (i, 0)),
                  pl.BlockSpec((PROJ_TILE, PAIR), lambda b, i: (i, 0)),
                  pl.BlockSpec((1, PAIR), const2)],
        out_specs=[pl.BlockSpec((1, PROJ_TILE, w), row) for w in widths],
        out_shape=[jax.ShapeDtypeStruct((B, N, w), dt) for w, dt in zip(widths, dtypes)],
        compiler_params=_cparams(("arbitrary", "arbitrary")),
        name="rec_project",
    )(X, nw.reshape(1, D), modc, w_bf16, cos_t, sin_t, dtb_row)


def _conv_kernel(x_ref, w_ref, b_ref, o_ref, *, n_lat):
    x = x_ref[0].astype(F32)
    n = x.shape[0]
    t = lax.broadcasted_iota(jnp.int32, x.shape, 0)
    first = (t == 0) | (t == n_lat)
    last = (t == n_lat - 1) | (t == n - 1)
    prev = jnp.where(first, 0.0, pltpu.roll(x, 1, 0))
    nxt = jnp.where(last, 0.0, pltpu.roll(x, n - 1, 0))
    y = prev * w_ref[0:1, :] + x * w_ref[1:2, :] + nxt * w_ref[2:3, :] + b_ref[...]
    o_ref[0] = _silu(y).astype(BF16)


def _ssd_conv(XBC, conv_w, conv_b, n_lat):
    B, N, C = XBC.shape
    tc = 256
    return pl.pallas_call(
        functools.partial(_conv_kernel, n_lat=n_lat),
        grid=(B, C // tc),
        in_specs=[pl.BlockSpec((1, N, tc), lambda b, j: (b, 0, j)),
                  pl.BlockSpec((conv_w.shape[0], tc), lambda b, j: (0, j)),
                  pl.BlockSpec((1, tc), lambda b, j: (0, j))],
        out_specs=pl.BlockSpec((1, N, tc), lambda b, j: (b, 0, j)),
        out_shape=jax.ShapeDtypeStruct((B, N, C), BF16),
        compiler_params=_cparams(("arbitrary", "arbitrary")),
        name="ssd_conv",
    )(XBC, conv_w, conv_b.reshape(1, C))


def _split3(x):
    hi = x.astype(BF16)
    r1 = x - hi.astype(F32)
    mid = r1.astype(BF16)
    lo = (r1 - mid.astype(F32)).astype(BF16)
    return hi, mid, lo


def _scan_kernel(rq_ref, rk_ref, rv_ref, a_ref, dt_ref, alog_ref, logit_ref, y_ref, hret, hssd, *, reverse):
    T = rq_ref.shape[1]
    d = 1 if reverse else 0
    step = pl.program_id(1)

    @pl.when(step == 0)
    def _():
        hret[...] = jnp.zeros_like(hret)
        hssd[...] = jnp.zeros_like(hssd)

    lane = lax.broadcasted_iota(jnp.int32, (1, PAIR), 1)
    a_neg = jnp.where(lane < N_SSD_DT, -jnp.exp(alog_ref[...]), 0.0)
    logit = logit_ref[...]
    log_gamma = jnp.where((lane >= RET_LANE0) & (lane < RET_LANE0 + 2 * RET_HEADS),
                          jnp.minimum(logit, 0.0) - jnp.log(1.0 + jnp.exp(-jnp.abs(logit))), 0.0)
    dt = dt_ref[0]
    la = dt * a_neg + log_gamma

    ti = lax.broadcasted_iota(jnp.int32, (T, T), 0)
    tj = lax.broadcasted_iota(jnp.int32, (T, T), 1)
    causal = (tj >= ti) if reverse else (tj <= ti)
    tri = jnp.where(causal, 1.0, 0.0).astype(BF16)
    hi, mid, lo3 = _split3(la)
    P = (jnp.dot(tri, hi, preferred_element_type=F32) + jnp.dot(tri, mid, preferred_element_type=F32)
         + jnp.dot(tri, lo3, preferred_element_type=F32))
    PT = P.T
    tot = P[0:1, :] if reverse else P[T - 1:T, :]
    E = jnp.exp(P)
    KD = jnp.exp(tot - P)
    ET = jnp.exp(tot)
    lo = lax.broadcasted_iota(jnp.int32, (T, PAIR), 1) < HEAD_DIM
    lo_row = lane < HEAD_DIM
    nt = (((1,), (1,)), ((), ()))
    tn = (((0,), (0,)), ((), ()))

    def decay_matrix(c):
        diff = jnp.minimum(P[:, c:c + 1] - PT[c:c + 1, :], 0.0)
        return jnp.where(causal, jnp.exp(diff), 0.0)

    n_qk = RET_HEADS * RET_QK_DIM
    for h in range(RET_HEADS):
        c = RET_LANE0 + d * RET_HEADS + h
        p, half = h // 2, h % 2
        qv = rq_ref[0, :, half * n_qk + p * PAIR: half * n_qk + (p + 1) * PAIR]
        kp = rk_ref[0, :, p * PAIR:(p + 1) * PAIR]
        v = rv_ref[0, :, h * RET_V_DIM:(h + 1) * RET_V_DIM]
        s = lax.dot_general(qv, kp, nt, preferred_element_type=F32)
        aw = (s * decay_matrix(c)).astype(BF16)
        y = jnp.dot(aw, v, preferred_element_type=F32)
        y = y + E[:, c:c + 1] * jnp.dot(qv, hret[h].astype(BF16), preferred_element_type=F32)
        y_ref[0, :, h * RET_V_DIM:(h + 1) * RET_V_DIM] = y.astype(y_ref.dtype)
        kdec = (kp.astype(F32) * KD[:, c:c + 1]).astype(BF16)
        hret[h] = hret[h] * ET[:, c:c + 1] + lax.dot_general(kdec, v, tn, preferred_element_type=F32)

    y0 = RET_HEADS * RET_V_DIM
    r_per_g = SSD_HEADS // SSD_GROUPS
    o_b = SSD_INNER
    o_c = SSD_INNER + SSD_GROUPS * SSD_STATE
    for g in range(SSD_GROUPS):
        cg = a_ref[0, :, o_c + g * SSD_STATE: o_c + (g + 1) * SSD_STATE]
        bg = a_ref[0, :, o_b + g * SSD_STATE: o_b + (g + 1) * SSD_STATE]
        s = lax.dot_general(cg, bg, nt, preferred_element_type=F32)
        ch = jnp.dot(cg, hssd[g].astype(BF16), preferred_element_type=F32)
        vdec, dec_rows = [], []
        for p in range(r_per_g // 2):
            pg = g * (r_per_g // 2) + p
            xs = a_ref[0, :, pg * PAIR:(pg + 1) * PAIR].astype(F32)
            ys, vds, cols = [], [], []
            for half in range(2):
                c = d * SSD_HEADS + 2 * pg + half
                aw = (s * decay_matrix(c)).astype(BF16)
                vh = xs * dt[:, c:c + 1]
                ys.append(jnp.dot(aw, vh.astype(BF16), preferred_element_type=F32))
                vds.append(vh * KD[:, c:c + 1])
                cols.append(c)
            e_pair = jnp.where(lo, E[:, cols[0]:cols[0] + 1], E[:, cols[1]:cols[1] + 1])
            y_ref[0, :, y0 + pg * PAIR: y0 + (pg + 1) * PAIR] = (
                jnp.where(lo, ys[0], ys[1]) + e_pair * ch[:, p * PAIR:(p + 1) * PAIR]).astype(y_ref.dtype)
            vdec.append(jnp.where(lo, vds[0], vds[1]).astype(BF16))
            dec_rows.append(jnp.where(lo_row, ET[:, cols[0]:cols[0] + 1], ET[:, cols[1]:cols[1] + 1]))
        hssd[g] = (hssd[g] * jnp.concatenate(dec_rows, axis=1)
                   + lax.dot_general(bg, jnp.concatenate(vdec, axis=1), tn, preferred_element_type=F32))


def _bidir_scan(RQ, RK, RV, A, DT, alog_row, logit_row, n_lat_tiles, reverse):
    B, N, _ = RQ.shape
    n_tiles = N // ROW_TILE

    def tile(i):
        lat = (n_lat_tiles - i) if reverse else (i - 1)
        return jnp.where(i == 0, n_tiles - 1, lat)

    row = lambda b, i: (b, tile(i), 0)
    const2 = lambda b, i: (0, 0)
    y_w = RET_HEADS * RET_V_DIM + SSD_INNER
    return pl.pallas_call(
        functools.partial(_scan_kernel, reverse=reverse),
        grid=(B, n_tiles),
        in_specs=[pl.BlockSpec((1, ROW_TILE, RQ.shape[-1]), row),
                  pl.BlockSpec((1, ROW_TILE, RK.shape[-1]), row),
                  pl.BlockSpec((1, ROW_TILE, RV.shape[-1]), row),
                  pl.BlockSpec((1, ROW_TILE, A.shape[-1]), row),
                  pl.BlockSpec((1, ROW_TILE, PAIR), row),
                  pl.BlockSpec((1, PAIR), const2),
                  pl.BlockSpec((1, PAIR), const2)],
        out_specs=pl.BlockSpec((1, ROW_TILE, y_w), row),
        out_shape=jax.ShapeDtypeStruct((B, N, y_w), BF16),
        scratch_shapes=[pltpu.VMEM((RET_HEADS, PAIR, RET_V_DIM), F32),
                        pltpu.VMEM((SSD_GROUPS, SSD_STATE, SSD_INNER // SSD_GROUPS), F32)],
        compiler_params=_cparams(("arbitrary", "arbitrary")),
        name="scan_bwd" if reverse else "scan_fwd",
    )(RQ, RK, RV, A, DT, alog_row, logit_row)


def _rec_combine_kernel(x_ref, yf_ref, yb_ref, g_ref, a_ref, mod_ref, dskip_ref, snw_ref, w_ref, o_ref, y_scr,
                        *, n_lat):
    n_ret = RET_HEADS * RET_V_DIM
    y = yf_ref[0].astype(F32) + yb_ref[0].astype(F32)
    gates = g_ref[0].astype(F32)
    for h in range(RET_HEADS):
        sl = slice(h * RET_V_DIM, (h + 1) * RET_V_DIM)
        r = y[:, sl]
        r = r * lax.rsqrt(jnp.mean(r * r, axis=-1, keepdims=True) + EPS)
        y_scr[:, sl] = (r * _silu(gates[:, sl])).astype(BF16)
    gw = SSD_INNER // SSD_GROUPS
    for g in range(SSD_GROUPS):
        sl = slice(g * gw, (g + 1) * gw)
        s = y[:, n_ret + g * gw: n_ret + (g + 1) * gw] + dskip_ref[:, sl] * a_ref[0, :, sl].astype(F32)
        s = s * _silu(gates[:, n_ret + g * gw: n_ret + (g + 1) * gw])
        s = s * lax.rsqrt(jnp.mean(s * s, axis=-1, keepdims=True) + EPS) * snw_ref[:, sl]
        y_scr[:, n_ret + g * gw: n_ret + (g + 1) * gw] = s.astype(BF16)
    gate = _row_mod(mod_ref, 2, x_ref.shape[1], n_lat)
    o_ref[0] = x_ref[0] + gate * jnp.dot(y_scr[...], w_ref[...], preferred_element_type=F32)


def _rec_combine(X, Yf, Yb, G, A, modc, dskip_row, snw_row, w_bf16, n_lat_tiles):
    B, N, D = X.shape
    row = lambda b, i: (b, i, 0)
    const2 = lambda b, i: (0, 0)
    return pl.pallas_call(
        functools.partial(_rec_combine_kernel, n_lat=n_lat_tiles * ROW_TILE),
        grid=(B, N // PROJ_TILE),
        in_specs=[pl.BlockSpec((1, PROJ_TILE, D), row),
                  pl.BlockSpec((1, PROJ_TILE, Yf.shape[-1]), row),
                  pl.BlockSpec((1, PROJ_TILE, Yb.shape[-1]), row),
                  pl.BlockSpec((1, PROJ_TILE, G.shape[-1]), row),
                  pl.BlockSpec((1, PROJ_TILE, SSD_INNER), row),
                  pl.BlockSpec((1, 2, 6, D), _MOD_BOTH),
                  pl.BlockSpec((1, SSD_INNER), const2),
                  pl.BlockSpec((1, SSD_INNER), const2),
                  pl.BlockSpec(w_bf16.shape, const2, **_ONCE)],
        out_specs=pl.BlockSpec((1, PROJ_TILE, D), row),
        out_shape=jax.ShapeDtypeStruct((B, N, D), F32),
        scratch_shapes=[pltpu.VMEM((PROJ_TILE, w_bf16.shape[0]), BF16)],
        compiler_params=_cparams(("arbitrary", "arbitrary")),
        name="rec_combine",
    )(X, Yf, Yb, G, A, modc, dskip_row, snw_row, w_bf16)


def _recurrent_layer(X, nw, modc, w_in, w_out, cos_t, sin_t, ret_decay_logit, conv_w, conv_b, dt_bias,
                     a_log, d_skip, ssd_norm_w, n_lat_tiles):
    D = X.shape[-1]
    w_pad = jnp.pad(w_in, ((0, 0), (0, REC_IN_PAD - w_in.shape[1]))).astype(BF16)
    pad_row = lambda v, at: jnp.pad(v.reshape(1, -1), ((0, 0), (at, PAIR - at - v.size)))
    RQ, RK, RV, G, XBC, DT = _rec_project(X, nw, modc, w_pad, cos_t, sin_t, pad_row(dt_bias, 0), n_lat_tiles)
    A = _ssd_conv(XBC, conv_w, conv_b, n_lat_tiles * ROW_TILE)
    alog_row, logit_row = pad_row(a_log, 0), pad_row(ret_decay_logit, RET_LANE0)
    Yf = _bidir_scan(RQ, RK, RV, A, DT, alog_row, logit_row, n_lat_tiles, False)
    Yb = _bidir_scan(RQ, RK, RV, A, DT, alog_row, logit_row, n_lat_tiles, True)
    return _rec_combine(X, Yf, Yb, G, A, modc, jnp.repeat(d_skip, SSD_HEAD_DIM).reshape(1, -1),
                        ssd_norm_w.reshape(1, -1), w_out.astype(BF16), n_lat_tiles)


SEL_CHUNK = 256
COMBINE_WINDOW = 64
BF16_ROWS = 16
FFN_ROW_CHUNK = 1024
GATHER_EXPERTS = 8


def _router_kernel(x_ref, nw_ref, mod_ref, rw_ref, h_ref, aff_ref, *, n_lat):
    h = _modulated(x_ref[0], nw_ref[...], mod_ref, 3, 4, n_lat)
    h_hi = h.astype(BF16)
    h_ref[0] = h_hi
    h_lo = (h - h_hi.astype(F32)).astype(BF16)
    w = rw_ref[...]
    w_hi = w.astype(BF16)
    w_lo = (w - w_hi.astype(F32)).astype(BF16)
    logits = (jnp.dot(h_hi, w_hi, preferred_element_type=F32) + jnp.dot(h_lo, w_hi, preferred_element_type=F32)
              + jnp.dot(h_hi, w_lo, preferred_element_type=F32))
    logits = logits.T[0:aff_ref.shape[1], :]
    e = jnp.exp(logits - jnp.max(logits, axis=0, keepdims=True))
    aff_ref[0] = e / jnp.sum(e, axis=0, keepdims=True)


def _mod_router(X, nw, modc, router_w, n_lat, n_rows):
    B, _, D = X.shape
    E = router_w.shape[1]
    rw_pad = jnp.pad(router_w, ((0, 0), (0, PAIR - E)))
    tile = _row_tile(n_rows, PROJ_TILE)
    return pl.pallas_call(
        functools.partial(_router_kernel, n_lat=n_lat),
        grid=(B, n_rows // tile),
        in_specs=[pl.BlockSpec((1, tile, D), lambda b, i: (b, i, 0)),
                  pl.BlockSpec((1, D), lambda b, i: (0, 0)),
                  pl.BlockSpec((1, 2, 6, D), _MOD_BOTH),
                  pl.BlockSpec((D, PAIR), lambda b, i: (0, 0))],
        out_specs=[pl.BlockSpec((1, tile, D), lambda b, i: (b, i, 0)),
                   pl.BlockSpec((1, E, tile), lambda b, i: (b, 0, i))],
        out_shape=[jax.ShapeDtypeStruct((B, n_rows, D), BF16),
                   jax.ShapeDtypeStruct((B, E, n_rows), F32)],
        compiler_params=_cparams(("arbitrary", "arbitrary")),
        name="mod_router",
    )(X, nw.reshape(1, D), modc, rw_pad)


def _count(mask_f32):
    return jnp.sum(mask_f32, axis=-1, keepdims=True)


def _select_top(a, cap):
    E, n = a.shape
    v = pltpu.bitcast(a, jnp.int32)
    thr = jnp.zeros((E, 1), jnp.int32)
    for bit in range(30, -1, -1):
        cand = thr | (1 << bit)
        thr = jnp.where(_count(jnp.where(v >= cand, 1.0, 0.0)) >= cap, cand, thr)
    gt = v > thr
    eq = v == thr
    need = cap - _count(jnp.where(gt, 1.0, 0.0))
    idx = lax.broadcasted_iota(jnp.int32, (E, n), 1)
    last = jnp.zeros((E, 1), jnp.int32)
    for bit in range(n.bit_length() - 2, -1, -1):
        cand = last | (1 << bit)
        below = _count(jnp.where(eq, jnp.where(idx < cand, 1.0, 0.0), 0.0))
        last = jnp.where(below < need, cand, last)
    sel = jnp.where(gt, 1.0, jnp.where(eq, jnp.where(idx <= last, 1.0, 0.0), 0.0))
    si = lax.broadcasted_iota(jnp.int32, (SEL_CHUNK, SEL_CHUNK), 0)
    sj = lax.broadcasted_iota(jnp.int32, (SEL_CHUNK, SEL_CHUNK), 1)
    before = jnp.where(si < sj, 1.0, 0.0).astype(BF16)
    base = jnp.zeros((E, 1), F32)
    lane = lax.broadcasted_iota(jnp.int32, (E, PAIR), 1)
    starts = jnp.zeros((E, PAIR), F32)
    pos = []
    for k in range(n // SEL_CHUNK):
        sk = sel[:, k * SEL_CHUNK:(k + 1) * SEL_CHUNK]
        pos.append(jnp.dot(sk.astype(BF16), before, preferred_element_type=F32) + base)
        base = base + _count(sk)
        starts = jnp.where(lane == k + 1, base, starts)
    pos = jnp.concatenate(pos, axis=1) if len(pos) > 1 else pos[0]
    return jnp.where(sel > 0.0, pos, -1.0), jnp.where(sel > 0.0, a, 0.0), starts


def _token_sets(n_lat, n_rows):
    sets = [(0, n_lat, EC_CAPACITY * n_lat // N_EXPERTS)]
    if n_rows > n_lat:
        sets.append((n_lat, n_rows, EC_CAPACITY * (n_rows - n_lat) // N_EXPERTS))
    return sets


def _select_kernel(aff_ref, slot_ref, slot_t_ref, gate_t_ref, starts_ref, *, sets):
    E = aff_ref.shape[1]
    pad_rows = PAIR - E
    for n_set, (lo, hi, cap) in enumerate(sets):
        slot, gate, starts = _select_top(aff_ref[0, :, lo:hi], cap)
        if n_set == 0:
            starts_ref[0] = starts
        slot_ref[0, :, lo:hi] = slot
        slot_t_ref[0, lo:hi, :] = jnp.concatenate([slot, jnp.full((pad_rows, hi - lo), -1.0, F32)], axis=0).T
        gate_t_ref[0, lo:hi, :] = jnp.concatenate([gate, jnp.zeros((pad_rows, hi - lo), F32)], axis=0).T


def _ec_select(aff, sets):
    B, E, N = aff.shape
    return pl.pallas_call(
        functools.partial(_select_kernel, sets=sets),
        grid=(B,),
        in_specs=[pl.BlockSpec((1, E, N), lambda b: (b, 0, 0))],
        out_specs=[pl.BlockSpec((1, E, N), lambda b: (b, 0, 0)),
                   pl.BlockSpec((1, N, PAIR), lambda b: (b, 0, 0)),
                   pl.BlockSpec((1, N, PAIR), lambda b: (b, 0, 0)),
                   pl.BlockSpec((1, E, PAIR), lambda b: (b, 0, 0))],
        out_shape=[jax.ShapeDtypeStruct((B, E, N), F32),
                   jax.ShapeDtypeStruct((B, N, PAIR), F32),
                   jax.ShapeDtypeStruct((B, N, PAIR), F32),
                   jax.ShapeDtypeStruct((B, E, PAIR), F32)],
        compiler_params=_cparams(("arbitrary",)),
        name="ec_select",
    )(aff)


def _gather_kernel(slot_ref, h_ref, *x_refs, sets):
    n_e = x_refs[0].shape[0]
    e0 = pl.multiple_of(pl.program_id(1) * n_e, n_e)
    for (lo, hi, cap), x_ref in zip(sets, x_refs):
        c = lax.broadcasted_iota(jnp.int32, (cap, hi - lo), 0).astype(F32)
        onehot = jnp.concatenate(
            [jnp.where(slot_ref[0, pl.ds(e0 + j, 1), lo:hi] == c, 1.0, 0.0).astype(BF16)
             for j in range(n_e)], axis=0)
        x = jnp.dot(onehot, h_ref[0, lo:hi, :], preferred_element_type=F32)
        for j in range(n_e):
            x_ref[j] = x[j * cap:(j + 1) * cap, :].astype(BF16)


def _ec_gather(slot, H, sets):
    B, E, N = slot.shape
    D = H.shape[-1]
    return pl.pallas_call(
        functools.partial(_gather_kernel, sets=sets),
        grid=(B, E // GATHER_EXPERTS),
        in_specs=[pl.BlockSpec((1, E, N), lambda b, e: (b, 0, 0)),
                  pl.BlockSpec((1, N, D), lambda b, e: (b, 0, 0))],
        out_specs=[pl.BlockSpec((GATHER_EXPERTS, cap, D), lambda b, e: (e, b, 0)) for _, _, cap in sets],
        out_shape=[jax.ShapeDtypeStruct((E, B * cap, D), BF16) for _, _, cap in sets],
        compiler_params=_cparams(("arbitrary", "arbitrary")),
        name="ec_gather",
    )(slot, H)


def _ffn_kernel(*refs, n_groups, row_chunk):
    x_refs = refs[:n_groups]
    wg_ref, wu_ref, wd_ref = refs[n_groups:n_groups + 3]
    o_refs = refs[n_groups + 3:2 * n_groups + 3]
    acc_refs = refs[2 * n_groups + 3:3 * n_groups + 3]
    wg_s, wu_s, wd_s = refs[3 * n_groups + 3:]
    f = pl.program_id(2)
    n_f = pl.num_programs(2)
    wg_s[...] = wg_ref[0, 0].astype(BF16)
    wu_s[...] = wu_ref[0, 0].astype(BF16)
    wd_s[...] = wd_ref[0, 0].astype(BF16)

    @pl.when(f == 0)
    def _():
        for acc_ref in acc_refs:
            acc_ref[...] = jnp.zeros_like(acc_ref)

    def chunk(x_ref, acc_ref, start, size):
        rows = pl.ds(start, size)
        xr = x_ref[0, rows, :]
        a = jnp.dot(xr, wg_s[...], preferred_element_type=F32)
        u = jnp.dot(xr, wu_s[...], preferred_element_type=F32)
        hm = (a * jax.nn.sigmoid(a) * u).astype(BF16)
        acc_ref[rows, :] += jnp.dot(hm, wd_s[...], preferred_element_type=F32)

    for x_ref, acc_ref in zip(x_refs, acc_refs):
        n_rows = x_ref.shape[1]
        for start in range(0, n_rows, row_chunk):
            chunk(x_ref, acc_ref, start, min(row_chunk, n_rows - start))

    @pl.when(f == n_f - 1)
    def _():
        for o_ref, acc_ref in zip(o_refs, acc_refs):
            o_ref[0] = acc_ref[...].astype(o_ref.dtype)


def _expert_ffn(xs, w_gate, w_up, w_down, layer):
    E, _, D = xs[0].shape
    F = w_gate.shape[-1]
    m_tiles = 2
    tms = [x.shape[1] // m_tiles for x in xs]
    n = len(xs)
    return pl.pallas_call(
        functools.partial(_ffn_kernel, n_groups=n, row_chunk=FFN_ROW_CHUNK),
        grid=(E, m_tiles, F // FF_TILE),
        in_specs=[pl.BlockSpec((1, tm, D), lambda e, m, f: (e, m, 0)) for tm in tms]
        + [pl.BlockSpec((1, 1, D, FF_TILE), lambda e, m, f: (layer, e, 0, f)),
           pl.BlockSpec((1, 1, D, FF_TILE), lambda e, m, f: (layer, e, 0, f)),
           pl.BlockSpec((1, 1, FF_TILE, D), lambda e, m, f: (layer, e, f, 0))],
        out_specs=[pl.BlockSpec((1, tm, D), lambda e, m, f: (e, m, 0)) for tm in tms],
        out_shape=[jax.ShapeDtypeStruct(x.shape, BF16) for x in xs],
        scratch_shapes=[pltpu.VMEM((tm, D), F32) for tm in tms]
        + [pltpu.VMEM((D, FF_TILE), BF16), pltpu.VMEM((D, FF_TILE), BF16), pltpu.VMEM((FF_TILE, D), BF16)],
        compiler_params=_cparams(("arbitrary", "arbitrary", "arbitrary")),
        name="expert_ffn",
    )(*xs, w_gate, w_up, w_down)


def _combine_kernel(starts_ref, x_ref, slot_t_ref, gate_t_ref, mod_ref, fnw_ref, *refs, n_lat, final_norm):
    y_refs, o_ref = refs[:-1], refs[-1]
    E = y_refs[0].shape[0]
    tile = x_ref.shape[1]
    b, i = pl.program_id(0), pl.program_id(1)

    def finish(acc, seg):
        out = x_ref[0] + mod_ref[0, seg, 5:6, :] * acc
        if final_norm:
            out = out * lax.rsqrt(jnp.mean(out * out, axis=-1, keepdims=True) + EPS) * fnw_ref[...]
        o_ref[0] = out

    def scatter_full(seg):
        y_ref = y_refs[seg]
        cap = y_ref.shape[1]
        c = lax.broadcasted_iota(jnp.int32, (tile, cap), 1).astype(F32)
        acc = jnp.zeros((tile, x_ref.shape[2]), F32)
        for e in range(E):
            onehot = jnp.where(slot_t_ref[0, :, e:e + 1] == c, 1.0, 0.0).astype(BF16)
            acc = acc + gate_t_ref[0, :, e:e + 1] * jnp.dot(onehot, y_ref[e], preferred_element_type=F32)
        finish(acc, seg)

    def scatter_windowed(window_starts):
        y_ref = y_refs[0]
        k = lax.broadcasted_iota(jnp.int32, (tile, COMBINE_WINDOW), 1).astype(F32)
        sel, rows = [], []
        for e in range(E):
            w = window_starts[e]
            rel = slot_t_ref[0, :, e:e + 1] - w.astype(F32)
            sel.append(jnp.where(rel == k, gate_t_ref[0, :, e:e + 1], 0.0).astype(BF16))
            rows.append(y_ref[e, pl.ds(pl.multiple_of(w, BF16_ROWS), COMBINE_WINDOW), :])
        acc = jnp.dot(jnp.concatenate(sel, axis=1), jnp.concatenate(rows, axis=0),
                      preferred_element_type=F32)
        finish(acc, 0)

    n_lat_tiles = n_lat // tile
    cap_lat = y_refs[0].shape[1]

    @pl.when(i < n_lat_tiles)
    def _():
        if cap_lat <= COMBINE_WINDOW:
            scatter_full(0)
            return
        window_starts, fits = [], None
        for e in range(E):
            first, end = starts_ref[b, e, i], starts_ref[b, e, i + 1]
            w = jnp.minimum(first - lax.rem(first, BF16_ROWS), cap_lat - COMBINE_WINDOW)
            ok = end - w <= COMBINE_WINDOW
            fits = ok if fits is None else jnp.logical_and(fits, ok)
            window_starts.append(w)

        @pl.when(fits)
        def _():
            scatter_windowed(window_starts)

        @pl.when(jnp.logical_not(fits))
        def _():
            scatter_full(0)

    if len(y_refs) > 1:
        @pl.when(i >= n_lat_tiles)
        def _():
            scatter_full(1)


def _row_tile(n_rows, preferred):
    return preferred if n_rows % preferred == 0 else 2 * ROW_TILE


def _ec_combine(X, slot_t, gate_t, starts, Ys, modc, sets, n_lat, final_norm_w):
    B, _, D = X.shape
    E = Ys[0].shape[0]
    n_rows = sets[-1][1]
    tile = SEL_CHUNK
    row = lambda b, i, s: (b, i, 0)
    final_norm = final_norm_w is not None
    fnw = final_norm_w.reshape(1, D) if final_norm else jnp.ones((1, D), F32)
    n_starts = n_lat // tile + 1
    starts_i32 = starts[:, :, :n_starts].astype(jnp.int32)
    grid_spec = pltpu.PrefetchScalarGridSpec(
        num_scalar_prefetch=1,
        grid=(B, n_rows // tile),
        in_specs=[pl.BlockSpec((1, tile, D), row),
                  pl.BlockSpec((1, tile, PAIR), row),
                  pl.BlockSpec((1, tile, PAIR), row),
                  pl.BlockSpec((1, 2, 6, D), lambda b, i, s: (b, 0, 0, 0)),
                  pl.BlockSpec((1, D), lambda b, i, s: (0, 0))]
        + [pl.BlockSpec((E, cap, D), lambda b, i, s: (0, b, 0)) for _, _, cap in sets],
        out_specs=pl.BlockSpec((1, tile, D), row))
    return pl.pallas_call(
        functools.partial(_combine_kernel, n_lat=n_lat, final_norm=final_norm),
        grid_spec=grid_spec,
        out_shape=jax.ShapeDtypeStruct((B, n_rows, D), F32),
        compiler_params=_cparams(("arbitrary", "arbitrary")),
        name="ec_combine",
    )(starts_i32, X, slot_t, gate_t, modc, fnw, *Ys)


def _moe_layer(X, nw, modc, router_w, w_gate, w_up, w_down, layer, n_lat, n_rows, final_norm_w=None):
    sets = _token_sets(n_lat, n_rows)
    H, aff = _mod_router(X, nw, modc, router_w, n_lat, n_rows)
    slot, slot_t, gate_t, starts = _ec_select(aff, sets)
    Xs = _ec_gather(slot, H, sets)
    Ys = _expert_ffn(Xs, w_gate, w_up, w_down, layer)
    return _ec_combine(X, slot_t, gate_t, starts, Ys, modc, sets, n_lat, final_norm_w)


def _rope_tables(n):
    t = jnp.arange(n)
    row = (t // GRID_W).astype(F32)
    col = (t % GRID_W).astype(F32)
    n_freq = HEAD_DIM // 4
    inv = ROPE_THETA ** (-jnp.arange(n_freq, dtype=F32) / n_freq)
    ang = jnp.concatenate([row[:, None] * inv, col[:, None] * inv], axis=-1)
    return jnp.cos(ang), jnp.sin(ang)


def kernel(x, c, ctx, c_ctx, ada_w, ada_b, norm1_w, norm2_w, att_w_in, att_w_out, att_q_norm_w,
           att_k_norm_w, diff_lambda, diff_norm_w, rec_w_in, rec_w_out, ret_decay_logit, ssd_conv_w,
           ssd_conv_b, ssd_dt_bias, ssd_a_log, ssd_d_skip, ssd_norm_w, router_w, expert_w_gate,
           expert_w_up, expert_w_down, final_norm_w):
    B, n_lat, D = x.shape
    n_ctx = ctx.shape[1]
    n_lat_tiles = n_lat // ROW_TILE
    cos, sin = _rope_tables(n_lat)
    reps = PAIR // cos.shape[1]
    cos_t = jnp.concatenate([jnp.tile(cos, (1, reps)), jnp.ones((n_ctx, PAIR), F32)], axis=0)
    sin_t = jnp.concatenate([jnp.tile(jnp.concatenate([-sin, sin], axis=1), (1, reps // 2)),
                             jnp.zeros((n_ctx, PAIR), F32)], axis=0)
    gmat = jnp.kron(jnp.eye(GQA_HEADS, dtype=F32), jnp.ones((HEAD_DIM, HEAD_DIM), F32)).astype(BF16)

    n_cond = 24
    cvec = jnp.concatenate([c, c_ctx[None, :], jnp.zeros((n_cond - B - 1, D), F32)], axis=0)
    mods = _ada_modulation(cvec, ada_w, ada_b)

    X = jnp.concatenate([x, ctx], axis=1)
    for layer in range(DEPTH):
        i = layer // 2
        need_ctx = layer < DEPTH - 1
        mod_lat = mods[layer, :B].reshape(B, 1, 6, D)
        mod_ctx = jnp.broadcast_to(mods[layer, B].reshape(1, 1, 6, D), (B, 1, 6, D))
        modc = jnp.concatenate([mod_lat, mod_ctx], axis=1)

        if layer % 2 == 0:
            lambda_init = 0.8 - 0.6 * math.exp(-0.3 * layer)
            Q, K, V = _att_project(X, norm1_w[layer], modc, att_w_in[i].astype(BF16), cos_t, sin_t, gmat,
                                   jnp.tile(att_q_norm_w[i], GQA_HEADS).reshape(1, -1),
                                   jnp.tile(att_k_norm_w[i], GQA_KV_HEADS).reshape(1, -1), n_lat_tiles)
            X = _attention(X, Q, K, V, modc, diff_lambda[i], diff_norm_w[i], att_w_out[i].astype(BF16),
                           lambda_init, n_lat_tiles)
        else:
            X = _recurrent_layer(X, norm1_w[layer], modc, rec_w_in[i], rec_w_out[i], cos_t, sin_t,
                                 ret_decay_logit[i], ssd_conv_w[i], ssd_conv_b[i], ssd_dt_bias[i],
                                 ssd_a_log[i], ssd_d_skip[i], ssd_norm_w[i], n_lat_tiles)

        X = _moe_layer(X, norm2_w[layer], modc, router_w[layer], expert_w_gate, expert_w_up, expert_w_down,
                       layer, n_lat, n_lat + n_ctx if need_ctx else n_lat,
                       None if need_ctx else final_norm_w)

    return X
```

```python
import functools
import math

import jax
import jax.numpy as jnp
import numpy as np
from jax import lax
from jax.experimental import pallas as pl
from jax.experimental.pallas import tpu as pltpu

D_MODEL = 1024
DEPTH = 4
GRID_W = 64
HEAD_DIM = 64
ROPE_THETA = 10000.0
EPS = 1e-6
HALF_W = D_MODEL // 2
GQA_HEADS = HALF_W // HEAD_DIM
GQA_KV_HEADS = GQA_HEADS // 4
DIFF_HEADS = HALF_W // (2 * HEAD_DIM)
RET_HEADS = HALF_W // (2 * HEAD_DIM)
RET_QK_DIM = HEAD_DIM
RET_V_DIM = 2 * HEAD_DIM
SSD_HEAD_DIM = HEAD_DIM
SSD_HEADS = HALF_W // SSD_HEAD_DIM
SSD_GROUPS = 2
SSD_STATE = 128
SSD_INNER = SSD_HEADS * SSD_HEAD_DIM
SSD_XBC = SSD_INNER + 2 * SSD_GROUPS * SSD_STATE
N_EXPERTS = 16
EC_CAPACITY = 2
EXPERT_FF = ((8 * D_MODEL // 3 + 255) // 256) * 256

ATT_SPLITS = (GQA_HEADS * HEAD_DIM, GQA_KV_HEADS * HEAD_DIM, GQA_KV_HEADS * HEAD_DIM,
              DIFF_HEADS * 2 * HEAD_DIM, DIFF_HEADS * 2 * HEAD_DIM, DIFF_HEADS * 2 * HEAD_DIM)
REC_SPLITS = (RET_HEADS * RET_QK_DIM, RET_HEADS * RET_QK_DIM, RET_HEADS * RET_V_DIM,
              RET_HEADS * RET_V_DIM, SSD_INNER, SSD_XBC, 2 * SSD_HEADS)

ROW_TILE = 256
PROJ_TILE = 768
FF_TILE = 256
VMEM_LIMIT = 56 * 1024 * 1024
BF16 = jnp.bfloat16
F32 = jnp.float32


def _cparams(sem):
    return pltpu.CompilerParams(dimension_semantics=sem, vmem_limit_bytes=VMEM_LIMIT)


def _ada_kernel(c_ref, w_ref, b_ref, o_ref):
    c = c_ref[...]
    s = c * jax.nn.sigmoid(c)
    o_ref[0] = jnp.dot(s, w_ref[0], precision=lax.Precision.HIGHEST,
                       preferred_element_type=F32) + b_ref[0]


def _ada_modulation(cvec, ada_w, ada_b):
    R, D = cvec.shape
    n_out = ada_w.shape[-1]
    tn = 1536
    return pl.pallas_call(
        _ada_kernel,
        grid=(DEPTH, n_out // tn),
        in_specs=[pl.BlockSpec((R, D), lambda l, j: (0, 0)),
                  pl.BlockSpec((1, D, tn), lambda l, j: (l, 0, j)),
                  pl.BlockSpec((1, 1, tn), lambda l, j: (l, 0, j))],
        out_specs=pl.BlockSpec((1, R, tn), lambda l, j: (l, 0, j)),
        out_shape=jax.ShapeDtypeStruct((DEPTH, R, n_out), F32),
        compiler_params=_cparams(("arbitrary", "arbitrary")),
        name="ada_modulation",
    )(cvec, ada_w, ada_b.reshape(DEPTH, 1, n_out))


def _row_mod(mod_ref, idx, n_rows, n_lat):
    row = pl.program_id(1) * n_rows + lax.broadcasted_iota(jnp.int32, (n_rows, 1), 0)
    return jnp.where(row < n_lat, mod_ref[0, 0, idx:idx + 1, :], mod_ref[0, 1, idx:idx + 1, :])


def _modulated(x, nw, mod_ref, shift_idx, scale_idx, n_lat):
    n_rows = x.shape[0]
    ms = jnp.mean(x * x, axis=-1, keepdims=True)
    y = x * lax.rsqrt(ms + EPS) * nw
    return (y * (1.0 + _row_mod(mod_ref, scale_idx, n_rows, n_lat))
            + _row_mod(mod_ref, shift_idx, n_rows, n_lat))


_MOD_BOTH = lambda b, i: (b, 0, 0, 0)
_ONCE = dict(pipeline_mode=pl.Buffered(1))


PAIR = 2 * HEAD_DIM
N_PAIR_GROUPS = 2 * HALF_W // PAIR
ATT_KV_W = GQA_KV_HEADS * PAIR + DIFF_HEADS * PAIR


def _head_sumsq(x, g):
    x2 = x * x
    hi = x2.astype(BF16)
    lo = (x2 - hi.astype(F32)).astype(BF16)
    return jnp.dot(hi, g, preferred_element_type=F32) + jnp.dot(lo, g, preferred_element_type=F32)


def _rope_lanes(x, cos, sin_signed):
    W = x.shape[-1]
    half = HEAD_DIM // 2
    lane = lax.broadcasted_iota(jnp.int32, x.shape, 1) & (HEAD_DIM - 1)
    partner = jnp.where(lane < half, pltpu.roll(x, W - half, 1), pltpu.roll(x, half, 1))
    return x * cos + partner * sin_signed


def _att_project_kernel(x_ref, nw_ref, mod_ref, w_ref, cos_ref, sin_ref, g_ref, qw_ref, kw_ref,
                        q_ref, k_ref, v_ref, *, n_lat):
    h = _modulated(x_ref[0], nw_ref[...], mod_ref, 0, 1, n_lat)
    z = jnp.dot(h.astype(BF16), w_ref[...], preferred_element_type=F32)
    o_aq, o_ak, o_av, o_bq, o_bk, o_bv = np.cumsum((0,) + ATT_SPLITS[:-1]).tolist()
    n_aq, n_ak, n_bq = ATT_SPLITS[0], ATT_SPLITS[1], ATT_SPLITS[3]
    cos, sin = cos_ref[...], sin_ref[...]
    cos4 = jnp.concatenate([cos] * (n_aq // PAIR), axis=1)
    sin4 = jnp.concatenate([sin] * (n_aq // PAIR), axis=1)
    inv_d = 1.0 / HEAD_DIM
    q_scale = HEAD_DIM ** -0.5 * math.log2(math.e)

    aq = z[:, o_aq:o_aq + n_aq]
    aq = aq * lax.rsqrt(_head_sumsq(aq, g_ref[...]) * inv_d + EPS) * qw_ref[...]
    aq = _rope_lanes(aq, cos4, sin4) * q_scale
    bq = _rope_lanes(z[:, o_bq:o_bq + n_bq], cos4, sin4) * q_scale
    q = jnp.concatenate([aq, bq], axis=1)
    lo = (lax.broadcasted_iota(jnp.int32, q.shape, 1) & (PAIR - 1)) < HEAD_DIM
    q_ref[0, :, 0:q.shape[1]] = jnp.where(lo, q, 0.0).astype(BF16)
    q_ref[0, :, q.shape[1]:2 * q.shape[1]] = jnp.where(lo, 0.0, q).astype(BF16)

    ak = z[:, o_ak:o_ak + n_ak]
    ak = ak * lax.rsqrt(_head_sumsq(ak, g_ref[0:n_ak, 0:n_ak]) * inv_d + EPS) * kw_ref[...]
    ak = _rope_lanes(ak, cos, sin)
    av = z[:, o_av:o_av + n_ak]
    lo_kv = lax.broadcasted_iota(jnp.int32, ak.shape, 1) < HEAD_DIM
    ak_sw, av_sw = pltpu.roll(ak, HEAD_DIM, 1), pltpu.roll(av, HEAD_DIM, 1)
    k_ref[0, :, 0:PAIR] = jnp.where(lo_kv, ak, ak_sw).astype(BF16)
    k_ref[0, :, PAIR:2 * PAIR] = jnp.where(lo_kv, ak_sw, ak).astype(BF16)
    v_ref[0, :, 0:PAIR] = jnp.where(lo_kv, av, av_sw).astype(BF16)
    v_ref[0, :, PAIR:2 * PAIR] = jnp.where(lo_kv, av_sw, av).astype(BF16)
    k_ref[0, :, 2 * PAIR:] = _rope_lanes(z[:, o_bk:o_bk + n_bq], cos4, sin4).astype(BF16)
    v_ref[0, :, 2 * PAIR:] = z[:, o_bv:o_bv + n_bq].astype(BF16)


def _att_project(X, nw, modc, w_bf16, cos_t, sin_t, gmat, qw_t, kw_t, n_lat_tiles):
    B, N, D = X.shape
    n_in = w_bf16.shape[1]
    q_w = 2 * N_PAIR_GROUPS * PAIR
    row = lambda b, i: (b, i, 0)
    const2 = lambda b, i: (0, 0)
    return pl.pallas_call(
        functools.partial(_att_project_kernel, n_lat=n_lat_tiles * ROW_TILE),
        grid=(B, N // PROJ_TILE),
        in_specs=[pl.BlockSpec((1, PROJ_TILE, D), row),
                  pl.BlockSpec((1, D), const2),
                  pl.BlockSpec((1, 2, 6, D), _MOD_BOTH),
                  pl.BlockSpec((D, n_in), const2, **_ONCE),
                  pl.BlockSpec((PROJ_TILE, PAIR), lambda b, i: (i, 0)),
                  pl.BlockSpec((PROJ_TILE, PAIR), lambda b, i: (i, 0)),
                  pl.BlockSpec(gmat.shape, const2),
                  pl.BlockSpec(qw_t.shape, const2),
                  pl.BlockSpec(kw_t.shape, const2)],
        out_specs=[pl.BlockSpec((1, PROJ_TILE, q_w), row),
                   pl.BlockSpec((1, PROJ_TILE, ATT_KV_W), row),
                   pl.BlockSpec((1, PROJ_TILE, ATT_KV_W), row)],
        out_shape=[jax.ShapeDtypeStruct((B, N, q_w), BF16),
                   jax.ShapeDtypeStruct((B, N, ATT_KV_W), BF16),
                   jax.ShapeDtypeStruct((B, N, ATT_KV_W), BF16)],
        compiler_params=_cparams(("arbitrary", "arbitrary")),
        name="att_project",
    )(X, nw.reshape(1, D), modc, w_bf16, cos_t, sin_t, gmat, qw_t, kw_t)


def _attention_kernel(x_ref, q_ref, k_ref, v_ref, mod_ref, lam_ref, dnw_ref, w_ref, o_ref,
                      s_scr, p_scr, y_scr, *, lambda_init, n_lat_tiles):
    n_lat = n_lat_tiles * ROW_TILE
    n_all = k_ref.shape[1]
    q_half = N_PAIR_GROUPS * PAIR
    lam = lam_ref[...]
    lam_val = (jnp.exp(jnp.sum(lam[0:1] * lam[1:2], axis=-1, keepdims=True))
               - jnp.exp(jnp.sum(lam[2:3] * lam[3:4], axis=-1, keepdims=True)) + lambda_init)

    def attend(k0, nk):
        keys = slice(k0, k0 + nk)

        def pair(pg):
            qv = jnp.concatenate([q_ref[0, :, h * q_half + pg * PAIR: h * q_half + (pg + 1) * PAIR]
                                  for h in range(2)], axis=0)
            kd = k_ref[0, keys, _kv_lane(pg): _kv_lane(pg) + PAIR]
            vd = v_ref[0, keys, _kv_lane(pg): _kv_lane(pg) + PAIR]
            s_scr[:, 0:nk] = lax.dot_general(qv, kd, (((1,), (1,)), ((), ())),
                                             preferred_element_type=F32)
            s = s_scr[:, 0:nk]
            p_scr[:, 0:nk] = jnp.exp2(s - jnp.max(s, axis=-1, keepdims=True)).astype(BF16)
            p = p_scr[:, 0:nk]
            o = jnp.dot(p, vd, preferred_element_type=F32)
            o = o * (1.0 / jnp.sum(p.astype(F32), axis=-1, keepdims=True))
            return o[0:ROW_TILE], o[ROW_TILE:2 * ROW_TILE]

        lo = lax.broadcasted_iota(jnp.int32, (ROW_TILE, PAIR), 1) < HEAD_DIM
        for pg in range(2 * GQA_KV_HEADS):
            o_even, o_odd = pair(pg)
            y_scr[:, pg * PAIR:(pg + 1) * PAIR] = jnp.where(lo, o_even, o_odd).astype(BF16)
        for pg in range(2 * GQA_KV_HEADS, N_PAIR_GROUPS):
            o_0, o_1 = pair(pg)
            od = o_0 - lam_val * o_1
            od = od * lax.rsqrt(jnp.mean(od * od, axis=-1, keepdims=True) + EPS)
            od = od * dnw_ref[...] * (1.0 - lambda_init)
            y_scr[:, pg * PAIR:(pg + 1) * PAIR] = od.astype(BF16)
        y = jnp.dot(y_scr[...], w_ref[...], preferred_element_type=F32)
        o_ref[0] = x_ref[0] + mod_ref[0, 0, 2:3, :] * y

    i = pl.program_id(1)

    @pl.when(i < n_lat_tiles)
    def _():
        attend(0, n_all)

    @pl.when(i >= n_lat_tiles)
    def _():
        attend(n_lat, n_all - n_lat)


def _kv_lane(pg):
    n_gqa_pairs = 2 * GQA_KV_HEADS
    if pg < n_gqa_pairs:
        return (pg // 2) * PAIR
    return (GQA_KV_HEADS + pg - n_gqa_pairs) * PAIR


def _attention(X, Q, K, V, modc, lam, dnw, w_bf16, lambda_init, n_lat_tiles):
    B, N, D = X.shape
    row = lambda b, i: (b, i, 0)
    whole = lambda b, i: (b, 0, 0)
    const2 = lambda b, i: (0, 0)
    return pl.pallas_call(
        functools.partial(_attention_kernel, lambda_init=lambda_init, n_lat_tiles=n_lat_tiles),
        grid=(B, N // ROW_TILE),
        in_specs=[pl.BlockSpec((1, ROW_TILE, D), row),
                  pl.BlockSpec((1, ROW_TILE, Q.shape[-1]), row),
                  pl.BlockSpec((1, N, K.shape[-1]), whole, pipeline_mode=pl.Buffered(1)),
                  pl.BlockSpec((1, N, V.shape[-1]), whole, pipeline_mode=pl.Buffered(1)),
                  pl.BlockSpec((1, 1, 6, D), lambda b, i: (b, i // n_lat_tiles, 0, 0)),
                  pl.BlockSpec(lam.shape, const2),
                  pl.BlockSpec((1, PAIR), const2),
                  pl.BlockSpec(w_bf16.shape, const2, pipeline_mode=pl.Buffered(1))],
        out_specs=pl.BlockSpec((1, ROW_TILE, D), row),
        out_shape=jax.ShapeDtypeStruct((B, N, D), F32),
        scratch_shapes=[pltpu.VMEM((2 * ROW_TILE, N), F32), pltpu.VMEM((2 * ROW_TILE, N), BF16),
                        pltpu.VMEM((ROW_TILE, w_bf16.shape[0]), BF16)],
        compiler_params=_cparams(("arbitrary", "arbitrary")),
        name="attention",
    )(X, Q, K, V, modc, lam, dnw.reshape(1, PAIR), w_bf16)


REC_IN_PAD = 3200
N_SSD_DT = 2 * SSD_HEADS
RET_LANE0 = N_SSD_DT


def _softplus(x):
    return jnp.maximum(x, 0.0) + jnp.log(1.0 + jnp.exp(-jnp.abs(x)))


def _silu(x):
    return x * jax.nn.sigmoid(x)


def _rec_project_kernel(x_ref, nw_ref, mod_ref, w_ref, cos_ref, sin_ref, dtb_ref,
                        rq_ref, rk_ref, rv_ref, g_ref, xbc_ref, dt_ref, *, n_lat):
    h = _modulated(x_ref[0], nw_ref[...], mod_ref, 0, 1, n_lat)
    z = jnp.dot(h.astype(BF16), w_ref[...], preferred_element_type=F32)
    o_rq, o_rk, o_rv, o_rg, o_z, o_xbc, o_dt = np.cumsum((0,) + REC_SPLITS[:-1]).tolist()
    n_qk = REC_SPLITS[0]
    cos, sin = cos_ref[...], sin_ref[...]
    cos2 = jnp.concatenate([cos] * (n_qk // PAIR), axis=1)
    sin2 = jnp.concatenate([sin] * (n_qk // PAIR), axis=1)
    rq = _rope_lanes(z[:, o_rq:o_rq + n_qk], cos2, sin2)
    lo = (lax.broadcasted_iota(jnp.int32, rq.shape, 1) & (PAIR - 1)) < HEAD_DIM
    rq_ref[0, :, 0:n_qk] = jnp.where(lo, rq, 0.0).astype(BF16)
    rq_ref[0, :, n_qk:2 * n_qk] = jnp.where(lo, 0.0, rq).astype(BF16)
    rk_ref[0] = _rope_lanes(z[:, o_rk:o_rk + n_qk] * (RET_QK_DIM ** -0.5), cos2, sin2).astype(BF16)
    rv_ref[0] = z[:, o_rv:o_rg].astype(BF16)
    g_ref[0] = z[:, o_rg:o_xbc].astype(BF16)
    xbc_ref[0] = z[:, o_xbc:o_dt].astype(BF16)
    dt_ref[0] = _softplus(z[:, o_dt:o_dt + PAIR] + dtb_ref[...])


def _rec_project(X, nw, modc, w_bf16, cos_t, sin_t, dtb_row, n_lat_tiles):
    B, N, D = X.shape
    n_in = w_bf16.shape[1]
    row = lambda b, i: (b, i, 0)
    const2 = lambda b, i: (0, 0)
    widths = (2 * REC_SPLITS[0], REC_SPLITS[1], REC_SPLITS[2], REC_SPLITS[3] + REC_SPLITS[4],
              REC_SPLITS[5], PAIR)
    dtypes = (BF16, BF16, BF16, BF16, BF16, F32)
    return pl.pallas_call(
        functools.partial(_rec_project_kernel, n_lat=n_lat_tiles * ROW_TILE),
        grid=(B, N // PROJ_TILE),
        in_specs=[pl.BlockSpec((1, PROJ_TILE, D), row),
                  pl.BlockSpec((1, D), const2),
                  pl.BlockSpec((1, 2, 6, D), _MOD_BOTH),
                  pl.BlockSpec((D, n_in), const2, **_ONCE),
                  pl.BlockSpec((PROJ_TILE, PAIR), lambda b, i: (i, 0)),
                  pl.BlockSpec((PROJ_TILE, PAIR), lambda b, i: (i, 0)),
                  pl.BlockSpec((1, PAIR), const2)],
        out_specs=[pl.BlockSpec((1, PROJ_TILE, w), row) for w in widths],
        out_shape=[jax.ShapeDtypeStruct((B, N, w), dt) for w, dt in zip(widths, dtypes)],
        compiler_params=_cparams(("arbitrary", "arbitrary")),
        name="rec_project",
    )(X, nw.reshape(1, D), modc, w_bf16, cos_t, sin_t, dtb_row)


def _conv_kernel(x_ref, w_ref, b_ref, o_ref, *, n_lat):
    x = x_ref[0].astype(F32)
    n = x.shape[0]
    t = lax.broadcasted_iota(jnp.int32, x.shape, 0)
    first = (t == 0) | (t == n_lat)
    last = (t == n_lat - 1) | (t == n - 1)
    prev = jnp.where(first, 0.0, pltpu.roll(x, 1, 0))
    nxt = jnp.where(last, 0.0, pltpu.roll(x, n - 1, 0))
    y = prev * w_ref[0:1, :] + x * w_ref[1:2, :] + nxt * w_ref[2:3, :] + b_ref[...]
    o_ref[0] = _silu(y).astype(BF16)


def _ssd_conv(XBC, conv_w, conv_b, n_lat):
    B, N, C = XBC.shape
    tc = 256
    return pl.pallas_call(
        functools.partial(_conv_kernel, n_lat=n_lat),
        grid=(B, C // tc),
        in_specs=[pl.BlockSpec((1, N, tc), lambda b, j: (b, 0, j)),
                  pl.BlockSpec((conv_w.shape[0], tc), lambda b, j: (0, j)),
                  pl.BlockSpec((1, tc), lambda b, j: (0, j))],
        out_specs=pl.BlockSpec((1, N, tc), lambda b, j: (b, 0, j)),
        out_shape=jax.ShapeDtypeStruct((B, N, C), BF16),
        compiler_params=_cparams(("arbitrary", "arbitrary")),
        name="ssd_conv",
    )(XBC, conv_w, conv_b.reshape(1, C))


def _split3(x):
    hi = x.astype(BF16)
    r1 = x - hi.astype(F32)
    mid = r1.astype(BF16)
    lo = (r1 - mid.astype(F32)).astype(BF16)
    return hi, mid, lo


def _scan_kernel(rq_ref, rk_ref, rv_ref, a_ref, dt_ref, alog_ref, logit_ref, y_ref, hret, hssd, *, reverse):
    T = rq_ref.shape[1]
    d = 1 if reverse else 0
    step = pl.program_id(1)

    @pl.when(step == 0)
    def _():
        hret[...] = jnp.zeros_like(hret)
        hssd[...] = jnp.zeros_like(hssd)

    lane = lax.broadcasted_iota(jnp.int32, (1, PAIR), 1)
    a_neg = jnp.where(lane < N_SSD_DT, -jnp.exp(alog_ref[...]), 0.0)
    logit = logit_ref[...]
    log_gamma = jnp.where((lane >= RET_LANE0) & (lane < RET_LANE0 + 2 * RET_HEADS),
                          jnp.minimum(logit, 0.0) - jnp.log(1.0 + jnp.exp(-jnp.abs(logit))), 0.0)
    dt = dt_ref[0]
    la = dt * a_neg + log_gamma

    ti = lax.broadcasted_iota(jnp.int32, (T, T), 0)
    tj = lax.broadcasted_iota(jnp.int32, (T, T), 1)
    causal = (tj >= ti) if reverse else (tj <= ti)
    tri = jnp.where(causal, 1.0, 0.0).astype(BF16)
    hi, mid, lo3 = _split3(la)
    P = (jnp.dot(tri, hi, preferred_element_type=F32) + jnp.dot(tri, mid, preferred_element_type=F32)
         + jnp.dot(tri, lo3, preferred_element_type=F32))
    PT = P.T
    tot = P[0:1, :] if reverse else P[T - 1:T, :]
    E = jnp.exp(P)
    KD = jnp.exp(tot - P)
    ET = jnp.exp(tot)
    lo = lax.broadcasted_iota(jnp.int32, (T, PAIR), 1) < HEAD_DIM
    lo_row = lane < HEAD_DIM
    nt = (((1,), (1,)), ((), ()))
    tn = (((0,), (0,)), ((), ()))

    def decay_matrix(c):
        diff = jnp.minimum(P[:, c:c + 1] - PT[c:c + 1, :], 0.0)
        return jnp.where(causal, jnp.exp(diff), 0.0)

    n_qk = RET_HEADS * RET_QK_DIM
    for h in range(RET_HEADS):
        c = RET_LANE0 + d * RET_HEADS + h
        p, half = h // 2, h % 2
        qv = rq_ref[0, :, half * n_qk + p * PAIR: half * n_qk + (p + 1) * PAIR]
        kp = rk_ref[0, :, p * PAIR:(p + 1) * PAIR]
        v = rv_ref[0, :, h * RET_V_DIM:(h + 1) * RET_V_DIM]
        s = lax.dot_general(qv, kp, nt, preferred_element_type=F32)
        aw = (s * decay_matrix(c)).astype(BF16)
        y = jnp.dot(aw, v, preferred_element_type=F32)
        y = y + E[:, c:c + 1] * jnp.dot(qv, hret[h].astype(BF16), preferred_element_type=F32)
        y_ref[0, :, h * RET_V_DIM:(h + 1) * RET_V_DIM] = y.astype(y_ref.dtype)
        kdec = (kp.astype(F32) * KD[:, c:c + 1]).astype(BF16)
        hret[h] = hret[h] * ET[:, c:c + 1] + lax.dot_general(kdec, v, tn, preferred_element_type=F32)

    y0 = RET_HEADS * RET_V_DIM
    r_per_g = SSD_HEADS // SSD_GROUPS
    o_b = SSD_INNER
    o_c = SSD_INNER + SSD_GROUPS * SSD_STATE
    for g in range(SSD_GROUPS):
        cg = a_ref[0, :, o_c + g * SSD_STATE: o_c + (g + 1) * SSD_STATE]
        bg = a_ref[0, :, o_b + g * SSD_STATE: o_b + (g + 1) * SSD_STATE]
        s = lax.dot_general(cg, bg, nt, preferred_element_type=F32)
        ch = jnp.dot(cg, hssd[g].astype(BF16), preferred_element_type=F32)
        vdec, dec_rows = [], []
        for p in range(r_per_g // 2):
            pg = g * (r_per_g // 2) + p
            xs = a_ref[0, :, pg * PAIR:(pg + 1) * PAIR].astype(F32)
            ys, vds, cols = [], [], []
            for half in range(2):
                c = d * SSD_HEADS + 2 * pg + half
                aw = (s * decay_matrix(c)).astype(BF16)
                vh = xs * dt[:, c:c + 1]
                ys.append(jnp.dot(aw, vh.astype(BF16), preferred_element_type=F32))
                vds.append(vh * KD[:, c:c + 1])
                cols.append(c)
            e_pair = jnp.where(lo, E[:, cols[0]:cols[0] + 1], E[:, cols[1]:cols[1] + 1])
            y_ref[0, :, y0 + pg * PAIR: y0 + (pg + 1) * PAIR] = (
                jnp.where(lo, ys[0], ys[1]) + e_pair * ch[:, p * PAIR:(p + 1) * PAIR]).astype(y_ref.dtype)
            vdec.append(jnp.where(lo, vds[0], vds[1]).astype(BF16))
            dec_rows.append(jnp.where(lo_row, ET[:, cols[0]:cols[0] + 1], ET[:, cols[1]:cols[1] + 1]))
        hssd[g] = (hssd[g] * jnp.concatenate(dec_rows, axis=1)
                   + lax.dot_general(bg, jnp.concatenate(vdec, axis=1), tn, preferred_element_type=F32))


def _bidir_scan(RQ, RK, RV, A, DT, alog_row, logit_row, n_lat_tiles, reverse):
    B, N, _ = RQ.shape
    n_tiles = N // ROW_TILE

    def tile(i):
        lat = (n_lat_tiles - i) if reverse else (i - 1)
        return jnp.where(i == 0, n_tiles - 1, lat)

    row = lambda b, i: (b, tile(i), 0)
    const2 = lambda b, i: (0, 0)
    y_w = RET_HEADS * RET_V_DIM + SSD_INNER
    return pl.pallas_call(
        functools.partial(_scan_kernel, reverse=reverse),
        grid=(B, n_tiles),
        in_specs=[pl.BlockSpec((1, ROW_TILE, RQ.shape[-1]), row),
                  pl.BlockSpec((1, ROW_TILE, RK.shape[-1]), row),
                  pl.BlockSpec((1, ROW_TILE, RV.shape[-1]), row),
                  pl.BlockSpec((1, ROW_TILE, A.shape[-1]), row),
                  pl.BlockSpec((1, ROW_TILE, PAIR), row),
                  pl.BlockSpec((1, PAIR), const2),
                  pl.BlockSpec((1, PAIR), const2)],
        out_specs=pl.BlockSpec((1, ROW_TILE, y_w), row),
        out_shape=jax.ShapeDtypeStruct((B, N, y_w), BF16),
        scratch_shapes=[pltpu.VMEM((RET_HEADS, PAIR, RET_V_DIM), F32),
                        pltpu.VMEM((SSD_GROUPS, SSD_STATE, SSD_INNER // SSD_GROUPS), F32)],
        compiler_params=_cparams(("arbitrary", "arbitrary")),
        name="scan_bwd" if reverse else "scan_fwd",
    )(RQ, RK, RV, A, DT, alog_row, logit_row)


def _rec_combine_kernel(x_ref, yf_ref, yb_ref, g_ref, a_ref, mod_ref, dskip_ref, snw_ref, w_ref, o_ref, y_scr,
                        *, n_lat):
    n_ret = RET_HEADS * RET_V_DIM
    y = yf_ref[0].astype(F32) + yb_ref[0].astype(F32)
    gates = g_ref[0].astype(F32)
    for h in range(RET_HEADS):
        sl = slice(h * RET_V_DIM, (h + 1) * RET_V_DIM)
        r = y[:, sl]
        r = r * lax.rsqrt(jnp.mean(r * r, axis=-1, keepdims=True) + EPS)
        y_scr[:, sl] = (r * _silu(gates[:, sl])).astype(BF16)
    gw = SSD_INNER // SSD_GROUPS
    for g in range(SSD_GROUPS):
        sl = slice(g * gw, (g + 1) * gw)
        s = y[:, n_ret + g * gw: n_ret + (g + 1) * gw] + dskip_ref[:, sl] * a_ref[0, :, sl].astype(F32)
        s = s * _silu(gates[:, n_ret + g * gw: n_ret + (g + 1) * gw])
        s = s * lax.rsqrt(jnp.mean(s * s, axis=-1, keepdims=True) + EPS) * snw_ref[:, sl]
        y_scr[:, n_ret + g * gw: n_ret + (g + 1) * gw] = s.astype(BF16)
    gate = _row_mod(mod_ref, 2, x_ref.shape[1], n_lat)
    o_ref[0] = x_ref[0] + gate * jnp.dot(y_scr[...], w_ref[...], preferred_element_type=F32)


def _rec_combine(X, Yf, Yb, G, A, modc, dskip_row, snw_row, w_bf16, n_lat_tiles):
    B, N, D = X.shape
    row = lambda b, i: (b, i, 0)
    const2 = lambda b, i: (0, 0)
    return pl.pallas_call(
        functools.partial(_rec_combine_kernel, n_lat=n_lat_tiles * ROW_TILE),
        grid=(B, N // PROJ_TILE),
        in_specs=[pl.BlockSpec((1, PROJ_TILE, D), row),
                  pl.BlockSpec((1, PROJ_TILE, Yf.shape[-1]), row),
                  pl.BlockSpec((1, PROJ_TILE, Yb.shape[-1]), row),
                  pl.BlockSpec((1, PROJ_TILE, G.shape[-1]), row),
                  pl.BlockSpec((1, PROJ_TILE, SSD_INNER), row),
                  pl.BlockSpec((1, 2, 6, D), _MOD_BOTH),
                  pl.BlockSpec((1, SSD_INNER), const2),
                  pl.BlockSpec((1, SSD_INNER), const2),
                  pl.BlockSpec(w_bf16.shape, const2, **_ONCE)],
        out_specs=pl.BlockSpec((1, PROJ_TILE, D), row),
        out_shape=jax.ShapeDtypeStruct((B, N, D), F32),
        scratch_shapes=[pltpu.VMEM((PROJ_TILE, w_bf16.shape[0]), BF16)],
        compiler_params=_cparams(("arbitrary", "arbitrary")),
        name="rec_combine",
    )(X, Yf, Yb, G, A, modc, dskip_row, snw_row, w_bf16)


def _recurrent_layer(X, nw, modc, w_in, w_out, cos_t, sin_t, ret_decay_logit, conv_w, conv_b, dt_bias,
                     a_log, d_skip, ssd_norm_w, n_lat_tiles):
    D = X.shape[-1]
    w_pad = jnp.pad(w_in, ((0, 0), (0, REC_IN_PAD - w_in.shape[1]))).astype(BF16)
    pad_row = lambda v, at: jnp.pad(v.reshape(1, -1), ((0, 0), (at, PAIR - at - v.size)))
    RQ, RK, RV, G, XBC, DT = _rec_project(X, nw, modc, w_pad, cos_t, sin_t, pad_row(dt_bias, 0), n_lat_tiles)
    A = _ssd_conv(XBC, conv_w, conv_b, n_lat_tiles * ROW_TILE)
    alog_row, logit_row = pad_row(a_log, 0), pad_row(ret_decay_logit, RET_LANE0)
    Yf = _bidir_scan(RQ, RK, RV, A, DT, alog_row, logit_row, n_lat_tiles, False)
    Yb = _bidir_scan(RQ, RK, RV, A, DT, alog_row, logit_row, n_lat_tiles, True)
    return _rec_combine(X, Yf, Yb, G, A, modc, jnp.repeat(d_skip, SSD_HEAD_DIM).reshape(1, -1),
                        ssd_norm_w.reshape(1, -1), w_out.astype(BF16), n_lat_tiles)


SEL_CHUNK = 256
COMBINE_WINDOW = 64
BF16_ROWS = 16
FFN_ROW_CHUNK = 1024


def _router_kernel(x_ref, nw_ref, mod_ref, rw_ref, h_ref, aff_ref, *, n_lat):
    h = _modulated(x_ref[0], nw_ref[...], mod_ref, 3, 4, n_lat)
    h_hi = h.astype(BF16)
    h_ref[0] = h_hi
    h_lo = (h - h_hi.astype(F32)).astype(BF16)
    w = rw_ref[...]
    w_hi = w.astype(BF16)
    w_lo = (w - w_hi.astype(F32)).astype(BF16)
    logits = (jnp.dot(h_hi, w_hi, preferred_element_type=F32) + jnp.dot(h_lo, w_hi, preferred_element_type=F32)
              + jnp.dot(h_hi, w_lo, preferred_element_type=F32))
    logits = logits.T[0:aff_ref.shape[1], :]
    e = jnp.exp(logits - jnp.max(logits, axis=0, keepdims=True))
    aff_ref[0] = e / jnp.sum(e, axis=0, keepdims=True)


def _mod_router(X, nw, modc, router_w, n_lat, n_rows):
    B, _, D = X.shape
    E = router_w.shape[1]
    rw_pad = jnp.pad(router_w, ((0, 0), (0, PAIR - E)))
    tile = _row_tile(n_rows, PROJ_TILE)
    return pl.pallas_call(
        functools.partial(_router_kernel, n_lat=n_lat),
        grid=(B, n_rows // tile),
        in_specs=[pl.BlockSpec((1, tile, D), lambda b, i: (b, i, 0)),
                  pl.BlockSpec((1, D), lambda b, i: (0, 0)),
                  pl.BlockSpec((1, 2, 6, D), _MOD_BOTH),
                  pl.BlockSpec((D, PAIR), lambda b, i: (0, 0))],
        out_specs=[pl.BlockSpec((1, tile, D), lambda b, i: (b, i, 0)),
                   pl.BlockSpec((1, E, tile), lambda b, i: (b, 0, i))],
        out_shape=[jax.ShapeDtypeStruct((B, n_rows, D), BF16),
                   jax.ShapeDtypeStruct((B, E, n_rows), F32)],
        compiler_params=_cparams(("arbitrary", "arbitrary")),
        name="mod_router",
    )(X, nw.reshape(1, D), modc, rw_pad)


def _count(mask_f32):
    return jnp.sum(mask_f32, axis=-1, keepdims=True)


def _select_top(a, cap):
    E, n = a.shape
    v = pltpu.bitcast(a, jnp.int32)
    thr = jnp.zeros((E, 1), jnp.int32)
    for bit in range(30, -1, -1):
        cand = thr | (1 << bit)
        thr = jnp.where(_count(jnp.where(v >= cand, 1.0, 0.0)) >= cap, cand, thr)
    gt = v > thr
    eq = v == thr
    need = cap - _count(jnp.where(gt, 1.0, 0.0))
    idx = lax.broadcasted_iota(jnp.int32, (E, n), 1)
    last = jnp.zeros((E, 1), jnp.int32)
    for bit in range(n.bit_length() - 2, -1, -1):
        cand = last | (1 << bit)
        below = _count(jnp.where(eq, jnp.where(idx < cand, 1.0, 0.0), 0.0))
        last = jnp.where(below < need, cand, last)
    sel = jnp.where(gt, 1.0, jnp.where(eq, jnp.where(idx <= last, 1.0, 0.0), 0.0))
    si = lax.broadcasted_iota(jnp.int32, (SEL_CHUNK, SEL_CHUNK), 0)
    sj = lax.broadcasted_iota(jnp.int32, (SEL_CHUNK, SEL_CHUNK), 1)
    before = jnp.where(si < sj, 1.0, 0.0).astype(BF16)
    base = jnp.zeros((E, 1), F32)
    lane = lax.broadcasted_iota(jnp.int32, (E, PAIR), 1)
    starts = jnp.zeros((E, PAIR), F32)
    pos = []
    for k in range(n // SEL_CHUNK):
        sk = sel[:, k * SEL_CHUNK:(k + 1) * SEL_CHUNK]
        pos.append(jnp.dot(sk.astype(BF16), before, preferred_element_type=F32) + base)
        base = base + _count(sk)
        starts = jnp.where(lane == k + 1, base, starts)
    pos = jnp.concatenate(pos, axis=1) if len(pos) > 1 else pos[0]
    return jnp.where(sel > 0.0, pos, -1.0), jnp.where(sel > 0.0, a, 0.0), starts


def _token_sets(n_lat, n_rows):
    sets = [(0, n_lat, EC_CAPACITY * n_lat // N_EXPERTS)]
    if n_rows > n_lat:
        sets.append((n_lat, n_rows, EC_CAPACITY * (n_rows - n_lat) // N_EXPERTS))
    return sets


def _select_kernel(aff_ref, slot_ref, slot_t_ref, gate_t_ref, starts_ref, *, sets):
    E = aff_ref.shape[1]
    pad_rows = PAIR - E
    for n_set, (lo, hi, cap) in enumerate(sets):
        slot, gate, starts = _select_top(aff_ref[0, :, lo:hi], cap)
        if n_set == 0:
            starts_ref[0] = starts
        slot_ref[0, :, lo:hi] = slot
        slot_t_ref[0, lo:hi, :] = jnp.concatenate([slot, jnp.full((pad_rows, hi - lo), -1.0, F32)], axis=0).T
        gate_t_ref[0, lo:hi, :] = jnp.concatenate([gate, jnp.zeros((pad_rows, hi - lo), F32)], axis=0).T


def _ec_select(aff, sets):
    B, E, N = aff.shape
    return pl.pallas_call(
        functools.partial(_select_kernel, sets=sets),
        grid=(B,),
        in_specs=[pl.BlockSpec((1, E, N), lambda b: (b, 0, 0))],
        out_specs=[pl.BlockSpec((1, E, N), lambda b: (b, 0, 0)),
                   pl.BlockSpec((1, N, PAIR), lambda b: (b, 0, 0)),
                   pl.BlockSpec((1, N, PAIR), lambda b: (b, 0, 0)),
                   pl.BlockSpec((1, E, PAIR), lambda b: (b, 0, 0))],
        out_shape=[jax.ShapeDtypeStruct((B, E, N), F32),
                   jax.ShapeDtypeStruct((B, N, PAIR), F32),
                   jax.ShapeDtypeStruct((B, N, PAIR), F32),
                   jax.ShapeDtypeStruct((B, E, PAIR), F32)],
        compiler_params=_cparams(("arbitrary",)),
        name="ec_select",
    )(aff)


def _gather_kernel(starts_ref, slot_ref, h_ref, *refs, n_lat):
    x_refs, acc_ref = refs[:-1], refs[-1]
    xl_ref = x_refs[0]
    E, cap_lat, _ = xl_ref.shape
    tile = h_ref.shape[1]
    b, i = pl.program_id(0), pl.program_id(1)
    n_lat_tiles = n_lat // tile

    def picked(e, first_slot, n_slots):
        c = lax.broadcasted_iota(jnp.int32, (n_slots, tile), 0).astype(F32) + first_slot
        return jnp.where(slot_ref[0, e:e + 1, :] == c, 1.0, 0.0).astype(BF16)

    @pl.when(i == 0)
    def _():
        acc_ref[...] = jnp.zeros_like(acc_ref)

    @pl.when(i < n_lat_tiles)
    def _():
        def all_slots():
            for e in range(E):
                acc_ref[e] += jnp.dot(picked(e, 0.0, cap_lat), h_ref[0], preferred_element_type=F32)

        if cap_lat <= COMBINE_WINDOW:
            all_slots()
            return
        window_starts, fits = _slot_windows(starts_ref, b, i, E, cap_lat)

        @pl.when(fits)
        def _():
            onehot = jnp.concatenate([picked(e, window_starts[e].astype(F32), COMBINE_WINDOW)
                                      for e in range(E)], axis=0)
            x = jnp.dot(onehot, h_ref[0], preferred_element_type=F32)
            for e in range(E):
                rows = pl.ds(pl.multiple_of(window_starts[e], BF16_ROWS), COMBINE_WINDOW)
                acc_ref[e, rows, :] += x[e * COMBINE_WINDOW:(e + 1) * COMBINE_WINDOW, :]

        @pl.when(jnp.logical_not(fits))
        def _():
            all_slots()

    @pl.when(i == n_lat_tiles - 1)
    def _():
        xl_ref[...] = acc_ref[...].astype(BF16)

    if len(x_refs) > 1:
        xc_ref = x_refs[1]

        @pl.when(i >= n_lat_tiles)
        def _():
            for e in range(E):
                xc_ref[e] = jnp.dot(picked(e, 0.0, xc_ref.shape[1]), h_ref[0],
                                    preferred_element_type=F32).astype(BF16)


def _slot_windows(starts_ref, b, i, n_experts, cap):
    window_starts, fits = [], None
    for e in range(n_experts):
        first, end = starts_ref[b, e, i], starts_ref[b, e, i + 1]
        w = jnp.minimum(first - lax.rem(first, BF16_ROWS), cap - COMBINE_WINDOW)
        ok = end - w <= COMBINE_WINDOW
        fits = ok if fits is None else jnp.logical_and(fits, ok)
        window_starts.append(w)
    return window_starts, fits


def _ec_gather(slot, H, starts_i32, sets, n_lat):
    B, E, N = slot.shape
    D = H.shape[-1]
    tile = SEL_CHUNK
    cap_lat = sets[0][2]
    grid_spec = pltpu.PrefetchScalarGridSpec(
        num_scalar_prefetch=1,
        grid=(B, N // tile),
        in_specs=[pl.BlockSpec((1, E, tile), lambda b, i, s: (b, 0, i)),
                  pl.BlockSpec((1, tile, D), lambda b, i, s: (b, i, 0))],
        out_specs=[pl.BlockSpec((E, cap, D), lambda b, i, s: (0, b, 0)) for _, _, cap in sets],
        scratch_shapes=[pltpu.VMEM((E, cap_lat, D), F32)])
    return pl.pallas_call(
        functools.partial(_gather_kernel, n_lat=n_lat),
        grid_spec=grid_spec,
        out_shape=[jax.ShapeDtypeStruct((E, B * cap, D), BF16) for _, _, cap in sets],
        compiler_params=_cparams(("arbitrary", "arbitrary")),
        name="ec_gather",
    )(starts_i32, slot, H)


def _ffn_kernel(*refs, n_groups, row_chunk):
    x_refs = refs[:n_groups]
    wg_ref, wu_ref, wd_ref = refs[n_groups:n_groups + 3]
    o_refs = refs[n_groups + 3:2 * n_groups + 3]
    acc_refs = refs[2 * n_groups + 3:3 * n_groups + 3]
    wg_s, wu_s, wd_s = refs[3 * n_groups + 3:]
    f = pl.program_id(2)
    n_f = pl.num_programs(2)
    wg_s[...] = wg_ref[0, 0].astype(BF16)
    wu_s[...] = wu_ref[0, 0].astype(BF16)
    wd_s[...] = wd_ref[0, 0].astype(BF16)

    @pl.when(f == 0)
    def _():
        for acc_ref in acc_refs:
            acc_ref[...] = jnp.zeros_like(acc_ref)

    def chunk(x_ref, acc_ref, start, size):
        rows = pl.ds(start, size)
        xr = x_ref[0, rows, :]
        a = jnp.dot(xr, wg_s[...], preferred_element_type=F32)
        u = jnp.dot(xr, wu_s[...], preferred_element_type=F32)
        hm = (a * jax.nn.sigmoid(a) * u).astype(BF16)
        acc_ref[rows, :] += jnp.dot(hm, wd_s[...], preferred_element_type=F32)

    for x_ref, acc_ref in zip(x_refs, acc_refs):
        n_rows = x_ref.shape[1]
        for start in range(0, n_rows, row_chunk):
            chunk(x_ref, acc_ref, start, min(row_chunk, n_rows - start))

    @pl.when(f == n_f - 1)
    def _():
        for o_ref, acc_ref in zip(o_refs, acc_refs):
            o_ref[0] = acc_ref[...].astype(o_ref.dtype)


def _expert_ffn(xs, w_gate, w_up, w_down, layer):
    E, _, D = xs[0].shape
    F = w_gate.shape[-1]
    m_tiles = 2
    tms = [x.shape[1] // m_tiles for x in xs]
    n = len(xs)
    return pl.pallas_call(
        functools.partial(_ffn_kernel, n_groups=n, row_chunk=FFN_ROW_CHUNK),
        grid=(E, m_tiles, F // FF_TILE),
        in_specs=[pl.BlockSpec((1, tm, D), lambda e, m, f: (e, m, 0)) for tm in tms]
        + [pl.BlockSpec((1, 1, D, FF_TILE), lambda e, m, f: (layer, e, 0, f)),
           pl.BlockSpec((1, 1, D, FF_TILE), lambda e, m, f: (layer, e, 0, f)),
           pl.BlockSpec((1, 1, FF_TILE, D), lambda e, m, f: (layer, e, f, 0))],
        out_specs=[pl.BlockSpec((1, tm, D), lambda e, m, f: (e, m, 0)) for tm in tms],
        out_shape=[jax.ShapeDtypeStruct(x.shape, BF16) for x in xs],
        scratch_shapes=[pltpu.VMEM((tm, D), F32) for tm in tms]
        + [pltpu.VMEM((D, FF_TILE), BF16), pltpu.VMEM((D, FF_TILE), BF16), pltpu.VMEM((FF_TILE, D), BF16)],
        compiler_params=_cparams(("arbitrary", "arbitrary", "arbitrary")),
        name="expert_ffn",
    )(*xs, w_gate, w_up, w_down)


def _combine_kernel(starts_ref, x_ref, slot_t_ref, gate_t_ref, mod_ref, fnw_ref, *refs, n_lat, final_norm):
    y_refs, o_ref = refs[:-1], refs[-1]
    E = y_refs[0].shape[0]
    tile = x_ref.shape[1]
    b, i = pl.program_id(0), pl.program_id(1)

    def finish(acc, seg):
        out = x_ref[0] + mod_ref[0, seg, 5:6, :] * acc
        if final_norm:
            out = out * lax.rsqrt(jnp.mean(out * out, axis=-1, keepdims=True) + EPS) * fnw_ref[...]
        o_ref[0] = out

    def scatter_full(seg):
        y_ref = y_refs[seg]
        cap = y_ref.shape[1]
        c = lax.broadcasted_iota(jnp.int32, (tile, cap), 1).astype(F32)
        acc = jnp.zeros((tile, x_ref.shape[2]), F32)
        for e in range(E):
            onehot = jnp.where(slot_t_ref[0, :, e:e + 1] == c, 1.0, 0.0).astype(BF16)
            acc = acc + gate_t_ref[0, :, e:e + 1] * jnp.dot(onehot, y_ref[e], preferred_element_type=F32)
        finish(acc, seg)

    def scatter_windowed(window_starts):
        y_ref = y_refs[0]
        k = lax.broadcasted_iota(jnp.int32, (tile, COMBINE_WINDOW), 1).astype(F32)
        sel, rows = [], []
        for e in range(E):
            w = window_starts[e]
            rel = slot_t_ref[0, :, e:e + 1] - w.astype(F32)
            sel.append(jnp.where(rel == k, gate_t_ref[0, :, e:e + 1], 0.0).astype(BF16))
            rows.append(y_ref[e, pl.ds(pl.multiple_of(w, BF16_ROWS), COMBINE_WINDOW), :])
        acc = jnp.dot(jnp.concatenate(sel, axis=1), jnp.concatenate(rows, axis=0),
                      preferred_element_type=F32)
        finish(acc, 0)

    n_lat_tiles = n_lat // tile
    cap_lat = y_refs[0].shape[1]

    @pl.when(i < n_lat_tiles)
    def _():
        if cap_lat <= COMBINE_WINDOW:
            scatter_full(0)
            return
        window_starts, fits = _slot_windows(starts_ref, b, i, E, cap_lat)

        @pl.when(fits)
        def _():
            scatter_windowed(window_starts)

        @pl.when(jnp.logical_not(fits))
        def _():
            scatter_full(0)

    if len(y_refs) > 1:
        @pl.when(i >= n_lat_tiles)
        def _():
            scatter_full(1)


def _row_tile(n_rows, preferred):
    return preferred if n_rows % preferred == 0 else 2 * ROW_TILE


def _ec_combine(X, slot_t, gate_t, starts_i32, Ys, modc, sets, n_lat, final_norm_w):
    B, _, D = X.shape
    E = Ys[0].shape[0]
    n_rows = sets[-1][1]
    tile = SEL_CHUNK
    row = lambda b, i, s: (b, i, 0)
    final_norm = final_norm_w is not None
    fnw = final_norm_w.reshape(1, D) if final_norm else jnp.ones((1, D), F32)
    grid_spec = pltpu.PrefetchScalarGridSpec(
        num_scalar_prefetch=1,
        grid=(B, n_rows // tile),
        in_specs=[pl.BlockSpec((1, tile, D), row),
                  pl.BlockSpec((1, tile, PAIR), row),
                  pl.BlockSpec((1, tile, PAIR), row),
                  pl.BlockSpec((1, 2, 6, D), lambda b, i, s: (b, 0, 0, 0)),
                  pl.BlockSpec((1, D), lambda b, i, s: (0, 0))]
        + [pl.BlockSpec((E, cap, D), lambda b, i, s: (0, b, 0)) for _, _, cap in sets],
        out_specs=pl.BlockSpec((1, tile, D), row))
    return pl.pallas_call(
        functools.partial(_combine_kernel, n_lat=n_lat, final_norm=final_norm),
        grid_spec=grid_spec,
        out_shape=jax.ShapeDtypeStruct((B, n_rows, D), F32),
        compiler_params=_cparams(("arbitrary", "arbitrary")),
        name="ec_combine",
    )(starts_i32, X, slot_t, gate_t, modc, fnw, *Ys)


def _moe_layer(X, nw, modc, router_w, w_gate, w_up, w_down, layer, n_lat, n_rows, final_norm_w=None):
    sets = _token_sets(n_lat, n_rows)
    H, aff = _mod_router(X, nw, modc, router_w, n_lat, n_rows)
    slot, slot_t, gate_t, starts = _ec_select(aff, sets)
    starts_i32 = starts[:, :, :n_lat // SEL_CHUNK + 1].astype(jnp.int32)
    Xs = _ec_gather(slot, H, starts_i32, sets, n_lat)
    Ys = _expert_ffn(Xs, w_gate, w_up, w_down, layer)
    return _ec_combine(X, slot_t, gate_t, starts_i32, Ys, modc, sets, n_lat, final_norm_w)


def _rope_tables(n):
    t = jnp.arange(n)
    row = (t // GRID_W).astype(F32)
    col = (t % GRID_W).astype(F32)
    n_freq = HEAD_DIM // 4
    inv = ROPE_THETA ** (-jnp.arange(n_freq, dtype=F32) / n_freq)
    ang = jnp.concatenate([row[:, None] * inv, col[:, None] * inv], axis=-1)
    return jnp.cos(ang), jnp.sin(ang)


def kernel(x, c, ctx, c_ctx, ada_w, ada_b, norm1_w, norm2_w, att_w_in, att_w_out, att_q_norm_w,
           att_k_norm_w, diff_lambda, diff_norm_w, rec_w_in, rec_w_out, ret_decay_logit, ssd_conv_w,
           ssd_conv_b, ssd_dt_bias, ssd_a_log, ssd_d_skip, ssd_norm_w, router_w, expert_w_gate,
           expert_w_up, expert_w_down, final_norm_w):
    B, n_lat, D = x.shape
    n_ctx = ctx.shape[1]
    n_lat_tiles = n_lat // ROW_TILE
    cos, sin = _rope_tables(n_lat)
    reps = PAIR // cos.shape[1]
    cos_t = jnp.concatenate([jnp.tile(cos, (1, reps)), jnp.ones((n_ctx, PAIR), F32)], axis=0)
    sin_t = jnp.concatenate([jnp.tile(jnp.concatenate([-sin, sin], axis=1), (1, reps // 2)),
                             jnp.zeros((n_ctx, PAIR), F32)], axis=0)
    gmat = jnp.kron(jnp.eye(GQA_HEADS, dtype=F32), jnp.ones((HEAD_DIM, HEAD_DIM), F32)).astype(BF16)

    n_cond = 24
    cvec = jnp.concatenate([c, c_ctx[None, :], jnp.zeros((n_cond - B - 1, D), F32)], axis=0)
    mods = _ada_modulation(cvec, ada_w, ada_b)

    X = jnp.concatenate([x, ctx], axis=1)
    for layer in range(DEPTH):
        i = layer // 2
        need_ctx = layer < DEPTH - 1
        mod_lat = mods[layer, :B].reshape(B, 1, 6, D)
        mod_ctx = jnp.broadcast_to(mods[layer, B].reshape(1, 1, 6, D), (B, 1, 6, D))
        modc = jnp.concatenate([mod_lat, mod_ctx], axis=1)

        if layer % 2 == 0:
            lambda_init = 0.8 - 0.6 * math.exp(-0.3 * layer)
            Q, K, V = _att_project(X, norm1_w[layer], modc, att_w_in[i].astype(BF16), cos_t, sin_t, gmat,
                                   jnp.tile(att_q_norm_w[i], GQA_HEADS).reshape(1, -1),
                                   jnp.tile(att_k_norm_w[i], GQA_KV_HEADS).reshape(1, -1), n_lat_tiles)
            X = _attention(X, Q, K, V, modc, diff_lambda[i], diff_norm_w[i], att_w_out[i].astype(BF16),
                           lambda_init, n_lat_tiles)
        else:
            X = _recurrent_layer(X, norm1_w[layer], modc, rec_w_in[i], rec_w_out[i], cos_t, sin_t,
                                 ret_decay_logit[i], ssd_conv_w[i], ssd_conv_b[i], ssd_dt_bias[i],
                                 ssd_a_log[i], ssd_d_skip[i], ssd_norm_w[i], n_lat_tiles)

        X = _moe_layer(X, norm2_w[layer], modc, router_w[layer], expert_w_gate, expert_w_up, expert_w_down,
                       layer, n_lat, n_lat + n_ctx if need_ctx else n_lat,
                       None if need_ctx else final_norm_w)

    return X
```

```python
import functools
import math

import jax
import jax.numpy as jnp
import numpy as np
from jax import lax
from jax.experimental import pallas as pl
from jax.experimental.pallas import tpu as pltpu

D_MODEL = 1024
DEPTH = 4
GRID_W = 64
HEAD_DIM = 64
ROPE_THETA = 10000.0
EPS = 1e-6
HALF_W = D_MODEL // 2
GQA_HEADS = HALF_W // HEAD_DIM
GQA_KV_HEADS = GQA_HEADS // 4
DIFF_HEADS = HALF_W // (2 * HEAD_DIM)
RET_HEADS = HALF_W // (2 * HEAD_DIM)
RET_QK_DIM = HEAD_DIM
RET_V_DIM = 2 * HEAD_DIM
SSD_HEAD_DIM = HEAD_DIM
SSD_HEADS = HALF_W // SSD_HEAD_DIM
SSD_GROUPS = 2
SSD_STATE = 128
SSD_INNER = SSD_HEADS * SSD_HEAD_DIM
SSD_XBC = SSD_INNER + 2 * SSD_GROUPS * SSD_STATE
N_EXPERTS = 16
EC_CAPACITY = 2
EXPERT_FF = ((8 * D_MODEL // 3 + 255) // 256) * 256

ATT_SPLITS = (GQA_HEADS * HEAD_DIM, GQA_KV_HEADS * HEAD_DIM, GQA_KV_HEADS * HEAD_DIM,
              DIFF_HEADS * 2 * HEAD_DIM, DIFF_HEADS * 2 * HEAD_DIM, DIFF_HEADS * 2 * HEAD_DIM)
REC_SPLITS = (RET_HEADS * RET_QK_DIM, RET_HEADS * RET_QK_DIM, RET_HEADS * RET_V_DIM,
              RET_HEADS * RET_V_DIM, SSD_INNER, SSD_XBC, 2 * SSD_HEADS)

ROW_TILE = 256
PROJ_TILE = 768
FF_TILE = 256
VMEM_LIMIT = 56 * 1024 * 1024
BF16 = jnp.bfloat16
F32 = jnp.float32


def _cparams(sem):
    return pltpu.CompilerParams(dimension_semantics=sem, vmem_limit_bytes=VMEM_LIMIT)


def _ada_kernel(c_ref, w_ref, b_ref, o_ref):
    c = c_ref[...]
    s = c * jax.nn.sigmoid(c)
    o_ref[0] = jnp.dot(s, w_ref[0], precision=lax.Precision.HIGHEST,
                       preferred_element_type=F32) + b_ref[0]


def _ada_modulation(cvec, ada_w, ada_b):
    R, D = cvec.shape
    n_out = ada_w.shape[-1]
    tn = 1536
    return pl.pallas_call(
        _ada_kernel,
        grid=(DEPTH, n_out // tn),
        in_specs=[pl.BlockSpec((R, D), lambda l, j: (0, 0)),
                  pl.BlockSpec((1, D, tn), lambda l, j: (l, 0, j)),
                  pl.BlockSpec((1, 1, tn), lambda l, j: (l, 0, j))],
        out_specs=pl.BlockSpec((1, R, tn), lambda l, j: (l, 0, j)),
        out_shape=jax.ShapeDtypeStruct((DEPTH, R, n_out), F32),
        compiler_params=_cparams(("arbitrary", "arbitrary")),
        name="ada_modulation",
    )(cvec, ada_w, ada_b.reshape(DEPTH, 1, n_out))


def _row_mod(mod_ref, idx, n_rows, n_lat):
    row = pl.program_id(1) * n_rows + lax.broadcasted_iota(jnp.int32, (n_rows, 1), 0)
    return jnp.where(row < n_lat, mod_ref[0, 0, idx:idx + 1, :], mod_ref[0, 1, idx:idx + 1, :])


def _modulated(x, nw, mod_ref, shift_idx, scale_idx, n_lat):
    n_rows = x.shape[0]
    ms = jnp.mean(x * x, axis=-1, keepdims=True)
    y = x * lax.rsqrt(ms + EPS) * nw
    return (y * (1.0 + _row_mod(mod_ref, scale_idx, n_rows, n_lat))
            + _row_mod(mod_ref, shift_idx, n_rows, n_lat))


_MOD_BOTH = lambda b, i: (b, 0, 0, 0)
_ONCE = dict(pipeline_mode=pl.Buffered(1))


PAIR = 2 * HEAD_DIM
N_PAIR_GROUPS = 2 * HALF_W // PAIR
ATT_KV_W = GQA_KV_HEADS * PAIR + DIFF_HEADS * PAIR


def _head_sumsq(x, g):
    x2 = x * x
    hi = x2.astype(BF16)
    lo = (x2 - hi.astype(F32)).astype(BF16)
    return jnp.dot(hi, g, preferred_element_type=F32) + jnp.dot(lo, g, preferred_element_type=F32)


def _rope_lanes(x, cos, sin_signed):
    W = x.shape[-1]
    half = HEAD_DIM // 2
    lane = lax.broadcasted_iota(jnp.int32, x.shape, 1) & (HEAD_DIM - 1)
    partner = jnp.where(lane < half, pltpu.roll(x, W - half, 1), pltpu.roll(x, half, 1))
    return x * cos + partner * sin_signed


def _att_project_kernel(x_ref, nw_ref, mod_ref, w_ref, cos_ref, sin_ref, g_ref, qw_ref, kw_ref,
                        q_ref, k_ref, v_ref, *, n_lat):
    h = _modulated(x_ref[0], nw_ref[...], mod_ref, 0, 1, n_lat)
    z = jnp.dot(h.astype(BF16), w_ref[...], preferred_element_type=F32)
    o_aq, o_ak, o_av, o_bq, o_bk, o_bv = np.cumsum((0,) + ATT_SPLITS[:-1]).tolist()
    n_aq, n_ak, n_bq = ATT_SPLITS[0], ATT_SPLITS[1], ATT_SPLITS[3]
    cos, sin = cos_ref[...], sin_ref[...]
    cos4 = jnp.concatenate([cos] * (n_aq // PAIR), axis=1)
    sin4 = jnp.concatenate([sin] * (n_aq // PAIR), axis=1)
    inv_d = 1.0 / HEAD_DIM
    q_scale = HEAD_DIM ** -0.5 * math.log2(math.e)

    aq = z[:, o_aq:o_aq + n_aq]
    aq = aq * lax.rsqrt(_head_sumsq(aq, g_ref[...]) * inv_d + EPS) * qw_ref[...]
    aq = _rope_lanes(aq, cos4, sin4) * q_scale
    bq = _rope_lanes(z[:, o_bq:o_bq + n_bq], cos4, sin4) * q_scale
    q = jnp.concatenate([aq, bq], axis=1)
    lo = (lax.broadcasted_iota(jnp.int32, q.shape, 1) & (PAIR - 1)) < HEAD_DIM
    q_ref[0, :, 0:q.shape[1]] = jnp.where(lo, q, 0.0).astype(BF16)
    q_ref[0, :, q.shape[1]:2 * q.shape[1]] = jnp.where(lo, 0.0, q).astype(BF16)

    ak = z[:, o_ak:o_ak + n_ak]
    ak = ak * lax.rsqrt(_head_sumsq(ak, g_ref[0:n_ak, 0:n_ak]) * inv_d + EPS) * kw_ref[...]
    ak = _rope_lanes(ak, cos, sin)
    av = z[:, o_av:o_av + n_ak]
    lo_kv = lax.broadcasted_iota(jnp.int32, ak.shape, 1) < HEAD_DIM
    ak_sw, av_sw = pltpu.roll(ak, HEAD_DIM, 1), pltpu.roll(av, HEAD_DIM, 1)
    k_ref[0, :, 0:PAIR] = jnp.where(lo_kv, ak, ak_sw).astype(BF16)
    k_ref[0, :, PAIR:2 * PAIR] = jnp.where(lo_kv, ak_sw, ak).astype(BF16)
    v_ref[0, :, 0:PAIR] = jnp.where(lo_kv, av, av_sw).astype(BF16)
    v_ref[0, :, PAIR:2 * PAIR] = jnp.where(lo_kv, av_sw, av).astype(BF16)
    k_ref[0, :, 2 * PAIR:] = _rope_lanes(z[:, o_bk:o_bk + n_bq], cos4, sin4).astype(BF16)
    v_ref[0, :, 2 * PAIR:] = z[:, o_bv:o_bv + n_bq].astype(BF16)


def _att_project(X, nw, modc, w_bf16, cos_t, sin_t, gmat, qw_t, kw_t, n_lat_tiles):
    B, N, D = X.shape
    n_in = w_bf16.shape[1]
    q_w = 2 * N_PAIR_GROUPS * PAIR
    row = lambda b, i: (b, i, 0)
    const2 = lambda b, i: (0, 0)
    return pl.pallas_call(
        functools.partial(_att_project_kernel, n_lat=n_lat_tiles * ROW_TILE),
        grid=(B, N // PROJ_TILE),
        in_specs=[pl.BlockSpec((1, PROJ_TILE, D), row),
                  pl.BlockSpec((1, D), const2),
                  pl.BlockSpec((1, 2, 6, D), _MOD_BOTH),
                  pl.BlockSpec((D, n_in), const2, **_ONCE),
                  pl.BlockSpec((PROJ_TILE, PAIR), lambda b, i: (i, 0)),
                  pl.BlockSpec((PROJ_TILE, PAIR), lambda b, i: (i, 0)),
                  pl.BlockSpec(gmat.shape, const2),
                  pl.BlockSpec(qw_t.shape, const2),
                  pl.BlockSpec(kw_t.shape, const2)],
        out_specs=[pl.BlockSpec((1, PROJ_TILE, q_w), row),
                   pl.BlockSpec((1, PROJ_TILE, ATT_KV_W), row),
                   pl.BlockSpec((1, PROJ_TILE, ATT_KV_W), row)],
        out_shape=[jax.ShapeDtypeStruct((B, N, q_w), BF16),
                   jax.ShapeDtypeStruct((B, N, ATT_KV_W), BF16),
                   jax.ShapeDtypeStruct((B, N, ATT_KV_W), BF16)],
        compiler_params=_cparams(("arbitrary", "arbitrary")),
        name="att_project",
    )(X, nw.reshape(1, D), modc, w_bf16, cos_t, sin_t, gmat, qw_t, kw_t)


def _attention_kernel(x_ref, q_ref, k_ref, v_ref, mod_ref, lam_ref, dnw_ref, w_ref, o_ref,
                      s_scr, p_scr, y_scr, *, lambda_init, n_lat_tiles):
    n_lat = n_lat_tiles * ROW_TILE
    n_all = k_ref.shape[1]
    q_half = N_PAIR_GROUPS * PAIR
    lam = lam_ref[...]
    lam_val = (jnp.exp(jnp.sum(lam[0:1] * lam[1:2], axis=-1, keepdims=True))
               - jnp.exp(jnp.sum(lam[2:3] * lam[3:4], axis=-1, keepdims=True)) + lambda_init)

    def attend(k0, nk):
        keys = slice(k0, k0 + nk)

        def pair(pg):
            qv = jnp.concatenate([q_ref[0, :, h * q_half + pg * PAIR: h * q_half + (pg + 1) * PAIR]
                                  for h in range(2)], axis=0)
            kd = k_ref[0, keys, _kv_lane(pg): _kv_lane(pg) + PAIR]
            vd = v_ref[0, keys, _kv_lane(pg): _kv_lane(pg) + PAIR]
            s_scr[:, 0:nk] = lax.dot_general(qv, kd, (((1,), (1,)), ((), ())),
                                             preferred_element_type=F32)
            s = s_scr[:, 0:nk]
            p_scr[:, 0:nk] = jnp.exp2(s - jnp.max(s, axis=-1, keepdims=True)).astype(BF16)
            p = p_scr[:, 0:nk]
            o = jnp.dot(p, vd, preferred_element_type=F32)
            o = o * (1.0 / jnp.sum(p.astype(F32), axis=-1, keepdims=True))
            return o[0:ROW_TILE], o[ROW_TILE:2 * ROW_TILE]

        lo = lax.broadcasted_iota(jnp.int32, (ROW_TILE, PAIR), 1) < HEAD_DIM
        for pg in range(2 * GQA_KV_HEADS):
            o_even, o_odd = pair(pg)
            y_scr[:, pg * PAIR:(pg + 1) * PAIR] = jnp.where(lo, o_even, o_odd).astype(BF16)
        for pg in range(2 * GQA_KV_HEADS, N_PAIR_GROUPS):
            o_0, o_1 = pair(pg)
            od = o_0 - lam_val * o_1
            od = od * lax.rsqrt(jnp.mean(od * od, axis=-1, keepdims=True) + EPS)
            od = od * dnw_ref[...] * (1.0 - lambda_init)
            y_scr[:, pg * PAIR:(pg + 1) * PAIR] = od.astype(BF16)
        y = jnp.dot(y_scr[...], w_ref[...], preferred_element_type=F32)
        o_ref[0] = x_ref[0] + mod_ref[0, 0, 2:3, :] * y

    i = pl.program_id(1)

    @pl.when(i < n_lat_tiles)
    def _():
        attend(0, n_all)

    @pl.when(i >= n_lat_tiles)
    def _():
        attend(n_lat, n_all - n_lat)


def _kv_lane(pg):
    n_gqa_pairs = 2 * GQA_KV_HEADS
    if pg < n_gqa_pairs:
        return (pg // 2) * PAIR
    return (GQA_KV_HEADS + pg - n_gqa_pairs) * PAIR


def _attention(X, Q, K, V, modc, lam, dnw, w_bf16, lambda_init, n_lat_tiles):
    B, N, D = X.shape
    row = lambda b, i: (b, i, 0)
    whole = lambda b, i: (b, 0, 0)
    const2 = lambda b, i: (0, 0)
    return pl.pallas_call(
        functools.partial(_attention_kernel, lambda_init=lambda_init, n_lat_tiles=n_lat_tiles),
        grid=(B, N // ROW_TILE),
        in_specs=[pl.BlockSpec((1, ROW_TILE, D), row),
                  pl.BlockSpec((1, ROW_TILE, Q.shape[-1]), row),
                  pl.BlockSpec((1, N, K.shape[-1]), whole, pipeline_mode=pl.Buffered(1)),
                  pl.BlockSpec((1, N, V.shape[-1]), whole, pipeline_mode=pl.Buffered(1)),
                  pl.BlockSpec((1, 1, 6, D), lambda b, i: (b, i // n_lat_tiles, 0, 0)),
                  pl.BlockSpec(lam.shape, const2),
                  pl.BlockSpec((1, PAIR), const2),
                  pl.BlockSpec(w_bf16.shape, const2, pipeline_mode=pl.Buffered(1))],
        out_specs=pl.BlockSpec((1, ROW_TILE, D), row),
        out_shape=jax.ShapeDtypeStruct((B, N, D), F32),
        scratch_shapes=[pltpu.VMEM((2 * ROW_TILE, N), F32), pltpu.VMEM((2 * ROW_TILE, N), BF16),
                        pltpu.VMEM((ROW_TILE, w_bf16.shape[0]), BF16)],
        compiler_params=_cparams(("arbitrary", "arbitrary")),
        name="attention",
    )(X, Q, K, V, modc, lam, dnw.reshape(1, PAIR), w_bf16)


REC_IN_PAD = 3200
N_SSD_DT = 2 * SSD_HEADS
RET_LANE0 = N_SSD_DT
SCAN_SAMPLES = 2


def _softplus(x):
    return jnp.maximum(x, 0.0) + jnp.log(1.0 + jnp.exp(-jnp.abs(x)))


def _silu(x):
    return x * jax.nn.sigmoid(x)


def _rec_project_kernel(x_ref, nw_ref, mod_ref, w_ref, cos_ref, sin_ref, dtb_ref,
                        rq_ref, rk_ref, rv_ref, g_ref, xbc_ref, dt_ref, *, n_lat):
    h = _modulated(x_ref[0], nw_ref[...], mod_ref, 0, 1, n_lat)
    z = jnp.dot(h.astype(BF16), w_ref[...], preferred_element_type=F32)
    o_rq, o_rk, o_rv, o_rg, o_z, o_xbc, o_dt = np.cumsum((0,) + REC_SPLITS[:-1]).tolist()
    n_qk = REC_SPLITS[0]
    cos, sin = cos_ref[...], sin_ref[...]
    cos2 = jnp.concatenate([cos] * (n_qk // PAIR), axis=1)
    sin2 = jnp.concatenate([sin] * (n_qk // PAIR), axis=1)
    rq = _rope_lanes(z[:, o_rq:o_rq + n_qk], cos2, sin2)
    lo = (lax.broadcasted_iota(jnp.int32, rq.shape, 1) & (PAIR - 1)) < HEAD_DIM
    rq_ref[0, :, 0:n_qk] = jnp.where(lo, rq, 0.0).astype(BF16)
    rq_ref[0, :, n_qk:2 * n_qk] = jnp.where(lo, 0.0, rq).astype(BF16)
    rk_ref[0] = _rope_lanes(z[:, o_rk:o_rk + n_qk] * (RET_QK_DIM ** -0.5), cos2, sin2).astype(BF16)
    rv_ref[0] = z[:, o_rv:o_rg].astype(BF16)
    g_ref[0] = z[:, o_rg:o_xbc].astype(BF16)
    xbc_ref[0] = z[:, o_xbc:o_dt].astype(BF16)
    dt_ref[0] = _softplus(z[:, o_dt:o_dt + PAIR] + dtb_ref[...])


def _rec_project(X, nw, modc, w_bf16, cos_t, sin_t, dtb_row, n_lat_tiles):
    B, N, D = X.shape
    n_in = w_bf16.shape[1]
    row = lambda b, i: (b, i, 0)
    const2 = lambda b, i: (0, 0)
    widths = (2 * REC_SPLITS[0], REC_SPLITS[1], REC_SPLITS[2], REC_SPLITS[3] + REC_SPLITS[4],
              REC_SPLITS[5], PAIR)
    dtypes = (BF16, BF16, BF16, BF16, BF16, F32)
    return pl.pallas_call(
        functools.partial(_rec_project_kernel, n_lat=n_lat_tiles * ROW_TILE),
        grid=(B, N // PROJ_TILE),
        in_specs=[pl.BlockSpec((1, PROJ_TILE, D), row),
                  pl.BlockSpec((1, D), const2),
                  pl.BlockSpec((1, 2, 6, D), _MOD_BOTH),
                  pl.BlockSpec((D, n_in), const2, **_ONCE),
                  pl.BlockSpec((PROJ_TILE, PAIR), lambda b, i: (i, 0)),
                  pl.BlockSpec((PROJ_TILE, PAIR), lambda b, i: (i, 0)),
                  pl.BlockSpec((1, PAIR), const2)],
        out_specs=[pl.BlockSpec((1, PROJ_TILE, w), row) for w in widths],
        out_shape=[jax.ShapeDtypeStruct((B, N, w), dt) for w, dt in zip(widths, dtypes)],
        compiler_params=_cparams(("arbitrary", "arbitrary")),
        name="rec_project",
    )(X, nw.reshape(1, D), modc, w_bf16, cos_t, sin_t, dtb_row)


def _conv_kernel(x_ref, w_ref, b_ref, o_ref, *, n_lat):
    x = x_ref[0].astype(F32)
    n = x.shape[0]
    t = lax.broadcasted_iota(jnp.int32, x.shape, 0)
    first = (t == 0) | (t == n_lat)
    last = (t == n_lat - 1) | (t == n - 1)
    prev = jnp.where(first, 0.0, pltpu.roll(x, 1, 0))
    nxt = jnp.where(last, 0.0, pltpu.roll(x, n - 1, 0))
    y = prev * w_ref[0:1, :] + x * w_ref[1:2, :] + nxt * w_ref[2:3, :] + b_ref[...]
    o_ref[0] = _silu(y).astype(BF16)


def _ssd_conv(XBC, conv_w, conv_b, n_lat):
    B, N, C = XBC.shape
    tc = 256
    return pl.pallas_call(
        functools.partial(_conv_kernel, n_lat=n_lat),
        grid=(B, C // tc),
        in_specs=[pl.BlockSpec((1, N, tc), lambda b, j: (b, 0, j)),
                  pl.BlockSpec((conv_w.shape[0], tc), lambda b, j: (0, j)),
                  pl.BlockSpec((1, tc), lambda b, j: (0, j))],
        out_specs=pl.BlockSpec((1, N, tc), lambda b, j: (b, 0, j)),
        out_shape=jax.ShapeDtypeStruct((B, N, C), BF16),
        compiler_params=_cparams(("arbitrary", "arbitrary")),
        name="ssd_conv",
    )(XBC, conv_w, conv_b.reshape(1, C))


def _split3(x):
    hi = x.astype(BF16)
    r1 = x - hi.astype(F32)
    mid = r1.astype(BF16)
    lo = (r1 - mid.astype(F32)).astype(BF16)
    return hi, mid, lo


def _scan_kernel(rq_ref, rk_ref, rv_ref, a_ref, dt_ref, alog_ref, logit_ref, y_ref, hret, hssd, *, reverse):
    T = rq_ref.shape[1]

    @pl.when(pl.program_id(1) == 0)
    def _():
        hret[...] = jnp.zeros_like(hret)
        hssd[...] = jnp.zeros_like(hssd)

    lane = lax.broadcasted_iota(jnp.int32, (1, PAIR), 1)
    a_neg = jnp.where(lane < N_SSD_DT, -jnp.exp(alog_ref[...]), 0.0)
    logit = logit_ref[...]
    log_gamma = jnp.where((lane >= RET_LANE0) & (lane < RET_LANE0 + 2 * RET_HEADS),
                          jnp.minimum(logit, 0.0) - jnp.log(1.0 + jnp.exp(-jnp.abs(logit))), 0.0)
    ti = lax.broadcasted_iota(jnp.int32, (T, T), 0)
    tj = lax.broadcasted_iota(jnp.int32, (T, T), 1)
    causal = (tj >= ti) if reverse else (tj <= ti)
    tri = jnp.where(causal, 1.0, 0.0).astype(BF16)
    for smp in range(rq_ref.shape[0]):
        _scan_tile(smp, rq_ref, rk_ref, rv_ref, a_ref, dt_ref, y_ref, hret, hssd, a_neg, log_gamma, causal,
                   tri, lane, reverse)


def _scan_tile(smp, rq_ref, rk_ref, rv_ref, a_ref, dt_ref, y_ref, hret, hssd, a_neg, log_gamma, causal, tri,
               lane, reverse):
    T = rq_ref.shape[1]
    d = 1 if reverse else 0
    dt = dt_ref[smp]
    la = dt * a_neg + log_gamma
    hi, mid, lo3 = _split3(la)
    P = (jnp.dot(tri, hi, preferred_element_type=F32) + jnp.dot(tri, mid, preferred_element_type=F32)
         + jnp.dot(tri, lo3, preferred_element_type=F32))
    PT = P.T
    tot = P[0:1, :] if reverse else P[T - 1:T, :]
    E = jnp.exp(P)
    KD = jnp.exp(tot - P)
    ET = jnp.exp(tot)
    lo = lax.broadcasted_iota(jnp.int32, (T, PAIR), 1) < HEAD_DIM
    lo_row = lane < HEAD_DIM
    nt = (((1,), (1,)), ((), ()))
    tn = (((0,), (0,)), ((), ()))

    def decay_matrix(c):
        diff = jnp.minimum(P[:, c:c + 1] - PT[c:c + 1, :], 0.0)
        return jnp.where(causal, jnp.exp(diff), 0.0)

    n_qk = RET_HEADS * RET_QK_DIM
    for h in range(RET_HEADS):
        c = RET_LANE0 + d * RET_HEADS + h
        p, half = h // 2, h % 2
        qv = rq_ref[smp, :, half * n_qk + p * PAIR: half * n_qk + (p + 1) * PAIR]
        kp = rk_ref[smp, :, p * PAIR:(p + 1) * PAIR]
        v = rv_ref[smp, :, h * RET_V_DIM:(h + 1) * RET_V_DIM]
        s = lax.dot_general(qv, kp, nt, preferred_element_type=F32)
        aw = (s * decay_matrix(c)).astype(BF16)
        y = jnp.dot(aw, v, preferred_element_type=F32)
        y = y + E[:, c:c + 1] * jnp.dot(qv, hret[smp, h].astype(BF16), preferred_element_type=F32)
        y_ref[smp, :, h * RET_V_DIM:(h + 1) * RET_V_DIM] = y.astype(y_ref.dtype)
        kdec = (kp.astype(F32) * KD[:, c:c + 1]).astype(BF16)
        hret[smp, h] = hret[smp, h] * ET[:, c:c + 1] + lax.dot_general(kdec, v, tn, preferred_element_type=F32)

    y0 = RET_HEADS * RET_V_DIM
    r_per_g = SSD_HEADS // SSD_GROUPS
    o_b = SSD_INNER
    o_c = SSD_INNER + SSD_GROUPS * SSD_STATE
    for g in range(SSD_GROUPS):
        cg = a_ref[smp, :, o_c + g * SSD_STATE: o_c + (g + 1) * SSD_STATE]
        bg = a_ref[smp, :, o_b + g * SSD_STATE: o_b + (g + 1) * SSD_STATE]
        s = lax.dot_general(cg, bg, nt, preferred_element_type=F32)
        ch = jnp.dot(cg, hssd[smp, g].astype(BF16), preferred_element_type=F32)
        vdec, dec_rows = [], []
        for p in range(r_per_g // 2):
            pg = g * (r_per_g // 2) + p
            xs = a_ref[smp, :, pg * PAIR:(pg + 1) * PAIR].astype(F32)
            ys, vds, cols = [], [], []
            for half in range(2):
                c = d * SSD_HEADS + 2 * pg + half
                aw = (s * decay_matrix(c)).astype(BF16)
                vh = xs * dt[:, c:c + 1]
                ys.append(jnp.dot(aw, vh.astype(BF16), preferred_element_type=F32))
                vds.append(vh * KD[:, c:c + 1])
                cols.append(c)
            e_pair = jnp.where(lo, E[:, cols[0]:cols[0] + 1], E[:, cols[1]:cols[1] + 1])
            y_ref[smp, :, y0 + pg * PAIR: y0 + (pg + 1) * PAIR] = (
                jnp.where(lo, ys[0], ys[1]) + e_pair * ch[:, p * PAIR:(p + 1) * PAIR]).astype(y_ref.dtype)
            vdec.append(jnp.where(lo, vds[0], vds[1]).astype(BF16))
            dec_rows.append(jnp.where(lo_row, ET[:, cols[0]:cols[0] + 1], ET[:, cols[1]:cols[1] + 1]))
        hssd[smp, g] = (hssd[smp, g] * jnp.concatenate(dec_rows, axis=1)
                   + lax.dot_general(bg, jnp.concatenate(vdec, axis=1), tn, preferred_element_type=F32))


def _bidir_scan(RQ, RK, RV, A, DT, alog_row, logit_row, n_lat_tiles, reverse):
    B, N, _ = RQ.shape
    n_tiles = N // ROW_TILE

    def tile(i):
        lat = (n_lat_tiles - i) if reverse else (i - 1)
        return jnp.where(i == 0, n_tiles - 1, lat)

    row = lambda b, i: (b, tile(i), 0)
    const2 = lambda b, i: (0, 0)
    y_w = RET_HEADS * RET_V_DIM + SSD_INNER
    ns = SCAN_SAMPLES
    return pl.pallas_call(
        functools.partial(_scan_kernel, reverse=reverse),
        grid=(B // ns, n_tiles),
        in_specs=[pl.BlockSpec((ns, ROW_TILE, RQ.shape[-1]), row),
                  pl.BlockSpec((ns, ROW_TILE, RK.shape[-1]), row),
                  pl.BlockSpec((ns, ROW_TILE, RV.shape[-1]), row),
                  pl.BlockSpec((ns, ROW_TILE, A.shape[-1]), row),
                  pl.BlockSpec((ns, ROW_TILE, PAIR), row),
                  pl.BlockSpec((1, PAIR), const2),
                  pl.BlockSpec((1, PAIR), const2)],
        out_specs=pl.BlockSpec((ns, ROW_TILE, y_w), row),
        out_shape=jax.ShapeDtypeStruct((B, N, y_w), BF16),
        scratch_shapes=[pltpu.VMEM((ns, RET_HEADS, PAIR, RET_V_DIM), F32),
                        pltpu.VMEM((ns, SSD_GROUPS, SSD_STATE, SSD_INNER // SSD_GROUPS), F32)],
        compiler_params=_cparams(("arbitrary", "arbitrary")),
        name="scan_bwd" if reverse else "scan_fwd",
    )(RQ, RK, RV, A, DT, alog_row, logit_row)


def _rec_combine_kernel(x_ref, yf_ref, yb_ref, g_ref, a_ref, mod_ref, dskip_ref, snw_ref, w_ref, o_ref, y_scr,
                        *, n_lat):
    n_ret = RET_HEADS * RET_V_DIM
    y = yf_ref[0].astype(F32) + yb_ref[0].astype(F32)
    gates = g_ref[0].astype(F32)
    for h in range(RET_HEADS):
        sl = slice(h * RET_V_DIM, (h + 1) * RET_V_DIM)
        r = y[:, sl]
        r = r * lax.rsqrt(jnp.mean(r * r, axis=-1, keepdims=True) + EPS)
        y_scr[:, sl] = (r * _silu(gates[:, sl])).astype(BF16)
    gw = SSD_INNER // SSD_GROUPS
    for g in range(SSD_GROUPS):
        sl = slice(g * gw, (g + 1) * gw)
        s = y[:, n_ret + g * gw: n_ret + (g + 1) * gw] + dskip_ref[:, sl] * a_ref[0, :, sl].astype(F32)
        s = s * _silu(gates[:, n_ret + g * gw: n_ret + (g + 1) * gw])
        s = s * lax.rsqrt(jnp.mean(s * s, axis=-1, keepdims=True) + EPS) * snw_ref[:, sl]
        y_scr[:, n_ret + g * gw: n_ret + (g + 1) * gw] = s.astype(BF16)
    gate = _row_mod(mod_ref, 2, x_ref.shape[1], n_lat)
    o_ref[0] = x_ref[0] + gate * jnp.dot(y_scr[...], w_ref[...], preferred_element_type=F32)


def _rec_combine(X, Yf, Yb, G, A, modc, dskip_row, snw_row, w_bf16, n_lat_tiles):
    B, N, D = X.shape
    row = lambda b, i: (b, i, 0)
    const2 = lambda b, i: (0, 0)
    return pl.pallas_call(
        functools.partial(_rec_combine_kernel, n_lat=n_lat_tiles * ROW_TILE),
        grid=(B, N // PROJ_TILE),
        in_specs=[pl.BlockSpec((1, PROJ_TILE, D), row),
                  pl.BlockSpec((1, PROJ_TILE, Yf.shape[-1]), row),
                  pl.BlockSpec((1, PROJ_TILE, Yb.shape[-1]), row),
                  pl.BlockSpec((1, PROJ_TILE, G.shape[-1]), row),
                  pl.BlockSpec((1, PROJ_TILE, SSD_INNER), row),
                  pl.BlockSpec((1, 2, 6, D), _MOD_BOTH),
                  pl.BlockSpec((1, SSD_INNER), const2),
                  pl.BlockSpec((1, SSD_INNER), const2),
                  pl.BlockSpec(w_bf16.shape, const2, **_ONCE)],
        out_specs=pl.BlockSpec((1, PROJ_TILE, D), row),
        out_shape=jax.ShapeDtypeStruct((B, N, D), F32),
        scratch_shapes=[pltpu.VMEM((PROJ_TILE, w_bf16.shape[0]), BF16)],
        compiler_params=_cparams(("arbitrary", "arbitrary")),
        name="rec_combine",
    )(X, Yf, Yb, G, A, modc, dskip_row, snw_row, w_bf16)


def _recurrent_layer(X, nw, modc, w_in, w_out, cos_t, sin_t, ret_decay_logit, conv_w, conv_b, dt_bias,
                     a_log, d_skip, ssd_norm_w, n_lat_tiles):
    D = X.shape[-1]
    w_pad = jnp.pad(w_in, ((0, 0), (0, REC_IN_PAD - w_in.shape[1]))).astype(BF16)
    pad_row = lambda v, at: jnp.pad(v.reshape(1, -1), ((0, 0), (at, PAIR - at - v.size)))
    RQ, RK, RV, G, XBC, DT = _rec_project(X, nw, modc, w_pad, cos_t, sin_t, pad_row(dt_bias, 0), n_lat_tiles)
    A = _ssd_conv(XBC, conv_w, conv_b, n_lat_tiles * ROW_TILE)
    alog_row, logit_row = pad_row(a_log, 0), pad_row(ret_decay_logit, RET_LANE0)
    Yf = _bidir_scan(RQ, RK, RV, A, DT, alog_row, logit_row, n_lat_tiles, False)
    Yb = _bidir_scan(RQ, RK, RV, A, DT, alog_row, logit_row, n_lat_tiles, True)
    return _rec_combine(X, Yf, Yb, G, A, modc, jnp.repeat(d_skip, SSD_HEAD_DIM).reshape(1, -1),
                        ssd_norm_w.reshape(1, -1), w_out.astype(BF16), n_lat_tiles)


SEL_CHUNK = 256
SELECT_SAMPLES = 2
COMBINE_WINDOW = 64
BF16_ROWS = 16
FFN_ROW_CHUNK = 1024


def _router_kernel(x_ref, nw_ref, mod_ref, rw_ref, h_ref, aff_ref, *, n_lat):
    h = _modulated(x_ref[0], nw_ref[...], mod_ref, 3, 4, n_lat)
    h_hi = h.astype(BF16)
    h_ref[0] = h_hi
    h_lo = (h - h_hi.astype(F32)).astype(BF16)
    w = rw_ref[...]
    w_hi = w.astype(BF16)
    w_lo = (w - w_hi.astype(F32)).astype(BF16)
    logits = (jnp.dot(h_hi, w_hi, preferred_element_type=F32) + jnp.dot(h_lo, w_hi, preferred_element_type=F32)
              + jnp.dot(h_hi, w_lo, preferred_element_type=F32))
    logits = logits.T[0:aff_ref.shape[1], :]
    e = jnp.exp(logits - jnp.max(logits, axis=0, keepdims=True))
    aff_ref[0] = e / jnp.sum(e, axis=0, keepdims=True)


def _mod_router(X, nw, modc, router_w, n_lat, n_rows):
    B, _, D = X.shape
    E = router_w.shape[1]
    rw_pad = jnp.pad(router_w, ((0, 0), (0, PAIR - E)))
    tile = _row_tile(n_rows, PROJ_TILE)
    return pl.pallas_call(
        functools.partial(_router_kernel, n_lat=n_lat),
        grid=(B, n_rows // tile),
        in_specs=[pl.BlockSpec((1, tile, D), lambda b, i: (b, i, 0)),
                  pl.BlockSpec((1, D), lambda b, i: (0, 0)),
                  pl.BlockSpec((1, 2, 6, D), _MOD_BOTH),
                  pl.BlockSpec((D, PAIR), lambda b, i: (0, 0))],
        out_specs=[pl.BlockSpec((1, tile, D), lambda b, i: (b, i, 0)),
                   pl.BlockSpec((1, E, tile), lambda b, i: (b, 0, i))],
        out_shape=[jax.ShapeDtypeStruct((B, n_rows, D), BF16),
                   jax.ShapeDtypeStruct((B, E, n_rows), F32)],
        compiler_params=_cparams(("arbitrary", "arbitrary")),
        name="mod_router",
    )(X, nw.reshape(1, D), modc, rw_pad)


def _count(mask_f32):
    return jnp.sum(mask_f32, axis=-1, keepdims=True)


def _select_top(a, cap):
    E, n = a.shape
    v = pltpu.bitcast(a, jnp.int32)
    thr = jnp.zeros((E, 1), jnp.int32)
    for bit in range(30, -1, -1):
        cand = thr | (1 << bit)
        thr = jnp.where(_count(jnp.where(v >= cand, 1.0, 0.0)) >= cap, cand, thr)
    gt = v > thr
    eq = v == thr
    need = cap - _count(jnp.where(gt, 1.0, 0.0))
    idx = lax.broadcasted_iota(jnp.int32, (E, n), 1)
    last = jnp.zeros((E, 1), jnp.int32)
    for bit in range(n.bit_length() - 2, -1, -1):
        cand = last | (1 << bit)
        below = _count(jnp.where(eq, jnp.where(idx < cand, 1.0, 0.0), 0.0))
        last = jnp.where(below < need, cand, last)
    sel = jnp.where(gt, 1.0, jnp.where(eq, jnp.where(idx <= last, 1.0, 0.0), 0.0))
    si = lax.broadcasted_iota(jnp.int32, (SEL_CHUNK, SEL_CHUNK), 0)
    sj = lax.broadcasted_iota(jnp.int32, (SEL_CHUNK, SEL_CHUNK), 1)
    before = jnp.where(si < sj, 1.0, 0.0).astype(BF16)
    base = jnp.zeros((E, 1), F32)
    lane = lax.broadcasted_iota(jnp.int32, (E, PAIR), 1)
    starts = jnp.zeros((E, PAIR), F32)
    pos = []
    for k in range(n // SEL_CHUNK):
        sk = sel[:, k * SEL_CHUNK:(k + 1) * SEL_CHUNK]
        pos.append(jnp.dot(sk.astype(BF16), before, preferred_element_type=F32) + base)
        base = base + _count(sk)
        starts = jnp.where(lane == k + 1, base, starts)
    pos = jnp.concatenate(pos, axis=1) if len(pos) > 1 else pos[0]
    return jnp.where(sel > 0.0, pos, -1.0), jnp.where(sel > 0.0, a, 0.0), starts


def _token_sets(n_lat, n_rows):
    sets = [(0, n_lat, EC_CAPACITY * n_lat // N_EXPERTS)]
    if n_rows > n_lat:
        sets.append((n_lat, n_rows, EC_CAPACITY * (n_rows - n_lat) // N_EXPERTS))
    return sets


def _select_kernel(aff_ref, slot_ref, slot_t_ref, gate_t_ref, starts_ref, *, sets):
    E = aff_ref.shape[1]
    pad_rows = PAIR - E
    for smp in range(aff_ref.shape[0]):
        for n_set, (lo, hi, cap) in enumerate(sets):
            slot, gate, starts = _select_top(aff_ref[smp, :, lo:hi], cap)
            if n_set == 0:
                starts_ref[smp] = starts
            slot_ref[smp, :, lo:hi] = slot
            slot_t_ref[smp, lo:hi, :] = jnp.concatenate(
                [slot, jnp.full((pad_rows, hi - lo), -1.0, F32)], axis=0).T
            gate_t_ref[smp, lo:hi, :] = jnp.concatenate(
                [gate, jnp.zeros((pad_rows, hi - lo), F32)], axis=0).T


def _ec_select(aff, sets):
    B, E, N = aff.shape
    return pl.pallas_call(
        functools.partial(_select_kernel, sets=sets),
        grid=(B // SELECT_SAMPLES,),
        in_specs=[pl.BlockSpec((SELECT_SAMPLES, E, N), lambda b: (b, 0, 0))],
        out_specs=[pl.BlockSpec((SELECT_SAMPLES, E, N), lambda b: (b, 0, 0)),
                   pl.BlockSpec((SELECT_SAMPLES, N, PAIR), lambda b: (b, 0, 0)),
                   pl.BlockSpec((SELECT_SAMPLES, N, PAIR), lambda b: (b, 0, 0)),
                   pl.BlockSpec((SELECT_SAMPLES, E, PAIR), lambda b: (b, 0, 0))],
        out_shape=[jax.ShapeDtypeStruct((B, E, N), F32),
                   jax.ShapeDtypeStruct((B, N, PAIR), F32),
                   jax.ShapeDtypeStruct((B, N, PAIR), F32),
                   jax.ShapeDtypeStruct((B, E, PAIR), F32)],
        compiler_params=_cparams(("arbitrary",)),
        name="ec_select",
    )(aff)


def _gather_kernel(starts_ref, slot_ref, h_ref, *refs, n_lat):
    x_refs, acc_ref = refs[:-1], refs[-1]
    xl_ref = x_refs[0]
    E, cap_lat, _ = xl_ref.shape
    tile = h_ref.shape[1]
    b, i = pl.program_id(0), pl.program_id(1)
    n_lat_tiles = n_lat // tile

    def picked(e, first_slot, n_slots):
        c = lax.broadcasted_iota(jnp.int32, (n_slots, tile), 0).astype(F32) + first_slot
        return jnp.where(slot_ref[0, e:e + 1, :] == c, 1.0, 0.0).astype(BF16)

    @pl.when(i == 0)
    def _():
        acc_ref[...] = jnp.zeros_like(acc_ref)

    @pl.when(i < n_lat_tiles)
    def _():
        def all_slots():
            for e in range(E):
                acc_ref[e] += jnp.dot(picked(e, 0.0, cap_lat), h_ref[0], preferred_element_type=F32)

        if cap_lat <= COMBINE_WINDOW:
            all_slots()
            return
        window_starts, fits = _slot_windows(starts_ref, b, i, E, cap_lat)

        @pl.when(fits)
        def _():
            onehot = jnp.concatenate([picked(e, window_starts[e].astype(F32), COMBINE_WINDOW)
                                      for e in range(E)], axis=0)
            x = jnp.dot(onehot, h_ref[0], preferred_element_type=F32)
            for e in range(E):
                rows = pl.ds(pl.multiple_of(window_starts[e], BF16_ROWS), COMBINE_WINDOW)
                acc_ref[e, rows, :] += x[e * COMBINE_WINDOW:(e + 1) * COMBINE_WINDOW, :]

        @pl.when(jnp.logical_not(fits))
        def _():
            all_slots()

    @pl.when(i == n_lat_tiles - 1)
    def _():
        xl_ref[...] = acc_ref[...].astype(BF16)

    if len(x_refs) > 1:
        xc_ref = x_refs[1]

        @pl.when(i >= n_lat_tiles)
        def _():
            for e in range(E):
                xc_ref[e] = jnp.dot(picked(e, 0.0, xc_ref.shape[1]), h_ref[0],
                                    preferred_element_type=F32).astype(BF16)


def _slot_windows(starts_ref, b, i, n_experts, cap):
    window_starts, fits = [], None
    for e in range(n_experts):
        first, end = starts_ref[b, e, i], starts_ref[b, e, i + 1]
        w = jnp.minimum(first - lax.rem(first, BF16_ROWS), cap - COMBINE_WINDOW)
        ok = end - w <= COMBINE_WINDOW
        fits = ok if fits is None else jnp.logical_and(fits, ok)
        window_starts.append(w)
    return window_starts, fits


def _ec_gather(slot, H, starts_i32, sets, n_lat):
    B, E, N = slot.shape
    D = H.shape[-1]
    tile = SEL_CHUNK
    cap_lat = sets[0][2]
    grid_spec = pltpu.PrefetchScalarGridSpec(
        num_scalar_prefetch=1,
        grid=(B, N // tile),
        in_specs=[pl.BlockSpec((1, E, tile), lambda b, i, s: (b, 0, i)),
                  pl.BlockSpec((1, tile, D), lambda b, i, s: (b, i, 0))],
        out_specs=[pl.BlockSpec((E, cap, D), lambda b, i, s: (0, b, 0)) for _, _, cap in sets],
        scratch_shapes=[pltpu.VMEM((E, cap_lat, D), F32)])
    return pl.pallas_call(
        functools.partial(_gather_kernel, n_lat=n_lat),
        grid_spec=grid_spec,
        out_shape=[jax.ShapeDtypeStruct((E, B * cap, D), BF16) for _, _, cap in sets],
        compiler_params=_cparams(("arbitrary", "arbitrary")),
        name="ec_gather",
    )(starts_i32, slot, H)


def _ffn_kernel(*refs, n_groups, row_chunk):
    x_refs = refs[:n_groups]
    wg_ref, wu_ref, wd_ref = refs[n_groups:n_groups + 3]
    o_refs = refs[n_groups + 3:2 * n_groups + 3]
    acc_refs = refs[2 * n_groups + 3:3 * n_groups + 3]
    wg_s, wu_s, wd_s = refs[3 * n_groups + 3:]
    f = pl.program_id(2)
    n_f = pl.num_programs(2)
    wg_s[...] = wg_ref[0, 0].astype(BF16)
    wu_s[...] = wu_ref[0, 0].astype(BF16)
    wd_s[...] = wd_ref[0, 0].astype(BF16)

    @pl.when(f == 0)
    def _():
        for acc_ref in acc_refs:
            acc_ref[...] = jnp.zeros_like(acc_ref)

    def chunk(x_ref, acc_ref, start, size):
        rows = pl.ds(start, size)
        xr = x_ref[0, rows, :]
        a = jnp.dot(xr, wg_s[...], preferred_element_type=F32)
        u = jnp.dot(xr, wu_s[...], preferred_element_type=F32)
        hm = (a * jax.nn.sigmoid(a) * u).astype(BF16)
        acc_ref[rows, :] += jnp.dot(hm, wd_s[...], preferred_element_type=F32)

    for x_ref, acc_ref in zip(x_refs, acc_refs):
        n_rows = x_ref.shape[1]
        for start in range(0, n_rows, row_chunk):
            chunk(x_ref, acc_ref, start, min(row_chunk, n_rows - start))

    @pl.when(f == n_f - 1)
    def _():
        for o_ref, acc_ref in zip(o_refs, acc_refs):
            o_ref[0] = acc_ref[...].astype(o_ref.dtype)


def _expert_ffn(xs, w_gate, w_up, w_down, layer):
    E, _, D = xs[0].shape
    F = w_gate.shape[-1]
    m_tiles = 2
    tms = [x.shape[1] // m_tiles for x in xs]
    n = len(xs)
    return pl.pallas_call(
        functools.partial(_ffn_kernel, n_groups=n, row_chunk=FFN_ROW_CHUNK),
        grid=(E, m_tiles, F // FF_TILE),
        in_specs=[pl.BlockSpec((1, tm, D), lambda e, m, f: (e, m, 0)) for tm in tms]
        + [pl.BlockSpec((1, 1, D, FF_TILE), lambda e, m, f: (layer, e, 0, f)),
           pl.BlockSpec((1, 1, D, FF_TILE), lambda e, m, f: (layer, e, 0, f)),
           pl.BlockSpec((1, 1, FF_TILE, D), lambda e, m, f: (layer, e, f, 0))],
        out_specs=[pl.BlockSpec((1, tm, D), lambda e, m, f: (e, m, 0)) for tm in tms],
        out_shape=[jax.ShapeDtypeStruct(x.shape, BF16) for x in xs],
        scratch_shapes=[pltpu.VMEM((tm, D), F32) for tm in tms]
        + [pltpu.VMEM((D, FF_TILE), BF16), pltpu.VMEM((D, FF_TILE), BF16), pltpu.VMEM((FF_TILE, D), BF16)],
        compiler_params=_cparams(("arbitrary", "arbitrary", "arbitrary")),
        name="expert_ffn",
    )(*xs, w_gate, w_up, w_down)


def _combine_kernel(starts_ref, x_ref, slot_t_ref, gate_t_ref, mod_ref, fnw_ref, *refs, n_lat, final_norm):
    y_refs, o_ref = refs[:-1], refs[-1]
    E = y_refs[0].shape[0]
    tile = x_ref.shape[1]
    b, i = pl.program_id(0), pl.program_id(1)

    def finish(acc, seg):
        out = x_ref[0] + mod_ref[0, seg, 5:6, :] * acc
        if final_norm:
            out = out * lax.rsqrt(jnp.mean(out * out, axis=-1, keepdims=True) + EPS) * fnw_ref[...]
        o_ref[0] = out

    def scatter_full(seg):
        y_ref = y_refs[seg]
        cap = y_ref.shape[1]
        c = lax.broadcasted_iota(jnp.int32, (tile, cap), 1).astype(F32)
        acc = jnp.zeros((tile, x_ref.shape[2]), F32)
        for e in range(E):
            onehot = jnp.where(slot_t_ref[0, :, e:e + 1] == c, 1.0, 0.0).astype(BF16)
            acc = acc + gate_t_ref[0, :, e:e + 1] * jnp.dot(onehot, y_ref[e], preferred_element_type=F32)
        finish(acc, seg)

    def scatter_windowed(window_starts):
        y_ref = y_refs[0]
        k = lax.broadcasted_iota(jnp.int32, (tile, COMBINE_WINDOW), 1).astype(F32)
        sel, rows = [], []
        for e in range(E):
            w = window_starts[e]
            rel = slot_t_ref[0, :, e:e + 1] - w.astype(F32)
            sel.append(jnp.where(rel == k, gate_t_ref[0, :, e:e + 1], 0.0).astype(BF16))
            rows.append(y_ref[e, pl.ds(pl.multiple_of(w, BF16_ROWS), COMBINE_WINDOW), :])
        acc = jnp.dot(jnp.concatenate(sel, axis=1), jnp.concatenate(rows, axis=0),
                      preferred_element_type=F32)
        finish(acc, 0)

    n_lat_tiles = n_lat // tile
    cap_lat = y_refs[0].shape[1]

    @pl.when(i < n_lat_tiles)
    def _():
        if cap_lat <= COMBINE_WINDOW:
            scatter_full(0)
            return
        window_starts, fits = _slot_windows(starts_ref, b, i, E, cap_lat)

        @pl.when(fits)
        def _():
            scatter_windowed(window_starts)

        @pl.when(jnp.logical_not(fits))
        def _():
            scatter_full(0)

    if len(y_refs) > 1:
        @pl.when(i >= n_lat_tiles)
        def _():
            scatter_full(1)


def _row_tile(n_rows, preferred):
    return preferred if n_rows % preferred == 0 else 2 * ROW_TILE


def _ec_combine(X, slot_t, gate_t, starts_i32, Ys, modc, sets, n_lat, final_norm_w):
    B, _, D = X.shape
    E = Ys[0].shape[0]
    n_rows = sets[-1][1]
    tile = SEL_CHUNK
    row = lambda b, i, s: (b, i, 0)
    final_norm = final_norm_w is not None
    fnw = final_norm_w.reshape(1, D) if final_norm else jnp.ones((1, D), F32)
    grid_spec = pltpu.PrefetchScalarGridSpec(
        num_scalar_prefetch=1,
        grid=(B, n_rows // tile),
        in_specs=[pl.BlockSpec((1, tile, D), row),
                  pl.BlockSpec((1, tile, PAIR), row),
                  pl.BlockSpec((1, tile, PAIR), row),
                  pl.BlockSpec((1, 2, 6, D), lambda b, i, s: (b, 0, 0, 0)),
                  pl.BlockSpec((1, D), lambda b, i, s: (0, 0))]
        + [pl.BlockSpec((E, cap, D), lambda b, i, s: (0, b, 0)) for _, _, cap in sets],
        out_specs=pl.BlockSpec((1, tile, D), row))
    return pl.pallas_call(
        functools.partial(_combine_kernel, n_lat=n_lat, final_norm=final_norm),
        grid_spec=grid_spec,
        out_shape=jax.ShapeDtypeStruct((B, n_rows, D), F32),
        compiler_params=_cparams(("arbitrary", "arbitrary")),
        name="ec_combine",
    )(starts_i32, X, slot_t, gate_t, modc, fnw, *Ys)


def _moe_layer(X, nw, modc, router_w, w_gate, w_up, w_down, layer, n_lat, n_rows, final_norm_w=None):
    sets = _token_sets(n_lat, n_rows)
    H, aff = _mod_router(X, nw, modc, router_w, n_lat, n_rows)
    slot, slot_t, gate_t, starts = _ec_select(aff, sets)
    starts_i32 = starts[:, :, :n_lat // SEL_CHUNK + 1].astype(jnp.int32)
    Xs = _ec_gather(slot, H, starts_i32, sets, n_lat)
    Ys = _expert_ffn(Xs, w_gate, w_up, w_down, layer)
    return _ec_combine(X, slot_t, gate_t, starts_i32, Ys, modc, sets, n_lat, final_norm_w)


def _rope_tables(n):
    t = jnp.arange(n)
    row = (t // GRID_W).astype(F32)
    col = (t % GRID_W).astype(F32)
    n_freq = HEAD_DIM // 4
    inv = ROPE_THETA ** (-jnp.arange(n_freq, dtype=F32) / n_freq)
    ang = jnp.concatenate([row[:, None] * inv, col[:, None] * inv], axis=-1)
    return jnp.cos(ang), jnp.sin(ang)


def kernel(x, c, ctx, c_ctx, ada_w, ada_b, norm1_w, norm2_w, att_w_in, att_w_out, att_q_norm_w,
           att_k_norm_w, diff_lambda, diff_norm_w, rec_w_in, rec_w_out, ret_decay_logit, ssd_conv_w,
           ssd_conv_b, ssd_dt_bias, ssd_a_log, ssd_d_skip, ssd_norm_w, router_w, expert_w_gate,
           expert_w_up, expert_w_down, final_norm_w):
    B, n_lat, D = x.shape
    n_ctx = ctx.shape[1]
    n_lat_tiles = n_lat // ROW_TILE
    cos, sin = _rope_tables(n_lat)
    reps = PAIR // cos.shape[1]
    cos_t = jnp.concatenate([jnp.tile(cos, (1, reps)), jnp.ones((n_ctx, PAIR), F32)], axis=0)
    sin_t = jnp.concatenate([jnp.tile(jnp.concatenate([-sin, sin], axis=1), (1, reps // 2)),
                             jnp.zeros((n_ctx, PAIR), F32)], axis=0)
    gmat = jnp.kron(jnp.eye(GQA_HEADS, dtype=F32), jnp.ones((HEAD_DIM, HEAD_DIM), F32)).astype(BF16)

    n_cond = 24
    cvec = jnp.concatenate([c, c_ctx[None, :], jnp.zeros((n_cond - B - 1, D), F32)], axis=0)
    mods = _ada_modulation(cvec, ada_w, ada_b)

    X = jnp.concatenate([x, ctx], axis=1)
    for layer in range(DEPTH):
        i = layer // 2
        need_ctx = layer < DEPTH - 1
        mod_lat = mods[layer, :B].reshape(B, 1, 6, D)
        mod_ctx = jnp.broadcast_to(mods[layer, B].reshape(1, 1, 6, D), (B, 1, 6, D))
        modc = jnp.concatenate([mod_lat, mod_ctx], axis=1)

        if layer % 2 == 0:
            lambda_init = 0.8 - 0.6 * math.exp(-0.3 * layer)
            Q, K, V = _att_project(X, norm1_w[layer], modc, att_w_in[i].astype(BF16), cos_t, sin_t, gmat,
                                   jnp.tile(att_q_norm_w[i], GQA_HEADS).reshape(1, -1),
                                   jnp.tile(att_k_norm_w[i], GQA_KV_HEADS).reshape(1, -1), n_lat_tiles)
            X = _attention(X, Q, K, V, modc, diff_lambda[i], diff_norm_w[i], att_w_out[i].astype(BF16),
                           lambda_init, n_lat_tiles)
        else:
            X = _recurrent_layer(X, norm1_w[layer], modc, rec_w_in[i], rec_w_out[i], cos_t, sin_t,
                                 ret_decay_logit[i], ssd_conv_w[i], ssd_conv_b[i], ssd_dt_bias[i],
                                 ssd_a_log[i], ssd_d_skip[i], ssd_norm_w[i], n_lat_tiles)

        X = _moe_layer(X, norm2_w[layer], modc, router_w[layer], expert_w_gate, expert_w_up, expert_w_down,
                       layer, n_lat, n_lat + n_ctx if need_ctx else n_lat,
                       None if need_ctx else final_norm_w)

    return X
```

```python
import functools
import math

import jax
import jax.numpy as jnp
import numpy as np
from jax import lax
from jax.experimental import pallas as pl
from jax.experimental.pallas import tpu as pltpu

D_MODEL = 1024
DEPTH = 4
GRID_W = 64
HEAD_DIM = 64
ROPE_THETA = 10000.0
EPS = 1e-6
HALF_W = D_MODEL // 2
GQA_HEADS = HALF_W // HEAD_DIM
GQA_KV_HEADS = GQA_HEADS // 4
DIFF_HEADS = HALF_W // (2 * HEAD_DIM)
RET_HEADS = HALF_W // (2 * HEAD_DIM)
RET_QK_DIM = HEAD_DIM
RET_V_DIM = 2 * HEAD_DIM
SSD_HEAD_DIM = HEAD_DIM
SSD_HEADS = HALF_W // SSD_HEAD_DIM
SSD_GROUPS = 2
SSD_STATE = 128
SSD_INNER = SSD_HEADS * SSD_HEAD_DIM
SSD_XBC = SSD_INNER + 2 * SSD_GROUPS * SSD_STATE
N_EXPERTS = 16
EC_CAPACITY = 2

ATT_SPLITS = (GQA_HEADS * HEAD_DIM, GQA_KV_HEADS * HEAD_DIM, GQA_KV_HEADS * HEAD_DIM,
              DIFF_HEADS * 2 * HEAD_DIM, DIFF_HEADS * 2 * HEAD_DIM, DIFF_HEADS * 2 * HEAD_DIM)
REC_SPLITS = (RET_HEADS * RET_QK_DIM, RET_HEADS * RET_QK_DIM, RET_HEADS * RET_V_DIM,
              RET_HEADS * RET_V_DIM, SSD_INNER, SSD_XBC, 2 * SSD_HEADS)

ROW_TILE = 256
PROJ_TILE = 768
FF_TILE = 256
ADA_COL_TILE = 1536
CONV_CH_TILE = 256
FFN_ROW_SPLITS = 2
SUBLANES = 8
VMEM_LIMIT = 56 * 1024 * 1024
BF16 = jnp.bfloat16
F32 = jnp.float32


def _cparams(sem):
    return pltpu.CompilerParams(dimension_semantics=sem, vmem_limit_bytes=VMEM_LIMIT)


def _ada_kernel(c_ref, w_ref, b_ref, o_ref):
    c = c_ref[...]
    s = c * jax.nn.sigmoid(c)
    o_ref[0] = jnp.dot(s, w_ref[0], precision=lax.Precision.HIGHEST,
                       preferred_element_type=F32) + b_ref[0]


def _ada_modulation(cvec, ada_w, ada_b):
    R, D = cvec.shape
    n_out = ada_w.shape[-1]
    tn = ADA_COL_TILE
    return pl.pallas_call(
        _ada_kernel,
        grid=(DEPTH, n_out // tn),
        in_specs=[pl.BlockSpec((R, D), lambda l, j: (0, 0)),
                  pl.BlockSpec((1, D, tn), lambda l, j: (l, 0, j)),
                  pl.BlockSpec((1, 1, tn), lambda l, j: (l, 0, j))],
        out_specs=pl.BlockSpec((1, R, tn), lambda l, j: (l, 0, j)),
        out_shape=jax.ShapeDtypeStruct((DEPTH, R, n_out), F32),
        compiler_params=_cparams(("arbitrary", "arbitrary")),
        name="ada_modulation",
    )(cvec, ada_w, ada_b.reshape(DEPTH, 1, n_out))


def _row_mod(mod_ref, idx, n_rows, n_lat):
    row = pl.program_id(1) * n_rows + lax.broadcasted_iota(jnp.int32, (n_rows, 1), 0)
    return jnp.where(row < n_lat, mod_ref[0, 0, idx:idx + 1, :], mod_ref[0, 1, idx:idx + 1, :])


def _modulated(x, nw, mod_ref, shift_idx, scale_idx, n_lat):
    n_rows = x.shape[0]
    ms = jnp.mean(x * x, axis=-1, keepdims=True)
    y = x * lax.rsqrt(ms + EPS) * nw
    return (y * (1.0 + _row_mod(mod_ref, scale_idx, n_rows, n_lat))
            + _row_mod(mod_ref, shift_idx, n_rows, n_lat))


_MOD_BOTH = lambda b, i: (b, 0, 0, 0)
_ONCE = dict(pipeline_mode=pl.Buffered(1))


PAIR = 2 * HEAD_DIM
N_PAIR_GROUPS = 2 * HALF_W // PAIR
ATT_KV_W = GQA_KV_HEADS * PAIR + DIFF_HEADS * PAIR


def _head_sumsq(x, g):
    x2 = x * x
    hi = x2.astype(BF16)
    lo = (x2 - hi.astype(F32)).astype(BF16)
    return jnp.dot(hi, g, preferred_element_type=F32) + jnp.dot(lo, g, preferred_element_type=F32)


def _rope_lanes(x, cos, sin_signed):
    W = x.shape[-1]
    half = HEAD_DIM // 2
    lane = lax.broadcasted_iota(jnp.int32, x.shape, 1) & (HEAD_DIM - 1)
    partner = jnp.where(lane < half, pltpu.roll(x, W - half, 1), pltpu.roll(x, half, 1))
    return x * cos + partner * sin_signed


def _att_project_kernel(x_ref, nw_ref, mod_ref, w_ref, cos_ref, sin_ref, g_ref, qw_ref, kw_ref,
                        q_ref, k_ref, v_ref, *, n_lat):
    h = _modulated(x_ref[0], nw_ref[...], mod_ref, 0, 1, n_lat)
    z = jnp.dot(h.astype(BF16), w_ref[...], preferred_element_type=F32)
    o_aq, o_ak, o_av, o_bq, o_bk, o_bv = np.cumsum((0,) + ATT_SPLITS[:-1]).tolist()
    n_aq, n_ak, n_bq = ATT_SPLITS[0], ATT_SPLITS[1], ATT_SPLITS[3]
    cos, sin = cos_ref[...], sin_ref[...]
    cos4 = jnp.concatenate([cos] * (n_aq // PAIR), axis=1)
    sin4 = jnp.concatenate([sin] * (n_aq // PAIR), axis=1)
    inv_d = 1.0 / HEAD_DIM
    q_scale = HEAD_DIM ** -0.5 * math.log2(math.e)

    aq = z[:, o_aq:o_aq + n_aq]
    aq = aq * lax.rsqrt(_head_sumsq(aq, g_ref[...]) * inv_d + EPS) * qw_ref[...]
    aq = _rope_lanes(aq, cos4, sin4) * q_scale
    bq = _rope_lanes(z[:, o_bq:o_bq + n_bq], cos4, sin4) * q_scale
    q = jnp.concatenate([aq, bq], axis=1)
    lo = (lax.broadcasted_iota(jnp.int32, q.shape, 1) & (PAIR - 1)) < HEAD_DIM
    q_ref[0, :, 0:q.shape[1]] = jnp.where(lo, q, 0.0).astype(BF16)
    q_ref[0, :, q.shape[1]:2 * q.shape[1]] = jnp.where(lo, 0.0, q).astype(BF16)

    ak = z[:, o_ak:o_ak + n_ak]
    ak = ak * lax.rsqrt(_head_sumsq(ak, g_ref[0:n_ak, 0:n_ak]) * inv_d + EPS) * kw_ref[...]
    ak = _rope_lanes(ak, cos, sin)
    av = z[:, o_av:o_av + n_ak]
    lo_kv = lax.broadcasted_iota(jnp.int32, ak.shape, 1) < HEAD_DIM
    ak_sw, av_sw = pltpu.roll(ak, HEAD_DIM, 1), pltpu.roll(av, HEAD_DIM, 1)
    k_ref[0, :, 0:PAIR] = jnp.where(lo_kv, ak, ak_sw).astype(BF16)
    k_ref[0, :, PAIR:2 * PAIR] = jnp.where(lo_kv, ak_sw, ak).astype(BF16)
    v_ref[0, :, 0:PAIR] = jnp.where(lo_kv, av, av_sw).astype(BF16)
    v_ref[0, :, PAIR:2 * PAIR] = jnp.where(lo_kv, av_sw, av).astype(BF16)
    k_ref[0, :, 2 * PAIR:] = _rope_lanes(z[:, o_bk:o_bk + n_bq], cos4, sin4).astype(BF16)
    v_ref[0, :, 2 * PAIR:] = z[:, o_bv:o_bv + n_bq].astype(BF16)


def _att_project(X, nw, modc, w_bf16, cos_t, sin_t, gmat, qw_t, kw_t, n_lat_tiles):
    B, N, D = X.shape
    n_in = w_bf16.shape[1]
    q_w = 2 * N_PAIR_GROUPS * PAIR
    row = lambda b, i: (b, i, 0)
    const2 = lambda b, i: (0, 0)
    return pl.pallas_call(
        functools.partial(_att_project_kernel, n_lat=n_lat_tiles * ROW_TILE),
        grid=(B, N // PROJ_TILE),
        in_specs=[pl.BlockSpec((1, PROJ_TILE, D), row),
                  pl.BlockSpec((1, D), const2),
                  pl.BlockSpec((1, 2, 6, D), _MOD_BOTH),
                  pl.BlockSpec((D, n_in), const2, **_ONCE),
                  pl.BlockSpec((PROJ_TILE, PAIR), lambda b, i: (i, 0)),
                  pl.BlockSpec((PROJ_TILE, PAIR), lambda b, i: (i, 0)),
                  pl.BlockSpec(gmat.shape, const2),
                  pl.BlockSpec(qw_t.shape, const2),
                  pl.BlockSpec(kw_t.shape, const2)],
        out_specs=[pl.BlockSpec((1, PROJ_TILE, q_w), row),
                   pl.BlockSpec((1, PROJ_TILE, ATT_KV_W), row),
                   pl.BlockSpec((1, PROJ_TILE, ATT_KV_W), row)],
        out_shape=[jax.ShapeDtypeStruct((B, N, q_w), BF16),
                   jax.ShapeDtypeStruct((B, N, ATT_KV_W), BF16),
                   jax.ShapeDtypeStruct((B, N, ATT_KV_W), BF16)],
        compiler_params=_cparams(("arbitrary", "arbitrary")),
        name="att_project",
    )(X, nw.reshape(1, D), modc, w_bf16, cos_t, sin_t, gmat, qw_t, kw_t)


def _attention_kernel(x_ref, q_ref, k_ref, v_ref, mod_ref, lam_ref, dnw_ref, w_ref, o_ref,
                      s_scr, p_scr, y_scr, *, lambda_init, n_lat_tiles):
    n_lat = n_lat_tiles * ROW_TILE
    n_all = k_ref.shape[1]
    q_half = N_PAIR_GROUPS * PAIR
    lam = lam_ref[...]
    lam_val = (jnp.exp(jnp.sum(lam[0:1] * lam[1:2], axis=-1, keepdims=True))
               - jnp.exp(jnp.sum(lam[2:3] * lam[3:4], axis=-1, keepdims=True)) + lambda_init)

    def attend(k0, nk):
        keys = slice(k0, k0 + nk)

        def pair(pg):
            qv = jnp.concatenate([q_ref[0, :, h * q_half + pg * PAIR: h * q_half + (pg + 1) * PAIR]
                                  for h in range(2)], axis=0)
            kd = k_ref[0, keys, _kv_lane(pg): _kv_lane(pg) + PAIR]
            vd = v_ref[0, keys, _kv_lane(pg): _kv_lane(pg) + PAIR]
            s_scr[:, 0:nk] = lax.dot_general(qv, kd, (((1,), (1,)), ((), ())),
                                             preferred_element_type=F32)
            s = s_scr[:, 0:nk]
            p_scr[:, 0:nk] = jnp.exp2(s - jnp.max(s, axis=-1, keepdims=True)).astype(BF16)
            p = p_scr[:, 0:nk]
            o = jnp.dot(p, vd, preferred_element_type=F32)
            o = o * (1.0 / jnp.sum(p.astype(F32), axis=-1, keepdims=True))
            return o[0:ROW_TILE], o[ROW_TILE:2 * ROW_TILE]

        lo = lax.broadcasted_iota(jnp.int32, (ROW_TILE, PAIR), 1) < HEAD_DIM
        for pg in range(2 * GQA_KV_HEADS):
            o_even, o_odd = pair(pg)
            y_scr[:, pg * PAIR:(pg + 1) * PAIR] = jnp.where(lo, o_even, o_odd).astype(BF16)
        for pg in range(2 * GQA_KV_HEADS, N_PAIR_GROUPS):
            o_0, o_1 = pair(pg)
            od = o_0 - lam_val * o_1
            od = od * lax.rsqrt(jnp.mean(od * od, axis=-1, keepdims=True) + EPS)
            od = od * dnw_ref[...] * (1.0 - lambda_init)
            y_scr[:, pg * PAIR:(pg + 1) * PAIR] = od.astype(BF16)
        y = jnp.dot(y_scr[...], w_ref[...], preferred_element_type=F32)
        o_ref[0] = x_ref[0] + mod_ref[0, 0, 2:3, :] * y

    i = pl.program_id(1)

    @pl.when(i < n_lat_tiles)
    def _():
        attend(0, n_all)

    @pl.when(i >= n_lat_tiles)
    def _():
        attend(n_lat, n_all - n_lat)


def _kv_lane(pg):
    n_gqa_pairs = 2 * GQA_KV_HEADS
    if pg < n_gqa_pairs:
        return (pg // 2) * PAIR
    return (GQA_KV_HEADS + pg - n_gqa_pairs) * PAIR


def _attention(X, Q, K, V, modc, lam, dnw, w_bf16, lambda_init, n_lat_tiles):
    B, N, D = X.shape
    row = lambda b, i: (b, i, 0)
    whole = lambda b, i: (b, 0, 0)
    const2 = lambda b, i: (0, 0)
    return pl.pallas_call(
        functools.partial(_attention_kernel, lambda_init=lambda_init, n_lat_tiles=n_lat_tiles),
        grid=(B, N // ROW_TILE),
        in_specs=[pl.BlockSpec((1, ROW_TILE, D), row),
                  pl.BlockSpec((1, ROW_TILE, Q.shape[-1]), row),
                  pl.BlockSpec((1, N, K.shape[-1]), whole, pipeline_mode=pl.Buffered(1)),
                  pl.BlockSpec((1, N, V.shape[-1]), whole, pipeline_mode=pl.Buffered(1)),
                  pl.BlockSpec((1, 1, 6, D), lambda b, i: (b, i // n_lat_tiles, 0, 0)),
                  pl.BlockSpec(lam.shape, const2),
                  pl.BlockSpec((1, PAIR), const2),
                  pl.BlockSpec(w_bf16.shape, const2, pipeline_mode=pl.Buffered(1))],
        out_specs=pl.BlockSpec((1, ROW_TILE, D), row),
        out_shape=jax.ShapeDtypeStruct((B, N, D), F32),
        scratch_shapes=[pltpu.VMEM((2 * ROW_TILE, N), F32), pltpu.VMEM((2 * ROW_TILE, N), BF16),
                        pltpu.VMEM((ROW_TILE, w_bf16.shape[0]), BF16)],
        compiler_params=_cparams(("arbitrary", "arbitrary")),
        name="attention",
    )(X, Q, K, V, modc, lam, dnw.reshape(1, PAIR), w_bf16)


REC_IN_PAD = 3200
N_SSD_DT = 2 * SSD_HEADS
RET_LANE0 = N_SSD_DT
SCAN_SAMPLES = 1


def _softplus(x):
    return jnp.maximum(x, 0.0) + jnp.log(1.0 + jnp.exp(-jnp.abs(x)))


def _silu(x):
    return x * jax.nn.sigmoid(x)


def _rec_project_kernel(x_ref, nw_ref, mod_ref, w_ref, cos_ref, sin_ref, dtb_ref,
                        rq_ref, rk_ref, rv_ref, g_ref, xbc_ref, dt_ref, *, n_lat):
    h = _modulated(x_ref[0], nw_ref[...], mod_ref, 0, 1, n_lat)
    z = jnp.dot(h.astype(BF16), w_ref[...], preferred_element_type=F32)
    o_rq, o_rk, o_rv, o_rg, o_z, o_xbc, o_dt = np.cumsum((0,) + REC_SPLITS[:-1]).tolist()
    n_qk = REC_SPLITS[0]
    cos, sin = cos_ref[...], sin_ref[...]
    cos2 = jnp.concatenate([cos] * (n_qk // PAIR), axis=1)
    sin2 = jnp.concatenate([sin] * (n_qk // PAIR), axis=1)
    rq = _rope_lanes(z[:, o_rq:o_rq + n_qk], cos2, sin2)
    lo = (lax.broadcasted_iota(jnp.int32, rq.shape, 1) & (PAIR - 1)) < HEAD_DIM
    rq_ref[0, :, 0:n_qk] = jnp.where(lo, rq, 0.0).astype(BF16)
    rq_ref[0, :, n_qk:2 * n_qk] = jnp.where(lo, 0.0, rq).astype(BF16)
    rk_ref[0] = _rope_lanes(z[:, o_rk:o_rk + n_qk] * (RET_QK_DIM ** -0.5), cos2, sin2).astype(BF16)
    rv_ref[0] = z[:, o_rv:o_rg].astype(BF16)
    g_ref[0] = z[:, o_rg:o_xbc].astype(BF16)
    xbc_ref[0] = z[:, o_xbc:o_dt].astype(BF16)
    dt_ref[0] = _softplus(z[:, o_dt:o_dt + PAIR] + dtb_ref[...])


def _rec_project(X, nw, modc, w_bf16, cos_t, sin_t, dtb_row, n_lat_tiles):
    B, N, D = X.shape
    n_in = w_bf16.shape[1]
    row = lambda b, i: (b, i, 0)
    const2 = lambda b, i: (0, 0)
    widths = (2 * REC_SPLITS[0], REC_SPLITS[1], REC_SPLITS[2], REC_SPLITS[3] + REC_SPLITS[4],
              REC_SPLITS[5], PAIR)
    dtypes = (BF16, BF16, BF16, BF16, BF16, F32)
    return pl.pallas_call(
        functools.partial(_rec_project_kernel, n_lat=n_lat_tiles * ROW_TILE),
        grid=(B, N // PROJ_TILE),
        in_specs=[pl.BlockSpec((1, PROJ_TILE, D), row),
                  pl.BlockSpec((1, D), const2),
                  pl.BlockSpec((1, 2, 6, D), _MOD_BOTH),
                  pl.BlockSpec((D, n_in), const2, **_ONCE),
                  pl.BlockSpec((PROJ_TILE, PAIR), lambda b, i: (i, 0)),
                  pl.BlockSpec((PROJ_TILE, PAIR), lambda b, i: (i, 0)),
                  pl.BlockSpec((1, PAIR), const2)],
        out_specs=[pl.BlockSpec((1, PROJ_TILE, w), row) for w in widths],
        out_shape=[jax.ShapeDtypeStruct((B, N, w), dt) for w, dt in zip(widths, dtypes)],
        compiler_params=_cparams(("arbitrary", "arbitrary")),
        name="rec_project",
    )(X, nw.reshape(1, D), modc, w_bf16, cos_t, sin_t, dtb_row)


def _conv_kernel(x_ref, w_ref, b_ref, o_ref, *, n_lat):
    x = x_ref[0].astype(F32)
    n = x.shape[0]
    t = lax.broadcasted_iota(jnp.int32, x.shape, 0)
    first = (t == 0) | (t == n_lat)
    last = (t == n_lat - 1) | (t == n - 1)
    prev = jnp.where(first, 0.0, pltpu.roll(x, 1, 0))
    nxt = jnp.where(last, 0.0, pltpu.roll(x, n - 1, 0))
    y = prev * w_ref[0:1, :] + x * w_ref[1:2, :] + nxt * w_ref[2:3, :] + b_ref[...]
    o_ref[0] = _silu(y).astype(BF16)


def _ssd_conv(XBC, conv_w, conv_b, n_lat):
    B, N, C = XBC.shape
    tc = CONV_CH_TILE
    return pl.pallas_call(
        functools.partial(_conv_kernel, n_lat=n_lat),
        grid=(B, C // tc),
        in_specs=[pl.BlockSpec((1, N, tc), lambda b, j: (b, 0, j)),
                  pl.BlockSpec((conv_w.shape[0], tc), lambda b, j: (0, j)),
                  pl.BlockSpec((1, tc), lambda b, j: (0, j))],
        out_specs=pl.BlockSpec((1, N, tc), lambda b, j: (b, 0, j)),
        out_shape=jax.ShapeDtypeStruct((B, N, C), BF16),
        compiler_params=_cparams(("arbitrary", "arbitrary")),
        name="ssd_conv",
    )(XBC, conv_w, conv_b.reshape(1, C))


def _split3(x):
    hi = x.astype(BF16)
    r1 = x - hi.astype(F32)
    mid = r1.astype(BF16)
    lo = (r1 - mid.astype(F32)).astype(BF16)
    return hi, mid, lo


def _scan_kernel(rq_ref, rk_ref, rv_ref, a_ref, dt_ref, alog_ref, logit_ref, y_ref, hret, hssd, *, reverse):
    T = rq_ref.shape[1]

    @pl.when(pl.program_id(1) == 0)
    def _():
        hret[...] = jnp.zeros_like(hret)
        hssd[...] = jnp.zeros_like(hssd)

    lane = lax.broadcasted_iota(jnp.int32, (1, PAIR), 1)
    a_neg = jnp.where(lane < N_SSD_DT, -jnp.exp(alog_ref[...]), 0.0)
    logit = logit_ref[...]
    log_gamma = jnp.where((lane >= RET_LANE0) & (lane < RET_LANE0 + 2 * RET_HEADS),
                          jnp.minimum(logit, 0.0) - jnp.log(1.0 + jnp.exp(-jnp.abs(logit))), 0.0)
    ti = lax.broadcasted_iota(jnp.int32, (T, T), 0)
    tj = lax.broadcasted_iota(jnp.int32, (T, T), 1)
    causal = (tj >= ti) if reverse else (tj <= ti)
    tri = jnp.where(causal, 1.0, 0.0).astype(BF16)
    d = 1 if reverse else 0
    n_ssd = SSD_INNER
    k_row = lax.broadcasted_iota(jnp.int32, (PAIR, 2 * n_ssd), 0)
    lane_x = lax.broadcasted_iota(jnp.int32, (PAIR, 2 * n_ssd), 1)
    src = jnp.where(lane_x < n_ssd, d * SSD_HEADS + lax.shift_right_logical(lane_x, 6),
                    RET_LANE0 + d * RET_HEADS + lax.shift_right_logical(lane_x - n_ssd, 7))
    expand = jnp.where(k_row == src, 1.0, 0.0).astype(BF16)
    for smp in range(rq_ref.shape[0]):
        _scan_tile(smp, rq_ref, rk_ref, rv_ref, a_ref, dt_ref, y_ref, hret, hssd, a_neg, log_gamma, causal,
                   tri, lane, expand, reverse)


def _scan_tile(smp, rq_ref, rk_ref, rv_ref, a_ref, dt_ref, y_ref, hret, hssd, a_neg, log_gamma, causal, tri,
               lane, expand, reverse):
    T = rq_ref.shape[1]
    d = 1 if reverse else 0
    dt = dt_ref[smp]
    la = dt * a_neg + log_gamma
    hi, mid, lo3 = _split3(la)
    P = (jnp.dot(tri, hi, preferred_element_type=F32) + jnp.dot(tri, mid, preferred_element_type=F32)
         + jnp.dot(tri, lo3, preferred_element_type=F32))
    PT = P.T
    tot = P[0:1, :] if reverse else P[T - 1:T, :]
    E = jnp.exp(P)
    KD = jnp.exp(tot - P)
    ET = jnp.exp(tot)
    n_ssd = SSD_INNER
    Ex = jnp.dot(E.astype(BF16), expand, preferred_element_type=F32)
    KDx = jnp.dot(KD.astype(BF16), expand, preferred_element_type=F32)
    DTx = jnp.dot(dt.astype(BF16), expand[:, 0:n_ssd], preferred_element_type=F32)
    lo = lax.broadcasted_iota(jnp.int32, (T, PAIR), 1) < HEAD_DIM
    lo_row = lane < HEAD_DIM
    nt = (((1,), (1,)), ((), ()))
    tn = (((0,), (0,)), ((), ()))

    def decay_matrix(c):
        diff = jnp.minimum(P[:, c:c + 1] - PT[c:c + 1, :], 0.0)
        return jnp.where(causal, jnp.exp(diff), 0.0)

    n_qk = RET_HEADS * RET_QK_DIM
    for h in range(RET_HEADS):
        c = RET_LANE0 + d * RET_HEADS + h
        p, half = h // 2, h % 2
        qv = rq_ref[smp, :, half * n_qk + p * PAIR: half * n_qk + (p + 1) * PAIR]
        kp = rk_ref[smp, :, p * PAIR:(p + 1) * PAIR]
        v = rv_ref[smp, :, h * RET_V_DIM:(h + 1) * RET_V_DIM]
        s = lax.dot_general(qv, kp, nt, preferred_element_type=F32)
        aw = (s * decay_matrix(c)).astype(BF16)
        y = jnp.dot(aw, v, preferred_element_type=F32)
        e_h = Ex[:, n_ssd + h * PAIR: n_ssd + (h + 1) * PAIR]
        y = y + e_h * jnp.dot(qv, hret[smp, h].astype(BF16), preferred_element_type=F32)
        y_ref[smp, :, h * RET_V_DIM:(h + 1) * RET_V_DIM] = y.astype(y_ref.dtype)
        kdec = (kp.astype(F32) * KDx[:, n_ssd + h * PAIR: n_ssd + (h + 1) * PAIR]).astype(BF16)
        hret[smp, h] = hret[smp, h] * ET[:, c:c + 1] + lax.dot_general(kdec, v, tn, preferred_element_type=F32)

    y0 = RET_HEADS * RET_V_DIM
    r_per_g = SSD_HEADS // SSD_GROUPS
    o_b = SSD_INNER
    o_c = SSD_INNER + SSD_GROUPS * SSD_STATE
    for g in range(SSD_GROUPS):
        cg = a_ref[smp, :, o_c + g * SSD_STATE: o_c + (g + 1) * SSD_STATE]
        bg = a_ref[smp, :, o_b + g * SSD_STATE: o_b + (g + 1) * SSD_STATE]
        s = lax.dot_general(cg, bg, nt, preferred_element_type=F32)
        ch = jnp.dot(cg, hssd[smp, g].astype(BF16), preferred_element_type=F32)
        vdec, dec_rows = [], []
        for p in range(r_per_g // 2):
            pg = g * (r_per_g // 2) + p
            xs = a_ref[smp, :, pg * PAIR:(pg + 1) * PAIR].astype(F32)
            pair = slice(pg * PAIR, (pg + 1) * PAIR)
            v_pair = xs * DTx[:, pair]
            v_bf = v_pair.astype(BF16)
            cols = [d * SSD_HEADS + 2 * pg + half for half in range(2)]
            ys = [jnp.dot((s * decay_matrix(c)).astype(BF16), v_bf, preferred_element_type=F32) for c in cols]
            y_ref[smp, :, y0 + pg * PAIR: y0 + (pg + 1) * PAIR] = (
                jnp.where(lo, ys[0], ys[1]) + Ex[:, pair] * ch[:, p * PAIR:(p + 1) * PAIR]).astype(y_ref.dtype)
            vdec.append((v_pair * KDx[:, pair]).astype(BF16))
            dec_rows.append(jnp.where(lo_row, ET[:, cols[0]:cols[0] + 1], ET[:, cols[1]:cols[1] + 1]))
        hssd[smp, g] = (hssd[smp, g] * jnp.concatenate(dec_rows, axis=1)
                   + lax.dot_general(bg, jnp.concatenate(vdec, axis=1), tn, preferred_element_type=F32))


def _bidir_scan(RQ, RK, RV, A, DT, alog_row, logit_row, n_lat_tiles, reverse):
    B, N, _ = RQ.shape
    n_tiles = N // ROW_TILE

    def tile(i):
        lat = (n_lat_tiles - i) if reverse else (i - 1)
        return jnp.where(i == 0, n_tiles - 1, lat)

    row = lambda b, i: (b, tile(i), 0)
    const2 = lambda b, i: (0, 0)
    y_w = RET_HEADS * RET_V_DIM + SSD_INNER
    ns = math.gcd(B, SCAN_SAMPLES)
    return pl.pallas_call(
        functools.partial(_scan_kernel, reverse=reverse),
        grid=(B // ns, n_tiles),
        in_specs=[pl.BlockSpec((ns, ROW_TILE, RQ.shape[-1]), row),
                  pl.BlockSpec((ns, ROW_TILE, RK.shape[-1]), row),
                  pl.BlockSpec((ns, ROW_TILE, RV.shape[-1]), row),
                  pl.BlockSpec((ns, ROW_TILE, A.shape[-1]), row),
                  pl.BlockSpec((ns, ROW_TILE, PAIR), row),
                  pl.BlockSpec((1, PAIR), const2),
                  pl.BlockSpec((1, PAIR), const2)],
        out_specs=pl.BlockSpec((ns, ROW_TILE, y_w), row),
        out_shape=jax.ShapeDtypeStruct((B, N, y_w), BF16),
        scratch_shapes=[pltpu.VMEM((ns, RET_HEADS, PAIR, RET_V_DIM), F32),
                        pltpu.VMEM((ns, SSD_GROUPS, SSD_STATE, SSD_INNER // SSD_GROUPS), F32)],
        compiler_params=_cparams(("arbitrary", "arbitrary")),
        name="scan_bwd" if reverse else "scan_fwd",
    )(RQ, RK, RV, A, DT, alog_row, logit_row)


def _rec_combine_kernel(x_ref, yf_ref, yb_ref, g_ref, a_ref, mod_ref, dskip_ref, snw_ref, w_ref, o_ref, y_scr,
                        *, n_lat):
    n_ret = RET_HEADS * RET_V_DIM
    y = yf_ref[0].astype(F32) + yb_ref[0].astype(F32)
    gates = g_ref[0].astype(F32)
    for h in range(RET_HEADS):
        sl = slice(h * RET_V_DIM, (h + 1) * RET_V_DIM)
        r = y[:, sl]
        r = r * lax.rsqrt(jnp.mean(r * r, axis=-1, keepdims=True) + EPS)
        y_scr[:, sl] = (r * _silu(gates[:, sl])).astype(BF16)
    gw = SSD_INNER // SSD_GROUPS
    for g in range(SSD_GROUPS):
        sl = slice(g * gw, (g + 1) * gw)
        s = y[:, n_ret + g * gw: n_ret + (g + 1) * gw] + dskip_ref[:, sl] * a_ref[0, :, sl].astype(F32)
        s = s * _silu(gates[:, n_ret + g * gw: n_ret + (g + 1) * gw])
        s = s * lax.rsqrt(jnp.mean(s * s, axis=-1, keepdims=True) + EPS) * snw_ref[:, sl]
        y_scr[:, n_ret + g * gw: n_ret + (g + 1) * gw] = s.astype(BF16)
    gate = _row_mod(mod_ref, 2, x_ref.shape[1], n_lat)
    o_ref[0] = x_ref[0] + gate * jnp.dot(y_scr[...], w_ref[...], preferred_element_type=F32)


def _rec_combine(X, Yf, Yb, G, A, modc, dskip_row, snw_row, w_bf16, n_lat_tiles):
    B, N, D = X.shape
    row = lambda b, i: (b, i, 0)
    const2 = lambda b, i: (0, 0)
    return pl.pallas_call(
        functools.partial(_rec_combine_kernel, n_lat=n_lat_tiles * ROW_TILE),
        grid=(B, N // PROJ_TILE),
        in_specs=[pl.BlockSpec((1, PROJ_TILE, D), row),
                  pl.BlockSpec((1, PROJ_TILE, Yf.shape[-1]), row),
                  pl.BlockSpec((1, PROJ_TILE, Yb.shape[-1]), row),
                  pl.BlockSpec((1, PROJ_TILE, G.shape[-1]), row),
                  pl.BlockSpec((1, PROJ_TILE, SSD_INNER), row),
                  pl.BlockSpec((1, 2, 6, D), _MOD_BOTH),
                  pl.BlockSpec((1, SSD_INNER), const2),
                  pl.BlockSpec((1, SSD_INNER), const2),
                  pl.BlockSpec(w_bf16.shape, const2, **_ONCE)],
        out_specs=pl.BlockSpec((1, PROJ_TILE, D), row),
        out_shape=jax.ShapeDtypeStruct((B, N, D), F32),
        scratch_shapes=[pltpu.VMEM((PROJ_TILE, w_bf16.shape[0]), BF16)],
        compiler_params=_cparams(("arbitrary", "arbitrary")),
        name="rec_combine",
    )(X, Yf, Yb, G, A, modc, dskip_row, snw_row, w_bf16)


def _recurrent_layer(X, nw, modc, w_in, w_out, cos_t, sin_t, ret_decay_logit, conv_w, conv_b, dt_bias,
                     a_log, d_skip, ssd_norm_w, n_lat_tiles):
    D = X.shape[-1]
    w_pad = jnp.pad(w_in, ((0, 0), (0, REC_IN_PAD - w_in.shape[1]))).astype(BF16)
    pad_row = lambda v, at: jnp.pad(v.reshape(1, -1), ((0, 0), (at, PAIR - at - v.size)))
    RQ, RK, RV, G, XBC, DT = _rec_project(X, nw, modc, w_pad, cos_t, sin_t, pad_row(dt_bias, 0), n_lat_tiles)
    A = _ssd_conv(XBC, conv_w, conv_b, n_lat_tiles * ROW_TILE)
    alog_row, logit_row = pad_row(a_log, 0), pad_row(ret_decay_logit, RET_LANE0)
    Yf = _bidir_scan(RQ, RK, RV, A, DT, alog_row, logit_row, n_lat_tiles, False)
    Yb = _bidir_scan(RQ, RK, RV, A, DT, alog_row, logit_row, n_lat_tiles, True)
    return _rec_combine(X, Yf, Yb, G, A, modc, jnp.repeat(d_skip, SSD_HEAD_DIM).reshape(1, -1),
                        ssd_norm_w.reshape(1, -1), w_out.astype(BF16), n_lat_tiles)


SEL_CHUNK = 256
SELECT_SAMPLES = 4
COMBINE_WINDOW = 64
BF16_ROWS = 16
FFN_ROW_CHUNK = 1024


def _router_kernel(x_ref, nw_ref, mod_ref, rw_ref, h_ref, aff_ref, *, n_lat):
    h = _modulated(x_ref[0], nw_ref[...], mod_ref, 3, 4, n_lat)
    h_hi = h.astype(BF16)
    h_ref[0] = h_hi
    h_lo = (h - h_hi.astype(F32)).astype(BF16)
    w = rw_ref[...]
    w_hi = w.astype(BF16)
    w_lo = (w - w_hi.astype(F32)).astype(BF16)
    logits = (jnp.dot(h_hi, w_hi, preferred_element_type=F32) + jnp.dot(h_lo, w_hi, preferred_element_type=F32)
              + jnp.dot(h_hi, w_lo, preferred_element_type=F32))
    logits = logits.T[0:aff_ref.shape[1], :]
    e = jnp.exp(logits - jnp.max(logits, axis=0, keepdims=True))
    aff_ref[0] = e / jnp.sum(e, axis=0, keepdims=True)


def _mod_router(X, nw, modc, router_w, n_lat, n_rows):
    B, _, D = X.shape
    E = router_w.shape[1]
    rw_pad = jnp.pad(router_w, ((0, 0), (0, PAIR - E)))
    tile = _row_tile(n_rows, PROJ_TILE)
    return pl.pallas_call(
        functools.partial(_router_kernel, n_lat=n_lat),
        grid=(B, n_rows // tile),
        in_specs=[pl.BlockSpec((1, tile, D), lambda b, i: (b, i, 0)),
                  pl.BlockSpec((1, D), lambda b, i: (0, 0)),
                  pl.BlockSpec((1, 2, 6, D), _MOD_BOTH),
                  pl.BlockSpec((D, PAIR), lambda b, i: (0, 0))],
        out_specs=[pl.BlockSpec((1, tile, D), lambda b, i: (b, i, 0)),
                   pl.BlockSpec((1, E, tile), lambda b, i: (b, 0, i))],
        out_shape=[jax.ShapeDtypeStruct((B, n_rows, D), BF16),
                   jax.ShapeDtypeStruct((B, E, n_rows), F32)],
        compiler_params=_cparams(("arbitrary", "arbitrary")),
        name="mod_router",
    )(X, nw.reshape(1, D), modc, rw_pad)


def _count(mask_f32):
    return jnp.sum(mask_f32, axis=-1, keepdims=True)


def _select_top(a, cap):
    E, n = a.shape
    v = pltpu.bitcast(a, jnp.int32)
    thr = jnp.zeros((E, 1), jnp.int32)
    for bit in range(30, -1, -1):
        cand = thr | (1 << bit)
        thr = jnp.where(_count(jnp.where(v >= cand, 1.0, 0.0)) >= cap, cand, thr)
    gt = v > thr
    eq = v == thr
    need = cap - _count(jnp.where(gt, 1.0, 0.0))
    idx = lax.broadcasted_iota(jnp.int32, (E, n), 1)
    last = jnp.zeros((E, 1), jnp.int32)
    for bit in range(n.bit_length() - 2, -1, -1):
        cand = last | (1 << bit)
        below = _count(jnp.where(eq, jnp.where(idx < cand, 1.0, 0.0), 0.0))
        last = jnp.where(below < need, cand, last)
    sel = jnp.where(gt, 1.0, jnp.where(eq, jnp.where(idx <= last, 1.0, 0.0), 0.0))
    si = lax.broadcasted_iota(jnp.int32, (SEL_CHUNK, SEL_CHUNK), 0)
    sj = lax.broadcasted_iota(jnp.int32, (SEL_CHUNK, SEL_CHUNK), 1)
    before = jnp.where(si < sj, 1.0, 0.0).astype(BF16)
    base = jnp.zeros((E, 1), F32)
    lane = lax.broadcasted_iota(jnp.int32, (E, PAIR), 1)
    starts = jnp.zeros((E, PAIR), F32)
    pos = []
    for k in range(n // SEL_CHUNK):
        sk = sel[:, k * SEL_CHUNK:(k + 1) * SEL_CHUNK]
        pos.append(jnp.dot(sk.astype(BF16), before, preferred_element_type=F32) + base)
        base = base + _count(sk)
        starts = jnp.where(lane == k + 1, base, starts)
    pos = jnp.concatenate(pos, axis=1) if len(pos) > 1 else pos[0]
    return jnp.where(sel > 0.0, pos, -1.0), jnp.where(sel > 0.0, a, 0.0), starts


def _token_sets(n_lat, n_rows):
    sets = [(0, n_lat, EC_CAPACITY * n_lat // N_EXPERTS)]
    if n_rows > n_lat:
        sets.append((n_lat, n_rows, EC_CAPACITY * (n_rows - n_lat) // N_EXPERTS))
    return sets


def _select_kernel(aff_ref, slot_ref, slot_t_ref, gate_t_ref, starts_ref, *, sets):
    E = aff_ref.shape[1]
    pad_rows = PAIR - E
    for smp in range(aff_ref.shape[0]):
        for n_set, (lo, hi, cap) in enumerate(sets):
            slot, gate, starts = _select_top(aff_ref[smp, :, lo:hi], cap)
            if n_set == 0:
                starts_ref[smp] = starts
            slot_ref[smp, :, lo:hi] = slot
            slot_t_ref[smp, lo:hi, :] = jnp.concatenate(
                [slot, jnp.full((pad_rows, hi - lo), -1.0, F32)], axis=0).T
            gate_t_ref[smp, lo:hi, :] = jnp.concatenate(
                [gate, jnp.zeros((pad_rows, hi - lo), F32)], axis=0).T


def _ec_select(aff, sets):
    B, E, N = aff.shape
    ns = math.gcd(B, SELECT_SAMPLES)
    return pl.pallas_call(
        functools.partial(_select_kernel, sets=sets),
        grid=(B // ns,),
        in_specs=[pl.BlockSpec((ns, E, N), lambda b: (b, 0, 0))],
        out_specs=[pl.BlockSpec((ns, E, N), lambda b: (b, 0, 0)),
                   pl.BlockSpec((ns, N, PAIR), lambda b: (b, 0, 0)),
                   pl.BlockSpec((ns, N, PAIR), lambda b: (b, 0, 0)),
                   pl.BlockSpec((ns, E, PAIR), lambda b: (b, 0, 0))],
        out_shape=[jax.ShapeDtypeStruct((B, E, N), F32),
                   jax.ShapeDtypeStruct((B, N, PAIR), F32),
                   jax.ShapeDtypeStruct((B, N, PAIR), F32),
                   jax.ShapeDtypeStruct((B, E, PAIR), F32)],
        compiler_params=_cparams(("arbitrary",)),
        name="ec_select",
    )(aff)


def _gather_kernel(starts_ref, slot_ref, h_ref, *refs, n_lat):
    x_refs, acc_ref = refs[:-1], refs[-1]
    xl_ref = x_refs[0]
    E, cap_lat, _ = xl_ref.shape
    tile = h_ref.shape[1]
    b, i = pl.program_id(0), pl.program_id(1)
    n_lat_tiles = n_lat // tile

    def picked(e, first_slot, n_slots):
        c = lax.broadcasted_iota(jnp.int32, (n_slots, tile), 0).astype(F32) + first_slot
        return jnp.where(slot_ref[0, e:e + 1, :] == c, 1.0, 0.0).astype(BF16)

    @pl.when(i == 0)
    def _():
        acc_ref[...] = jnp.zeros_like(acc_ref)

    @pl.when(i < n_lat_tiles)
    def _():
        def all_slots():
            for e in range(E):
                acc_ref[e] += jnp.dot(picked(e, 0.0, cap_lat), h_ref[0], preferred_element_type=F32)

        if cap_lat <= COMBINE_WINDOW:
            all_slots()
            return
        window_starts, fits = _slot_windows(starts_ref, b, i, E, cap_lat)

        @pl.when(fits)
        def _():
            onehot = jnp.concatenate([picked(e, window_starts[e].astype(F32), COMBINE_WINDOW)
                                      for e in range(E)], axis=0)
            x = jnp.dot(onehot, h_ref[0], preferred_element_type=F32)
            for e in range(E):
                rows = pl.ds(pl.multiple_of(window_starts[e], BF16_ROWS), COMBINE_WINDOW)
                acc_ref[e, rows, :] += x[e * COMBINE_WINDOW:(e + 1) * COMBINE_WINDOW, :]

        @pl.when(jnp.logical_not(fits))
        def _():
            all_slots()

    @pl.when(i == n_lat_tiles - 1)
    def _():
        xl_ref[...] = acc_ref[...].astype(BF16)

    if len(x_refs) > 1:
        xc_ref = x_refs[1]

        @pl.when(i >= n_lat_tiles)
        def _():
            for e in range(E):
                xc_ref[e] = jnp.dot(picked(e, 0.0, xc_ref.shape[1]), h_ref[0],
                                    preferred_element_type=F32).astype(BF16)


def _slot_windows(starts_ref, b, i, n_experts, cap):
    window_starts, fits = [], None
    for e in range(n_experts):
        first, end = starts_ref[b, e, i], starts_ref[b, e, i + 1]
        w = jnp.minimum(first - lax.rem(first, BF16_ROWS), cap - COMBINE_WINDOW)
        ok = end - w <= COMBINE_WINDOW
        fits = ok if fits is None else jnp.logical_and(fits, ok)
        window_starts.append(w)
    return window_starts, fits


def _ec_gather(slot, H, starts_i32, sets, n_lat):
    B, E, N = slot.shape
    D = H.shape[-1]
    tile = SEL_CHUNK
    cap_lat = sets[0][2]
    grid_spec = pltpu.PrefetchScalarGridSpec(
        num_scalar_prefetch=1,
        grid=(B, N // tile),
        in_specs=[pl.BlockSpec((1, E, tile), lambda b, i, s: (b, 0, i)),
                  pl.BlockSpec((1, tile, D), lambda b, i, s: (b, i, 0))],
        out_specs=[pl.BlockSpec((E, cap, D), lambda b, i, s: (0, b, 0)) for _, _, cap in sets],
        scratch_shapes=[pltpu.VMEM((E, cap_lat, D), F32)])
    return pl.pallas_call(
        functools.partial(_gather_kernel, n_lat=n_lat),
        grid_spec=grid_spec,
        out_shape=[jax.ShapeDtypeStruct((E, B * cap, D), BF16) for _, _, cap in sets],
        compiler_params=_cparams(("arbitrary", "arbitrary")),
        name="ec_gather",
    )(starts_i32, slot, H)


def _ffn_kernel(*refs, n_groups, row_chunk):
    x_refs = refs[:n_groups]
    wg_ref, wu_ref, wd_ref = refs[n_groups:n_groups + 3]
    o_refs = refs[n_groups + 3:2 * n_groups + 3]
    acc_refs = refs[2 * n_groups + 3:3 * n_groups + 3]
    wg_s, wu_s, wd_s = refs[3 * n_groups + 3:]
    f = pl.program_id(2)
    n_f = pl.num_programs(2)
    wg_s[...] = wg_ref[0, 0].astype(BF16)
    wu_s[...] = wu_ref[0, 0].astype(BF16)
    wd_s[...] = wd_ref[0, 0].astype(BF16)

    @pl.when(f == 0)
    def _():
        for acc_ref in acc_refs:
            acc_ref[...] = jnp.zeros_like(acc_ref)

    def chunk(x_ref, acc_ref, start, size):
        rows = pl.ds(start, size)
        xr = x_ref[0, rows, :]
        a = jnp.dot(xr, wg_s[...], preferred_element_type=F32)
        u = jnp.dot(xr, wu_s[...], preferred_element_type=F32)
        hm = (a * jax.nn.sigmoid(a) * u).astype(BF16)
        acc_ref[rows, :] += jnp.dot(hm, wd_s[...], preferred_element_type=F32)

    for x_ref, acc_ref in zip(x_refs, acc_refs):
        n_rows = x_ref.shape[1]
        for start in range(0, n_rows, row_chunk):
            chunk(x_ref, acc_ref, start, min(row_chunk, n_rows - start))

    @pl.when(f == n_f - 1)
    def _():
        for o_ref, acc_ref in zip(o_refs, acc_refs):
            o_ref[0] = acc_ref[...].astype(o_ref.dtype)


def _expert_ffn(xs, w_gate, w_up, w_down, layer):
    E, _, D = xs[0].shape
    F = w_gate.shape[-1]
    m_tiles = FFN_ROW_SPLITS
    tms = [x.shape[1] // m_tiles for x in xs]
    n = len(xs)
    return pl.pallas_call(
        functools.partial(_ffn_kernel, n_groups=n, row_chunk=FFN_ROW_CHUNK),
        grid=(E, m_tiles, F // FF_TILE),
        in_specs=[pl.BlockSpec((1, tm, D), lambda e, m, f: (e, m, 0)) for tm in tms]
        + [pl.BlockSpec((1, 1, D, FF_TILE), lambda e, m, f: (layer, e, 0, f)),
           pl.BlockSpec((1, 1, D, FF_TILE), lambda e, m, f: (layer, e, 0, f)),
           pl.BlockSpec((1, 1, FF_TILE, D), lambda e, m, f: (layer, e, f, 0))],
        out_specs=[pl.BlockSpec((1, tm, D), lambda e, m, f: (e, m, 0)) for tm in tms],
        out_shape=[jax.ShapeDtypeStruct(x.shape, BF16) for x in xs],
        scratch_shapes=[pltpu.VMEM((tm, D), F32) for tm in tms]
        + [pltpu.VMEM((D, FF_TILE), BF16), pltpu.VMEM((D, FF_TILE), BF16), pltpu.VMEM((FF_TILE, D), BF16)],
        compiler_params=_cparams(("arbitrary", "arbitrary", "arbitrary")),
        name="expert_ffn",
    )(*xs, w_gate, w_up, w_down)


def _combine_kernel(starts_ref, x_ref, slot_t_ref, gate_t_ref, mod_ref, fnw_ref, *refs, n_lat, final_norm):
    y_refs, o_ref = refs[:-1], refs[-1]
    E = y_refs[0].shape[0]
    tile = x_ref.shape[1]
    b, i = pl.program_id(0), pl.program_id(1)

    def finish(acc, seg):
        out = x_ref[0] + mod_ref[0, seg, 5:6, :] * acc
        if final_norm:
            out = out * lax.rsqrt(jnp.mean(out * out, axis=-1, keepdims=True) + EPS) * fnw_ref[...]
        o_ref[0] = out

    def scatter_full(seg):
        y_ref = y_refs[seg]
        cap = y_ref.shape[1]
        c = lax.broadcasted_iota(jnp.int32, (tile, cap), 1).astype(F32)
        acc = jnp.zeros((tile, x_ref.shape[2]), F32)
        for e in range(E):
            onehot = jnp.where(slot_t_ref[0, :, e:e + 1] == c, 1.0, 0.0).astype(BF16)
            acc = acc + gate_t_ref[0, :, e:e + 1] * jnp.dot(onehot, y_ref[e], preferred_element_type=F32)
        finish(acc, seg)

    def scatter_windowed(window_starts):
        y_ref = y_refs[0]
        k = lax.broadcasted_iota(jnp.int32, (tile, COMBINE_WINDOW), 1).astype(F32)
        sel, rows = [], []
        for e in range(E):
            w = window_starts[e]
            rel = slot_t_ref[0, :, e:e + 1] - w.astype(F32)
            sel.append(jnp.where(rel == k, gate_t_ref[0, :, e:e + 1], 0.0).astype(BF16))
            rows.append(y_ref[e, pl.ds(pl.multiple_of(w, BF16_ROWS), COMBINE_WINDOW), :])
        acc = jnp.dot(jnp.concatenate(sel, axis=1), jnp.concatenate(rows, axis=0),
                      preferred_element_type=F32)
        finish(acc, 0)

    n_lat_tiles = n_lat // tile
    cap_lat = y_refs[0].shape[1]

    @pl.when(i < n_lat_tiles)
    def _():
        if cap_lat <= COMBINE_WINDOW:
            scatter_full(0)
            return
        window_starts, fits = _slot_windows(starts_ref, b, i, E, cap_lat)

        @pl.when(fits)
        def _():
            scatter_windowed(window_starts)

        @pl.when(jnp.logical_not(fits))
        def _():
            scatter_full(0)

    if len(y_refs) > 1:
        @pl.when(i >= n_lat_tiles)
        def _():
            scatter_full(1)


def _row_tile(n_rows, preferred):
    return preferred if n_rows % preferred == 0 else 2 * ROW_TILE


def _ec_combine(X, slot_t, gate_t, starts_i32, Ys, modc, sets, n_lat, final_norm_w):
    B, _, D = X.shape
    E = Ys[0].shape[0]
    n_rows = sets[-1][1]
    tile = SEL_CHUNK
    row = lambda b, i, s: (b, i, 0)
    final_norm = final_norm_w is not None
    fnw = final_norm_w.reshape(1, D) if final_norm else jnp.ones((1, D), F32)
    grid_spec = pltpu.PrefetchScalarGridSpec(
        num_scalar_prefetch=1,
        grid=(B, n_rows // tile),
        in_specs=[pl.BlockSpec((1, tile, D), row),
                  pl.BlockSpec((1, tile, PAIR), row),
                  pl.BlockSpec((1, tile, PAIR), row),
                  pl.BlockSpec((1, 2, 6, D), lambda b, i, s: (b, 0, 0, 0)),
                  pl.BlockSpec((1, D), lambda b, i, s: (0, 0))]
        + [pl.BlockSpec((E, cap, D), lambda b, i, s: (0, b, 0)) for _, _, cap in sets],
        out_specs=pl.BlockSpec((1, tile, D), row))
    return pl.pallas_call(
        functools.partial(_combine_kernel, n_lat=n_lat, final_norm=final_norm),
        grid_spec=grid_spec,
        out_shape=jax.ShapeDtypeStruct((B, n_rows, D), F32),
        compiler_params=_cparams(("arbitrary", "arbitrary")),
        name="ec_combine",
    )(starts_i32, X, slot_t, gate_t, modc, fnw, *Ys)


def _moe_layer(X, nw, modc, router_w, w_gate, w_up, w_down, layer, n_lat, n_rows, final_norm_w=None):
    sets = _token_sets(n_lat, n_rows)
    H, aff = _mod_router(X, nw, modc, router_w, n_lat, n_rows)
    slot, slot_t, gate_t, starts = _ec_select(aff, sets)
    starts_i32 = starts[:, :, :n_lat // SEL_CHUNK + 1].astype(jnp.int32)
    Xs = _ec_gather(slot, H, starts_i32, sets, n_lat)
    Ys = _expert_ffn(Xs, w_gate, w_up, w_down, layer)
    return _ec_combine(X, slot_t, gate_t, starts_i32, Ys, modc, sets, n_lat, final_norm_w)


def _rope_tables(n):
    t = jnp.arange(n)
    row = (t // GRID_W).astype(F32)
    col = (t % GRID_W).astype(F32)
    n_freq = HEAD_DIM // 4
    inv = ROPE_THETA ** (-jnp.arange(n_freq, dtype=F32) / n_freq)
    ang = jnp.concatenate([row[:, None] * inv, col[:, None] * inv], axis=-1)
    return jnp.cos(ang), jnp.sin(ang)


def kernel(x, c, ctx, c_ctx, ada_w, ada_b, norm1_w, norm2_w, att_w_in, att_w_out, att_q_norm_w,
           att_k_norm_w, diff_lambda, diff_norm_w, rec_w_in, rec_w_out, ret_decay_logit, ssd_conv_w,
           ssd_conv_b, ssd_dt_bias, ssd_a_log, ssd_d_skip, ssd_norm_w, router_w, expert_w_gate,
           expert_w_up, expert_w_down, final_norm_w):
    B, n_lat, D = x.shape
    n_ctx = ctx.shape[1]
    n_lat_tiles = n_lat // ROW_TILE
    cos, sin = _rope_tables(n_lat)
    reps = PAIR // cos.shape[1]
    cos_t = jnp.concatenate([jnp.tile(cos, (1, reps)), jnp.ones((n_ctx, PAIR), F32)], axis=0)
    sin_t = jnp.concatenate([jnp.tile(jnp.concatenate([-sin, sin], axis=1), (1, reps // 2)),
                             jnp.zeros((n_ctx, PAIR), F32)], axis=0)
    gmat = jnp.kron(jnp.eye(GQA_HEADS, dtype=F32), jnp.ones((HEAD_DIM, HEAD_DIM), F32)).astype(BF16)

    n_cond = -(-(B + 1) // SUBLANES) * SUBLANES
    cvec = jnp.concatenate([c, c_ctx[None, :], jnp.zeros((n_cond - B - 1, D), F32)], axis=0)
    mods = _ada_modulation(cvec, ada_w, ada_b)

    X = jnp.concatenate([x, ctx], axis=1)
    for layer in range(DEPTH):
        i = layer // 2
        need_ctx = layer < DEPTH - 1
        mod_lat = mods[layer, :B].reshape(B, 1, 6, D)
        mod_ctx = jnp.broadcast_to(mods[layer, B].reshape(1, 1, 6, D), (B, 1, 6, D))
        modc = jnp.concatenate([mod_lat, mod_ctx], axis=1)

        if layer % 2 == 0:
            lambda_init = 0.8 - 0.6 * math.exp(-0.3 * layer)
            Q, K, V = _att_project(X, norm1_w[layer], modc, att_w_in[i].astype(BF16), cos_t, sin_t, gmat,
                                   jnp.tile(att_q_norm_w[i], GQA_HEADS).reshape(1, -1),
                                   jnp.tile(att_k_norm_w[i], GQA_KV_HEADS).reshape(1, -1), n_lat_tiles)
            X = _attention(X, Q, K, V, modc, diff_lambda[i], diff_norm_w[i], att_w_out[i].astype(BF16),
                           lambda_init, n_lat_tiles)
        else:
            X = _recurrent_layer(X, norm1_w[layer], modc, rec_w_in[i], rec_w_out[i], cos_t, sin_t,
                                 ret_decay_logit[i], ssd_conv_w[i], ssd_conv_b[i], ssd_dt_bias[i],
                                 ssd_a_log[i], ssd_d_skip[i], ssd_norm_w[i], n_lat_tiles)

        X = _moe_layer(X, norm2_w[layer], modc, router_w[layer], expert_w_gate, expert_w_up, expert_w_down,
                       layer, n_lat, n_lat + n_ctx if need_ctx else n_lat,
                       None if need_ctx else final_norm_w)

    return X
```
